```python
import math
import jax, jax.numpy as jnp
from jax import lax
import numpy as np

D_MODEL = 2048
BATCH = 8
SEQ = 4096
DEPTH = 4

D_INNER_A = D_MODEL
SSD_HEAD_DIM = 64
SSD_HEADS = D_INNER_A // SSD_HEAD_DIM
SSD_GROUPS = 4
SSD_STATE = 128
SSD_CHUNK = 128
CONV_WIDTH = 4
CONV_DIM = D_INNER_A + 2 * SSD_GROUPS * SSD_STATE
D_INNER_B = D_MODEL
SGU_CHUNK = 128
SGU_GROUP_DIM = 128
SGU_GROUPS = D_INNER_B // SGU_GROUP_DIM
D_INNER_C = D_MODEL
SB_HEAD_DIM = 128
SB_HEADS = D_INNER_C // SB_HEAD_DIM
SB_BLOCK = 128
EVEN_IN = 2 * D_INNER_A + 2 * SSD_GROUPS * SSD_STATE + SSD_HEADS + 3 * D_INNER_B
ODD_IN = 4 * D_INNER_C
N_EVEN = (DEPTH + 1) // 2
N_ODD = DEPTH // 2
NORM_EPS = 1e-5

kernel_name = "hybrid_ssd_sgu_stickbreak_trunk"


def rms_norm(x, w, eps=NORM_EPS):
    xf = x.astype(jnp.float32)
    xf = xf * lax.rsqrt(jnp.mean(xf * xf, axis=-1, keepdims=True) + eps)
    return xf.astype(x.dtype) * w


def layer_norm(x, w, b, eps=NORM_EPS):
    xf = x.astype(jnp.float32)
    mu = jnp.mean(xf, axis=-1, keepdims=True)
    xc = xf - mu
    xf = xc * lax.rsqrt(jnp.mean(xc * xc, axis=-1, keepdims=True) + eps)
    return xf.astype(x.dtype) * w + b


def causal_depthwise_conv(x, w, b):
    c = x.shape[-1]
    k = w.shape[0]
    y = lax.conv_general_dilated(
        x, w[:, None, :], window_strides=(1,), padding=((k - 1, 0),),
        dimension_numbers=("NWC", "WIO", "NWC"), feature_group_count=c)
    return y + b


def ssd_scan(xs, dt, b_in, c_in, a_log, d_skip):
    bsz, s, _ = xs.shape
    nc, L = s // SSD_CHUNK, SSD_CHUNK
    G, R, P, N = SSD_GROUPS, SSD_HEADS // SSD_GROUPS, SSD_HEAD_DIM, SSD_STATE
    f32 = jnp.float32
    a = -jnp.exp(a_log.astype(f32))
    x_h = xs.reshape(bsz, s, SSD_HEADS, P).astype(f32)
    x_c = (x_h * dt[..., None]).reshape(bsz, nc, L, G, R, P)
    b_c = b_in.astype(f32).reshape(bsz, nc, L, G, N)
    c_c = c_in.astype(f32).reshape(bsz, nc, L, G, N)
    a_dt = (dt * a).reshape(bsz, nc, L, G, R)
    cs = jnp.cumsum(a_dt, axis=2)
    causal = jnp.tril(jnp.ones((L, L), dtype=bool))[None, None, :, :, None, None]
    seg = cs[:, :, :, None] - cs[:, :, None, :]
    decay = jnp.exp(jnp.where(causal, seg, -jnp.inf))
    scores = jnp.einsum("bclgn,bcsgn->bclsg", c_c, b_c)
    y_diag = jnp.einsum("bclsg,bclsgr,bcsgrp->bclgrp", scores, decay, x_c)
    end_decay = jnp.exp(cs[:, :, -1:] - cs)
    states = jnp.einsum("bclgn,bclgr,bclgrp->bcgrpn", b_c, end_decay, x_c)
    chunk_decay = jnp.exp(cs[:, :, -1])

    def step(h, inp):
        st, dec = inp
        return dec[..., None, None] * h + st, h

    h0 = jnp.zeros((bsz, G, R, P, N), f32)
    _, prev = lax.scan(step, h0, (jnp.moveaxis(states, 1, 0), jnp.moveaxis(chunk_decay, 1, 0)))
    prev = jnp.moveaxis(prev, 0, 1)
    y_off = jnp.einsum("bclgn,bcgrpn,bclgr->bclgrp", c_c, prev, jnp.exp(cs))
    y = (y_diag + y_off).reshape(bsz, s, SSD_HEADS, P) + x_h * d_skip.astype(f32)[:, None]
    return y.reshape(bsz, s, D_INNER_A).astype(xs.dtype)


def chunked_sgu(u, v, ln_w, ln_b, w_s, b_s):
    bsz, s, _ = u.shape
    nc = s // SGU_CHUNK
    u = jax.nn.gelu(u)
    v = layer_norm(jax.nn.gelu(v), ln_w, ln_b)
    v_c = v.reshape(bsz, nc, SGU_CHUNK, SGU_GROUPS, SGU_GROUP_DIM)
    w_causal = jnp.tril(w_s)
    mixed = jnp.einsum("gts,bcsgd->bctgd", w_causal, v_c) + b_s.T[:, :, None]
    return u * mixed.reshape(bsz, s, D_INNER_B)


def stick_breaking_attention(q, k, v):
    bsz, s, _, d = q.shape
    scale = d ** -0.5
    outs = []
    for blk in range(s // SB_BLOCK):
        q0, q1 = blk * SB_BLOCK, (blk + 1) * SB_BLOCK
        qb = q[:, q0:q1].astype(jnp.float32)
        kb = k[:, :q1].astype(jnp.float32)
        z = jnp.einsum("bthd,bshd->bhts", qb, kb) * scale
        mask = jnp.arange(q1)[None, :] < jnp.arange(q0, q1)[:, None]
        log_keep = jnp.where(mask, -jax.nn.softplus(z), 0.0)
        after = lax.cumsum(log_keep, axis=3, reverse=True) - log_keep
        w = jnp.where(mask, jnp.exp(jax.nn.log_sigmoid(z) + after), 0.0)
        outs.append(jnp.einsum("bhts,bshd->bthd", w.astype(v.dtype), v[:, :q1]))
    return jnp.concatenate(outs, axis=1)


def even_layer(h, w_in, conv_w, conv_b, dt_bias, a_log, d_skip, ssd_norm_w,
               sgu_ln_w, sgu_ln_b, sgu_w, sgu_b, w_out):
    proj = h @ w_in
    o1 = D_INNER_A
    o2 = o1 + CONV_DIM
    o3 = o2 + SSD_HEADS
    o4 = o3 + D_INNER_B
    o5 = o4 + D_INNER_B
    z_a, xbc, dt_raw, g_b, u_b, v_b = jnp.split(proj, [o1, o2, o3, o4, o5], axis=-1)
    xbc = jax.nn.silu(causal_depthwise_conv(xbc, conv_w, conv_b))
    xs, b_in, c_in = jnp.split(xbc, [D_INNER_A, D_INNER_A + SSD_GROUPS * SSD_STATE], axis=-1)
    dt = jax.nn.softplus(dt_raw.astype(jnp.float32) + dt_bias.astype(jnp.float32))
    y_a = ssd_scan(xs, dt, b_in, c_in, a_log, d_skip)
    y_a = rms_norm(y_a * jax.nn.silu(z_a), ssd_norm_w)
    y_b = chunked_sgu(u_b, v_b, sgu_ln_w, sgu_ln_b, sgu_w, sgu_b) * jax.nn.silu(g_b)
    return jnp.concatenate([y_a, y_b], axis=-1) @ w_out


def odd_layer(h, w_in, w_out):
    bsz, s, _ = h.shape
    q, k, v, g = jnp.split(h @ w_in, 4, axis=-1)
    shp = (bsz, s, SB_HEADS, SB_HEAD_DIM)
    y = stick_breaking_attention(q.reshape(shp), k.reshape(shp), v.reshape(shp))
    return (y.reshape(bsz, s, D_INNER_C) * jax.nn.silu(g)) @ w_out


def _fwd_setup_inputs(seed: int = 0) -> dict:
    key = jax.random.key(seed)
    ks = jax.random.split(key, 20)
    f32 = jnp.float32
    nrm = lambda k, shp, sc: jax.random.normal(k, shp, f32) * sc
    dt = jnp.exp(jax.random.uniform(ks[5], (N_EVEN, SSD_HEADS), f32)
                 * (math.log(0.1) - math.log(0.001)) + math.log(0.001))
    return {
        "x": jax.random.normal(ks[0], (BATCH, SEQ, D_MODEL), f32),
        "norm_w": 1.0 + nrm(ks[1], (DEPTH, D_MODEL), 0.05),
        "final_norm_w": 1.0 + nrm(ks[2], (D_MODEL,), 0.05),
        "ev_w_in": nrm(ks[3], (N_EVEN, D_MODEL, EVEN_IN), D_MODEL ** -0.5),
        "ev_conv_w": nrm(ks[4], (N_EVEN, CONV_WIDTH, CONV_DIM), CONV_WIDTH ** -0.5),
        "ev_conv_b": nrm(ks[6], (N_EVEN, CONV_DIM), 0.02),
        "ev_dt_bias": dt + jnp.log(-jnp.expm1(-dt)),
        "ev_a_log": jnp.log(jax.random.uniform(ks[7], (N_EVEN, SSD_HEADS), f32, 1.0, 16.0)),
        "ev_d_skip": 1.0 + nrm(ks[8], (N_EVEN, SSD_HEADS), 0.1),
        "ev_ssd_norm_w": 1.0 + nrm(ks[9], (N_EVEN, D_INNER_A), 0.05),
        "ev_sgu_ln_w": 1.0 + nrm(ks[10], (N_EVEN, D_INNER_B), 0.05),
        "ev_sgu_ln_b": nrm(ks[11], (N_EVEN, D_INNER_B), 0.02),
        "ev_sgu_w": nrm(ks[12], (N_EVEN, SGU_GROUPS, SGU_CHUNK, SGU_CHUNK), SGU_CHUNK ** -0.5),
        "ev_sgu_b": 1.0 + nrm(ks[13], (N_EVEN, SGU_GROUPS, SGU_CHUNK), 0.1),
        "ev_w_out": nrm(ks[14], (N_EVEN, D_INNER_A + D_INNER_B, D_MODEL), (D_INNER_A + D_INNER_B) ** -0.5),
        "od_w_in": nrm(ks[15], (N_ODD, D_MODEL, ODD_IN), D_MODEL ** -0.5),
        "od_w_out": nrm(ks[16], (N_ODD, D_INNER_C, D_MODEL), D_INNER_C ** -0.5),
    }


def _fwd_reference(x, norm_w, final_norm_w, ev_w_in, ev_conv_w, ev_conv_b, ev_dt_bias,
              ev_a_log, ev_d_skip, ev_ssd_norm_w, ev_sgu_ln_w, ev_sgu_ln_b,
              ev_sgu_w, ev_sgu_b, ev_w_out, od_w_in, od_w_out):
    h = x
    for layer in range(DEPTH):
        hn = rms_norm(h, norm_w[layer])
        i = layer // 2
        if layer % 2 == 0:
            delta = even_layer(hn, ev_w_in[i], ev_conv_w[i], ev_conv_b[i], ev_dt_bias[i],
                               ev_a_log[i], ev_d_skip[i], ev_ssd_norm_w[i], ev_sgu_ln_w[i],
                               ev_sgu_ln_b[i], ev_sgu_w[i], ev_sgu_b[i], ev_w_out[i])
        else:
            delta = odd_layer(hn, od_w_in[i], od_w_out[i])
        h = h + delta
    return rms_norm(h, final_norm_w)


import jax as _jax
import jax.numpy as _jnp

TWIN_FORMAT = 'train_step'
FWD_PARAMS = ['x', 'norm_w', 'final_norm_w', 'ev_w_in', 'ev_conv_w', 'ev_conv_b', 'ev_dt_bias', 'ev_a_log', 'ev_d_skip', 'ev_ssd_norm_w', 'ev_sgu_ln_w', 'ev_sgu_ln_b', 'ev_sgu_w', 'ev_sgu_b', 'ev_w_out', 'od_w_in', 'od_w_out']
TWIN_WEIGHTS = ['norm_w', 'final_norm_w', 'ev_w_in', 'ev_conv_w', 'ev_conv_b', 'ev_dt_bias', 'ev_a_log', 'ev_d_skip', 'ev_ssd_norm_w', 'ev_sgu_ln_w', 'ev_sgu_ln_b', 'ev_sgu_w', 'ev_sgu_b', 'ev_w_out', 'od_w_in', 'od_w_out']
TWIN_DIFF_INPUT = 'x'
TWIN_INPUTS = ['x', 'norm_w', 'final_norm_w', 'ev_w_in', 'ev_conv_w', 'ev_conv_b', 'ev_dt_bias', 'ev_a_log', 'ev_d_skip', 'ev_ssd_norm_w', 'ev_sgu_ln_w', 'ev_sgu_ln_b', 'ev_sgu_w', 'ev_sgu_b', 'ev_w_out', 'od_w_in', 'od_w_out', 'loss_target', 'm_norm_w', 'm_final_norm_w', 'm_ev_w_in', 'm_ev_conv_w', 'm_ev_conv_b', 'm_ev_dt_bias', 'm_ev_a_log', 'm_ev_d_skip', 'm_ev_ssd_norm_w', 'm_ev_sgu_ln_w', 'm_ev_sgu_ln_b', 'm_ev_sgu_w', 'm_ev_sgu_b', 'm_ev_w_out', 'm_od_w_in', 'm_od_w_out', 'v_norm_w', 'v_final_norm_w', 'v_ev_w_in', 'v_ev_conv_w', 'v_ev_conv_b', 'v_ev_dt_bias', 'v_ev_a_log', 'v_ev_d_skip', 'v_ev_ssd_norm_w', 'v_ev_sgu_ln_w', 'v_ev_sgu_ln_b', 'v_ev_sgu_w', 'v_ev_sgu_b', 'v_ev_w_out', 'v_od_w_in', 'v_od_w_out']
TWIN_OUTPUTS = ['loss', 'grad_x', 'grad_norm_w', 'grad_final_norm_w', 'grad_ev_w_in', 'grad_ev_conv_w', 'grad_ev_conv_b', 'grad_ev_dt_bias', 'grad_ev_a_log', 'grad_ev_d_skip', 'grad_ev_ssd_norm_w', 'grad_ev_sgu_ln_w', 'grad_ev_sgu_ln_b', 'grad_ev_sgu_w', 'grad_ev_sgu_b', 'grad_ev_w_out', 'grad_od_w_in', 'grad_od_w_out', 'delta_norm_w', 'delta_final_norm_w', 'delta_ev_w_in', 'delta_ev_conv_w', 'delta_ev_conv_b', 'delta_ev_dt_bias', 'delta_ev_a_log', 'delta_ev_d_skip', 'delta_ev_ssd_norm_w', 'delta_ev_sgu_ln_w', 'delta_ev_sgu_ln_b', 'delta_ev_sgu_w', 'delta_ev_sgu_b', 'delta_ev_w_out', 'delta_od_w_in', 'delta_od_w_out', 'new_m_norm_w', 'new_m_final_norm_w', 'new_m_ev_w_in', 'new_m_ev_conv_w', 'new_m_ev_conv_b', 'new_m_ev_dt_bias', 'new_m_ev_a_log', 'new_m_ev_d_skip', 'new_m_ev_ssd_norm_w', 'new_m_ev_sgu_ln_w', 'new_m_ev_sgu_ln_b', 'new_m_ev_sgu_w', 'new_m_ev_sgu_b', 'new_m_ev_w_out', 'new_m_od_w_in', 'new_m_od_w_out', 'new_v_norm_w', 'new_v_final_norm_w', 'new_v_ev_w_in', 'new_v_ev_conv_w', 'new_v_ev_conv_b', 'new_v_ev_dt_bias', 'new_v_ev_a_log', 'new_v_ev_d_skip', 'new_v_ev_ssd_norm_w', 'new_v_ev_sgu_ln_w', 'new_v_ev_sgu_ln_b', 'new_v_ev_sgu_w', 'new_v_ev_sgu_b', 'new_v_ev_w_out', 'new_v_od_w_in', 'new_v_od_w_out']
TWIN_LEAF_KINDS = {'loss': 'loss', 'grad_x': 'grad_x', 'grad_norm_w': 'grad_w', 'grad_final_norm_w': 'grad_w', 'grad_ev_w_in': 'grad_w', 'grad_ev_conv_w': 'grad_w', 'grad_ev_conv_b': 'grad_w', 'grad_ev_dt_bias': 'grad_w', 'grad_ev_a_log': 'grad_w', 'grad_ev_d_skip': 'grad_w', 'grad_ev_ssd_norm_w': 'grad_w', 'grad_ev_sgu_ln_w': 'grad_w', 'grad_ev_sgu_ln_b': 'grad_w', 'grad_ev_sgu_w': 'grad_w', 'grad_ev_sgu_b': 'grad_w', 'grad_ev_w_out': 'grad_w', 'grad_od_w_in': 'grad_w', 'grad_od_w_out': 'grad_w', 'delta_norm_w': 'delta_w', 'delta_final_norm_w': 'delta_w', 'delta_ev_w_in': 'delta_w', 'delta_ev_conv_w': 'delta_w', 'delta_ev_conv_b': 'delta_w', 'delta_ev_dt_bias': 'delta_w', 'delta_ev_a_log': 'delta_w', 'delta_ev_d_skip': 'delta_w', 'delta_ev_ssd_norm_w': 'delta_w', 'delta_ev_sgu_ln_w': 'delta_w', 'delta_ev_sgu_ln_b': 'delta_w', 'delta_ev_sgu_w': 'delta_w', 'delta_ev_sgu_b': 'delta_w', 'delta_ev_w_out': 'delta_w', 'delta_od_w_in': 'delta_w', 'delta_od_w_out': 'delta_w', 'new_m_norm_w': 'new_m', 'new_m_final_norm_w': 'new_m', 'new_m_ev_w_in': 'new_m', 'new_m_ev_conv_w': 'new_m', 'new_m_ev_conv_b': 'new_m', 'new_m_ev_dt_bias': 'new_m', 'new_m_ev_a_log': 'new_m', 'new_m_ev_d_skip': 'new_m', 'new_m_ev_ssd_norm_w': 'new_m', 'new_m_ev_sgu_ln_w': 'new_m', 'new_m_ev_sgu_ln_b': 'new_m', 'new_m_ev_sgu_w': 'new_m', 'new_m_ev_sgu_b': 'new_m', 'new_m_ev_w_out': 'new_m', 'new_m_od_w_in': 'new_m', 'new_m_od_w_out': 'new_m', 'new_v_norm_w': 'new_v', 'new_v_final_norm_w': 'new_v', 'new_v_ev_w_in': 'new_v', 'new_v_ev_conv_w': 'new_v', 'new_v_ev_conv_b': 'new_v', 'new_v_ev_dt_bias': 'new_v', 'new_v_ev_a_log': 'new_v', 'new_v_ev_d_skip': 'new_v', 'new_v_ev_ssd_norm_w': 'new_v', 'new_v_ev_sgu_ln_w': 'new_v', 'new_v_ev_sgu_ln_b': 'new_v', 'new_v_ev_sgu_w': 'new_v', 'new_v_ev_sgu_b': 'new_v', 'new_v_ev_w_out': 'new_v', 'new_v_od_w_in': 'new_v', 'new_v_od_w_out': 'new_v'}


def _forward(args):
    return _fwd_reference(*[args[k] for k in FWD_PARAMS])


def _output_shape():
    def fwd():
        inp = _fwd_setup_inputs(0)
        return _fwd_reference(*[inp[k] for k in FWD_PARAMS])
    out = _jax.eval_shape(fwd)
    return out.shape, out.dtype

N_MICROBATCH = 1
ADAM_LR = 0.001
ADAM_B1 = 0.9
ADAM_B2 = 0.999
ADAM_EPS = 1e-08
ADAM_WD = 0.01
ADAM_STEP = 10
PER_EXAMPLE_BATCH_AXIS = {'x': 0, 'loss_target': 0}
SHARED_INPUTS = []
_WEIGHT_DTYPES = {'norm_w': _jnp.float32, 'final_norm_w': _jnp.float32, 'ev_w_in': _jnp.float32, 'ev_conv_w': _jnp.float32, 'ev_conv_b': _jnp.float32, 'ev_dt_bias': _jnp.float32, 'ev_a_log': _jnp.float32, 'ev_d_skip': _jnp.float32, 'ev_ssd_norm_w': _jnp.float32, 'ev_sgu_ln_w': _jnp.float32, 'ev_sgu_ln_b': _jnp.float32, 'ev_sgu_w': _jnp.float32, 'ev_sgu_b': _jnp.float32, 'ev_w_out': _jnp.float32, 'od_w_in': _jnp.float32, 'od_w_out': _jnp.float32}
MOMENT_SCALE = {'norm_w': 7.280074e-02, 'final_norm_w': 1.601138e+01, 'ev_w_in': 4.000023e-02, 'ev_conv_w': 4.844708e-02, 'ev_conv_b': 6.840328e-02, 'ev_dt_bias': 1.373374e-01, 'ev_a_log': 5.270926e-01, 'ev_d_skip': 2.697045e-01, 'ev_ssd_norm_w': 5.836156e-02, 'ev_sgu_ln_w': 1.522268e-02, 'ev_sgu_ln_b': 1.533094e-02, 'ev_sgu_w': 1.497075e-02, 'ev_sgu_b': 2.138120e-02, 'ev_w_out': 6.267861e-02, 'od_w_in': 2.159422e-02, 'od_w_out': 2.756565e-02}


def _to_microbatches(a, axis):
    t = _jnp.moveaxis(a, axis, 0)
    t = t.reshape((N_MICROBATCH, t.shape[0] // N_MICROBATCH) + t.shape[1:])
    return _jnp.moveaxis(t, 1, axis + 1)


def setup_inputs(seed: int = 0) -> dict:
    inp = _fwd_setup_inputs(seed)
    key = _jax.random.fold_in(_jax.random.key(seed), 7919)
    shape, _ = _output_shape()
    out = dict(inp)
    out["loss_target"] = _jax.random.normal(_jax.random.fold_in(key, 0), shape, _jnp.float32)
    for i, name in enumerate(TWIN_WEIGHTS):
        w = inp[name].astype(_jnp.float32)
        if MOMENT_SCALE is None:
            s = _jnp.sqrt(_jnp.mean(_jnp.square(w)) + 1e-30)
        else:
            s = MOMENT_SCALE[name]
        km, kv = _jax.random.split(_jax.random.fold_in(key, i + 1))
        out[name] = w
        out["m_" + name] = s * _jax.random.normal(km, w.shape, _jnp.float32)
        out["v_" + name] = (s * s) * _jax.random.uniform(kv, w.shape, _jnp.float32, 0.5, 1.5)
    if N_MICROBATCH > 1:
        for name, axis in PER_EXAMPLE_BATCH_AXIS.items():
            out[name] = _to_microbatches(out[name], axis)
    return {'x': out['x'], 'norm_w': out['norm_w'], 'final_norm_w': out['final_norm_w'], 'ev_w_in': out['ev_w_in'], 'ev_conv_w': out['ev_conv_w'], 'ev_conv_b': out['ev_conv_b'], 'ev_dt_bias': out['ev_dt_bias'], 'ev_a_log': out['ev_a_log'], 'ev_d_skip': out['ev_d_skip'], 'ev_ssd_norm_w': out['ev_ssd_norm_w'], 'ev_sgu_ln_w': out['ev_sgu_ln_w'], 'ev_sgu_ln_b': out['ev_sgu_ln_b'], 'ev_sgu_w': out['ev_sgu_w'], 'ev_sgu_b': out['ev_sgu_b'], 'ev_w_out': out['ev_w_out'], 'od_w_in': out['od_w_in'], 'od_w_out': out['od_w_out'], 'loss_target': out['loss_target'], 'm_norm_w': out['m_norm_w'], 'm_final_norm_w': out['m_final_norm_w'], 'm_ev_w_in': out['m_ev_w_in'], 'm_ev_conv_w': out['m_ev_conv_w'], 'm_ev_conv_b': out['m_ev_conv_b'], 'm_ev_dt_bias': out['m_ev_dt_bias'], 'm_ev_a_log': out['m_ev_a_log'], 'm_ev_d_skip': out['m_ev_d_skip'], 'm_ev_ssd_norm_w': out['m_ev_ssd_norm_w'], 'm_ev_sgu_ln_w': out['m_ev_sgu_ln_w'], 'm_ev_sgu_ln_b': out['m_ev_sgu_ln_b'], 'm_ev_sgu_w': out['m_ev_sgu_w'], 'm_ev_sgu_b': out['m_ev_sgu_b'], 'm_ev_w_out': out['m_ev_w_out'], 'm_od_w_in': out['m_od_w_in'], 'm_od_w_out': out['m_od_w_out'], 'v_norm_w': out['v_norm_w'], 'v_final_norm_w': out['v_final_norm_w'], 'v_ev_w_in': out['v_ev_w_in'], 'v_ev_conv_w': out['v_ev_conv_w'], 'v_ev_conv_b': out['v_ev_conv_b'], 'v_ev_dt_bias': out['v_ev_dt_bias'], 'v_ev_a_log': out['v_ev_a_log'], 'v_ev_d_skip': out['v_ev_d_skip'], 'v_ev_ssd_norm_w': out['v_ev_ssd_norm_w'], 'v_ev_sgu_ln_w': out['v_ev_sgu_ln_w'], 'v_ev_sgu_ln_b': out['v_ev_sgu_ln_b'], 'v_ev_sgu_w': out['v_ev_sgu_w'], 'v_ev_sgu_b': out['v_ev_sgu_b'], 'v_ev_w_out': out['v_ev_w_out'], 'v_od_w_in': out['v_od_w_in'], 'v_od_w_out': out['v_od_w_out']}


def _loss(weights, diff, rest, loss_target):
    with _jax.named_scope("forward"):
        args = {**rest, TWIN_DIFF_INPUT: diff, **{k: w.astype(_WEIGHT_DTYPES[k]) for k, w in weights.items()}}
        y = _forward(args)
    with _jax.named_scope("loss_head"):
        err = _jnp.square(y.astype(_jnp.float32) - loss_target)
        return 0.5 * _jnp.sum(_jnp.mean(err, axis=-1)) if err.ndim else 0.5 * err


def _adamw(w, g, m, v):
    m = ADAM_B1 * m + (1.0 - ADAM_B1) * g
    v = ADAM_B2 * v + (1.0 - ADAM_B2) * _jnp.square(g)
    m_hat = m / (1.0 - ADAM_B1 ** ADAM_STEP)
    v_hat = v / (1.0 - ADAM_B2 ** ADAM_STEP)
    delta = -ADAM_LR * (m_hat / (_jnp.sqrt(v_hat) + ADAM_EPS) + ADAM_WD * w)
    return delta, m, v


def reference(x, norm_w, final_norm_w, ev_w_in, ev_conv_w, ev_conv_b, ev_dt_bias, ev_a_log, ev_d_skip, ev_ssd_norm_w, ev_sgu_ln_w, ev_sgu_ln_b, ev_sgu_w, ev_sgu_b, ev_w_out, od_w_in, od_w_out, loss_target, m_norm_w, m_final_norm_w, m_ev_w_in, m_ev_conv_w, m_ev_conv_b, m_ev_dt_bias, m_ev_a_log, m_ev_d_skip, m_ev_ssd_norm_w, m_ev_sgu_ln_w, m_ev_sgu_ln_b, m_ev_sgu_w, m_ev_sgu_b, m_ev_w_out, m_od_w_in, m_od_w_out, v_norm_w, v_final_norm_w, v_ev_w_in, v_ev_conv_w, v_ev_conv_b, v_ev_dt_bias, v_ev_a_log, v_ev_d_skip, v_ev_ssd_norm_w, v_ev_sgu_ln_w, v_ev_sgu_ln_b, v_ev_sgu_w, v_ev_sgu_b, v_ev_w_out, v_od_w_in, v_od_w_out):
    given = dict(x=x, norm_w=norm_w, final_norm_w=final_norm_w, ev_w_in=ev_w_in, ev_conv_w=ev_conv_w, ev_conv_b=ev_conv_b, ev_dt_bias=ev_dt_bias, ev_a_log=ev_a_log, ev_d_skip=ev_d_skip, ev_ssd_norm_w=ev_ssd_norm_w, ev_sgu_ln_w=ev_sgu_ln_w, ev_sgu_ln_b=ev_sgu_ln_b, ev_sgu_w=ev_sgu_w, ev_sgu_b=ev_sgu_b, ev_w_out=ev_w_out, od_w_in=od_w_in, od_w_out=od_w_out, loss_target=loss_target, m_norm_w=m_norm_w, m_final_norm_w=m_final_norm_w, m_ev_w_in=m_ev_w_in, m_ev_conv_w=m_ev_conv_w, m_ev_conv_b=m_ev_conv_b, m_ev_dt_bias=m_ev_dt_bias, m_ev_a_log=m_ev_a_log, m_ev_d_skip=m_ev_d_skip, m_ev_ssd_norm_w=m_ev_ssd_norm_w, m_ev_sgu_ln_w=m_ev_sgu_ln_w, m_ev_sgu_ln_b=m_ev_sgu_ln_b, m_ev_sgu_w=m_ev_sgu_w, m_ev_sgu_b=m_ev_sgu_b, m_ev_w_out=m_ev_w_out, m_od_w_in=m_od_w_in, m_od_w_out=m_od_w_out, v_norm_w=v_norm_w, v_final_norm_w=v_final_norm_w, v_ev_w_in=v_ev_w_in, v_ev_conv_w=v_ev_conv_w, v_ev_conv_b=v_ev_conv_b, v_ev_dt_bias=v_ev_dt_bias, v_ev_a_log=v_ev_a_log, v_ev_d_skip=v_ev_d_skip, v_ev_ssd_norm_w=v_ev_ssd_norm_w, v_ev_sgu_ln_w=v_ev_sgu_ln_w, v_ev_sgu_ln_b=v_ev_sgu_ln_b, v_ev_sgu_w=v_ev_sgu_w, v_ev_sgu_b=v_ev_sgu_b, v_ev_w_out=v_ev_w_out, v_od_w_in=v_od_w_in, v_od_w_out=v_od_w_out)
    weights = {n: given[n] for n in TWIN_WEIGHTS}
    shared = {n: given[n] for n in SHARED_INPUTS}
    per_example = {n: given[n] for n in ['x']}
    grad_fn = _jax.value_and_grad(_loss, argnums=(0, 1))

    def one_microbatch(ex, loss_target):
        ex = dict(ex)
        diff = ex.pop(TWIN_DIFF_INPUT)
        return grad_fn(weights, diff, {**shared, **ex}, loss_target)

    if N_MICROBATCH == 1:
        loss, (grad_w, grad_x) = one_microbatch(per_example, given["loss_target"])
    else:
        def body(carry, xs):
            loss_sum, grad_sum = carry
            l_k, (gw_k, gx_k) = one_microbatch(xs[0], xs[1])
            with _jax.named_scope("update"):
                return (loss_sum + l_k, _jax.tree.map(_jnp.add, grad_sum, gw_k)), gx_k

        init = (_jnp.zeros((), _jnp.float32), _jax.tree.map(_jnp.zeros_like, weights))
        (loss, grad_w), grad_x = _jax.lax.scan(body, init, (per_example, given["loss_target"]))
    with _jax.named_scope("update"):
        delta_w, new_m, new_v = {}, {}, {}
        for n in TWIN_WEIGHTS:
            delta_w[n], new_m[n], new_v[n] = _adamw(weights[n], grad_w[n], given["m_" + n], given["v_" + n])
    return (loss, grad_x, *[grad_w[n] for n in TWIN_WEIGHTS], *[delta_w[n] for n in TWIN_WEIGHTS],
            *[new_m[n] for n in TWIN_WEIGHTS], *[new_v[n] for n in TWIN_WEIGHTS])
```

```python
import jax
import jax.numpy as jnp
from jax import lax
from jax.experimental import pallas as pl
from jax.experimental.pallas import tpu as pltpu

F32, BF16 = jnp.float32, jnp.bfloat16

D_MODEL = 2048
SSD_HEADS = 32
SSD_HEAD_DIM = 64
SSD_GROUPS = 4
SSD_STATE = 128
CHUNK = 128
CONV_WIDTH = 4
CONV_DIM = D_MODEL + 2 * SSD_GROUPS * SSD_STATE
SGU_GROUPS = 16
SB_HEADS = 16
LANES = 128
N_PAIRS = SSD_HEADS // 2
PAIRS_PER_GROUP = N_PAIRS // SSD_GROUPS
NORM_EPS = 1e-5
N_DEV = 8

ADAM_LR, ADAM_B1, ADAM_B2, ADAM_EPS, ADAM_WD, ADAM_STEP = 0.001, 0.9, 0.999, 1e-08, 0.01, 10

VMEM_LIMIT_BYTES = 48 * 1024 * 1024

NN = ((1,), (0,))
NT = ((1,), (1,))
TN = ((0,), (0,))


def _cparams(*sem):
    return pltpu.CompilerParams(dimension_semantics=sem, vmem_limit_bytes=VMEM_LIMIT_BYTES)


def _dg(a, b, dims):
    return lax.dot_general(a, b, (dims, ((), ())), preferred_element_type=F32)


def _make_bdot(dims):
    @jax.custom_vjp
    def f(a, b):
        return _dg(a.astype(BF16), b.astype(BF16), dims)

    def fwd(a, b):
        return f(a, b), (a, b)

    def bwd(res, g):
        a, b = res
        a16, b16, g16 = a.astype(BF16), b.astype(BF16), g.astype(BF16)
        if dims == NN:
            da, db = _dg(g16, b16, NT), _dg(a16, g16, TN)
        elif dims == NT:
            da, db = _dg(g16, b16, NN), _dg(g16, a16, TN)
        else:
            da, db = _dg(b16, g16, NT), _dg(a16, g16, NN)
        return da.astype(a.dtype), db.astype(b.dtype)

    f.defvjp(fwd, bwd)
    return f


_bdot_nn, _bdot_nt, _bdot_tn = _make_bdot(NN), _make_bdot(NT), _make_bdot(TN)


def _hdot(a, b):
    return jnp.dot(a, b, precision=lax.Precision.HIGHEST, preferred_element_type=F32)


def _softplus(x):
    return jnp.maximum(x, 0.0) + jnp.log1p(jnp.exp(-jnp.abs(x)))


def _silu(x):
    return x * jax.nn.sigmoid(x)


def _gelu(x):
    return 0.5 * x * (1.0 + jnp.tanh(0.7978845608028654 * (x + 0.044715 * (x * x * x))))


def _rms(x, w):
    return x * lax.rsqrt(jnp.mean(x * x, axis=-1, keepdims=True) + NORM_EPS) * w


def _layer_norm(x, w, b):
    xc = x - jnp.mean(x, axis=-1, keepdims=True)
    return xc * lax.rsqrt(jnp.mean(xc * xc, axis=-1, keepdims=True) + NORM_EPS) * w + b


def _iota2(shape, axis):
    return lax.broadcasted_iota(jnp.int32, shape, axis)


def _split3(x):
    hi = x.astype(BF16)
    r = x - hi.astype(F32)
    mid = r.astype(BF16)
    lo = (r - mid.astype(F32)).astype(BF16)
    return jnp.concatenate([hi, mid, lo], axis=1)


def _row_spec(width, tm):
    return pl.BlockSpec((tm, width), lambda i: (i, 0))


def _full_spec(p):
    zeros = (0,) * p.ndim
    return pl.BlockSpec(p.shape, lambda i: zeros)


def _rows_fwd(name, fn, tiled, params, outs, tm):
    n_rows = tiled[0].shape[0]
    n_in = len(tiled) + len(params)

    def body(*refs):
        res = fn(*[r[...] for r in refs[:n_in]])
        for o_ref, o in zip(refs[n_in:], res):
            o_ref[...] = o.astype(o_ref.dtype)

    return pl.pallas_call(
        body, name=name, grid=(n_rows // tm,),
        in_specs=[_row_spec(a.shape[1], tm) for a in tiled] + [_full_spec(p) for p in params],
        out_specs=[_row_spec(w, tm) for w, _ in outs],
        out_shape=[jax.ShapeDtypeStruct((n_rows, w), d) for w, d in outs],
        compiler_params=_cparams("parallel"),
    )(*tiled, *params)


def _rows_bwd(name, fn, tiled, params, cots, grad_dtypes, tm, add=None):
    n_rows = tiled[0].shape[0]
    nt, npar, nc = len(tiled), len(params), len(cots)
    want = [k for k, d in enumerate(grad_dtypes) if d is not None]
    n_add = 0 if add is None else 1

    def body(*refs):
        ins = [r[...] for r in refs[:nt + npar]]
        c_refs = refs[nt + npar:nt + npar + nc]
        add_refs = refs[nt + npar + nc:nt + npar + nc + n_add]
        o_refs = refs[nt + npar + nc + n_add:]
        res, vjp = jax.vjp(fn, *ins)
        grads = vjp(tuple(c[...].astype(r.dtype) for c, r in zip(c_refs, res)))
        for pos, k in enumerate(want):
            gk = grads[k]
            if add is not None and add[0] == pos:
                gk = gk + add_refs[0][...]
            o_refs[pos][...] = gk.astype(o_refs[pos].dtype)
        p_refs = o_refs[len(want):]

        @pl.when(pl.program_id(0) == 0)
        def _():
            for r in p_refs:
                r[...] = jnp.zeros_like(r)

        for r, gp in zip(p_refs, grads[nt:]):
            r[...] += gp

    add_arrays = [] if add is None else [add[1]]
    out = pl.pallas_call(
        body, name=name, grid=(n_rows // tm,),
        in_specs=([_row_spec(a.shape[1], tm) for a in tiled] + [_full_spec(p) for p in params]
                  + [_row_spec(c.shape[1], tm) for c in cots] + [_row_spec(a.shape[1], tm) for a in add_arrays]),
        out_specs=([_row_spec(tiled[k].shape[1], tm) for k in want] + [_full_spec(p) for p in params]),
        out_shape=([jax.ShapeDtypeStruct(tiled[k].shape, grad_dtypes[k]) for k in want]
                   + [jax.ShapeDtypeStruct(p.shape, F32) for p in params]),
        compiler_params=_cparams("arbitrary"),
    )(*tiled, *params, *cots, *add_arrays)
    return out


def _matmul(name, a, b, dims, out_dtype, add=None, tm=512, tn=1024, tk=512):
    if dims == NN:
        (m, k), n = a.shape, b.shape[1]
    elif dims == NT:
        (m, k), n = a.shape, b.shape[0]
    else:
        (k, m), n = a.shape, b.shape[1]
    tm, tn, tk = min(tm, m), min(tn, n), min(tk, k)
    assert m % tm == 0 and n % tn == 0 and k % tk == 0, (name, m, n, k)
    nk = k // tk
    a_spec = (pl.BlockSpec((tk, tm), lambda i, j, kk: (kk, i)) if dims == TN
              else pl.BlockSpec((tm, tk), lambda i, j, kk: (i, kk)))
    b_spec = (pl.BlockSpec((tn, tk), lambda i, j, kk: (j, kk)) if dims == NT
              else pl.BlockSpec((tk, tn), lambda i, j, kk: (kk, j)))
    o_spec = pl.BlockSpec((tm, tn), lambda i, j, kk: (i, j))
    has_add = add is not None

    def body(*refs):
        a_ref, b_ref = refs[0], refs[1]
        o_ref, acc = refs[-2], refs[-1]
        kk = pl.program_id(2)

        @pl.when(kk == 0)
        def _():
            acc[...] = jnp.zeros_like(acc)

        acc[...] += _dg(a_ref[...].astype(BF16), b_ref[...].astype(BF16), dims)

        @pl.when(kk == nk - 1)
        def _():
            r = acc[...]
            if has_add:
                r = r + refs[2][...]
            o_ref[...] = r.astype(o_ref.dtype)

    return pl.pallas_call(
        body, name=name, grid=(m // tm, n // tn, nk),
        in_specs=[a_spec, b_spec] + ([o_spec] if has_add else []),
        out_specs=o_spec,
        out_shape=jax.ShapeDtypeStruct((m, n), out_dtype),
        scratch_shapes=[pltpu.VMEM((tm, tn), F32)],
        compiler_params=_cparams("parallel", "parallel", "arbitrary"),
    )(a, b, *([add] if has_add else []))


CONV_TM = 256
HALO = 8


def _shift_down(x, halo, j):
    if j == 0:
        return x, x[:HALO]
    xr = pltpu.roll(x, j, 0)
    hr = pltpu.roll(halo, j, 0)
    top = jnp.where(_iota2((HALO, x.shape[1]), 0) < j, hr, xr[:HALO])
    return xr, top


def _conv_fwd(name, x, w, b):
    t, c = x.shape
    tm = min(CONV_TM, t)
    hb = tm // HALO

    def body(x_ref, halo_ref, w_ref, b_ref, pre_ref, xs_ref, bm_ref, cm_ref):
        i = pl.program_id(0)
        xv = x_ref[...]
        halo = jnp.where(i > 0, halo_ref[...], 0.0)
        main = jnp.zeros_like(xv) + b_ref[...]
        top = jnp.zeros((HALO, c), F32) + b_ref[...]
        for kk in range(CONV_WIDTH):
            xr, tp = _shift_down(xv, halo, CONV_WIDTH - 1 - kk)
            main = main + w_ref[kk:kk + 1, :] * xr
            top = top + w_ref[kk:kk + 1, :] * tp
        pre = jnp.concatenate([top, main[HALO:]], axis=0)
        pre_ref[...] = pre
        act = _silu(pre)
        xs_ref[...] = act[:, :D_MODEL]
        bm_ref[...] = act[:, D_MODEL:D_MODEL + SSD_GROUPS * SSD_STATE]
        cm_ref[...] = act[:, D_MODEL + SSD_GROUPS * SSD_STATE:]

    gs = SSD_GROUPS * SSD_STATE
    return pl.pallas_call(
        body, name=name, grid=(t // tm,),
        in_specs=[_row_spec(c, tm),
                  pl.BlockSpec((HALO, c), lambda i: (jnp.maximum(i * hb - 1, 0), 0)),
                  _full_spec(w), _full_spec(b)],
        out_specs=[_row_spec(c, tm), _row_spec(D_MODEL, tm), _row_spec(gs, tm), _row_spec(gs, tm)],
        out_shape=[jax.ShapeDtypeStruct((t, c), F32), jax.ShapeDtypeStruct((t, D_MODEL), F32),
                   jax.ShapeDtypeStruct((t, gs), F32), jax.ShapeDtypeStruct((t, gs), F32)],
        compiler_params=_cparams("parallel"),
    )(x, x, w, b)


def _dsilu(pre, dact):
    s = jax.nn.sigmoid(pre)
    return dact * (s * (1.0 + pre * (1.0 - s)))


def _conv_bwd(name, x, pre, w, dxs, dbm, dcm):
    t, c = x.shape
    tm = min(CONV_TM, t)
    hb = tm // HALO
    last_hb = t // HALO - 1
    n_tiles = t // tm

    def body(x_ref, xh_ref, pre_ref, preh_ref, w_ref, dxs_ref, dbm_ref, dcm_ref,
             dxsh_ref, dbmh_ref, dcmh_ref, dx_ref, dw_ref, db_ref):
        i = pl.program_id(0)
        dact = jnp.concatenate([dxs_ref[...], dbm_ref[...], dcm_ref[...]], axis=1)
        dpre = _dsilu(pre_ref[...], dact)
        dact_h = jnp.concatenate([dxsh_ref[...], dbmh_ref[...], dcmh_ref[...]], axis=1)
        dpre_h = jnp.where(i < n_tiles - 1, _dsilu(preh_ref[...], dact_h), 0.0)
        xv = x_ref[...]
        xh = jnp.where(i > 0, xh_ref[...], 0.0)

        @pl.when(i == 0)
        def _():
            dw_ref[...] = jnp.zeros_like(dw_ref)
            db_ref[...] = jnp.zeros_like(db_ref)

        db_ref[...] += jnp.sum(dpre, axis=0, keepdims=True)
        dxm = jnp.zeros_like(xv)
        dxt = jnp.zeros((HALO, c), F32)
        row8 = _iota2((HALO, c), 0)
        for kk in range(CONV_WIDTH):
            j = CONV_WIDTH - 1 - kk
            wk = w_ref[kk:kk + 1, :]
            xr, tp = _shift_down(xv, xh, j)
            full = jnp.sum(dpre * xr, axis=0, keepdims=True)
            fix = jnp.sum(dpre[:HALO] * (tp - xr[:HALO]), axis=0, keepdims=True)
            dw_ref[kk:kk + 1, :] += full + fix
            if j == 0:
                dxm = dxm + wk * dpre
                dxt = dxt + wk * dpre[tm - HALO:]
            else:
                dr = pltpu.roll(dpre, tm - j, 0)
                hr = pltpu.roll(dpre_h, HALO - j, 0)
                dxm = dxm + wk * dr
                dxt = dxt + wk * jnp.where(row8 >= HALO - j, hr, dr[tm - HALO:])
        dx_ref[...] = jnp.concatenate([dxm[:tm - HALO], dxt], axis=0).astype(dx_ref.dtype)

    gs = SSD_GROUPS * SSD_STATE
    prev_halo = lambda i: (jnp.maximum(i * hb - 1, 0), 0)
    next_halo = lambda i: (jnp.minimum((i + 1) * hb, last_hb), 0)
    return pl.pallas_call(
        body, name=name, grid=(n_tiles,),
        in_specs=[_row_spec(c, tm), pl.BlockSpec((HALO, c), prev_halo),
                  _row_spec(c, tm), pl.BlockSpec((HALO, c), next_halo), _full_spec(w),
                  _row_spec(D_MODEL, tm), _row_spec(gs, tm), _row_spec(gs, tm),
                  pl.BlockSpec((HALO, D_MODEL), next_halo), pl.BlockSpec((HALO, gs), next_halo),
                  pl.BlockSpec((HALO, gs), next_halo)],
        out_specs=[_row_spec(c, tm), _full_spec(w), pl.BlockSpec((1, c), lambda i: (0, 0))],
        out_shape=[jax.ShapeDtypeStruct((t, c), BF16), jax.ShapeDtypeStruct(w.shape, F32),
                   jax.ShapeDtypeStruct((1, c), F32)],
        compiler_params=_cparams("arbitrary"),
    )(x, x, pre, pre, w, dxs, dbm, dcm, dxs, dbm, dcm)


def _ssd_stage1(dt_raw, dt_bias, a_log, d_skip8):
    l = dt_raw.shape[0]
    n_ch = SSD_HEADS * SSD_HEAD_DIM
    expand = (_iota2((LANES, n_ch), 1) // SSD_HEAD_DIM == _iota2((LANES, n_ch), 0)).astype(F32)
    expand_a = (_iota2((LANES, SSD_HEADS * LANES), 1) // LANES
                == _iota2((LANES, SSD_HEADS * LANES), 0)).astype(F32)
    tri = (_iota2((l, l), 1) <= _iota2((l, l), 0)).astype(F32)
    dt = _softplus(dt_raw + dt_bias)
    cs = _hdot(tri, dt * (-jnp.exp(a_log)))
    return _hdot(dt, expand), _hdot(cs, expand), _hdot(cs, expand_a), _hdot(d_skip8, expand)


def _ssd_pair(xs, dtf, csf, cs_last, a0, a1, bg, cg, prev, dskf8):
    l = xs.shape[0]
    xc = xs * dtf
    scores = _bdot_nt(cg, bg)
    causal = _iota2((l, l), 0) >= _iota2((l, l), 1)

    def decay(a):
        return jnp.where(causal, jnp.exp(jnp.where(causal, a - a.T, 0.0)), 0.0)

    first = _iota2((l, LANES), 1) < SSD_HEAD_DIM
    y_diag = (_bdot_nn(scores * decay(a0), jnp.where(first, xc, 0.0))
              + _bdot_nn(scores * decay(a1), jnp.where(first, 0.0, xc)))
    states = _bdot_tn(bg, xc * jnp.exp(cs_last - csf))
    new_state = jnp.exp(cs_last) * prev + states
    y_off = _bdot_nn(cg, prev) * jnp.exp(csf)
    y = y_diag + y_off + xs * jnp.mean(dskf8, axis=0, keepdims=True)
    return y, new_state


def _ssd_specs(nc_rev=None):
    def ch(c):
        return c if nc_rev is None else nc_rev - 1 - c

    blk = (CHUNK, LANES)
    return dict(
        pair=pl.BlockSpec(blk, lambda c, p: (ch(c), p)),
        group=pl.BlockSpec(blk, lambda c, p: (ch(c), p // PAIRS_PER_GROUP)),
        chunk=pl.BlockSpec(blk, lambda c, p: (ch(c), 0)),
        vec=pl.BlockSpec((1, LANES), lambda c, p: (0, 0)),
        state=pl.BlockSpec((1, 1, LANES, LANES), lambda c, p: (ch(c), p, 0, 0)),
    )


def _store_stage1(vals, dtf_s, csf_s, csa_s, dsk_s):
    dtf, csf, csa, dskf = vals
    for p in range(N_PAIRS):
        sl = slice(p * LANES, (p + 1) * LANES)
        dtf_s[p] = dtf[:, sl]
        csf_s[p] = csf[:, sl]
        dsk_s[p] = dskf[:, sl]
    for h in range(SSD_HEADS):
        csa_s[h] = csa[:, h * LANES:(h + 1) * LANES]


def _ssd_scratch():
    return [pltpu.VMEM((N_PAIRS, CHUNK, LANES), F32), pltpu.VMEM((N_PAIRS, CHUNK, LANES), F32),
            pltpu.VMEM((SSD_HEADS, CHUNK, LANES), F32), pltpu.VMEM((N_PAIRS, 8, LANES), F32)]


def _ssd_fwd(name, xs, bm, cm, dt_raw, dt_bias, a_log, d_skip):
    t = xs.shape[0]
    nc = t // CHUNK

    def body(xs_ref, b_ref, c_ref, dt_ref, bias_ref, alog_ref, dsk_ref, y_ref, prev_ref,
             state, dtf_s, csf_s, csa_s, dsk_s):
        c, p = pl.program_id(0), pl.program_id(1)

        @pl.when(p == 0)
        def _():
            d8 = jnp.broadcast_to(dsk_ref[...], (8, LANES))
            _store_stage1(_ssd_stage1(dt_ref[...], bias_ref[...], alog_ref[...], d8),
                          dtf_s, csf_s, csa_s, dsk_s)

        @pl.when(c == 0)
        def _():
            state[p] = jnp.zeros((LANES, LANES), F32)

        prev = state[p]
        prev_ref[0, 0] = prev
        y, new_state = _ssd_pair(xs_ref[...], dtf_s[p], csf_s[p], csf_s[p, CHUNK - 1:CHUNK, :],
                                 csa_s[2 * p], csa_s[2 * p + 1], b_ref[...], c_ref[...], prev, dsk_s[p])
        y_ref[...] = y
        state[p] = new_state

    sp = _ssd_specs()
    return pl.pallas_call(
        body, name=name, grid=(nc, N_PAIRS),
        in_specs=[sp["pair"], sp["group"], sp["group"], sp["chunk"], sp["vec"], sp["vec"], sp["vec"]],
        out_specs=[sp["pair"], sp["state"]],
        out_shape=[jax.ShapeDtypeStruct((t, D_MODEL), F32),
                   jax.ShapeDtypeStruct((nc, N_PAIRS, LANES, LANES), F32)],
        scratch_shapes=[pltpu.VMEM((N_PAIRS, LANES, LANES), F32)] + _ssd_scratch(),
        compiler_params=_cparams("arbitrary", "arbitrary"),
    )(xs, bm, cm, dt_raw, dt_bias, a_log, d_skip)


def _ssd_bwd(name, xs, bm, cm, dt_raw, dt_bias, a_log, d_skip, prev_states, dy):
    t = xs.shape[0]
    nc = t // CHUNK

    def body(xs_ref, b_ref, c_ref, dt_ref, bias_ref, alog_ref, dsk_ref, prev_ref, dy_ref,
             dxs_ref, db_ref, dc_ref, ddt_ref, dbias_ref, dalog_ref, ddsk_ref,
             dstate, dtf_s, csf_s, csa_s, dsk_s, g_dtf, g_csf, g_csa, g_dsk):
        c, p = pl.program_id(0), pl.program_id(1)

        @pl.when(p == 0)
        def _():
            d8 = jnp.broadcast_to(dsk_ref[...], (8, LANES))
            _store_stage1(_ssd_stage1(dt_ref[...], bias_ref[...], alog_ref[...], d8),
                          dtf_s, csf_s, csa_s, dsk_s)

        @pl.when(c == 0)
        def _():
            dstate[p] = jnp.zeros((LANES, LANES), F32)

        @pl.when((c == 0) & (p == 0))
        def _():
            dbias_ref[...] = jnp.zeros_like(dbias_ref)
            dalog_ref[...] = jnp.zeros_like(dalog_ref)
            ddsk_ref[...] = jnp.zeros_like(ddsk_ref)

        _, vjp = jax.vjp(_ssd_pair, xs_ref[...], dtf_s[p], csf_s[p], csf_s[p, CHUNK - 1:CHUNK, :],
                         csa_s[2 * p], csa_s[2 * p + 1], b_ref[...], c_ref[...], prev_ref[0, 0], dsk_s[p])
        dxs, ddtf, dcsf, dlast, da0, da1, dbg, dcg, dprev, ddsk8 = vjp((dy_ref[...], dstate[p]))
        dxs_ref[...] = dxs
        g_dtf[p] = ddtf
        g_csf[p] = dcsf
        g_csf[p, CHUNK - 1:CHUNK, :] += dlast
        g_csa[2 * p] = da0
        g_csa[2 * p + 1] = da1
        g_dsk[p] = ddsk8
        dstate[p] = dprev

        @pl.when(p % PAIRS_PER_GROUP == 0)
        def _():
            db_ref[...] = dbg
            dc_ref[...] = dcg

        @pl.when(p % PAIRS_PER_GROUP != 0)
        def _():
            db_ref[...] += dbg
            dc_ref[...] += dcg

        @pl.when(p == N_PAIRS - 1)
        def _():
            d8 = jnp.broadcast_to(dsk_ref[...], (8, LANES))
            _, vjp1 = jax.vjp(_ssd_stage1, dt_ref[...], bias_ref[...], alog_ref[...], d8)
            cot = (jnp.concatenate([g_dtf[q] for q in range(N_PAIRS)], axis=1),
                   jnp.concatenate([g_csf[q] for q in range(N_PAIRS)], axis=1),
                   jnp.concatenate([g_csa[h] for h in range(SSD_HEADS)], axis=1),
                   jnp.concatenate([g_dsk[q] for q in range(N_PAIRS)], axis=1))
            ddt, dbias, dalog, dd8 = vjp1(cot)
            ddt_ref[...] = ddt
            dbias_ref[...] += dbias
            dalog_ref[...] += dalog
            ddsk_ref[...] += jnp.sum(dd8, axis=0, keepdims=True)

    sp = _ssd_specs(nc)
    gs = SSD_GROUPS * SSD_STATE
    return pl.pallas_call(
        body, name=name, grid=(nc, N_PAIRS),
        in_specs=[sp["pair"], sp["group"], sp["group"], sp["chunk"], sp["vec"], sp["vec"], sp["vec"],
                  sp["state"], sp["pair"]],
        out_specs=[sp["pair"], sp["group"], sp["group"], sp["chunk"], sp["vec"], sp["vec"], sp["vec"]],
        out_shape=[jax.ShapeDtypeStruct((t, D_MODEL), F32), jax.ShapeDtypeStruct((t, gs), F32),
                   jax.ShapeDtypeStruct((t, gs), F32), jax.ShapeDtypeStruct((t, LANES), F32),
                   jax.ShapeDtypeStruct((1, LANES), F32), jax.ShapeDtypeStruct((1, LANES), F32),
                   jax.ShapeDtypeStruct((1, LANES), F32)],
        scratch_shapes=[pltpu.VMEM((N_PAIRS, LANES, LANES), F32)] + _ssd_scratch() + _ssd_scratch(),
        compiler_params=_cparams("arbitrary", "arbitrary"),
    )(xs, bm, cm, dt_raw, dt_bias, a_log, d_skip, prev_states, dy)


def _ssd_gate(y, z, w):
    return (_rms(y * _silu(z), w),)


def _sgu_norm(v, w, b):
    return (_layer_norm(_gelu(v), w, b),)


def _sgu_group(u, gate, vn, w, bcol):
    l = u.shape[0]
    wc = jnp.where(_iota2((l, l), 0) >= _iota2((l, l), 1), w, 0.0)
    return _gelu(u) * (_bdot_nn(wc, vn) + bcol) * _silu(gate)


def _sgu_specs():
    blk = pl.BlockSpec((CHUNK, LANES), lambda g, c: (c, g))
    wsp = pl.BlockSpec((1, CHUNK, CHUNK), lambda g, c: (g, 0, 0))
    bsp = pl.BlockSpec((1, CHUNK, 1), lambda g, c: (g, 0, 0))
    return blk, wsp, bsp


def _sgu_fwd(name, u, gate, vn, w, bcol):
    t = u.shape[0]
    blk, wsp, bsp = _sgu_specs()

    def body(u_ref, g_ref, vn_ref, w_ref, b_ref, y_ref):
        y_ref[...] = _sgu_group(u_ref[...], g_ref[...], vn_ref[...], w_ref[0], b_ref[0]).astype(y_ref.dtype)

    return pl.pallas_call(
        body, name=name, grid=(SGU_GROUPS, t // CHUNK),
        in_specs=[blk, blk, blk, wsp, bsp], out_specs=blk,
        out_shape=jax.ShapeDtypeStruct((t, D_MODEL), BF16),
        compiler_params=_cparams("parallel", "parallel"),
    )(u, gate, vn, w, bcol)


def _sgu_bwd(name, u, gate, vn, w, bcol, dy):
    t = u.shape[0]
    blk, wsp, bsp = _sgu_specs()

    def body(u_ref, g_ref, vn_ref, w_ref, b_ref, dy_ref, du_ref, dg_ref, dvn_ref, dw_ref, db_ref):
        _, vjp = jax.vjp(_sgu_group, u_ref[...], g_ref[...], vn_ref[...], w_ref[0], b_ref[0])
        du, dg, dvn, dw, db = vjp(dy_ref[...])
        du_ref[...] = du.astype(du_ref.dtype)
        dg_ref[...] = dg.astype(dg_ref.dtype)
        dvn_ref[...] = dvn

        @pl.when(pl.program_id(1) == 0)
        def _():
            dw_ref[...] = jnp.zeros_like(dw_ref)
            db_ref[...] = jnp.zeros_like(db_ref)

        dw_ref[0] += dw
        db_ref[0] += db

    return pl.pallas_call(
        body, name=name, grid=(SGU_GROUPS, t // CHUNK),
        in_specs=[blk, blk, blk, wsp, bsp, blk], out_specs=[blk, blk, blk, wsp, bsp],
        out_shape=[jax.ShapeDtypeStruct((t, D_MODEL), BF16), jax.ShapeDtypeStruct((t, D_MODEL), BF16),
                   jax.ShapeDtypeStruct((t, D_MODEL), F32), jax.ShapeDtypeStruct(w.shape, F32),
                   jax.ShapeDtypeStruct(bcol.shape, F32)],
        compiler_params=_cparams("arbitrary", "arbitrary"),
    )(u, gate, vn, w, bcol, dy)


SB_SCALE = LANES ** -0.5


def _sb_block(qb, kb, off, q_off):
    z = _dg(qb, kb, NT) * SB_SCALE
    shape = z.shape
    mask = (_iota2(shape, 1) + off) < (_iota2(shape, 0) + q_off)
    tl = jnp.log1p(jnp.exp(-jnp.abs(z)))
    lk = jnp.where(mask, -(jnp.maximum(z, 0.0) + tl), 0.0)
    ls = jnp.minimum(z, 0.0) - tl
    return lk, ls, mask


def _tri3(cmp):
    sq = (CHUNK, CHUNK)
    m = cmp(_iota2(sq, 0), _iota2(sq, 1)).astype(BF16)
    return jnp.concatenate([m, m, m], axis=0)


def _attn_fwd(name, q, k, v):
    t = q.shape[0]
    nq = t // CHUNK

    def body(q_ref, k_ref, v_ref, y_ref, tot_ref):
        i = pl.program_id(1)
        qb = q_ref[...]
        later3 = _tri3(lambda r, c: r > c)

        def step(n, carry):
            acc, after = carry
            off = pl.multiple_of((i - n) * CHUNK, CHUNK)
            kb = k_ref[pl.ds(off, CHUNK), :]
            vb = v_ref[pl.ds(off, CHUNK), :]
            lk, ls, mask = _sb_block(qb, kb, off, i * CHUNK)
            inside = _dg(_split3(lk), later3, NN)
            w = jnp.where(mask, jnp.exp(ls + inside + after), 0.0)
            acc = acc + _dg(w.astype(BF16), vb, NN)
            return acc, after + jnp.sum(lk, axis=1, keepdims=True)

        acc, tot = lax.fori_loop(0, i + 1, step, (jnp.zeros((CHUNK, LANES), F32), jnp.zeros((CHUNK, 1), F32)))
        y_ref[...] = acc
        tot_ref[...] = jnp.broadcast_to(tot, (CHUNK, LANES))

    qsp = pl.BlockSpec((CHUNK, LANES), lambda h, i: (i, h))
    kvsp = pl.BlockSpec((t, LANES), lambda h, i: (0, h))
    return pl.pallas_call(
        body, name=name, grid=(SB_HEADS, nq),
        in_specs=[qsp, kvsp, kvsp], out_specs=[qsp, qsp],
        out_shape=[jax.ShapeDtypeStruct((t, D_MODEL), F32), jax.ShapeDtypeStruct((t, D_MODEL), F32)],
        compiler_params=_cparams("parallel", "arbitrary"),
    )(q, k, v)


def _attn_bwd(name, q, k, v, tot, dy):
    t = q.shape[0]
    nq = t // CHUNK

    def body(q_ref, k_ref, v_ref, tot_ref, dy_ref, dq_ref, dk_ref, dv_ref, dk_acc, dv_acc):
        i = pl.program_id(1)

        @pl.when(i == 0)
        def _():
            dk_acc[...] = jnp.zeros_like(dk_acc)
            dv_acc[...] = jnp.zeros_like(dv_acc)

        qb = q_ref[...]
        dyb = dy_ref[...].astype(BF16)
        totb = tot_ref[...]
        upto3 = _tri3(lambda r, c: r <= c)
        before3 = _tri3(lambda r, c: r < c)

        def step(j, carry):
            dq, lk_seen, e_seen = carry
            off = pl.multiple_of(j * CHUNK, CHUNK)
            kb = k_ref[pl.ds(off, CHUNK), :]
            vb = v_ref[pl.ds(off, CHUNK), :]
            lk, ls, mask = _sb_block(qb, kb, off, i * CHUNK)
            after = (totb - _dg(_split3(lk), upto3, NN)) - lk_seen
            w = jnp.where(mask, jnp.exp(ls + after), 0.0)
            e = _dg(dyb, vb, NT) * w
            dlk = e_seen + _dg(_split3(e), before3, NN)
            sig = jnp.exp(ls)
            dz = (jnp.where(mask, e * (1.0 - sig) - dlk * sig, 0.0) * SB_SCALE).astype(BF16)
            dq = dq + _dg(dz, kb, NN)
            dk_acc[pl.ds(off, CHUNK), :] += _dg(dz, qb, TN)
            dv_acc[pl.ds(off, CHUNK), :] += _dg(w.astype(BF16), dyb, TN)
            return (dq, lk_seen + jnp.sum(lk, axis=1, keepdims=True),
                    e_seen + jnp.sum(e, axis=1, keepdims=True))

        zero_col = jnp.zeros((CHUNK, 1), F32)
        dq, _, _ = lax.fori_loop(0, i + 1, step, (jnp.zeros((CHUNK, LANES), F32), zero_col, zero_col))
        dq_ref[...] = dq.astype(dq_ref.dtype)

        @pl.when(i == nq - 1)
        def _():
            dk_ref[...] = dk_acc[...].astype(dk_ref.dtype)
            dv_ref[...] = dv_acc[...].astype(dv_ref.dtype)

    qsp = pl.BlockSpec((CHUNK, LANES), lambda h, i: (i, h))
    kvsp = pl.BlockSpec((t, LANES), lambda h, i: (0, h))
    return pl.pallas_call(
        body, name=name, grid=(SB_HEADS, nq),
        in_specs=[qsp, kvsp, kvsp, qsp, qsp], out_specs=[qsp, kvsp, kvsp],
        out_shape=[jax.ShapeDtypeStruct((t, D_MODEL), BF16)] * 3,
        scratch_shapes=[pltpu.VMEM((t, LANES), F32), pltpu.VMEM((t, LANES), F32)],
        compiler_params=_cparams("arbitrary", "arbitrary"),
    )(q, k, v, tot, dy)


def _attn_gate(y, g):
    return (y * _silu(g),)


def _loss_head(name, h, target, w, tm):
    t, d = h.shape

    def body(h_ref, t_ref, w_ref, dh_ref, dw_ref, loss_ref):
        tgt = t_ref[...]

        def f(hv, wv):
            e = _rms(hv, wv) - tgt
            return 0.5 * jnp.mean(e * e, axis=-1, keepdims=True)

        row_loss, vjp = jax.vjp(f, h_ref[...], w_ref[...])
        dh, dw = vjp(jnp.ones_like(row_loss))
        dh_ref[...] = dh

        @pl.when(pl.program_id(0) == 0)
        def _():
            dw_ref[...] = jnp.zeros_like(dw_ref)
            loss_ref[...] = jnp.zeros_like(loss_ref)

        dw_ref[...] += dw
        loss_ref[...] += jnp.sum(row_loss, axis=0, keepdims=True)

    return pl.pallas_call(
        body, name=name, grid=(t // tm,),
        in_specs=[_row_spec(d, tm), _row_spec(d, tm), _full_spec(w)],
        out_specs=[_row_spec(d, tm), _full_spec(w), pl.BlockSpec((1, 1), lambda i: (0, 0))],
        out_shape=[jax.ShapeDtypeStruct((t, d), F32), jax.ShapeDtypeStruct(w.shape, F32),
                   jax.ShapeDtypeStruct((1, 1), F32)],
        compiler_params=_cparams("arbitrary"),
    )(h, target, w)


def _pick_tile(rows, cap):
    if rows <= cap:
        return rows
    for tm in range(cap - cap % 16, 0, -16):
        if rows % tm == 0:
            return tm
    raise ValueError(rows)


def _adamw(name, w, g, m, v, tm):
    r, c = w.shape
    tm = _pick_tile(r, tm)

    def body(w_ref, g_ref, m_ref, v_ref, d_ref, nm_ref, nv_ref):
        g_ = g_ref[...]
        m_ = ADAM_B1 * m_ref[...] + (1.0 - ADAM_B1) * g_
        v_ = ADAM_B2 * v_ref[...] + (1.0 - ADAM_B2) * (g_ * g_)
        m_hat = m_ / (1.0 - ADAM_B1 ** ADAM_STEP)
        v_hat = v_ / (1.0 - ADAM_B2 ** ADAM_STEP)
        d_ref[...] = -ADAM_LR * (m_hat / (jnp.sqrt(v_hat) + ADAM_EPS) + ADAM_WD * w_ref[...])
        nm_ref[...] = m_
        nv_ref[...] = v_

    spec = _row_spec(c, tm)
    return pl.pallas_call(
        body, name=name, grid=(r // tm,), in_specs=[spec] * 4, out_specs=[spec] * 3,
        out_shape=[jax.ShapeDtypeStruct((r, c), F32)] * 3, compiler_params=_cparams("parallel"),
    )(w, g, m, v)


def _sum_parts(name, parts, tm):
    n, r, c = parts.shape
    tm = _pick_tile(r, tm)

    def body(p_ref, o_ref):
        s = p_ref[0].astype(F32)
        for d in range(1, n):
            s = s + p_ref[d].astype(F32)
        o_ref[...] = s

    return pl.pallas_call(
        body, name=name, grid=(r // tm,),
        in_specs=[pl.BlockSpec((n, tm, c), lambda i: (0, i, 0))], out_specs=_row_spec(c, tm),
        out_shape=jax.ShapeDtypeStruct((r, c), F32), compiler_params=_cparams("parallel"),
    )(parts)


def _peer(k):
    x, y, c = lax.axis_index("x"), lax.axis_index("y"), lax.axis_index("c")
    px, py, pc = x ^ ((k >> 2) & 1), y ^ ((k >> 1) & 1), c ^ (k & 1)
    return (px, py, pc), 4 * px + 2 * py + pc


def _exchange(name, x, gather):
    blk = x.shape if gather else x.shape[1:]

    def body(x_ref, out_ref, send_sems, recv_sems, local_sem):
        _, me = _peer(0)
        mine = x_ref if gather else x_ref.at[me]
        local = pltpu.make_async_copy(mine, out_ref.at[me], local_sem)
        local.start()
        sends = []
        for k in range(1, N_DEV):
            dev, idx = _peer(k)
            cp = pltpu.make_async_remote_copy(
                src_ref=x_ref if gather else x_ref.at[idx], dst_ref=out_ref.at[me],
                send_sem=send_sems.at[k - 1], recv_sem=recv_sems.at[k - 1],
                device_id=dev, device_id_type=pl.DeviceIdType.MESH)
            cp.start()
            sends.append(cp)
        for k in range(1, N_DEV):
            dev, idx = _peer(k)
            pltpu.make_async_remote_copy(
                src_ref=mine, dst_ref=out_ref.at[idx],
                send_sem=send_sems.at[k - 1], recv_sem=recv_sems.at[k - 1],
                device_id=dev, device_id_type=pl.DeviceIdType.MESH).wait_recv()
        for cp in sends:
            cp.wait_send()
        local.wait()

    return pl.pallas_call(
        body, name=name,
        in_specs=[pl.BlockSpec(memory_space=pl.ANY)], out_specs=pl.BlockSpec(memory_space=pl.ANY),
        out_shape=jax.ShapeDtypeStruct((N_DEV,) + tuple(blk), x.dtype),
        scratch_shapes=[pltpu.SemaphoreType.DMA((N_DEV - 1,)), pltpu.SemaphoreType.DMA((N_DEV - 1,)),
                        pltpu.SemaphoreType.DMA],
    )(x)


PACK_ALIGN = 8 * LANES
PACK_ROWS = 512


def _pack(arrays, dtype):
    pieces, total = [], 0
    for a in arrays:
        f = a.reshape(-1).astype(dtype)
        pad = (-f.shape[0]) % PACK_ALIGN
        pieces.append(jnp.pad(f, (0, pad)) if pad else f)
        total += f.shape[0] + pad
    tail = (-total) % (PACK_ROWS * LANES)
    if tail:
        pieces.append(jnp.zeros((tail,), dtype))
    return jnp.concatenate(pieces).reshape(-1, LANES)


def _unpack(packed, shapes, lead=()):
    flat = packed.reshape(lead + (-1,))
    out, off = [], 0
    for s in shapes:
        n = 1
        for d in s:
            n *= d
        out.append(flat[..., off:off + n].reshape(lead + tuple(s)))
        off += n + ((-n) % PACK_ALIGN)
    return out


def _pad_lanes(a):
    return jnp.pad(a, (0, LANES - a.shape[0])).reshape(1, LANES)


ROW_TM = 256
EVEN_SEGS = (("z", D_MODEL), ("xbc", CONV_DIM), ("dt", SSD_HEADS), ("g", D_MODEL), ("u", D_MODEL), ("v", D_MODEL))
ODD_SEGS = (("q", D_MODEL), ("k", D_MODEL), ("v", D_MODEL), ("g", D_MODEL))


def _split_cols(w, segs):
    out, off = {}, 0
    for nm, n in segs:
        out[nm] = w[:, off:off + n]
        off += n
    return out


def _rms_fn(h, w):
    return (_rms(h, w),)


def _even_fwd(tag, h, p):
    hn, = _rows_fwd(tag + "_norm", _rms_fn, [h], [p["norm_w"]], [(D_MODEL, BF16)], ROW_TM)
    proj = {nm: _matmul(f"{tag}_in_{nm}", hn, p["w_in"][nm], NN, F32) for nm, _ in EVEN_SEGS}
    pre, xs, bm, cm = _conv_fwd(tag + "_conv", proj["xbc"], p["conv_w"], p["conv_b"])
    y_ssd, states = _ssd_fwd(tag + "_ssd", xs, bm, cm, proj["dt"], p["dt_bias"], p["a_log"], p["d_skip"])
    ya, = _rows_fwd(tag + "_ssdgate", _ssd_gate, [y_ssd, proj["z"]], [p["ssd_norm_w"]], [(D_MODEL, BF16)], ROW_TM)
    vn, = _rows_fwd(tag + "_sgunorm", _sgu_norm, [proj["v"]], [p["sgu_ln_w"], p["sgu_ln_b"]], [(D_MODEL, F32)], ROW_TM)
    yb = _sgu_fwd(tag + "_sgu", proj["u"], proj["g"], vn, p["sgu_w"], p["sgu_b"])
    h1 = _matmul(tag + "_out_a", ya, p["w_out_a"], NN, F32, add=h)
    h2 = _matmul(tag + "_out_b", yb, p["w_out_b"], NN, F32, add=h1)
    saved = dict(h=h, hn=hn, proj=proj, pre=pre, xs=xs, bm=bm, cm=cm, y_ssd=y_ssd, states=states,
                 ya=ya, vn=vn, yb=yb)
    return h2, saved


def _even_bwd(tag, dh, s, p):
    g = {}
    dh16 = dh.astype(BF16)
    proj = s["proj"]
    dya = _matmul(tag + "_dya", dh16, p["w_out_a"], NT, F32)
    dyb = _matmul(tag + "_dyb", dh16, p["w_out_b"], NT, F32)
    g["w_out_a"] = _matmul(tag + "_dwout_a", s["ya"], dh16, TN, BF16)
    g["w_out_b"] = _matmul(tag + "_dwout_b", s["yb"], dh16, TN, BF16)
    du, dg, dvn, g["sgu_w"], g["sgu_b"] = _sgu_bwd(tag + "_sgu_b", proj["u"], proj["g"], s["vn"],
                                                   p["sgu_w"], p["sgu_b"], dyb)
    dv, g["sgu_ln_w"], g["sgu_ln_b"] = _rows_bwd(tag + "_sgunorm_b", _sgu_norm, [proj["v"]],
                                                 [p["sgu_ln_w"], p["sgu_ln_b"]], [dvn], [BF16], ROW_TM // 2)
    dy_ssd, dz, g["ssd_norm_w"] = _rows_bwd(tag + "_ssdgate_b", _ssd_gate, [s["y_ssd"], proj["z"]],
                                            [p["ssd_norm_w"]], [dya], [F32, BF16], ROW_TM // 2)
    dxs, dbm, dcm, ddt, g["dt_bias"], g["a_log"], g["d_skip"] = _ssd_bwd(
        tag + "_ssd_b", s["xs"], s["bm"], s["cm"], proj["dt"], p["dt_bias"], p["a_log"], p["d_skip"],
        s["states"], dy_ssd)
    dxbc, g["conv_w"], g["conv_b"] = _conv_bwd(tag + "_conv_b", proj["xbc"], s["pre"], p["conv_w"], dxs, dbm, dcm)
    dproj = dict(z=dz, xbc=dxbc, dt=ddt.astype(BF16), g=dg, u=du, v=dv)
    dhn = None
    g["w_in"] = {}
    for nm, _ in EVEN_SEGS:
        dhn = _matmul(f"{tag}_dhn_{nm}", dproj[nm], p["w_in"][nm], NT, F32, add=dhn)
        g["w_in"][nm] = _matmul(f"{tag}_dwin_{nm}", s["hn"], dproj[nm], TN, BF16)
    dh_in, g["norm_w"] = _rows_bwd(tag + "_norm_b", _rms_fn, [s["h"]], [p["norm_w"]], [dhn], [F32],
                                   ROW_TM // 2, add=(0, dh))
    return dh_in, g


def _odd_fwd(tag, h, p):
    hn, = _rows_fwd(tag + "_norm", _rms_fn, [h], [p["norm_w"]], [(D_MODEL, BF16)], ROW_TM)
    q = _matmul(tag + "_in_q", hn, p["w_in"]["q"], NN, BF16)
    k = _matmul(tag + "_in_k", hn, p["w_in"]["k"], NN, BF16)
    v = _matmul(tag + "_in_v", hn, p["w_in"]["v"], NN, BF16)
    gate = _matmul(tag + "_in_g", hn, p["w_in"]["g"], NN, F32)
    y, tot = _attn_fwd(tag + "_attn", q, k, v)
    yg, = _rows_fwd(tag + "_gate", _attn_gate, [y, gate], [], [(D_MODEL, BF16)], ROW_TM)
    h1 = _matmul(tag + "_out", yg, p["w_out"], NN, F32, add=h)
    return h1, dict(h=h, hn=hn, q=q, k=k, v=v, gate=gate, y=y, tot=tot, yg=yg)


def _odd_bwd(tag, dh, s, p):
    g = {}
    dh16 = dh.astype(BF16)
    dyg = _matmul(tag + "_dyg", dh16, p["w_out"], NT, F32)
    g["w_out"] = _matmul(tag + "_dwout", s["yg"], dh16, TN, BF16)
    dy, dgate = _rows_bwd(tag + "_gate_b", _attn_gate, [s["y"], s["gate"]], [], [dyg], [F32, BF16], ROW_TM)
    dq, dk, dv = _attn_bwd(tag + "_attn_b", s["q"], s["k"], s["v"], s["tot"], dy)
    dproj = dict(q=dq, k=dk, v=dv, g=dgate)
    dhn = None
    g["w_in"] = {}
    for nm, _ in ODD_SEGS:
        dhn = _matmul(f"{tag}_dhn_{nm}", dproj[nm], p["w_in"][nm], NT, F32, add=dhn)
        g["w_in"][nm] = _matmul(f"{tag}_dwin_{nm}", s["hn"], dproj[nm], TN, BF16)
    dh_in, g["norm_w"] = _rows_bwd(tag + "_norm_b", _rms_fn, [s["h"]], [p["norm_w"]], [dhn], [F32],
                                   ROW_TM // 2, add=(0, dh))
    return dh_in, g


BIG = ("ev_w_in", "ev_w_out", "od_w_in", "od_w_out")
SMALL = ("norm_w", "final_norm_w", "ev_conv_b", "ev_dt_bias", "ev_a_log", "ev_d_skip", "ev_ssd_norm_w",
         "ev_sgu_ln_w", "ev_sgu_ln_b", "ev_sgu_w", "ev_sgu_b")
WEIGHTS = ("norm_w", "final_norm_w", "ev_w_in", "ev_conv_w", "ev_conv_b", "ev_dt_bias", "ev_a_log", "ev_d_skip",
           "ev_ssd_norm_w", "ev_sgu_ln_w", "ev_sgu_ln_b", "ev_sgu_w", "ev_sgu_b", "ev_w_out", "od_w_in", "od_w_out")


def _step(w, m, v, x, loss_target):
    h = x[0]
    tgt = loss_target[0]
    n_even, n_odd = w["ev_w_in"].shape[0], w["od_w_in"].shape[0]
    depth = n_even + n_odd
    big_shapes = [w[n].shape for n in BIG]

    gathered = _exchange("gather_big", _pack([w[n] for n in BIG], BF16), gather=True)
    ev_in, ev_out, od_in, od_out = _unpack(gathered, big_shapes, lead=(N_DEV,))
    ev_in = jnp.moveaxis(ev_in, 0, 2).reshape(n_even, D_MODEL, -1)
    od_in = jnp.moveaxis(od_in, 0, 2).reshape(n_odd, D_MODEL, -1)
    ev_out = jnp.moveaxis(ev_out, 0, 1).reshape(n_even, 2 * D_MODEL, D_MODEL)
    od_out = jnp.moveaxis(od_out, 0, 1).reshape(n_odd, D_MODEL, D_MODEL)
    conv_all = _exchange("gather_conv", _pack([w["ev_conv_w"]], F32), gather=True)
    conv_w, = _unpack(conv_all, [w["ev_conv_w"].shape], lead=(N_DEV,))
    conv_w = jnp.moveaxis(conv_w, 0, 2).reshape(n_even, CONV_WIDTH, CONV_DIM)

    def even_params(i, layer):
        w_in = _split_cols(ev_in[i], EVEN_SEGS)
        w_in["dt"] = jnp.pad(w_in["dt"], ((0, 0), (0, LANES - SSD_HEADS)))
        return dict(norm_w=w["norm_w"][layer][None], w_in=w_in, conv_w=conv_w[i], conv_b=w["ev_conv_b"][i][None],
                    dt_bias=_pad_lanes(w["ev_dt_bias"][i]), a_log=_pad_lanes(w["ev_a_log"][i]),
                    d_skip=_pad_lanes(w["ev_d_skip"][i]), ssd_norm_w=w["ev_ssd_norm_w"][i][None],
                    sgu_ln_w=w["ev_sgu_ln_w"][i][None], sgu_ln_b=w["ev_sgu_ln_b"][i][None],
                    sgu_w=w["ev_sgu_w"][i], sgu_b=w["ev_sgu_b"][i][:, :, None],
                    w_out_a=ev_out[i][:D_MODEL], w_out_b=ev_out[i][D_MODEL:])

    def odd_params(i, layer):
        return dict(norm_w=w["norm_w"][layer][None], w_in=_split_cols(od_in[i], ODD_SEGS), w_out=od_out[i])

    params, saved = [], []
    for layer in range(depth):
        if layer % 2 == 0:
            p = even_params(layer // 2, layer)
            h, s = _even_fwd(f"l{layer}", h, p)
        else:
            p = odd_params(layer // 2, layer)
            h, s = _odd_fwd(f"l{layer}", h, p)
        params.append(p)
        saved.append(s)

    dh, d_final, loss_part = _loss_head("loss_head", h, tgt, w["final_norm_w"][None], ROW_TM // 2)
    loss = lax.psum(loss_part[0, 0], ("x", "y", "c"))

    lg = [None] * depth
    for layer in reversed(range(depth)):
        if layer % 2 == 0:
            dh, lg[layer] = _even_bwd(f"l{layer}", dh, saved[layer], params[layer])
        else:
            dh, lg[layer] = _odd_bwd(f"l{layer}", dh, saved[layer], params[layer])
    grad_x = dh[None]

    ev = [lg[l] for l in range(0, depth, 2)]
    od = [lg[l] for l in range(1, depth, 2)]

    def by_owner_cols(full):
        i, kk, n8 = full.shape
        return jnp.moveaxis(full.reshape(i, kk, N_DEV, n8 // N_DEV), 2, 0)

    def by_owner_rows(full):
        i, r8, n = full.shape
        return jnp.moveaxis(full.reshape(i, N_DEV, r8 // N_DEV, n), 1, 0)

    d_ev_in = jnp.stack([jnp.concatenate(
        [e["w_in"][nm][:, :n] for nm, n in EVEN_SEGS], axis=1) for e in ev])
    d_od_in = jnp.stack([jnp.concatenate([o["w_in"][nm] for nm, _ in ODD_SEGS], axis=1) for o in od])
    d_ev_out = jnp.stack([jnp.concatenate([e["w_out_a"], e["w_out_b"]], axis=0) for e in ev])
    d_od_out = jnp.stack([o["w_out"] for o in od])
    per_owner = [by_owner_cols(d_ev_in), by_owner_rows(d_ev_out), by_owner_cols(d_od_in), by_owner_rows(d_od_out)]
    send = jnp.stack([_pack([a[d] for a in per_owner], BF16) for d in range(N_DEV)])
    parts = _exchange("scatter_big", send, gather=False)
    big_sum = _sum_parts("sum_big", parts, 1024)
    big_grads = dict(zip(BIG, _unpack(big_sum, big_shapes)))

    small_g = {
        "norm_w": jnp.concatenate([lg[l]["norm_w"] for l in range(depth)], axis=0),
        "final_norm_w": d_final[0],
        "ev_conv_b": jnp.concatenate([e["conv_b"] for e in ev], axis=0),
        "ev_dt_bias": jnp.concatenate([e["dt_bias"][:, :SSD_HEADS] for e in ev], axis=0),
        "ev_a_log": jnp.concatenate([e["a_log"][:, :SSD_HEADS] for e in ev], axis=0),
        "ev_d_skip": jnp.concatenate([e["d_skip"][:, :SSD_HEADS] for e in ev], axis=0),
        "ev_ssd_norm_w": jnp.concatenate([e["ssd_norm_w"] for e in ev], axis=0),
        "ev_sgu_ln_w": jnp.concatenate([e["sgu_ln_w"] for e in ev], axis=0),
        "ev_sgu_ln_b": jnp.concatenate([e["sgu_ln_b"] for e in ev], axis=0),
        "ev_sgu_w": jnp.stack([e["sgu_w"] for e in ev]),
        "ev_sgu_b": jnp.stack([e["sgu_b"][:, :, 0] for e in ev]),
    }
    conv_g = jnp.stack([e["conv_w"] for e in ev])
    small_shapes = [w[n].shape for n in SMALL]
    small_parts = _exchange("gather_small", _pack([small_g[n] for n in SMALL] + [conv_g], F32), gather=True)
    small_sum = _sum_parts("sum_small", small_parts, 1024)
    *small_list, conv_full = _unpack(small_sum, small_shapes + [conv_g.shape])
    grads = dict(zip(SMALL, small_list))
    grads.update(big_grads)
    me = 4 * lax.axis_index("x") + 2 * lax.axis_index("y") + lax.axis_index("c")
    n_cv = w["ev_conv_w"].shape[2]
    grads["ev_conv_w"] = lax.dynamic_slice_in_dim(conv_full, me * n_cv, n_cv, axis=2)

    deltas, new_m, new_v = {}, {}, {}
    for n in BIG + ("ev_conv_w",):
        shp = w[n].shape
        two_d = (-1, shp[-1])
        d_, m_, v_ = _adamw("adamw_" + n, w[n].reshape(two_d), grads[n].reshape(two_d), m[n].reshape(two_d),
                            v[n].reshape(two_d), 256)
        deltas[n], new_m[n], new_v[n] = d_.reshape(shp), m_.reshape(shp), v_.reshape(shp)
    packs = [_pack([src[n] for n in SMALL], F32) for src in (w, grads, m, v)]
    outs = _adamw("adamw_small", *packs, 1024)
    for dst, packed in zip((deltas, new_m, new_v), outs):
        dst.update(zip(SMALL, _unpack(packed, small_shapes)))
    return loss, grad_x, grads, deltas, new_m, new_v


def kernel(x, norm_w, final_norm_w, ev_w_in, ev_conv_w, ev_conv_b, ev_dt_bias, ev_a_log, ev_d_skip, ev_ssd_norm_w, ev_sgu_ln_w, ev_sgu_ln_b, ev_sgu_w, ev_sgu_b, ev_w_out, od_w_in, od_w_out, loss_target, m_norm_w, m_final_norm_w, m_ev_w_in, m_ev_conv_w, m_ev_conv_b, m_ev_dt_bias, m_ev_a_log, m_ev_d_skip, m_ev_ssd_norm_w, m_ev_sgu_ln_w, m_ev_sgu_ln_b, m_ev_sgu_w, m_ev_sgu_b, m_ev_w_out, m_od_w_in, m_od_w_out, v_norm_w, v_final_norm_w, v_ev_w_in, v_ev_conv_w, v_ev_conv_b, v_ev_dt_bias, v_ev_a_log, v_ev_d_skip, v_ev_ssd_norm_w, v_ev_sgu_ln_w, v_ev_sgu_ln_b, v_ev_sgu_w, v_ev_sgu_b, v_ev_w_out, v_od_w_in, v_od_w_out):
    w = dict(zip(WEIGHTS, (norm_w, final_norm_w, ev_w_in, ev_conv_w, ev_conv_b, ev_dt_bias, ev_a_log, ev_d_skip,
                           ev_ssd_norm_w, ev_sgu_ln_w, ev_sgu_ln_b, ev_sgu_w, ev_sgu_b, ev_w_out, od_w_in, od_w_out)))
    m = dict(zip(WEIGHTS, (m_norm_w, m_final_norm_w, m_ev_w_in, m_ev_conv_w, m_ev_conv_b, m_ev_dt_bias, m_ev_a_log,
                           m_ev_d_skip, m_ev_ssd_norm_w, m_ev_sgu_ln_w, m_ev_sgu_ln_b, m_ev_sgu_w, m_ev_sgu_b,
                           m_ev_w_out, m_od_w_in, m_od_w_out)))
    v = dict(zip(WEIGHTS, (v_norm_w, v_final_norm_w, v_ev_w_in, v_ev_conv_w, v_ev_conv_b, v_ev_dt_bias, v_ev_a_log,
                           v_ev_d_skip, v_ev_ssd_norm_w, v_ev_sgu_ln_w, v_ev_sgu_ln_b, v_ev_sgu_w, v_ev_sgu_b,
                           v_ev_w_out, v_od_w_in, v_od_w_out)))
    loss, grad_x, grads, deltas, new_m, new_v = _step(w, m, v, x, loss_target)
    return (loss, grad_x, *[grads[n] for n in WEIGHTS], *[deltas[n] for n in WEIGHTS],
            *[new_m[n] for n in WEIGHTS], *[new_v[n] for n in WEIGHTS])
```

```python
import jax
import jax.numpy as jnp
from jax import lax
from jax.experimental import pallas as pl
from jax.experimental.pallas import tpu as pltpu

F32, BF16 = jnp.float32, jnp.bfloat16

D_MODEL = 2048
SSD_HEADS = 32
SSD_HEAD_DIM = 64
SSD_GROUPS = 4
SSD_STATE = 128
CHUNK = 128
CONV_WIDTH = 4
CONV_DIM = D_MODEL + 2 * SSD_GROUPS * SSD_STATE
SGU_GROUPS = 16
SB_HEADS = 16
LANES = 128
N_PAIRS = SSD_HEADS // 2
PAIRS_PER_GROUP = N_PAIRS // SSD_GROUPS
NORM_EPS = 1e-5
N_DEV = 8

ADAM_LR, ADAM_B1, ADAM_B2, ADAM_EPS, ADAM_WD, ADAM_STEP = 0.001, 0.9, 0.999, 1e-08, 0.01, 10

VMEM_LIMIT_BYTES = 48 * 1024 * 1024

NN = ((1,), (0,))
NT = ((1,), (1,))
TN = ((0,), (0,))


def _cparams(*sem):
    return pltpu.CompilerParams(dimension_semantics=sem, vmem_limit_bytes=VMEM_LIMIT_BYTES)


def _dg(a, b, dims):
    return lax.dot_general(a, b, (dims, ((), ())), preferred_element_type=F32)


def _make_bdot(dims):
    @jax.custom_vjp
    def f(a, b):
        return _dg(a.astype(BF16), b.astype(BF16), dims)

    def fwd(a, b):
        return f(a, b), (a, b)

    def bwd(res, g):
        a, b = res
        a16, b16, g16 = a.astype(BF16), b.astype(BF16), g.astype(BF16)
        if dims == NN:
            da, db = _dg(g16, b16, NT), _dg(a16, g16, TN)
        elif dims == NT:
            da, db = _dg(g16, b16, NN), _dg(g16, a16, TN)
        else:
            da, db = _dg(b16, g16, NT), _dg(a16, g16, NN)
        return da.astype(a.dtype), db.astype(b.dtype)

    f.defvjp(fwd, bwd)
    return f


_bdot_nn, _bdot_nt, _bdot_tn = _make_bdot(NN), _make_bdot(NT), _make_bdot(TN)


def _hdot(a, b):
    return jnp.dot(a, b, precision=lax.Precision.HIGHEST, preferred_element_type=F32)


def _softplus(x):
    return jnp.maximum(x, 0.0) + jnp.log1p(jnp.exp(-jnp.abs(x)))


def _silu(x):
    return x * jax.nn.sigmoid(x)


def _gelu(x):
    return 0.5 * x * (1.0 + jnp.tanh(0.7978845608028654 * (x + 0.044715 * (x * x * x))))


def _rms(x, w):
    return x * lax.rsqrt(jnp.mean(x * x, axis=-1, keepdims=True) + NORM_EPS) * w


def _layer_norm(x, w, b):
    xc = x - jnp.mean(x, axis=-1, keepdims=True)
    return xc * lax.rsqrt(jnp.mean(xc * xc, axis=-1, keepdims=True) + NORM_EPS) * w + b


def _iota2(shape, axis):
    return lax.broadcasted_iota(jnp.int32, shape, axis)


def _split3(x):
    hi = x.astype(BF16)
    r = x - hi.astype(F32)
    mid = r.astype(BF16)
    lo = (r - mid.astype(F32)).astype(BF16)
    return jnp.concatenate([hi, mid, lo], axis=1)


def _row_spec(width, tm):
    return pl.BlockSpec((tm, width), lambda i: (i, 0))


def _full_spec(p):
    zeros = (0,) * p.ndim
    return pl.BlockSpec(p.shape, lambda i: zeros)


def _rows_fwd(name, fn, tiled, params, outs, tm):
    n_rows = tiled[0].shape[0]
    n_in = len(tiled) + len(params)

    def body(*refs):
        res = fn(*[r[...] for r in refs[:n_in]])
        for o_ref, o in zip(refs[n_in:], res):
            o_ref[...] = o.astype(o_ref.dtype)

    return pl.pallas_call(
        body, name=name, grid=(n_rows // tm,),
        in_specs=[_row_spec(a.shape[1], tm) for a in tiled] + [_full_spec(p) for p in params],
        out_specs=[_row_spec(w, tm) for w, _ in outs],
        out_shape=[jax.ShapeDtypeStruct((n_rows, w), d) for w, d in outs],
        compiler_params=_cparams("parallel"),
    )(*tiled, *params)


def _rows_bwd(name, fn, tiled, params, cots, grad_dtypes, tm, add=None):
    n_rows = tiled[0].shape[0]
    nt, npar, nc = len(tiled), len(params), len(cots)
    want = [k for k, d in enumerate(grad_dtypes) if d is not None]
    n_add = 0 if add is None else 1

    def body(*refs):
        ins = [r[...] for r in refs[:nt + npar]]
        c_refs = refs[nt + npar:nt + npar + nc]
        add_refs = refs[nt + npar + nc:nt + npar + nc + n_add]
        o_refs = refs[nt + npar + nc + n_add:]
        res, vjp = jax.vjp(fn, *ins)
        grads = vjp(tuple(c[...].astype(r.dtype) for c, r in zip(c_refs, res)))
        for pos, k in enumerate(want):
            gk = grads[k]
            if add is not None and add[0] == pos:
                gk = gk + add_refs[0][...]
            o_refs[pos][...] = gk.astype(o_refs[pos].dtype)
        p_refs = o_refs[len(want):]

        @pl.when(pl.program_id(0) == 0)
        def _():
            for r in p_refs:
                r[...] = jnp.zeros_like(r)

        for r, gp in zip(p_refs, grads[nt:]):
            r[...] += gp

    add_arrays = [] if add is None else [add[1]]
    out = pl.pallas_call(
        body, name=name, grid=(n_rows // tm,),
        in_specs=([_row_spec(a.shape[1], tm) for a in tiled] + [_full_spec(p) for p in params]
                  + [_row_spec(c.shape[1], tm) for c in cots] + [_row_spec(a.shape[1], tm) for a in add_arrays]),
        out_specs=([_row_spec(tiled[k].shape[1], tm) for k in want] + [_full_spec(p) for p in params]),
        out_shape=([jax.ShapeDtypeStruct(tiled[k].shape, grad_dtypes[k]) for k in want]
                   + [jax.ShapeDtypeStruct(p.shape, F32) for p in params]),
        compiler_params=_cparams("arbitrary"),
    )(*tiled, *params, *cots, *add_arrays)
    return out


def _matmul(name, a, b, dims, out_dtype, add=None, tm=512, tn=1024, tk=2048):
    if dims == NN:
        (m, k), n = a.shape, b.shape[1]
    elif dims == NT:
        (m, k), n = a.shape, b.shape[0]
    else:
        (k, m), n = a.shape, b.shape[1]
    tm, tn, tk = min(tm, m), min(tn, n), min(tk, k)
    while k % tk:
        tk -= LANES
    assert m % tm == 0 and n % tn == 0 and k % tk == 0, (name, m, n, k)
    nk = k // tk
    a_spec = (pl.BlockSpec((tk, tm), lambda i, j, kk: (kk, i)) if dims == TN
              else pl.BlockSpec((tm, tk), lambda i, j, kk: (i, kk)))
    b_spec = (pl.BlockSpec((tn, tk), lambda i, j, kk: (j, kk)) if dims == NT
              else pl.BlockSpec((tk, tn), lambda i, j, kk: (kk, j)))
    o_spec = pl.BlockSpec((tm, tn), lambda i, j, kk: (i, j))
    has_add = add is not None

    def body(*refs):
        a_ref, b_ref = refs[0], refs[1]
        part = _dg(a_ref[...].astype(BF16), b_ref[...].astype(BF16), dims)
        if nk == 1:
            o_ref = refs[-1]
            if has_add:
                part = part + refs[2][...]
            o_ref[...] = part.astype(o_ref.dtype)
            return
        o_ref, acc = refs[-2], refs[-1]
        kk = pl.program_id(2)

        @pl.when(kk == 0)
        def _():
            acc[...] = part

        @pl.when(kk > 0)
        def _():
            acc[...] += part

        @pl.when(kk == nk - 1)
        def _():
            r = acc[...]
            if has_add:
                r = r + refs[2][...]
            o_ref[...] = r.astype(o_ref.dtype)

    return pl.pallas_call(
        body, name=name, grid=(m // tm, n // tn, nk),
        in_specs=[a_spec, b_spec] + ([o_spec] if has_add else []),
        out_specs=o_spec,
        out_shape=jax.ShapeDtypeStruct((m, n), out_dtype),
        scratch_shapes=[pltpu.VMEM((tm, tn), F32)] if nk > 1 else [],
        compiler_params=_cparams("parallel", "parallel", "arbitrary"),
    )(a, b, *([add] if has_add else []))


CONV_TM = 256
HALO = 8


def _shift_down(x, halo, j):
    if j == 0:
        return x, x[:HALO]
    xr = pltpu.roll(x, j, 0)
    hr = pltpu.roll(halo, j, 0)
    top = jnp.where(_iota2((HALO, x.shape[1]), 0) < j, hr, xr[:HALO])
    return xr, top


def _conv_fwd(name, x, w, b):
    t, c = x.shape
    tm = min(CONV_TM, t)
    hb = tm // HALO

    def body(x_ref, halo_ref, w_ref, b_ref, pre_ref, xs_ref, bm_ref, cm_ref):
        i = pl.program_id(0)
        xv = x_ref[...]
        halo = jnp.where(i > 0, halo_ref[...], 0.0)
        main = jnp.zeros_like(xv) + b_ref[...]
        top = jnp.zeros((HALO, c), F32) + b_ref[...]
        for kk in range(CONV_WIDTH):
            xr, tp = _shift_down(xv, halo, CONV_WIDTH - 1 - kk)
            main = main + w_ref[kk:kk + 1, :] * xr
            top = top + w_ref[kk:kk + 1, :] * tp
        pre = jnp.concatenate([top, main[HALO:]], axis=0)
        pre_ref[...] = pre
        act = _silu(pre)
        xs_ref[...] = act[:, :D_MODEL]
        bm_ref[...] = act[:, D_MODEL:D_MODEL + SSD_GROUPS * SSD_STATE]
        cm_ref[...] = act[:, D_MODEL + SSD_GROUPS * SSD_STATE:]

    gs = SSD_GROUPS * SSD_STATE
    return pl.pallas_call(
        body, name=name, grid=(t // tm,),
        in_specs=[_row_spec(c, tm),
                  pl.BlockSpec((HALO, c), lambda i: (jnp.maximum(i * hb - 1, 0), 0)),
                  _full_spec(w), _full_spec(b)],
        out_specs=[_row_spec(c, tm), _row_spec(D_MODEL, tm), _row_spec(gs, tm), _row_spec(gs, tm)],
        out_shape=[jax.ShapeDtypeStruct((t, c), F32), jax.ShapeDtypeStruct((t, D_MODEL), F32),
                   jax.ShapeDtypeStruct((t, gs), F32), jax.ShapeDtypeStruct((t, gs), F32)],
        compiler_params=_cparams("parallel"),
    )(x, x, w, b)


def _dsilu(pre, dact):
    s = jax.nn.sigmoid(pre)
    return dact * (s * (1.0 + pre * (1.0 - s)))


def _conv_bwd(name, x, pre, w, dxs, dbm, dcm):
    t, c = x.shape
    tm = min(CONV_TM, t)
    hb = tm // HALO
    last_hb = t // HALO - 1
    n_tiles = t // tm

    def body(x_ref, xh_ref, pre_ref, preh_ref, w_ref, dxs_ref, dbm_ref, dcm_ref,
             dxsh_ref, dbmh_ref, dcmh_ref, dx_ref, dw_ref, db_ref):
        i = pl.program_id(0)
        dact = jnp.concatenate([dxs_ref[...], dbm_ref[...], dcm_ref[...]], axis=1)
        dpre = _dsilu(pre_ref[...], dact)
        dact_h = jnp.concatenate([dxsh_ref[...], dbmh_ref[...], dcmh_ref[...]], axis=1)
        dpre_h = jnp.where(i < n_tiles - 1, _dsilu(preh_ref[...], dact_h), 0.0)
        xv = x_ref[...]
        xh = jnp.where(i > 0, xh_ref[...], 0.0)

        @pl.when(i == 0)
        def _():
            dw_ref[...] = jnp.zeros_like(dw_ref)
            db_ref[...] = jnp.zeros_like(db_ref)

        db_ref[...] += jnp.sum(dpre, axis=0, keepdims=True)
        dxm = jnp.zeros_like(xv)
        dxt = jnp.zeros((HALO, c), F32)
        row8 = _iota2((HALO, c), 0)
        for kk in range(CONV_WIDTH):
            j = CONV_WIDTH - 1 - kk
            wk = w_ref[kk:kk + 1, :]
            xr, tp = _shift_down(xv, xh, j)
            full = jnp.sum(dpre * xr, axis=0, keepdims=True)
            fix = jnp.sum(dpre[:HALO] * (tp - xr[:HALO]), axis=0, keepdims=True)
            dw_ref[kk:kk + 1, :] += full + fix
            if j == 0:
                dxm = dxm + wk * dpre
                dxt = dxt + wk * dpre[tm - HALO:]
            else:
                dr = pltpu.roll(dpre, tm - j, 0)
                hr = pltpu.roll(dpre_h, HALO - j, 0)
                dxm = dxm + wk * dr
                dxt = dxt + wk * jnp.where(row8 >= HALO - j, hr, dr[tm - HALO:])
        dx_ref[...] = jnp.concatenate([dxm[:tm - HALO], dxt], axis=0).astype(dx_ref.dtype)

    gs = SSD_GROUPS * SSD_STATE
    prev_halo = lambda i: (jnp.maximum(i * hb - 1, 0), 0)
    next_halo = lambda i: (jnp.minimum((i + 1) * hb, last_hb), 0)
    return pl.pallas_call(
        body, name=name, grid=(n_tiles,),
        in_specs=[_row_spec(c, tm), pl.BlockSpec((HALO, c), prev_halo),
                  _row_spec(c, tm), pl.BlockSpec((HALO, c), next_halo), _full_spec(w),
                  _row_spec(D_MODEL, tm), _row_spec(gs, tm), _row_spec(gs, tm),
                  pl.BlockSpec((HALO, D_MODEL), next_halo), pl.BlockSpec((HALO, gs), next_halo),
                  pl.BlockSpec((HALO, gs), next_halo)],
        out_specs=[_row_spec(c, tm), _full_spec(w), pl.BlockSpec((1, c), lambda i: (0, 0))],
        out_shape=[jax.ShapeDtypeStruct((t, c), BF16), jax.ShapeDtypeStruct(w.shape, F32),
                   jax.ShapeDtypeStruct((1, c), F32)],
        compiler_params=_cparams("arbitrary"),
    )(x, x, pre, pre, w, dxs, dbm, dcm, dxs, dbm, dcm)


def _ssd_stage1(dt_raw, dt_bias, a_log, d_skip8):
    l = dt_raw.shape[0]
    n_ch = SSD_HEADS * SSD_HEAD_DIM
    expand = (_iota2((LANES, n_ch), 1) // SSD_HEAD_DIM == _iota2((LANES, n_ch), 0)).astype(F32)
    expand_a = (_iota2((LANES, SSD_HEADS * LANES), 1) // LANES
                == _iota2((LANES, SSD_HEADS * LANES), 0)).astype(F32)
    tri = (_iota2((l, l), 1) <= _iota2((l, l), 0)).astype(F32)
    dt = _softplus(dt_raw + dt_bias)
    cs = _hdot(tri, dt * (-jnp.exp(a_log)))
    return _hdot(dt, expand), _hdot(cs, expand), _hdot(cs, expand_a), _hdot(d_skip8, expand)


def _ssd_pair(xs, dtf, csf, cs_last, a0, a1, bg, cg, prev, dskf8):
    l = xs.shape[0]
    xc = xs * dtf
    scores = _bdot_nt(cg, bg)
    causal = _iota2((l, l), 0) >= _iota2((l, l), 1)

    def decay(a):
        return jnp.where(causal, jnp.exp(jnp.where(causal, a - a.T, 0.0)), 0.0)

    first = _iota2((l, LANES), 1) < SSD_HEAD_DIM
    y_diag = (_bdot_nn(scores * decay(a0), jnp.where(first, xc, 0.0))
              + _bdot_nn(scores * decay(a1), jnp.where(first, 0.0, xc)))
    states = _bdot_tn(bg, xc * jnp.exp(cs_last - csf))
    new_state = jnp.exp(cs_last) * prev + states
    y_off = _bdot_nn(cg, prev) * jnp.exp(csf)
    y = y_diag + y_off + xs * jnp.mean(dskf8, axis=0, keepdims=True)
    return y, new_state


def _ssd_specs(nc_rev=None):
    def ch(c):
        return c if nc_rev is None else nc_rev - 1 - c

    blk = (CHUNK, LANES)
    return dict(
        pair=pl.BlockSpec(blk, lambda c, p: (ch(c), p)),
        group=pl.BlockSpec(blk, lambda c, p: (ch(c), p // PAIRS_PER_GROUP)),
        chunk=pl.BlockSpec(blk, lambda c, p: (ch(c), 0)),
        vec=pl.BlockSpec((1, LANES), lambda c, p: (0, 0)),
        state=pl.BlockSpec((1, 1, LANES, LANES), lambda c, p: (ch(c), p, 0, 0)),
    )


def _store_stage1(vals, dtf_s, csf_s, csa_s, dsk_s):
    dtf, csf, csa, dskf = vals
    for p in range(N_PAIRS):
        sl = slice(p * LANES, (p + 1) * LANES)
        dtf_s[p] = dtf[:, sl]
        csf_s[p] = csf[:, sl]
        dsk_s[p] = dskf[:, sl]
    for h in range(SSD_HEADS):
        csa_s[h] = csa[:, h * LANES:(h + 1) * LANES]


def _ssd_scratch():
    return [pltpu.VMEM((N_PAIRS, CHUNK, LANES), F32), pltpu.VMEM((N_PAIRS, CHUNK, LANES), F32),
            pltpu.VMEM((SSD_HEADS, CHUNK, LANES), F32), pltpu.VMEM((N_PAIRS, 8, LANES), F32)]


def _ssd_fwd(name, xs, bm, cm, dt_raw, dt_bias, a_log, d_skip):
    t = xs.shape[0]
    nc = t // CHUNK

    def body(xs_ref, b_ref, c_ref, dt_ref, bias_ref, alog_ref, dsk_ref, y_ref, prev_ref,
             state, dtf_s, csf_s, csa_s, dsk_s):
        c, p = pl.program_id(0), pl.program_id(1)

        @pl.when(p == 0)
        def _():
            d8 = jnp.broadcast_to(dsk_ref[...], (8, LANES))
            _store_stage1(_ssd_stage1(dt_ref[...], bias_ref[...], alog_ref[...], d8),
                          dtf_s, csf_s, csa_s, dsk_s)

        @pl.when(c == 0)
        def _():
            state[p] = jnp.zeros((LANES, LANES), F32)

        prev = state[p]
        prev_ref[0, 0] = prev
        y, new_state = _ssd_pair(xs_ref[...], dtf_s[p], csf_s[p], csf_s[p, CHUNK - 1:CHUNK, :],
                                 csa_s[2 * p], csa_s[2 * p + 1], b_ref[...], c_ref[...], prev, dsk_s[p])
        y_ref[...] = y
        state[p] = new_state

    sp = _ssd_specs()
    return pl.pallas_call(
        body, name=name, grid=(nc, N_PAIRS),
        in_specs=[sp["pair"], sp["group"], sp["group"], sp["chunk"], sp["vec"], sp["vec"], sp["vec"]],
        out_specs=[sp["pair"], sp["state"]],
        out_shape=[jax.ShapeDtypeStruct((t, D_MODEL), F32),
                   jax.ShapeDtypeStruct((nc, N_PAIRS, LANES, LANES), F32)],
        scratch_shapes=[pltpu.VMEM((N_PAIRS, LANES, LANES), F32)] + _ssd_scratch(),
        compiler_params=_cparams("arbitrary", "arbitrary"),
    )(xs, bm, cm, dt_raw, dt_bias, a_log, d_skip)


def _ssd_bwd(name, xs, bm, cm, dt_raw, dt_bias, a_log, d_skip, prev_states, dy):
    t = xs.shape[0]
    nc = t // CHUNK

    def body(xs_ref, b_ref, c_ref, dt_ref, bias_ref, alog_ref, dsk_ref, prev_ref, dy_ref,
             dxs_ref, db_ref, dc_ref, ddt_ref, dbias_ref, dalog_ref, ddsk_ref,
             dstate, dtf_s, csf_s, csa_s, dsk_s, g_dtf, g_csf, g_csa, g_dsk):
        c, p = pl.program_id(0), pl.program_id(1)

        @pl.when(p == 0)
        def _():
            d8 = jnp.broadcast_to(dsk_ref[...], (8, LANES))
            _store_stage1(_ssd_stage1(dt_ref[...], bias_ref[...], alog_ref[...], d8),
                          dtf_s, csf_s, csa_s, dsk_s)

        @pl.when(c == 0)
        def _():
            dstate[p] = jnp.zeros((LANES, LANES), F32)

        @pl.when((c == 0) & (p == 0))
        def _():
            dbias_ref[...] = jnp.zeros_like(dbias_ref)
            dalog_ref[...] = jnp.zeros_like(dalog_ref)
            ddsk_ref[...] = jnp.zeros_like(ddsk_ref)

        _, vjp = jax.vjp(_ssd_pair, xs_ref[...], dtf_s[p], csf_s[p], csf_s[p, CHUNK - 1:CHUNK, :],
                         csa_s[2 * p], csa_s[2 * p + 1], b_ref[...], c_ref[...], prev_ref[0, 0], dsk_s[p])
        dxs, ddtf, dcsf, dlast, da0, da1, dbg, dcg, dprev, ddsk8 = vjp((dy_ref[...], dstate[p]))
        dxs_ref[...] = dxs
        g_dtf[p] = ddtf
        g_csf[p] = dcsf
        g_csf[p, CHUNK - 1:CHUNK, :] += dlast
        g_csa[2 * p] = da0
        g_csa[2 * p + 1] = da1
        g_dsk[p] = ddsk8
        dstate[p] = dprev

        @pl.when(p % PAIRS_PER_GROUP == 0)
        def _():
            db_ref[...] = dbg
            dc_ref[...] = dcg

        @pl.when(p % PAIRS_PER_GROUP != 0)
        def _():
            db_ref[...] += dbg
            dc_ref[...] += dcg

        @pl.when(p == N_PAIRS - 1)
        def _():
            d8 = jnp.broadcast_to(dsk_ref[...], (8, LANES))
            _, vjp1 = jax.vjp(_ssd_stage1, dt_ref[...], bias_ref[...], alog_ref[...], d8)
            cot = (jnp.concatenate([g_dtf[q] for q in range(N_PAIRS)], axis=1),
                   jnp.concatenate([g_csf[q] for q in range(N_PAIRS)], axis=1),
                   jnp.concatenate([g_csa[h] for h in range(SSD_HEADS)], axis=1),
                   jnp.concatenate([g_dsk[q] for q in range(N_PAIRS)], axis=1))
            ddt, dbias, dalog, dd8 = vjp1(cot)
            ddt_ref[...] = ddt
            dbias_ref[...] += dbias
            dalog_ref[...] += dalog
            ddsk_ref[...] += jnp.sum(dd8, axis=0, keepdims=True)

    sp = _ssd_specs(nc)
    gs = SSD_GROUPS * SSD_STATE
    return pl.pallas_call(
        body, name=name, grid=(nc, N_PAIRS),
        in_specs=[sp["pair"], sp["group"], sp["group"], sp["chunk"], sp["vec"], sp["vec"], sp["vec"],
                  sp["state"], sp["pair"]],
        out_specs=[sp["pair"], sp["group"], sp["group"], sp["chunk"], sp["vec"], sp["vec"], sp["vec"]],
        out_shape=[jax.ShapeDtypeStruct((t, D_MODEL), F32), jax.ShapeDtypeStruct((t, gs), F32),
                   jax.ShapeDtypeStruct((t, gs), F32), jax.ShapeDtypeStruct((t, LANES), F32),
                   jax.ShapeDtypeStruct((1, LANES), F32), jax.ShapeDtypeStruct((1, LANES), F32),
                   jax.ShapeDtypeStruct((1, LANES), F32)],
        scratch_shapes=[pltpu.VMEM((N_PAIRS, LANES, LANES), F32)] + _ssd_scratch() + _ssd_scratch(),
        compiler_params=_cparams("arbitrary", "arbitrary"),
    )(xs, bm, cm, dt_raw, dt_bias, a_log, d_skip, prev_states, dy)


def _ssd_gate(y, z, w):
    return (_rms(y * _silu(z), w),)


def _sgu_norm(v, w, b):
    return (_layer_norm(_gelu(v), w, b),)


def _sgu_group(u, gate, vn, w, bcol):
    l = u.shape[0]
    wc = jnp.where(_iota2((l, l), 0) >= _iota2((l, l), 1), w, 0.0)
    return _gelu(u) * (_bdot_nn(wc, vn) + bcol) * _silu(gate)


def _sgu_specs():
    blk = pl.BlockSpec((CHUNK, LANES), lambda g, c: (c, g))
    wsp = pl.BlockSpec((1, CHUNK, CHUNK), lambda g, c: (g, 0, 0))
    bsp = pl.BlockSpec((1, CHUNK, 1), lambda g, c: (g, 0, 0))
    return blk, wsp, bsp


def _sgu_fwd(name, u, gate, vn, w, bcol):
    t = u.shape[0]
    blk, wsp, bsp = _sgu_specs()

    def body(u_ref, g_ref, vn_ref, w_ref, b_ref, y_ref):
        y_ref[...] = _sgu_group(u_ref[...], g_ref[...], vn_ref[...], w_ref[0], b_ref[0]).astype(y_ref.dtype)

    return pl.pallas_call(
        body, name=name, grid=(SGU_GROUPS, t // CHUNK),
        in_specs=[blk, blk, blk, wsp, bsp], out_specs=blk,
        out_shape=jax.ShapeDtypeStruct((t, D_MODEL), BF16),
        compiler_params=_cparams("parallel", "parallel"),
    )(u, gate, vn, w, bcol)


def _sgu_bwd(name, u, gate, vn, w, bcol, dy):
    t = u.shape[0]
    blk, wsp, bsp = _sgu_specs()

    def body(u_ref, g_ref, vn_ref, w_ref, b_ref, dy_ref, du_ref, dg_ref, dvn_ref, dw_ref, db_ref):
        _, vjp = jax.vjp(_sgu_group, u_ref[...], g_ref[...], vn_ref[...], w_ref[0], b_ref[0])
        du, dg, dvn, dw, db = vjp(dy_ref[...])
        du_ref[...] = du.astype(du_ref.dtype)
        dg_ref[...] = dg.astype(dg_ref.dtype)
        dvn_ref[...] = dvn

        @pl.when(pl.program_id(1) == 0)
        def _():
            dw_ref[...] = jnp.zeros_like(dw_ref)
            db_ref[...] = jnp.zeros_like(db_ref)

        dw_ref[0] += dw
        db_ref[0] += db

    return pl.pallas_call(
        body, name=name, grid=(SGU_GROUPS, t // CHUNK),
        in_specs=[blk, blk, blk, wsp, bsp, blk], out_specs=[blk, blk, blk, wsp, bsp],
        out_shape=[jax.ShapeDtypeStruct((t, D_MODEL), BF16), jax.ShapeDtypeStruct((t, D_MODEL), BF16),
                   jax.ShapeDtypeStruct((t, D_MODEL), F32), jax.ShapeDtypeStruct(w.shape, F32),
                   jax.ShapeDtypeStruct(bcol.shape, F32)],
        compiler_params=_cparams("arbitrary", "arbitrary"),
    )(u, gate, vn, w, bcol, dy)


SB_SCALE = LANES ** -0.5


SB_TQ = 256
SB_TS = 512


def _sb_pieces(z, mask):
    tl = jnp.log1p(jnp.exp(-jnp.abs(z)))
    lk = jnp.where(mask, -(jnp.maximum(z, 0.0) + tl), 0.0)
    ls = jnp.minimum(z, 0.0) - tl
    return lk, ls


def _tri3(cmp):
    sq = (CHUNK, CHUNK)
    m = cmp(_iota2(sq, 0), _iota2(sq, 1)).astype(BF16)
    return jnp.concatenate([m, m, m], axis=0)


def _tri_sums(blocks, tri3):
    tq = blocks[0].shape[0]
    res = _dg(jnp.concatenate([_split3(b) for b in blocks], axis=0), tri3, NN)
    return [res[b * tq:(b + 1) * tq] for b in range(len(blocks))]


def _sb_tiles(t):
    tq, ts = min(SB_TQ, t), min(SB_TS, t)
    assert t % tq == 0 and t % ts == 0 and ts % CHUNK == 0
    return tq, ts, ts // CHUNK


def _sb_logits(qb, ks, off, q_off):
    tq = qb.shape[0]
    z = _dg(qb, ks, NT) * SB_SCALE
    row = _iota2((tq, CHUNK), 0) + q_off
    col = _iota2((tq, CHUNK), 1) + off
    out = []
    for b in range(ks.shape[0] // CHUNK):
        mask = (col + b * CHUNK) < row
        out.append(_sb_pieces(z[:, b * CHUNK:(b + 1) * CHUNK], mask) + (mask,))
    return out


def _attn_fwd(name, q, k, v):
    t = q.shape[0]
    tq, ts, nb = _sb_tiles(t)

    def body(q_ref, k_ref, v_ref, y_ref, tot_ref):
        i = pl.program_id(1)
        qb = q_ref[...]
        later3 = _tri3(lambda r, c: r > c)
        n_spans = (i * tq + tq - 1) // ts + 1

        def step(n, carry):
            acc, after = carry
            off = pl.multiple_of((n_spans - 1 - n) * ts, ts)
            pieces = _sb_logits(qb, k_ref[pl.ds(off, ts), :], off, i * tq)
            inside = _tri_sums([lk for lk, _, _ in pieces], later3)
            ws = [None] * nb
            for b in reversed(range(nb)):
                lk, ls, mask = pieces[b]
                ws[b] = jnp.where(mask, jnp.exp(ls + inside[b] + after), 0.0).astype(BF16)
                after = after + jnp.sum(lk, axis=1, keepdims=True)
            acc = acc + _dg(jnp.concatenate(ws, axis=1), v_ref[pl.ds(off, ts), :], NN)
            return acc, after

        acc, tot = lax.fori_loop(0, n_spans, step, (jnp.zeros((tq, LANES), F32), jnp.zeros((tq, 1), F32)))
        y_ref[...] = acc
        tot_ref[...] = jnp.broadcast_to(tot, (tq, LANES))

    qsp = pl.BlockSpec((tq, LANES), lambda h, i: (i, h))
    kvsp = pl.BlockSpec((t, LANES), lambda h, i: (0, h))
    return pl.pallas_call(
        body, name=name, grid=(SB_HEADS, t // tq),
        in_specs=[qsp, kvsp, kvsp], out_specs=[qsp, qsp],
        out_shape=[jax.ShapeDtypeStruct((t, D_MODEL), F32), jax.ShapeDtypeStruct((t, D_MODEL), F32)],
        compiler_params=_cparams("parallel", "arbitrary"),
    )(q, k, v)


def _attn_bwd(name, q, k, v, tot, dy):
    t = q.shape[0]
    tq, ts, nb = _sb_tiles(t)
    nq, ns = t // tq, t // ts

    def body(q_ref, k_ref, v_ref, tot_ref, dy_ref, dq_ref, dk_ref, dv_ref, dkt_acc, dvt_acc):
        i = pl.program_id(1)

        @pl.when(i == 0)
        def _():
            dkt_acc[...] = jnp.zeros_like(dkt_acc)
            dvt_acc[...] = jnp.zeros_like(dvt_acc)

        qb = q_ref[...]
        dy = dy_ref[...]
        dyb = dy.astype(BF16)
        q_t = qb.astype(F32).T.astype(BF16)
        dy_t = dy.T.astype(BF16)
        totb = tot_ref[...]
        upto3 = _tri3(lambda r, c: r <= c)
        before3 = _tri3(lambda r, c: r < c)
        n_spans = (i * tq + tq - 1) // ts + 1

        def step(j, carry):
            dq, lk_seen, e_seen = carry
            off = pl.multiple_of(j * ts, ts)
            ks = k_ref[pl.ds(off, ts), :]
            pieces = _sb_logits(qb, ks, off, i * tq)
            dw = _dg(dyb, v_ref[pl.ds(off, ts), :], NT)
            upto = _tri_sums([lk for lk, _, _ in pieces], upto3)
            ws, es = [], []
            for b in range(nb):
                lk, ls, mask = pieces[b]
                w = jnp.where(mask, jnp.exp(ls + ((totb - upto[b]) - lk_seen)), 0.0)
                ws.append(w.astype(BF16))
                es.append(dw[:, b * CHUNK:(b + 1) * CHUNK] * w)
                lk_seen = lk_seen + jnp.sum(lk, axis=1, keepdims=True)
            before = _tri_sums(es, before3)
            dzs = []
            for b in range(nb):
                _, ls, mask = pieces[b]
                sig = jnp.exp(ls)
                dlk = e_seen + before[b]
                dzs.append((jnp.where(mask, es[b] * (1.0 - sig) - dlk * sig, 0.0) * SB_SCALE).astype(BF16))
                e_seen = e_seen + jnp.sum(es[b], axis=1, keepdims=True)
            dz = jnp.concatenate(dzs, axis=1)
            dq = dq + _dg(dz, ks, NN)
            dkt_acc[j] += _dg(q_t, dz, NN)
            dvt_acc[j] += _dg(dy_t, jnp.concatenate(ws, axis=1), NN)
            return dq, lk_seen, e_seen

        zero_col = jnp.zeros((tq, 1), F32)
        dq, _, _ = lax.fori_loop(0, n_spans, step, (jnp.zeros((tq, LANES), F32), zero_col, zero_col))
        dq_ref[...] = dq.astype(dq_ref.dtype)

        @pl.when(i == nq - 1)
        def _():
            for s in range(ns):
                dk_ref[s * ts:(s + 1) * ts, :] = dkt_acc[s].T.astype(dk_ref.dtype)
                dv_ref[s * ts:(s + 1) * ts, :] = dvt_acc[s].T.astype(dv_ref.dtype)

    qsp = pl.BlockSpec((tq, LANES), lambda h, i: (i, h))
    kvsp = pl.BlockSpec((t, LANES), lambda h, i: (0, h))
    return pl.pallas_call(
        body, name=name, grid=(SB_HEADS, nq),
        in_specs=[qsp, kvsp, kvsp, qsp, qsp], out_specs=[qsp, kvsp, kvsp],
        out_shape=[jax.ShapeDtypeStruct((t, D_MODEL), BF16)] * 3,
        scratch_shapes=[pltpu.VMEM((ns, LANES, ts), F32), pltpu.VMEM((ns, LANES, ts), F32)],
        compiler_params=_cparams("arbitrary", "arbitrary"),
    )(q, k, v, tot, dy)


def _attn_gate(y, g):
    return (y * _silu(g),)


def _loss_head(name, h, target, w, tm):
    t, d = h.shape

    def body(h_ref, t_ref, w_ref, dh_ref, dw_ref, loss_ref):
        tgt = t_ref[...]

        def f(hv, wv):
            e = _rms(hv, wv) - tgt
            return 0.5 * jnp.mean(e * e, axis=-1, keepdims=True)

        row_loss, vjp = jax.vjp(f, h_ref[...], w_ref[...])
        dh, dw = vjp(jnp.ones_like(row_loss))
        dh_ref[...] = dh

        @pl.when(pl.program_id(0) == 0)
        def _():
            dw_ref[...] = jnp.zeros_like(dw_ref)
            loss_ref[...] = jnp.zeros_like(loss_ref)

        dw_ref[...] += dw
        loss_ref[...] += jnp.sum(row_loss, axis=0, keepdims=True)

    return pl.pallas_call(
        body, name=name, grid=(t // tm,),
        in_specs=[_row_spec(d, tm), _row_spec(d, tm), _full_spec(w)],
        out_specs=[_row_spec(d, tm), _full_spec(w), pl.BlockSpec((1, 1), lambda i: (0, 0))],
        out_shape=[jax.ShapeDtypeStruct((t, d), F32), jax.ShapeDtypeStruct(w.shape, F32),
                   jax.ShapeDtypeStruct((1, 1), F32)],
        compiler_params=_cparams("arbitrary"),
    )(h, target, w)


def _pick_tile(rows, cap):
    if rows <= cap:
        return rows
    for tm in range(cap - cap % 16, 0, -16):
        if rows % tm == 0:
            return tm
    raise ValueError(rows)


def _adamw(name, w, g, m, v, tm):
    r, c = w.shape
    tm = _pick_tile(r, tm)

    def body(w_ref, g_ref, m_ref, v_ref, d_ref, nm_ref, nv_ref):
        g_ = g_ref[...]
        m_ = ADAM_B1 * m_ref[...] + (1.0 - ADAM_B1) * g_
        v_ = ADAM_B2 * v_ref[...] + (1.0 - ADAM_B2) * (g_ * g_)
        m_hat = m_ / (1.0 - ADAM_B1 ** ADAM_STEP)
        v_hat = v_ / (1.0 - ADAM_B2 ** ADAM_STEP)
        d_ref[...] = -ADAM_LR * (m_hat / (jnp.sqrt(v_hat) + ADAM_EPS) + ADAM_WD * w_ref[...])
        nm_ref[...] = m_
        nv_ref[...] = v_

    spec = _row_spec(c, tm)
    return pl.pallas_call(
        body, name=name, grid=(r // tm,), in_specs=[spec] * 4, out_specs=[spec] * 3,
        out_shape=[jax.ShapeDtypeStruct((r, c), F32)] * 3, compiler_params=_cparams("parallel"),
    )(w, g, m, v)


def _sum_parts(name, parts, tm):
    n, r, c = parts.shape
    tm = _pick_tile(r, tm)

    def body(p_ref, o_ref):
        s = p_ref[0].astype(F32)
        for d in range(1, n):
            s = s + p_ref[d].astype(F32)
        o_ref[...] = s

    return pl.pallas_call(
        body, name=name, grid=(r // tm,),
        in_specs=[pl.BlockSpec((n, tm, c), lambda i: (0, i, 0))], out_specs=_row_spec(c, tm),
        out_shape=jax.ShapeDtypeStruct((r, c), F32), compiler_params=_cparams("parallel"),
    )(parts)


def _peer(k):
    x, y, c = lax.axis_index("x"), lax.axis_index("y"), lax.axis_index("c")
    px, py, pc = x ^ ((k >> 2) & 1), y ^ ((k >> 1) & 1), c ^ (k & 1)
    return (px, py, pc), 4 * px + 2 * py + pc


def _exchange(name, xs, gather):
    n = len(xs)

    def body(*refs):
        x_refs, out_refs = refs[:n], refs[n:2 * n]
        send_sems, recv_sems, local_sems = refs[2 * n:]
        _, me = _peer(0)
        mine = [x_refs[a] if gather else x_refs[a].at[me] for a in range(n)]
        local = [pltpu.make_async_copy(mine[a], out_refs[a].at[me], local_sems.at[a]) for a in range(n)]
        for cp in local:
            cp.start()
        sends = []
        for k in range(1, N_DEV):
            dev, idx = _peer(k)
            for a in range(n):
                cp = pltpu.make_async_remote_copy(
                    src_ref=x_refs[a] if gather else x_refs[a].at[idx], dst_ref=out_refs[a].at[me],
                    send_sem=send_sems.at[a, k - 1], recv_sem=recv_sems.at[a, k - 1],
                    device_id=dev, device_id_type=pl.DeviceIdType.MESH)
                cp.start()
                sends.append(cp)
        for k in range(1, N_DEV):
            dev, idx = _peer(k)
            for a in range(n):
                pltpu.make_async_remote_copy(
                    src_ref=mine[a], dst_ref=out_refs[a].at[idx],
                    send_sem=send_sems.at[a, k - 1], recv_sem=recv_sems.at[a, k - 1],
                    device_id=dev, device_id_type=pl.DeviceIdType.MESH).wait_recv()
        for cp in sends:
            cp.wait_send()
        for cp in local:
            cp.wait()

    any_spec = pl.BlockSpec(memory_space=pl.ANY)
    return pl.pallas_call(
        body, name=name, in_specs=[any_spec] * n, out_specs=[any_spec] * n,
        out_shape=[jax.ShapeDtypeStruct((N_DEV,) + tuple(x.shape if gather else x.shape[1:]), x.dtype) for x in xs],
        scratch_shapes=[pltpu.SemaphoreType.DMA((n, N_DEV - 1)), pltpu.SemaphoreType.DMA((n, N_DEV - 1)),
                        pltpu.SemaphoreType.DMA((n,))],
    )(*xs)


PACK_ALIGN = 8 * LANES
PACK_ROWS = 512


def _pack(arrays, dtype):
    pieces, total = [], 0
    for a in arrays:
        f = a.reshape(-1).astype(dtype)
        pad = (-f.shape[0]) % PACK_ALIGN
        pieces.append(jnp.pad(f, (0, pad)) if pad else f)
        total += f.shape[0] + pad
    tail = (-total) % (PACK_ROWS * LANES)
    if tail:
        pieces.append(jnp.zeros((tail,), dtype))
    return jnp.concatenate(pieces).reshape(-1, LANES)


def _unpack(packed, shapes, lead=()):
    flat = packed.reshape(lead + (-1,))
    out, off = [], 0
    for s in shapes:
        n = 1
        for d in s:
            n *= d
        out.append(flat[..., off:off + n].reshape(lead + tuple(s)))
        off += n + ((-n) % PACK_ALIGN)
    return out


def _pad_lanes(a):
    return jnp.pad(a, (0, LANES - a.shape[0])).reshape(1, LANES)


ROW_TM = 256
EVEN_SEGS = (("z", D_MODEL), ("xbc", CONV_DIM), ("dt", SSD_HEADS), ("g", D_MODEL), ("u", D_MODEL), ("v", D_MODEL))
ODD_SEGS = (("q", D_MODEL), ("k", D_MODEL), ("v", D_MODEL), ("g", D_MODEL))


def _split_cols(w, segs):
    out, off = {}, 0
    for nm, n in segs:
        out[nm] = w[:, off:off + n]
        off += n
    return out


def _rms_fn(h, w):
    return (_rms(h, w),)


def _even_fwd(tag, h, p):
    hn, = _rows_fwd(tag + "_norm", _rms_fn, [h], [p["norm_w"]], [(D_MODEL, BF16)], ROW_TM)
    proj = {nm: _matmul(f"{tag}_in_{nm}", hn, p["w_in"][nm], NN, F32) for nm, _ in EVEN_SEGS}
    pre, xs, bm, cm = _conv_fwd(tag + "_conv", proj["xbc"], p["conv_w"], p["conv_b"])
    y_ssd, states = _ssd_fwd(tag + "_ssd", xs, bm, cm, proj["dt"], p["dt_bias"], p["a_log"], p["d_skip"])
    ya, = _rows_fwd(tag + "_ssdgate", _ssd_gate, [y_ssd, proj["z"]], [p["ssd_norm_w"]], [(D_MODEL, BF16)], ROW_TM)
    vn, = _rows_fwd(tag + "_sgunorm", _sgu_norm, [proj["v"]], [p["sgu_ln_w"], p["sgu_ln_b"]], [(D_MODEL, F32)], ROW_TM)
    yb = _sgu_fwd(tag + "_sgu", proj["u"], proj["g"], vn, p["sgu_w"], p["sgu_b"])
    h1 = _matmul(tag + "_out_a", ya, p["w_out_a"], NN, F32, add=h)
    h2 = _matmul(tag + "_out_b", yb, p["w_out_b"], NN, F32, add=h1)
    saved = dict(h=h, hn=hn, proj=proj, pre=pre, xs=xs, bm=bm, cm=cm, y_ssd=y_ssd, states=states,
                 ya=ya, vn=vn, yb=yb)
    return h2, saved


def _even_bwd(tag, dh, s, p):
    g = {}
    dh16 = dh.astype(BF16)
    proj = s["proj"]
    dya = _matmul(tag + "_dya", dh16, p["w_out_a"], NT, F32)
    dyb = _matmul(tag + "_dyb", dh16, p["w_out_b"], NT, F32)
    g["w_out_a"] = _matmul(tag + "_dwout_a", s["ya"], dh16, TN, BF16)
    g["w_out_b"] = _matmul(tag + "_dwout_b", s["yb"], dh16, TN, BF16)
    du, dg, dvn, g["sgu_w"], g["sgu_b"] = _sgu_bwd(tag + "_sgu_b", proj["u"], proj["g"], s["vn"],
                                                   p["sgu_w"], p["sgu_b"], dyb)
    dv, g["sgu_ln_w"], g["sgu_ln_b"] = _rows_bwd(tag + "_sgunorm_b", _sgu_norm, [proj["v"]],
                                                 [p["sgu_ln_w"], p["sgu_ln_b"]], [dvn], [BF16], ROW_TM // 2)
    dy_ssd, dz, g["ssd_norm_w"] = _rows_bwd(tag + "_ssdgate_b", _ssd_gate, [s["y_ssd"], proj["z"]],
                                            [p["ssd_norm_w"]], [dya], [F32, BF16], ROW_TM // 2)
    dxs, dbm, dcm, ddt, g["dt_bias"], g["a_log"], g["d_skip"] = _ssd_bwd(
        tag + "_ssd_b", s["xs"], s["bm"], s["cm"], proj["dt"], p["dt_bias"], p["a_log"], p["d_skip"],
        s["states"], dy_ssd)
    dxbc, g["conv_w"], g["conv_b"] = _conv_bwd(tag + "_conv_b", proj["xbc"], s["pre"], p["conv_w"], dxs, dbm, dcm)
    dproj = dict(z=dz, xbc=dxbc, dt=ddt.astype(BF16), g=dg, u=du, v=dv)
    dhn = None
    g["w_in"] = {}
    for nm, _ in EVEN_SEGS:
        dhn = _matmul(f"{tag}_dhn_{nm}", dproj[nm], p["w_in"][nm], NT, F32, add=dhn)
        g["w_in"][nm] = _matmul(f"{tag}_dwin_{nm}", s["hn"], dproj[nm], TN, BF16)
    dh_in, g["norm_w"] = _rows_bwd(tag + "_norm_b", _rms_fn, [s["h"]], [p["norm_w"]], [dhn], [F32],
                                   ROW_TM // 2, add=(0, dh))
    return dh_in, g


def _odd_fwd(tag, h, p):
    hn, = _rows_fwd(tag + "_norm", _rms_fn, [h], [p["norm_w"]], [(D_MODEL, BF16)], ROW_TM)
    q = _matmul(tag + "_in_q", hn, p["w_in"]["q"], NN, BF16)
    k = _matmul(tag + "_in_k", hn, p["w_in"]["k"], NN, BF16)
    v = _matmul(tag + "_in_v", hn, p["w_in"]["v"], NN, BF16)
    gate = _matmul(tag + "_in_g", hn, p["w_in"]["g"], NN, F32)
    y, tot = _attn_fwd(tag + "_attn", q, k, v)
    yg, = _rows_fwd(tag + "_gate", _attn_gate, [y, gate], [], [(D_MODEL, BF16)], ROW_TM)
    h1 = _matmul(tag + "_out", yg, p["w_out"], NN, F32, add=h)
    return h1, dict(h=h, hn=hn, q=q, k=k, v=v, gate=gate, y=y, tot=tot, yg=yg)


def _odd_bwd(tag, dh, s, p):
    g = {}
    dh16 = dh.astype(BF16)
    dyg = _matmul(tag + "_dyg", dh16, p["w_out"], NT, F32)
    g["w_out"] = _matmul(tag + "_dwout", s["yg"], dh16, TN, BF16)
    dy, dgate = _rows_bwd(tag + "_gate_b", _attn_gate, [s["y"], s["gate"]], [], [dyg], [F32, BF16], ROW_TM)
    dq, dk, dv = _attn_bwd(tag + "_attn_b", s["q"], s["k"], s["v"], s["tot"], dy)
    dproj = dict(q=dq, k=dk, v=dv, g=dgate)
    dhn = None
    g["w_in"] = {}
    for nm, _ in ODD_SEGS:
        dhn = _matmul(f"{tag}_dhn_{nm}", dproj[nm], p["w_in"][nm], NT, F32, add=dhn)
        g["w_in"][nm] = _matmul(f"{tag}_dwin_{nm}", s["hn"], dproj[nm], TN, BF16)
    dh_in, g["norm_w"] = _rows_bwd(tag + "_norm_b", _rms_fn, [s["h"]], [p["norm_w"]], [dhn], [F32],
                                   ROW_TM // 2, add=(0, dh))
    return dh_in, g


BIG = ("ev_w_in", "ev_w_out", "od_w_in", "od_w_out")
SMALL = ("norm_w", "final_norm_w", "ev_conv_b", "ev_dt_bias", "ev_a_log", "ev_d_skip", "ev_ssd_norm_w",
         "ev_sgu_ln_w", "ev_sgu_ln_b", "ev_sgu_w", "ev_sgu_b")
WEIGHTS = ("norm_w", "final_norm_w", "ev_w_in", "ev_conv_w", "ev_conv_b", "ev_dt_bias", "ev_a_log", "ev_d_skip",
           "ev_ssd_norm_w", "ev_sgu_ln_w", "ev_sgu_ln_b", "ev_sgu_w", "ev_sgu_b", "ev_w_out", "od_w_in", "od_w_out")


def _step(w, m, v, x, loss_target):
    h = x[0]
    tgt = loss_target[0]
    n_even, n_odd = w["ev_w_in"].shape[0], w["od_w_in"].shape[0]
    depth = n_even + n_odd
    big_shapes = [w[n].shape for n in BIG]

    ev_in, ev_out, od_in, od_out, conv_w = _exchange(
        "gather_weights", [w[n].astype(BF16) for n in BIG] + [w["ev_conv_w"]], gather=True)
    ev_in = jnp.moveaxis(ev_in, 0, 2).reshape(n_even, D_MODEL, -1)
    od_in = jnp.moveaxis(od_in, 0, 2).reshape(n_odd, D_MODEL, -1)
    ev_out = jnp.moveaxis(ev_out, 0, 1).reshape(n_even, 2 * D_MODEL, D_MODEL)
    od_out = jnp.moveaxis(od_out, 0, 1).reshape(n_odd, D_MODEL, D_MODEL)
    conv_w = jnp.moveaxis(conv_w, 0, 2).reshape(n_even, CONV_WIDTH, CONV_DIM)

    def even_params(i, layer):
        w_in = _split_cols(ev_in[i], EVEN_SEGS)
        w_in["dt"] = jnp.pad(w_in["dt"], ((0, 0), (0, LANES - SSD_HEADS)))
        return dict(norm_w=w["norm_w"][layer][None], w_in=w_in, conv_w=conv_w[i], conv_b=w["ev_conv_b"][i][None],
                    dt_bias=_pad_lanes(w["ev_dt_bias"][i]), a_log=_pad_lanes(w["ev_a_log"][i]),
                    d_skip=_pad_lanes(w["ev_d_skip"][i]), ssd_norm_w=w["ev_ssd_norm_w"][i][None],
                    sgu_ln_w=w["ev_sgu_ln_w"][i][None], sgu_ln_b=w["ev_sgu_ln_b"][i][None],
                    sgu_w=w["ev_sgu_w"][i], sgu_b=w["ev_sgu_b"][i][:, :, None],
                    w_out_a=ev_out[i][:D_MODEL], w_out_b=ev_out[i][D_MODEL:])

    def odd_params(i, layer):
        return dict(norm_w=w["norm_w"][layer][None], w_in=_split_cols(od_in[i], ODD_SEGS), w_out=od_out[i])

    params, saved = [], []
    for layer in range(depth):
        if layer % 2 == 0:
            p = even_params(layer // 2, layer)
            h, s = _even_fwd(f"l{layer}", h, p)
        else:
            p = odd_params(layer // 2, layer)
            h, s = _odd_fwd(f"l{layer}", h, p)
        params.append(p)
        saved.append(s)

    dh, d_final, loss_part = _loss_head("loss_head", h, tgt, w["final_norm_w"][None], ROW_TM // 2)
    loss = lax.psum(loss_part[0, 0], ("x", "y", "c"))

    lg = [None] * depth
    for layer in reversed(range(depth)):
        if layer % 2 == 0:
            dh, lg[layer] = _even_bwd(f"l{layer}", dh, saved[layer], params[layer])
        else:
            dh, lg[layer] = _odd_bwd(f"l{layer}", dh, saved[layer], params[layer])
    grad_x = dh[None]

    ev = [lg[l] for l in range(0, depth, 2)]
    od = [lg[l] for l in range(1, depth, 2)]

    def by_owner_cols(full):
        i, kk, n8 = full.shape
        return jnp.moveaxis(full.reshape(i, kk, N_DEV, n8 // N_DEV), 2, 0)

    def by_owner_rows(full):
        i, r8, n = full.shape
        return jnp.moveaxis(full.reshape(i, N_DEV, r8 // N_DEV, n), 1, 0)

    d_ev_in = jnp.stack([jnp.concatenate(
        [e["w_in"][nm][:, :n] for nm, n in EVEN_SEGS], axis=1) for e in ev])
    d_od_in = jnp.stack([jnp.concatenate([o["w_in"][nm] for nm, _ in ODD_SEGS], axis=1) for o in od])
    d_ev_out = jnp.stack([jnp.concatenate([e["w_out_a"], e["w_out_b"]], axis=0) for e in ev])
    d_od_out = jnp.stack([o["w_out"] for o in od])
    per_owner = [by_owner_cols(d_ev_in), by_owner_rows(d_ev_out), by_owner_cols(d_od_in), by_owner_rows(d_od_out)]
    parts = _exchange("scatter_grads", per_owner, gather=False)
    big_grads = {}
    for n, part, shp in zip(BIG, parts, big_shapes):
        summed = _sum_parts("sum_" + n, part.reshape(N_DEV, -1, shp[-1]), 256)
        big_grads[n] = summed.reshape(shp)

    small_g = {
        "norm_w": jnp.concatenate([lg[l]["norm_w"] for l in range(depth)], axis=0),
        "final_norm_w": d_final[0],
        "ev_conv_b": jnp.concatenate([e["conv_b"] for e in ev], axis=0),
        "ev_dt_bias": jnp.concatenate([e["dt_bias"][:, :SSD_HEADS] for e in ev], axis=0),
        "ev_a_log": jnp.concatenate([e["a_log"][:, :SSD_HEADS] for e in ev], axis=0),
        "ev_d_skip": jnp.concatenate([e["d_skip"][:, :SSD_HEADS] for e in ev], axis=0),
        "ev_ssd_norm_w": jnp.concatenate([e["ssd_norm_w"] for e in ev], axis=0),
        "ev_sgu_ln_w": jnp.concatenate([e["sgu_ln_w"] for e in ev], axis=0),
        "ev_sgu_ln_b": jnp.concatenate([e["sgu_ln_b"] for e in ev], axis=0),
        "ev_sgu_w": jnp.stack([e["sgu_w"] for e in ev]),
        "ev_sgu_b": jnp.stack([e["sgu_b"][:, :, 0] for e in ev]),
    }
    conv_g = jnp.stack([e["conv_w"] for e in ev])
    small_shapes = [w[n].shape for n in SMALL]
    small_parts, = _exchange("gather_small", [_pack([small_g[n] for n in SMALL] + [conv_g], F32)], gather=True)
    small_sum = _sum_parts("sum_small", small_parts, 1024)
    *small_list, conv_full = _unpack(small_sum, small_shapes + [conv_g.shape])
    grads = dict(zip(SMALL, small_list))
    grads.update(big_grads)
    me = 4 * lax.axis_index("x") + 2 * lax.axis_index("y") + lax.axis_index("c")
    n_cv = w["ev_conv_w"].shape[2]
    grads["ev_conv_w"] = lax.dynamic_slice_in_dim(conv_full, me * n_cv, n_cv, axis=2)

    deltas, new_m, new_v = {}, {}, {}
    for n in BIG + ("ev_conv_w",):
        shp = w[n].shape
        two_d = (-1, shp[-1])
        d_, m_, v_ = _adamw("adamw_" + n, w[n].reshape(two_d), grads[n].reshape(two_d), m[n].reshape(two_d),
                            v[n].reshape(two_d), 256)
        deltas[n], new_m[n], new_v[n] = d_.reshape(shp), m_.reshape(shp), v_.reshape(shp)
    packs = [_pack([src[n] for n in SMALL], F32) for src in (w, grads, m, v)]
    outs = _adamw("adamw_small", *packs, 1024)
    for dst, packed in zip((deltas, new_m, new_v), outs):
        dst.update(zip(SMALL, _unpack(packed, small_shapes)))
    return loss, grad_x, grads, deltas, new_m, new_v


def kernel(x, norm_w, final_norm_w, ev_w_in, ev_conv_w, ev_conv_b, ev_dt_bias, ev_a_log, ev_d_skip, ev_ssd_norm_w, ev_sgu_ln_w, ev_sgu_ln_b, ev_sgu_w, ev_sgu_b, ev_w_out, od_w_in, od_w_out, loss_target, m_norm_w, m_final_norm_w, m_ev_w_in, m_ev_conv_w, m_ev_conv_b, m_ev_dt_bias, m_ev_a_log, m_ev_d_skip, m_ev_ssd_norm_w, m_ev_sgu_ln_w, m_ev_sgu_ln_b, m_ev_sgu_w, m_ev_sgu_b, m_ev_w_out, m_od_w_in, m_od_w_out, v_norm_w, v_final_norm_w, v_ev_w_in, v_ev_conv_w, v_ev_conv_b, v_ev_dt_bias, v_ev_a_log, v_ev_d_skip, v_ev_ssd_norm_w, v_ev_sgu_ln_w, v_ev_sgu_ln_b, v_ev_sgu_w, v_ev_sgu_b, v_ev_w_out, v_od_w_in, v_od_w_out):
    w = dict(zip(WEIGHTS, (norm_w, final_norm_w, ev_w_in, ev_conv_w, ev_conv_b, ev_dt_bias, ev_a_log, ev_d_skip,
                           ev_ssd_norm_w, ev_sgu_ln_w, ev_sgu_ln_b, ev_sgu_w, ev_sgu_b, ev_w_out, od_w_in, od_w_out)))
    m = dict(zip(WEIGHTS, (m_norm_w, m_final_norm_w, m_ev_w_in, m_ev_conv_w, m_ev_conv_b, m_ev_dt_bias, m_ev_a_log,
                           m_ev_d_skip, m_ev_ssd_norm_w, m_ev_sgu_ln_w, m_ev_sgu_ln_b, m_ev_sgu_w, m_ev_sgu_b,
                           m_ev_w_out, m_od_w_in, m_od_w_out)))
    v = dict(zip(WEIGHTS, (v_norm_w, v_final_norm_w, v_ev_w_in, v_ev_conv_w, v_ev_conv_b, v_ev_dt_bias, v_ev_a_log,
                           v_ev_d_skip, v_ev_ssd_norm_w, v_ev_sgu_ln_w, v_ev_sgu_ln_b, v_ev_sgu_w, v_ev_sgu_b,
                           v_ev_w_out, v_od_w_in, v_od_w_out)))
    loss, grad_x, grads, deltas, new_m, new_v = _step(w, m, v, x, loss_target)
    return (loss, grad_x, *[grads[n] for n in WEIGHTS], *[deltas[n] for n in WEIGHTS],
            *[new_m[n] for n in WEIGHTS], *[new_v[n] for n in WEIGHTS])
```

```python
import jax
import jax.numpy as jnp
from jax import lax
from jax.experimental import pallas as pl
from jax.experimental.pallas import tpu as pltpu

F32, BF16 = jnp.float32, jnp.bfloat16

D_MODEL = 2048
SSD_HEADS = 32
SSD_HEAD_DIM = 64
SSD_GROUPS = 4
SSD_STATE = 128
CHUNK = 128
CONV_WIDTH = 4
CONV_DIM = D_MODEL + 2 * SSD_GROUPS * SSD_STATE
SGU_GROUPS = 16
SB_HEADS = 16
LANES = 128
N_PAIRS = SSD_HEADS // 2
PAIRS_PER_GROUP = N_PAIRS // SSD_GROUPS
NORM_EPS = 1e-5
N_DEV = 8

ADAM_LR, ADAM_B1, ADAM_B2, ADAM_EPS, ADAM_WD, ADAM_STEP = 0.001, 0.9, 0.999, 1e-08, 0.01, 10

VMEM_LIMIT_BYTES = 48 * 1024 * 1024

NN = ((1,), (0,))
NT = ((1,), (1,))
TN = ((0,), (0,))


def _cparams(*sem):
    return pltpu.CompilerParams(dimension_semantics=sem, vmem_limit_bytes=VMEM_LIMIT_BYTES)


def _dg(a, b, dims):
    return lax.dot_general(a, b, (dims, ((), ())), preferred_element_type=F32)


def _make_bdot(dims):
    @jax.custom_vjp
    def f(a, b):
        return _dg(a.astype(BF16), b.astype(BF16), dims)

    def fwd(a, b):
        return f(a, b), (a, b)

    def bwd(res, g):
        a, b = res
        a16, b16, g16 = a.astype(BF16), b.astype(BF16), g.astype(BF16)
        if dims == NN:
            da, db = _dg(g16, b16, NT), _dg(a16, g16, TN)
        elif dims == NT:
            da, db = _dg(g16, b16, NN), _dg(g16, a16, TN)
        else:
            da, db = _dg(b16, g16, NT), _dg(a16, g16, NN)
        return da.astype(a.dtype), db.astype(b.dtype)

    f.defvjp(fwd, bwd)
    return f


_bdot_nn, _bdot_nt, _bdot_tn = _make_bdot(NN), _make_bdot(NT), _make_bdot(TN)


def _hdot(a, b):
    return jnp.dot(a, b, precision=lax.Precision.HIGHEST, preferred_element_type=F32)


def _softplus(x):
    return jnp.maximum(x, 0.0) + jnp.log1p(jnp.exp(-jnp.abs(x)))


def _silu(x):
    return x * jax.nn.sigmoid(x)


def _gelu(x):
    return 0.5 * x * (1.0 + jnp.tanh(0.7978845608028654 * (x + 0.044715 * (x * x * x))))


def _rms(x, w):
    return x * lax.rsqrt(jnp.mean(x * x, axis=-1, keepdims=True) + NORM_EPS) * w


def _layer_norm(x, w, b):
    xc = x - jnp.mean(x, axis=-1, keepdims=True)
    return xc * lax.rsqrt(jnp.mean(xc * xc, axis=-1, keepdims=True) + NORM_EPS) * w + b


def _iota2(shape, axis):
    return lax.broadcasted_iota(jnp.int32, shape, axis)


def _split3(x):
    hi = x.astype(BF16)
    r = x - hi.astype(F32)
    mid = r.astype(BF16)
    lo = (r - mid.astype(F32)).astype(BF16)
    return jnp.concatenate([hi, mid, lo], axis=1)


def _row_spec(width, tm):
    return pl.BlockSpec((tm, width), lambda i: (i, 0))


def _full_spec(p):
    zeros = (0,) * p.ndim
    return pl.BlockSpec(p.shape, lambda i: zeros)


def _rows_fwd(name, fn, tiled, params, outs, tm):
    n_rows = tiled[0].shape[0]
    n_in = len(tiled) + len(params)

    def body(*refs):
        res = fn(*[r[...] for r in refs[:n_in]])
        for o_ref, o in zip(refs[n_in:], res):
            o_ref[...] = o.astype(o_ref.dtype)

    return pl.pallas_call(
        body, name=name, grid=(n_rows // tm,),
        in_specs=[_row_spec(a.shape[1], tm) for a in tiled] + [_full_spec(p) for p in params],
        out_specs=[_row_spec(w, tm) for w, _ in outs],
        out_shape=[jax.ShapeDtypeStruct((n_rows, w), d) for w, d in outs],
        compiler_params=_cparams("parallel"),
    )(*tiled, *params)


def _rows_bwd(name, fn, tiled, params, cots, grad_dtypes, tm, add=None):
    n_rows = tiled[0].shape[0]
    nt, npar, nc = len(tiled), len(params), len(cots)
    want = [k for k, d in enumerate(grad_dtypes) if d is not None]
    n_add = 0 if add is None else 1

    def body(*refs):
        ins = [r[...] for r in refs[:nt + npar]]
        c_refs = refs[nt + npar:nt + npar + nc]
        add_refs = refs[nt + npar + nc:nt + npar + nc + n_add]
        o_refs = refs[nt + npar + nc + n_add:]
        res, vjp = jax.vjp(fn, *ins)
        grads = vjp(tuple(c[...].astype(r.dtype) for c, r in zip(c_refs, res)))
        for pos, k in enumerate(want):
            gk = grads[k]
            if add is not None and add[0] == pos:
                gk = gk + add_refs[0][...]
            o_refs[pos][...] = gk.astype(o_refs[pos].dtype)
        p_refs = o_refs[len(want):]

        @pl.when(pl.program_id(0) == 0)
        def _():
            for r in p_refs:
                r[...] = jnp.zeros_like(r)

        for r, gp in zip(p_refs, grads[nt:]):
            r[...] += gp

    add_arrays = [] if add is None else [add[1]]
    out = pl.pallas_call(
        body, name=name, grid=(n_rows // tm,),
        in_specs=([_row_spec(a.shape[1], tm) for a in tiled] + [_full_spec(p) for p in params]
                  + [_row_spec(c.shape[1], tm) for c in cots] + [_row_spec(a.shape[1], tm) for a in add_arrays]),
        out_specs=([_row_spec(tiled[k].shape[1], tm) for k in want] + [_full_spec(p) for p in params]),
        out_shape=([jax.ShapeDtypeStruct(tiled[k].shape, grad_dtypes[k]) for k in want]
                   + [jax.ShapeDtypeStruct(p.shape, F32) for p in params]),
        compiler_params=_cparams("arbitrary"),
    )(*tiled, *params, *cots, *add_arrays)
    return out


def _matmul(name, a, b, dims, out_dtype, add=None, tm=512, tn=1024, tk=2048):
    if dims == NN:
        (m, k), n = a.shape, b.shape[1]
    elif dims == NT:
        (m, k), n = a.shape, b.shape[0]
    else:
        (k, m), n = a.shape, b.shape[1]
    tm, tn, tk = min(tm, m), min(tn, n), min(tk, k)
    while k % tk:
        tk -= LANES
    assert m % tm == 0 and n % tn == 0 and k % tk == 0, (name, m, n, k)
    nk = k // tk
    a_spec = (pl.BlockSpec((tk, tm), lambda i, j, kk: (kk, i)) if dims == TN
              else pl.BlockSpec((tm, tk), lambda i, j, kk: (i, kk)))
    b_spec = (pl.BlockSpec((tn, tk), lambda i, j, kk: (j, kk)) if dims == NT
              else pl.BlockSpec((tk, tn), lambda i, j, kk: (kk, j)))
    o_spec = pl.BlockSpec((tm, tn), lambda i, j, kk: (i, j))
    has_add = add is not None

    def body(*refs):
        a_ref, b_ref = refs[0], refs[1]
        part = _dg(a_ref[...].astype(BF16), b_ref[...].astype(BF16), dims)
        if nk == 1:
            o_ref = refs[-1]
            if has_add:
                part = part + refs[2][...]
            o_ref[...] = part.astype(o_ref.dtype)
            return
        o_ref, acc = refs[-2], refs[-1]
        kk = pl.program_id(2)

        @pl.when(kk == 0)
        def _():
            acc[...] = part

        @pl.when(kk > 0)
        def _():
            acc[...] += part

        @pl.when(kk == nk - 1)
        def _():
            r = acc[...]
            if has_add:
                r = r + refs[2][...]
            o_ref[...] = r.astype(o_ref.dtype)

    return pl.pallas_call(
        body, name=name, grid=(m // tm, n // tn, nk),
        in_specs=[a_spec, b_spec] + ([o_spec] if has_add else []),
        out_specs=o_spec,
        out_shape=jax.ShapeDtypeStruct((m, n), out_dtype),
        scratch_shapes=[pltpu.VMEM((tm, tn), F32)] if nk > 1 else [],
        compiler_params=_cparams("parallel", "parallel", "arbitrary"),
    )(a, b, *([add] if has_add else []))


CONV_TM = 256
HALO = 8


def _shift_down(x, halo, j):
    if j == 0:
        return x, x[:HALO]
    xr = pltpu.roll(x, j, 0)
    hr = pltpu.roll(halo, j, 0)
    top = jnp.where(_iota2((HALO, x.shape[1]), 0) < j, hr, xr[:HALO])
    return xr, top


def _conv_fwd(name, x, w, b):
    t, c = x.shape
    tm = min(CONV_TM, t)
    hb = tm // HALO

    def body(x_ref, halo_ref, w_ref, b_ref, pre_ref, xs_ref, bm_ref, cm_ref):
        i = pl.program_id(0)
        xv = x_ref[...]
        halo = jnp.where(i > 0, halo_ref[...], 0.0)
        main = jnp.zeros_like(xv) + b_ref[...]
        top = jnp.zeros((HALO, c), F32) + b_ref[...]
        for kk in range(CONV_WIDTH):
            xr, tp = _shift_down(xv, halo, CONV_WIDTH - 1 - kk)
            main = main + w_ref[kk:kk + 1, :] * xr
            top = top + w_ref[kk:kk + 1, :] * tp
        pre = jnp.concatenate([top, main[HALO:]], axis=0)
        pre_ref[...] = pre
        act = _silu(pre)
        xs_ref[...] = act[:, :D_MODEL]
        bm_ref[...] = act[:, D_MODEL:D_MODEL + SSD_GROUPS * SSD_STATE]
        cm_ref[...] = act[:, D_MODEL + SSD_GROUPS * SSD_STATE:]

    gs = SSD_GROUPS * SSD_STATE
    return pl.pallas_call(
        body, name=name, grid=(t // tm,),
        in_specs=[_row_spec(c, tm),
                  pl.BlockSpec((HALO, c), lambda i: (jnp.maximum(i * hb - 1, 0), 0)),
                  _full_spec(w), _full_spec(b)],
        out_specs=[_row_spec(c, tm), _row_spec(D_MODEL, tm), _row_spec(gs, tm), _row_spec(gs, tm)],
        out_shape=[jax.ShapeDtypeStruct((t, c), F32), jax.ShapeDtypeStruct((t, D_MODEL), F32),
                   jax.ShapeDtypeStruct((t, gs), F32), jax.ShapeDtypeStruct((t, gs), F32)],
        compiler_params=_cparams("parallel"),
    )(x, x, w, b)


def _dsilu(pre, dact):
    s = jax.nn.sigmoid(pre)
    return dact * (s * (1.0 + pre * (1.0 - s)))


def _conv_bwd(name, x, pre, w, dxs, dbm, dcm):
    t, c = x.shape
    tm = min(CONV_TM, t)
    hb = tm // HALO
    last_hb = t // HALO - 1
    n_tiles = t // tm

    def body(x_ref, xh_ref, pre_ref, preh_ref, w_ref, dxs_ref, dbm_ref, dcm_ref,
             dxsh_ref, dbmh_ref, dcmh_ref, dx_ref, dw_ref, db_ref):
        i = pl.program_id(0)
        dact = jnp.concatenate([dxs_ref[...], dbm_ref[...], dcm_ref[...]], axis=1)
        dpre = _dsilu(pre_ref[...], dact)
        dact_h = jnp.concatenate([dxsh_ref[...], dbmh_ref[...], dcmh_ref[...]], axis=1)
        dpre_h = jnp.where(i < n_tiles - 1, _dsilu(preh_ref[...], dact_h), 0.0)
        xv = x_ref[...]
        xh = jnp.where(i > 0, xh_ref[...], 0.0)

        @pl.when(i == 0)
        def _():
            dw_ref[...] = jnp.zeros_like(dw_ref)
            db_ref[...] = jnp.zeros_like(db_ref)

        db_ref[...] += jnp.sum(dpre, axis=0, keepdims=True)
        dxm = jnp.zeros_like(xv)
        dxt = jnp.zeros((HALO, c), F32)
        row8 = _iota2((HALO, c), 0)
        for kk in range(CONV_WIDTH):
            j = CONV_WIDTH - 1 - kk
            wk = w_ref[kk:kk + 1, :]
            xr, tp = _shift_down(xv, xh, j)
            full = jnp.sum(dpre * xr, axis=0, keepdims=True)
            fix = jnp.sum(dpre[:HALO] * (tp - xr[:HALO]), axis=0, keepdims=True)
            dw_ref[kk:kk + 1, :] += full + fix
            if j == 0:
                dxm = dxm + wk * dpre
                dxt = dxt + wk * dpre[tm - HALO:]
            else:
                dr = pltpu.roll(dpre, tm - j, 0)
                hr = pltpu.roll(dpre_h, HALO - j, 0)
                dxm = dxm + wk * dr
                dxt = dxt + wk * jnp.where(row8 >= HALO - j, hr, dr[tm - HALO:])
        dx_ref[...] = jnp.concatenate([dxm[:tm - HALO], dxt], axis=0).astype(dx_ref.dtype)

    gs = SSD_GROUPS * SSD_STATE
    prev_halo = lambda i: (jnp.maximum(i * hb - 1, 0), 0)
    next_halo = lambda i: (jnp.minimum((i + 1) * hb, last_hb), 0)
    return pl.pallas_call(
        body, name=name, grid=(n_tiles,),
        in_specs=[_row_spec(c, tm), pl.BlockSpec((HALO, c), prev_halo),
                  _row_spec(c, tm), pl.BlockSpec((HALO, c), next_halo), _full_spec(w),
                  _row_spec(D_MODEL, tm), _row_spec(gs, tm), _row_spec(gs, tm),
                  pl.BlockSpec((HALO, D_MODEL), next_halo), pl.BlockSpec((HALO, gs), next_halo),
                  pl.BlockSpec((HALO, gs), next_halo)],
        out_specs=[_row_spec(c, tm), _full_spec(w), pl.BlockSpec((1, c), lambda i: (0, 0))],
        out_shape=[jax.ShapeDtypeStruct((t, c), BF16), jax.ShapeDtypeStruct(w.shape, F32),
                   jax.ShapeDtypeStruct((1, c), F32)],
        compiler_params=_cparams("arbitrary"),
    )(x, x, pre, pre, w, dxs, dbm, dcm, dxs, dbm, dcm)


def _ssd_stage1(dt_raw, dt_bias, a_log, d_skip8):
    l = dt_raw.shape[0]
    n_ch = SSD_HEADS * SSD_HEAD_DIM
    expand = (_iota2((LANES, n_ch), 1) // SSD_HEAD_DIM == _iota2((LANES, n_ch), 0)).astype(F32)
    expand_a = (_iota2((LANES, SSD_HEADS * LANES), 1) // LANES
                == _iota2((LANES, SSD_HEADS * LANES), 0)).astype(F32)
    tri = (_iota2((l, l), 1) <= _iota2((l, l), 0)).astype(F32)
    dt = _softplus(dt_raw + dt_bias)
    cs = _hdot(tri, dt * (-jnp.exp(a_log)))
    return _hdot(dt, expand), _hdot(cs, expand), _hdot(cs, expand_a), _hdot(d_skip8, expand)


def _ssd_pair(xs, dtf, csf, cs_last, a0, a1, bg, cg, prev, dskf8):
    l = xs.shape[0]
    xc = xs * dtf
    scores = _bdot_nt(cg, bg)
    causal = _iota2((l, l), 0) >= _iota2((l, l), 1)

    def decay(a):
        return jnp.where(causal, jnp.exp(jnp.where(causal, a - a.T, 0.0)), 0.0)

    first = _iota2((l, LANES), 1) < SSD_HEAD_DIM
    y_diag = (_bdot_nn(scores * decay(a0), jnp.where(first, xc, 0.0))
              + _bdot_nn(scores * decay(a1), jnp.where(first, 0.0, xc)))
    states = _bdot_tn(bg, xc * jnp.exp(cs_last - csf))
    new_state = jnp.exp(cs_last) * prev + states
    y_off = _bdot_nn(cg, prev) * jnp.exp(csf)
    y = y_diag + y_off + xs * jnp.mean(dskf8, axis=0, keepdims=True)
    return y, new_state


def _ssd_specs(nc_rev=None):
    def ch(c):
        return c if nc_rev is None else nc_rev - 1 - c

    gs = SSD_GROUPS * SSD_STATE
    return dict(
        wide=pl.BlockSpec((CHUNK, D_MODEL), lambda c: (ch(c), 0)),
        group=pl.BlockSpec((CHUNK, gs), lambda c: (ch(c), 0)),
        chunk=pl.BlockSpec((CHUNK, LANES), lambda c: (ch(c), 0)),
        vec=pl.BlockSpec((1, LANES), lambda c: (0, 0)),
        state=pl.BlockSpec((1, N_PAIRS, LANES, LANES), lambda c: (ch(c), 0, 0, 0)),
    )


def _ssd_scratch():
    return [pltpu.VMEM((CHUNK, D_MODEL), F32), pltpu.VMEM((CHUNK, D_MODEL), F32),
            pltpu.VMEM((CHUNK, SSD_HEADS * LANES), F32), pltpu.VMEM((8, D_MODEL), F32)]


def _pair_slices(p):
    g = p // PAIRS_PER_GROUP
    return (slice(p * LANES, (p + 1) * LANES), slice(g * LANES, (g + 1) * LANES),
            slice(2 * p * LANES, (2 * p + 1) * LANES), slice((2 * p + 1) * LANES, (2 * p + 2) * LANES))


def _pair_inputs(p, xs_ref, b_ref, c_ref, prev, stage1):
    dtf_s, csf_s, csa_s, dsk_s = stage1
    sl, gsl, h0, h1 = _pair_slices(p)
    return (xs_ref[:, sl], dtf_s[:, sl], csf_s[:, sl], csf_s[CHUNK - 1:CHUNK, sl], csa_s[:, h0], csa_s[:, h1],
            b_ref[:, gsl], c_ref[:, gsl], prev, dsk_s[:, sl])


def _ssd_fwd(name, xs, bm, cm, dt_raw, dt_bias, a_log, d_skip, rider=None):
    t = xs.shape[0]
    nc = t // CHUNK

    def body(xs_ref, b_ref, c_ref, dt_ref, bias_ref, alog_ref, dsk_ref, y_ref, prev_ref,
             state, dtf_s, csf_s, csa_s, dsk_s):
        c = pl.program_id(0)
        stage1 = (dtf_s, csf_s, csa_s, dsk_s)
        d8 = jnp.broadcast_to(dsk_ref[...], (8, LANES))
        for ref, val in zip(stage1, _ssd_stage1(dt_ref[...], bias_ref[...], alog_ref[...], d8)):
            ref[...] = val

        @pl.when(c == 0)
        def _():
            state[...] = jnp.zeros_like(state)

        for p in range(N_PAIRS):
            prev = state[p]
            prev_ref[0, p] = prev
            y, new_state = _ssd_pair(*_pair_inputs(p, xs_ref, b_ref, c_ref, prev, stage1))
            y_ref[:, _pair_slices(p)[0]] = y
            state[p] = new_state

    sp = _ssd_specs()
    return _hosted(
        body, rider, name=name, grid=(nc,),
        in_specs=[sp["wide"], sp["group"], sp["group"], sp["chunk"], sp["vec"], sp["vec"], sp["vec"]],
        out_specs=[sp["wide"], sp["state"]],
        out_shape=[jax.ShapeDtypeStruct((t, D_MODEL), F32),
                   jax.ShapeDtypeStruct((nc, N_PAIRS, LANES, LANES), F32)],
        scratch_shapes=[pltpu.VMEM((N_PAIRS, LANES, LANES), F32)] + _ssd_scratch(),
        args=(xs, bm, cm, dt_raw, dt_bias, a_log, d_skip))


def _ssd_bwd(name, xs, bm, cm, dt_raw, dt_bias, a_log, d_skip, prev_states, dy, rider=None):
    t = xs.shape[0]
    nc = t // CHUNK

    def body(xs_ref, b_ref, c_ref, dt_ref, bias_ref, alog_ref, dsk_ref, prev_ref, dy_ref,
             dxs_ref, db_ref, dc_ref, ddt_ref, dbias_ref, dalog_ref, ddsk_ref,
             dstate, dtf_s, csf_s, csa_s, dsk_s, g_dtf, g_csf, g_csa, g_dsk):
        c = pl.program_id(0)
        stage1 = (dtf_s, csf_s, csa_s, dsk_s)
        d8 = jnp.broadcast_to(dsk_ref[...], (8, LANES))
        vals, vjp1 = jax.vjp(_ssd_stage1, dt_ref[...], bias_ref[...], alog_ref[...], d8)
        for ref, val in zip(stage1, vals):
            ref[...] = val

        @pl.when(c == 0)
        def _():
            dstate[...] = jnp.zeros_like(dstate)
            dbias_ref[...] = jnp.zeros_like(dbias_ref)
            dalog_ref[...] = jnp.zeros_like(dalog_ref)
            ddsk_ref[...] = jnp.zeros_like(ddsk_ref)

        last_row = _iota2((CHUNK, LANES), 0) == CHUNK - 1
        for p in range(N_PAIRS):
            sl, gsl, h0, h1 = _pair_slices(p)
            _, vjp = jax.vjp(_ssd_pair, *_pair_inputs(p, xs_ref, b_ref, c_ref, prev_ref[0, p], stage1))
            dxs, ddtf, dcsf, dlast, da0, da1, dbg, dcg, dprev, ddsk8 = vjp((dy_ref[:, sl], dstate[p]))
            dxs_ref[:, sl] = dxs
            g_dtf[:, sl] = ddtf
            g_csf[:, sl] = dcsf + jnp.where(last_row, dlast, 0.0)
            g_csa[:, h0] = da0
            g_csa[:, h1] = da1
            g_dsk[:, sl] = ddsk8
            dstate[p] = dprev
            if p % PAIRS_PER_GROUP == 0:
                db_ref[:, gsl] = dbg
                dc_ref[:, gsl] = dcg
            else:
                db_ref[:, gsl] += dbg
                dc_ref[:, gsl] += dcg

        ddt, dbias, dalog, dd8 = vjp1((g_dtf[...], g_csf[...], g_csa[...], g_dsk[...]))
        ddt_ref[...] = ddt
        dbias_ref[...] += dbias
        dalog_ref[...] += dalog
        ddsk_ref[...] += jnp.sum(dd8, axis=0, keepdims=True)

    sp = _ssd_specs(nc)
    gs = SSD_GROUPS * SSD_STATE
    return _hosted(
        body, rider, name=name, grid=(nc,),
        in_specs=[sp["wide"], sp["group"], sp["group"], sp["chunk"], sp["vec"], sp["vec"], sp["vec"],
                  sp["state"], sp["wide"]],
        out_specs=[sp["wide"], sp["group"], sp["group"], sp["chunk"], sp["vec"], sp["vec"], sp["vec"]],
        out_shape=[jax.ShapeDtypeStruct((t, D_MODEL), F32), jax.ShapeDtypeStruct((t, gs), F32),
                   jax.ShapeDtypeStruct((t, gs), F32), jax.ShapeDtypeStruct((t, LANES), F32),
                   jax.ShapeDtypeStruct((1, LANES), F32), jax.ShapeDtypeStruct((1, LANES), F32),
                   jax.ShapeDtypeStruct((1, LANES), F32)],
        scratch_shapes=[pltpu.VMEM((N_PAIRS, LANES, LANES), F32)] + _ssd_scratch() + _ssd_scratch(),
        args=(xs, bm, cm, dt_raw, dt_bias, a_log, d_skip, prev_states, dy))


def _ssd_gate(y, z, w):
    return (_rms(y * _silu(z), w),)


def _sgu_norm(v, w, b):
    return (_layer_norm(_gelu(v), w, b),)


def _sgu_group(u, gate, vn, w, bcol):
    l = u.shape[0]
    wc = jnp.where(_iota2((l, l), 0) >= _iota2((l, l), 1), w, 0.0)
    return _gelu(u) * (_bdot_nn(wc, vn) + bcol) * _silu(gate)


def _sgu_fwd(name, u, gate, vn, w, bcol):
    t = u.shape[0]
    blk = _row_spec(D_MODEL, CHUNK)

    def body(u_ref, g_ref, vn_ref, w_ref, b_ref, y_ref):
        for g in range(SGU_GROUPS):
            sl = slice(g * LANES, (g + 1) * LANES)
            y_ref[:, sl] = _sgu_group(u_ref[:, sl], g_ref[:, sl], vn_ref[:, sl], w_ref[g], b_ref[g]).astype(y_ref.dtype)

    return pl.pallas_call(
        body, name=name, grid=(t // CHUNK,),
        in_specs=[blk, blk, blk, _full_spec(w), _full_spec(bcol)], out_specs=blk,
        out_shape=jax.ShapeDtypeStruct((t, D_MODEL), BF16),
        compiler_params=_cparams("parallel"),
    )(u, gate, vn, w, bcol)


def _sgu_bwd(name, u, gate, vn, w, bcol, dy):
    t = u.shape[0]
    blk = _row_spec(D_MODEL, CHUNK)

    def body(u_ref, g_ref, vn_ref, w_ref, b_ref, dy_ref, du_ref, dg_ref, dvn_ref, dw_ref, db_ref):
        @pl.when(pl.program_id(0) == 0)
        def _():
            dw_ref[...] = jnp.zeros_like(dw_ref)
            db_ref[...] = jnp.zeros_like(db_ref)

        for g in range(SGU_GROUPS):
            sl = slice(g * LANES, (g + 1) * LANES)
            _, vjp = jax.vjp(_sgu_group, u_ref[:, sl], g_ref[:, sl], vn_ref[:, sl], w_ref[g], b_ref[g])
            du, dg, dvn, dw, db = vjp(dy_ref[:, sl])
            du_ref[:, sl] = du.astype(du_ref.dtype)
            dg_ref[:, sl] = dg.astype(dg_ref.dtype)
            dvn_ref[:, sl] = dvn
            dw_ref[g] += dw
            db_ref[g] += db

    return pl.pallas_call(
        body, name=name, grid=(t // CHUNK,),
        in_specs=[blk, blk, blk, _full_spec(w), _full_spec(bcol), blk],
        out_specs=[blk, blk, blk, _full_spec(w), _full_spec(bcol)],
        out_shape=[jax.ShapeDtypeStruct((t, D_MODEL), BF16), jax.ShapeDtypeStruct((t, D_MODEL), BF16),
                   jax.ShapeDtypeStruct((t, D_MODEL), F32), jax.ShapeDtypeStruct(w.shape, F32),
                   jax.ShapeDtypeStruct(bcol.shape, F32)],
        compiler_params=_cparams("arbitrary"),
    )(u, gate, vn, w, bcol, dy)


SB_SCALE = LANES ** -0.5


SB_TQ = 256
SB_TS = 512


def _sb_pieces(z, mask):
    tl = jnp.log1p(jnp.exp(-jnp.abs(z)))
    lk = jnp.where(mask, -(jnp.maximum(z, 0.0) + tl), 0.0)
    ls = jnp.minimum(z, 0.0) - tl
    return lk, ls


def _tri3(cmp):
    sq = (CHUNK, CHUNK)
    m = cmp(_iota2(sq, 0), _iota2(sq, 1)).astype(BF16)
    return jnp.concatenate([m, m, m], axis=0)


def _tri_sums(blocks, tri3):
    tq = blocks[0].shape[0]
    res = _dg(jnp.concatenate([_split3(b) for b in blocks], axis=0), tri3, NN)
    return [res[b * tq:(b + 1) * tq] for b in range(len(blocks))]


def _sb_tiles(t):
    tq, ts = min(SB_TQ, t), min(SB_TS, t)
    assert t % tq == 0 and t % ts == 0 and ts % CHUNK == 0
    return tq, ts, ts // CHUNK


def _sb_logits(qb, ks, off, q_off):
    tq = qb.shape[0]
    z = _dg(qb, ks, NT) * SB_SCALE
    row = _iota2((tq, CHUNK), 0) + q_off
    col = _iota2((tq, CHUNK), 1) + off
    out = []
    for b in range(ks.shape[0] // CHUNK):
        mask = (col + b * CHUNK) < row
        out.append(_sb_pieces(z[:, b * CHUNK:(b + 1) * CHUNK], mask) + (mask,))
    return out


def _attn_fwd(name, q, k, v, rider=None):
    t = q.shape[0]
    tq, ts, nb = _sb_tiles(t)

    def body(q_ref, k_ref, v_ref, y_ref, tot_ref):
        i = pl.program_id(1)
        qb = q_ref[...]
        later3 = _tri3(lambda r, c: r > c)
        n_spans = (i * tq + tq - 1) // ts + 1

        def step(n, carry):
            acc, after = carry
            off = pl.multiple_of((n_spans - 1 - n) * ts, ts)
            pieces = _sb_logits(qb, k_ref[pl.ds(off, ts), :], off, i * tq)
            inside = _tri_sums([lk for lk, _, _ in pieces], later3)
            ws = [None] * nb
            for b in reversed(range(nb)):
                lk, ls, mask = pieces[b]
                ws[b] = jnp.where(mask, jnp.exp(ls + inside[b] + after), 0.0).astype(BF16)
                after = after + jnp.sum(lk, axis=1, keepdims=True)
            acc = acc + _dg(jnp.concatenate(ws, axis=1), v_ref[pl.ds(off, ts), :], NN)
            return acc, after

        acc, tot = lax.fori_loop(0, n_spans, step, (jnp.zeros((tq, LANES), F32), jnp.zeros((tq, 1), F32)))
        y_ref[...] = acc
        tot_ref[...] = jnp.broadcast_to(tot, (tq, LANES))

    qsp = pl.BlockSpec((tq, LANES), lambda h, i: (i, h))
    kvsp = pl.BlockSpec((t, LANES), lambda h, i: (0, h))
    return _hosted(
        body, rider, name=name, grid=(SB_HEADS, t // tq),
        in_specs=[qsp, kvsp, kvsp], out_specs=[qsp, qsp],
        out_shape=[jax.ShapeDtypeStruct((t, D_MODEL), F32), jax.ShapeDtypeStruct((t, D_MODEL), F32)],
        scratch_shapes=[], args=(q, k, v))


def _attn_bwd(name, q, k, v, tot, dy, rider=None):
    t = q.shape[0]
    tq, ts, nb = _sb_tiles(t)
    nq, ns = t // tq, t // ts

    def body(q_ref, k_ref, v_ref, tot_ref, dy_ref, dq_ref, dk_ref, dv_ref, dkt_acc, dvt_acc):
        i = pl.program_id(1)

        @pl.when(i == 0)
        def _():
            dkt_acc[...] = jnp.zeros_like(dkt_acc)
            dvt_acc[...] = jnp.zeros_like(dvt_acc)

        qb = q_ref[...]
        dy = dy_ref[...]
        dyb = dy.astype(BF16)
        q_t = qb.astype(F32).T.astype(BF16)
        dy_t = dy.T.astype(BF16)
        totb = tot_ref[...]
        upto3 = _tri3(lambda r, c: r <= c)
        before3 = _tri3(lambda r, c: r < c)
        n_spans = (i * tq + tq - 1) // ts + 1

        def step(j, carry):
            dq, lk_seen, e_seen = carry
            off = pl.multiple_of(j * ts, ts)
            ks = k_ref[pl.ds(off, ts), :]
            pieces = _sb_logits(qb, ks, off, i * tq)
            dw = _dg(dyb, v_ref[pl.ds(off, ts), :], NT)
            upto = _tri_sums([lk for lk, _, _ in pieces], upto3)
            ws, es = [], []
            for b in range(nb):
                lk, ls, mask = pieces[b]
                w = jnp.where(mask, jnp.exp(ls + ((totb - upto[b]) - lk_seen)), 0.0)
                ws.append(w.astype(BF16))
                es.append(dw[:, b * CHUNK:(b + 1) * CHUNK] * w)
                lk_seen = lk_seen + jnp.sum(lk, axis=1, keepdims=True)
            before = _tri_sums(es, before3)
            dzs = []
            for b in range(nb):
                _, ls, mask = pieces[b]
                sig = jnp.exp(ls)
                dlk = e_seen + before[b]
                dzs.append((jnp.where(mask, es[b] * (1.0 - sig) - dlk * sig, 0.0) * SB_SCALE).astype(BF16))
                e_seen = e_seen + jnp.sum(es[b], axis=1, keepdims=True)
            dz = jnp.concatenate(dzs, axis=1)
            dq = dq + _dg(dz, ks, NN)
            dkt_acc[j] += _dg(q_t, dz, NN)
            dvt_acc[j] += _dg(dy_t, jnp.concatenate(ws, axis=1), NN)
            return dq, lk_seen, e_seen

        zero_col = jnp.zeros((tq, 1), F32)
        dq, _, _ = lax.fori_loop(0, n_spans, step, (jnp.zeros((tq, LANES), F32), zero_col, zero_col))
        dq_ref[...] = dq.astype(dq_ref.dtype)

        @pl.when(i == nq - 1)
        def _():
            for s in range(ns):
                dk_ref[s * ts:(s + 1) * ts, :] = dkt_acc[s].T.astype(dk_ref.dtype)
                dv_ref[s * ts:(s + 1) * ts, :] = dvt_acc[s].T.astype(dv_ref.dtype)

    qsp = pl.BlockSpec((tq, LANES), lambda h, i: (i, h))
    kvsp = pl.BlockSpec((t, LANES), lambda h, i: (0, h))
    return _hosted(
        body, rider, name=name, grid=(SB_HEADS, nq),
        in_specs=[qsp, kvsp, kvsp, qsp, qsp], out_specs=[qsp, kvsp, kvsp],
        out_shape=[jax.ShapeDtypeStruct((t, D_MODEL), BF16)] * 3,
        scratch_shapes=[pltpu.VMEM((ns, LANES, ts), F32), pltpu.VMEM((ns, LANES, ts), F32)],
        args=(q, k, v, tot, dy))


def _attn_gate(y, g):
    return (y * _silu(g),)


def _loss_head(name, h, target, w, tm):
    t, d = h.shape

    def body(h_ref, t_ref, w_ref, dh_ref, dw_ref, loss_ref):
        tgt = t_ref[...]

        def f(hv, wv):
            e = _rms(hv, wv) - tgt
            return 0.5 * jnp.mean(e * e, axis=-1, keepdims=True)

        row_loss, vjp = jax.vjp(f, h_ref[...], w_ref[...])
        dh, dw = vjp(jnp.ones_like(row_loss))
        dh_ref[...] = dh

        @pl.when(pl.program_id(0) == 0)
        def _():
            dw_ref[...] = jnp.zeros_like(dw_ref)
            loss_ref[...] = jnp.zeros_like(loss_ref)

        dw_ref[...] += dw
        loss_ref[...] += jnp.sum(row_loss, axis=0, keepdims=True)

    return pl.pallas_call(
        body, name=name, grid=(t // tm,),
        in_specs=[_row_spec(d, tm), _row_spec(d, tm), _full_spec(w)],
        out_specs=[_row_spec(d, tm), _full_spec(w), pl.BlockSpec((1, 1), lambda i: (0, 0))],
        out_shape=[jax.ShapeDtypeStruct((t, d), F32), jax.ShapeDtypeStruct(w.shape, F32),
                   jax.ShapeDtypeStruct((1, 1), F32)],
        compiler_params=_cparams("arbitrary"),
    )(h, target, w)


def _pick_tile(rows, cap):
    if rows <= cap:
        return rows
    for tm in range(cap - cap % 16, 0, -16):
        if rows % tm == 0:
            return tm
    raise ValueError(rows)


def _adamw(name, w, g, m, v, tm):
    r, c = w.shape
    tm = _pick_tile(r, tm)

    def body(w_ref, g_ref, m_ref, v_ref, d_ref, nm_ref, nv_ref):
        g_ = g_ref[...]
        m_ = ADAM_B1 * m_ref[...] + (1.0 - ADAM_B1) * g_
        v_ = ADAM_B2 * v_ref[...] + (1.0 - ADAM_B2) * (g_ * g_)
        m_hat = m_ / (1.0 - ADAM_B1 ** ADAM_STEP)
        v_hat = v_ / (1.0 - ADAM_B2 ** ADAM_STEP)
        d_ref[...] = -ADAM_LR * (m_hat / (jnp.sqrt(v_hat) + ADAM_EPS) + ADAM_WD * w_ref[...])
        nm_ref[...] = m_
        nv_ref[...] = v_

    spec = _row_spec(c, tm)
    return pl.pallas_call(
        body, name=name, grid=(r // tm,), in_specs=[spec] * 4, out_specs=[spec] * 3,
        out_shape=[jax.ShapeDtypeStruct((r, c), F32)] * 3, compiler_params=_cparams("parallel"),
    )(w, g, m, v)


def _sum_parts(name, parts, tm):
    n, r, c = parts.shape
    tm = _pick_tile(r, tm)

    def body(p_ref, o_ref):
        s = p_ref[0].astype(F32)
        for d in range(1, n):
            s = s + p_ref[d].astype(F32)
        o_ref[...] = s

    return pl.pallas_call(
        body, name=name, grid=(r // tm,),
        in_specs=[pl.BlockSpec((n, tm, c), lambda i: (0, i, 0))], out_specs=_row_spec(c, tm),
        out_shape=jax.ShapeDtypeStruct((r, c), F32), compiler_params=_cparams("parallel"),
    )(parts)


def _peer(k):
    x, y, c = lax.axis_index("x"), lax.axis_index("y"), lax.axis_index("c")
    px, py, pc = x ^ ((k >> 2) & 1), y ^ ((k >> 1) & 1), c ^ (k & 1)
    return (px, py, pc), 4 * px + 2 * py + pc


def _exchange(name, xs, gather):
    rider = _Rider(xs, gather)
    n = rider.n

    def body(*refs):
        x_refs, out_refs, sems = refs[:n], refs[n:2 * n], refs[2 * n:]
        _exchange_start(x_refs, out_refs, sems, gather)
        _exchange_finish(x_refs, out_refs, sems, gather)

    return pl.pallas_call(
        body, name=name, in_specs=[ANY_SPEC] * n, out_specs=[ANY_SPEC] * n,
        out_shape=rider.out_shape(), scratch_shapes=rider.scratch(),
    )(*xs)


ANY_SPEC = pl.BlockSpec(memory_space=pl.ANY)
SAME_CORE = (2, 4, 6)


class _Rider:
    def __init__(self, xs, gather):
        self.xs, self.gather, self.n = list(xs), gather, len(xs)

    def out_shape(self):
        return [jax.ShapeDtypeStruct((N_DEV,) + tuple(x.shape if self.gather else x.shape[1:]), x.dtype)
                for x in self.xs]

    def scratch(self):
        return [pltpu.SemaphoreType.DMA((self.n, N_DEV - 1)), pltpu.SemaphoreType.DMA((self.n, N_DEV - 1)),
                pltpu.SemaphoreType.DMA((self.n,))]


def _remote(src, dst, sems, a, k, dev):
    return pltpu.make_async_remote_copy(
        src_ref=src, dst_ref=dst, send_sem=sems[0].at[a, k - 1], recv_sem=sems[1].at[a, k - 1],
        device_id=dev, device_id_type=pl.DeviceIdType.MESH)


def _exchange_start(x_refs, out_refs, sems, gather):
    _, me = _peer(0)
    for a, (x, out) in enumerate(zip(x_refs, out_refs)):
        pltpu.make_async_copy(x if gather else x.at[me], out.at[me], sems[2].at[a]).start()
    for k in ((1,) + SAME_CORE if gather else range(1, N_DEV)):
        dev, idx = _peer(k)
        for a, (x, out) in enumerate(zip(x_refs, out_refs)):
            _remote(x if gather else x.at[idx], out.at[me], sems, a, k, dev).start()


def _exchange_finish(x_refs, out_refs, sems, gather):
    _, me = _peer(0)
    sibling, _ = _peer(1)
    pairs = list(enumerate(zip(x_refs, out_refs)))
    waited = ()
    if gather:
        for k in SAME_CORE:
            dev, idx = _peer(k)
            for a, (x, out) in pairs:
                _remote(x, out.at[idx], sems, a, k, dev).wait_recv()
                _remote(out.at[idx], out.at[idx], sems, a, k + 1, sibling).start()
        waited = SAME_CORE
    for k in range(1, N_DEV):
        dev, idx = _peer(k)
        for a, (x, out) in pairs:
            cp = _remote(x if gather else x.at[idx], out.at[idx], sems, a, k, dev)
            if k not in waited:
                cp.wait_recv()
            cp.wait_send()
    for a, (x, out) in pairs:
        pltpu.make_async_copy(x if gather else x.at[me], out.at[me], sems[2].at[a]).wait()


def _hosted(body, rider, *, name, grid, in_specs, out_specs, out_shape, scratch_shapes, args):
    sem = ("arbitrary",) * len(grid)
    if rider is None:
        return pl.pallas_call(body, name=name, grid=grid, in_specs=in_specs, out_specs=out_specs, out_shape=out_shape,
                              scratch_shapes=scratch_shapes, compiler_params=_cparams(*sem))(*args), []
    n_in, n_out, n_scr, nr = len(in_specs), len(out_specs), len(scratch_shapes), rider.n

    def hosted(*refs):
        ins, refs = refs[:n_in], refs[n_in:]
        r_in, refs = refs[:nr], refs[nr:]
        outs, refs = refs[:n_out], refs[n_out:]
        r_out, refs = refs[:nr], refs[nr:]
        scr, sems = refs[:n_scr], refs[n_scr:]
        first = pl.program_id(0) == 0
        last = pl.program_id(0) == grid[0] - 1
        for ax in range(1, len(grid)):
            first = first & (pl.program_id(ax) == 0)
            last = last & (pl.program_id(ax) == grid[ax] - 1)

        @pl.when(first)
        def _():
            _exchange_start(r_in, r_out, sems, rider.gather)

        body(*ins, *outs, *scr)

        @pl.when(last)
        def _():
            _exchange_finish(r_in, r_out, sems, rider.gather)

    res = pl.pallas_call(
        hosted, name=name, grid=grid, in_specs=list(in_specs) + [ANY_SPEC] * nr,
        out_specs=list(out_specs) + [ANY_SPEC] * nr, out_shape=list(out_shape) + rider.out_shape(),
        scratch_shapes=list(scratch_shapes) + rider.scratch(), compiler_params=_cparams(*sem),
    )(*args, *rider.xs)
    return res[:n_out], res[n_out:]


PACK_ALIGN = 8 * LANES
PACK_ROWS = 512


def _pack(arrays, dtype):
    pieces, total = [], 0
    for a in arrays:
        f = a.reshape(-1).astype(dtype)
        pad = (-f.shape[0]) % PACK_ALIGN
        pieces.append(jnp.pad(f, (0, pad)) if pad else f)
        total += f.shape[0] + pad
    tail = (-total) % (PACK_ROWS * LANES)
    if tail:
        pieces.append(jnp.zeros((tail,), dtype))
    return jnp.concatenate(pieces).reshape(-1, LANES)


def _unpack(packed, shapes, lead=()):
    flat = packed.reshape(lead + (-1,))
    out, off = [], 0
    for s in shapes:
        n = 1
        for d in s:
            n *= d
        out.append(flat[..., off:off + n].reshape(lead + tuple(s)))
        off += n + ((-n) % PACK_ALIGN)
    return out


def _pad_lanes(a):
    return jnp.pad(a, (0, LANES - a.shape[0])).reshape(1, LANES)


ROW_TM = 256
EVEN_SEGS = (("z", D_MODEL), ("xbc", CONV_DIM), ("dt", SSD_HEADS), ("g", D_MODEL), ("u", D_MODEL), ("v", D_MODEL))
ODD_SEGS = (("q", D_MODEL), ("k", D_MODEL), ("v", D_MODEL), ("g", D_MODEL))


def _split_cols(w, segs):
    out, off = {}, 0
    for nm, n in segs:
        out[nm] = w[:, off:off + n]
        off += n
    return out


def _rms_fn(h, w):
    return (_rms(h, w),)


def _even_fwd(tag, h, p, rider=None, late=None):
    hn, = _rows_fwd(tag + "_norm", _rms_fn, [h], [p["norm_w"]], [(D_MODEL, BF16)], ROW_TM)
    proj = {nm: _matmul(f"{tag}_in_{nm}", hn, p["w_in"][nm], NN, F32) for nm, _ in EVEN_SEGS}
    pre, xs, bm, cm = _conv_fwd(tag + "_conv", proj["xbc"], p["conv_w"], p["conv_b"])
    (y_ssd, states), got = _ssd_fwd(tag + "_ssd", xs, bm, cm, proj["dt"], p["dt_bias"], p["a_log"], p["d_skip"], rider)
    if late is not None:
        p = dict(p, **late(got))
    ya, = _rows_fwd(tag + "_ssdgate", _ssd_gate, [y_ssd, proj["z"]], [p["ssd_norm_w"]], [(D_MODEL, BF16)], ROW_TM)
    vn, = _rows_fwd(tag + "_sgunorm", _sgu_norm, [proj["v"]], [p["sgu_ln_w"], p["sgu_ln_b"]], [(D_MODEL, F32)], ROW_TM)
    yb = _sgu_fwd(tag + "_sgu", proj["u"], proj["g"], vn, p["sgu_w"], p["sgu_b"])
    h1 = _matmul(tag + "_out_a", ya, p["w_out_a"], NN, F32, add=h)
    h2 = _matmul(tag + "_out_b", yb, p["w_out_b"], NN, F32, add=h1)
    saved = dict(h=h, hn=hn, proj=proj, pre=pre, xs=xs, bm=bm, cm=cm, y_ssd=y_ssd, states=states,
                 ya=ya, vn=vn, yb=yb)
    return h2, saved, p, got


def _even_bwd(tag, dh, s, p, make_rider=None):
    g = {}
    dh16 = dh.astype(BF16)
    proj = s["proj"]
    dya = _matmul(tag + "_dya", dh16, p["w_out_a"], NT, F32)
    dyb = _matmul(tag + "_dyb", dh16, p["w_out_b"], NT, F32)
    g["w_out_a"] = _matmul(tag + "_dwout_a", s["ya"], dh16, TN, BF16)
    g["w_out_b"] = _matmul(tag + "_dwout_b", s["yb"], dh16, TN, BF16)
    du, dg, dvn, g["sgu_w"], g["sgu_b"] = _sgu_bwd(tag + "_sgu_b", proj["u"], proj["g"], s["vn"],
                                                   p["sgu_w"], p["sgu_b"], dyb)
    dv, g["sgu_ln_w"], g["sgu_ln_b"] = _rows_bwd(tag + "_sgunorm_b", _sgu_norm, [proj["v"]],
                                                 [p["sgu_ln_w"], p["sgu_ln_b"]], [dvn], [BF16], ROW_TM // 2)
    dy_ssd, dz, g["ssd_norm_w"] = _rows_bwd(tag + "_ssdgate_b", _ssd_gate, [s["y_ssd"], proj["z"]],
                                            [p["ssd_norm_w"]], [dya], [F32, BF16], ROW_TM // 2)
    (dxs, dbm, dcm, ddt, g["dt_bias"], g["a_log"], g["d_skip"]), got = _ssd_bwd(
        tag + "_ssd_b", s["xs"], s["bm"], s["cm"], proj["dt"], p["dt_bias"], p["a_log"], p["d_skip"],
        s["states"], dy_ssd, None if make_rider is None else make_rider(g))
    dxbc, g["conv_w"], g["conv_b"] = _conv_bwd(tag + "_conv_b", proj["xbc"], s["pre"], p["conv_w"], dxs, dbm, dcm)
    dproj = dict(z=dz, xbc=dxbc, dt=ddt.astype(BF16), g=dg, u=du, v=dv)
    dhn = None
    g["w_in"] = {}
    for nm, _ in EVEN_SEGS:
        dhn = _matmul(f"{tag}_dhn_{nm}", dproj[nm], p["w_in"][nm], NT, F32, add=dhn)
        g["w_in"][nm] = _matmul(f"{tag}_dwin_{nm}", s["hn"], dproj[nm], TN, BF16)
    dh_in, g["norm_w"] = _rows_bwd(tag + "_norm_b", _rms_fn, [s["h"]], [p["norm_w"]], [dhn], [F32],
                                   ROW_TM // 2, add=(0, dh))
    return dh_in, g, got


def _odd_fwd(tag, h, p, rider=None):
    hn, = _rows_fwd(tag + "_norm", _rms_fn, [h], [p["norm_w"]], [(D_MODEL, BF16)], ROW_TM)
    q = _matmul(tag + "_in_q", hn, p["w_in"]["q"], NN, BF16)
    k = _matmul(tag + "_in_k", hn, p["w_in"]["k"], NN, BF16)
    v = _matmul(tag + "_in_v", hn, p["w_in"]["v"], NN, BF16)
    gate = _matmul(tag + "_in_g", hn, p["w_in"]["g"], NN, F32)
    (y, tot), got = _attn_fwd(tag + "_attn", q, k, v, rider)
    yg, = _rows_fwd(tag + "_gate", _attn_gate, [y, gate], [], [(D_MODEL, BF16)], ROW_TM)
    h1 = _matmul(tag + "_out", yg, p["w_out"], NN, F32, add=h)
    return h1, dict(h=h, hn=hn, q=q, k=k, v=v, gate=gate, y=y, tot=tot, yg=yg), got


def _odd_bwd(tag, dh, s, p, rider=None):
    g = {}
    dh16 = dh.astype(BF16)
    dyg = _matmul(tag + "_dyg", dh16, p["w_out"], NT, F32)
    g["w_out"] = _matmul(tag + "_dwout", s["yg"], dh16, TN, BF16)
    dy, dgate = _rows_bwd(tag + "_gate_b", _attn_gate, [s["y"], s["gate"]], [], [dyg], [F32, BF16], ROW_TM)
    (dq, dk, dv), got = _attn_bwd(tag + "_attn_b", s["q"], s["k"], s["v"], s["tot"], dy, rider)
    dproj = dict(q=dq, k=dk, v=dv, g=dgate)
    dhn = None
    g["w_in"] = {}
    for nm, _ in ODD_SEGS:
        dhn = _matmul(f"{tag}_dhn_{nm}", dproj[nm], p["w_in"][nm], NT, F32, add=dhn)
        g["w_in"][nm] = _matmul(f"{tag}_dwin_{nm}", s["hn"], dproj[nm], TN, BF16)
    dh_in, g["norm_w"] = _rows_bwd(tag + "_norm_b", _rms_fn, [s["h"]], [p["norm_w"]], [dhn], [F32],
                                   ROW_TM // 2, add=(0, dh))
    return dh_in, g, got


BIG = ("ev_w_in", "ev_w_out", "od_w_in", "od_w_out")
SMALL = ("norm_w", "final_norm_w", "ev_conv_b", "ev_dt_bias", "ev_a_log", "ev_d_skip", "ev_ssd_norm_w",
         "ev_sgu_ln_w", "ev_sgu_ln_b", "ev_sgu_w", "ev_sgu_b")
WEIGHTS = ("norm_w", "final_norm_w", "ev_w_in", "ev_conv_w", "ev_conv_b", "ev_dt_bias", "ev_a_log", "ev_d_skip",
           "ev_ssd_norm_w", "ev_sgu_ln_w", "ev_sgu_ln_b", "ev_sgu_w", "ev_sgu_b", "ev_w_out", "od_w_in", "od_w_out")


def _step(w, m, v, x, loss_target):
    h = x[0]
    tgt = loss_target[0]
    n_even, n_odd = w["ev_w_in"].shape[0], w["od_w_in"].shape[0]
    depth = n_even + n_odd
    assert (n_even, n_odd) == (2, 2), "the exchange schedule below is written for the four-layer trunk"

    def shard(n, i):
        return w[n][i].astype(BF16)

    def cols(gathered):
        return jnp.moveaxis(gathered, 0, 1).reshape(gathered.shape[1], -1)

    def rows(gathered):
        return gathered.reshape(-1, gathered.shape[2])

    def by_owner_cols(full):
        kk, n8 = full.shape
        return jnp.moveaxis(full.reshape(kk, N_DEV, n8 // N_DEV), 1, 0)

    def by_owner_rows(full):
        return full.reshape(N_DEV, full.shape[0] // N_DEV, full.shape[1])

    def even_params(layer, ev_in):
        i = layer // 2
        w_in = _split_cols(cols(ev_in), EVEN_SEGS)
        w_in["dt"] = jnp.pad(w_in["dt"], ((0, 0), (0, LANES - SSD_HEADS)))
        return dict(norm_w=w["norm_w"][layer][None], w_in=w_in, conv_w=conv_w[i], conv_b=w["ev_conv_b"][i][None],
                    dt_bias=_pad_lanes(w["ev_dt_bias"][i]), a_log=_pad_lanes(w["ev_a_log"][i]),
                    d_skip=_pad_lanes(w["ev_d_skip"][i]), ssd_norm_w=w["ev_ssd_norm_w"][i][None],
                    sgu_ln_w=w["ev_sgu_ln_w"][i][None], sgu_ln_b=w["ev_sgu_ln_b"][i][None],
                    sgu_w=w["ev_sgu_w"][i], sgu_b=w["ev_sgu_b"][i][:, :, None])

    def even_out_params(ev_out):
        full = rows(ev_out)
        return dict(w_out_a=full[:D_MODEL], w_out_b=full[D_MODEL:])

    def odd_params(layer, od_in, od_out):
        return dict(norm_w=w["norm_w"][layer][None], w_in=_split_cols(cols(od_in), ODD_SEGS), w_out=rows(od_out))

    def even_in_grads(g):
        return by_owner_cols(jnp.concatenate([g["w_in"][nm][:, :n] for nm, n in EVEN_SEGS], axis=1))

    def even_out_grads(g):
        return by_owner_rows(jnp.concatenate([g["w_out_a"], g["w_out_b"]], axis=0))

    def odd_grads(g):
        return [by_owner_cols(jnp.concatenate([g["w_in"][nm] for nm, _ in ODD_SEGS], axis=1)),
                by_owner_rows(g["w_out"])]

    ev_in0, conv_w = _exchange("gather_first", [shard("ev_w_in", 0), w["ev_conv_w"]], gather=True)
    conv_w = jnp.moveaxis(conv_w, 0, 2).reshape(n_even, CONV_WIDTH, CONV_DIM)
    h, s0, p0, got = _even_fwd(
        "l0", h, even_params(0, ev_in0),
        _Rider([shard("ev_w_out", 0), shard("od_w_in", 0), shard("od_w_out", 0)], True),
        late=lambda arrived: even_out_params(arrived[0]))
    p1 = odd_params(1, got[1], got[2])
    h, s1, got = _odd_fwd("l1", h, p1, _Rider([shard("ev_w_in", 1), shard("ev_w_out", 1)], True))
    p2 = dict(even_params(2, got[0]), **even_out_params(got[1]))
    h, s2, p2, got = _even_fwd("l2", h, p2, _Rider([shard("od_w_in", 1), shard("od_w_out", 1)], True))
    p3 = odd_params(3, got[0], got[1])
    h, s3, _ = _odd_fwd("l3", h, p3)

    dh, d_final, loss_part = _loss_head("loss_head", h, tgt, w["final_norm_w"][None], ROW_TM // 2)
    loss = lax.psum(loss_part[0, 0], ("x", "y", "c"))

    dh, g3, _ = _odd_bwd("l3", dh, s3, p3)
    dh, g2, from3 = _even_bwd("l2", dh, s2, p2, lambda g: _Rider(odd_grads(g3), False))
    dh, g1, from2 = _odd_bwd("l1", dh, s1, p1, _Rider([even_in_grads(g2), even_out_grads(g2)], False))
    dh, g0, from1 = _even_bwd("l0", dh, s0, p0, lambda g: _Rider(odd_grads(g1) + [even_out_grads(g)], False))
    from0, = _exchange("scatter_last", [even_in_grads(g0)], gather=False)
    grad_x = dh[None]
    lg = [g0, g1, g2, g3]
    ev, od = [g0, g2], [g1, g3]

    def total(tag, parts):
        return _sum_parts("sum_" + tag, parts, 256)

    big_grads = {
        "ev_w_in": jnp.stack([total("ev_in0", from0), total("ev_in1", from2[0])]),
        "ev_w_out": jnp.stack([total("ev_out0", from1[2]), total("ev_out1", from2[1])]),
        "od_w_in": jnp.stack([total("od_in0", from1[0]), total("od_in1", from3[0])]),
        "od_w_out": jnp.stack([total("od_out0", from1[1]), total("od_out1", from3[1])]),
    }

    small_g = {
        "norm_w": jnp.concatenate([lg[l]["norm_w"] for l in range(depth)], axis=0),
        "final_norm_w": d_final[0],
        "ev_conv_b": jnp.concatenate([e["conv_b"] for e in ev], axis=0),
        "ev_dt_bias": jnp.concatenate([e["dt_bias"][:, :SSD_HEADS] for e in ev], axis=0),
        "ev_a_log": jnp.concatenate([e["a_log"][:, :SSD_HEADS] for e in ev], axis=0),
        "ev_d_skip": jnp.concatenate([e["d_skip"][:, :SSD_HEADS] for e in ev], axis=0),
        "ev_ssd_norm_w": jnp.concatenate([e["ssd_norm_w"] for e in ev], axis=0),
        "ev_sgu_ln_w": jnp.concatenate([e["sgu_ln_w"] for e in ev], axis=0),
        "ev_sgu_ln_b": jnp.concatenate([e["sgu_ln_b"] for e in ev], axis=0),
        "ev_sgu_w": jnp.stack([e["sgu_w"] for e in ev]),
        "ev_sgu_b": jnp.stack([e["sgu_b"][:, :, 0] for e in ev]),
    }
    conv_g = jnp.stack([e["conv_w"] for e in ev])
    small_shapes = [w[n].shape for n in SMALL]
    small_parts, = _exchange("gather_small", [_pack([small_g[n] for n in SMALL] + [conv_g], F32)], gather=True)
    small_sum = _sum_parts("sum_small", small_parts, 1024)
    *small_list, conv_full = _unpack(small_sum, small_shapes + [conv_g.shape])
    grads = dict(zip(SMALL, small_list))
    grads.update(big_grads)
    me = 4 * lax.axis_index("x") + 2 * lax.axis_index("y") + lax.axis_index("c")
    n_cv = w["ev_conv_w"].shape[2]
    grads["ev_conv_w"] = lax.dynamic_slice_in_dim(conv_full, me * n_cv, n_cv, axis=2)

    deltas, new_m, new_v = {}, {}, {}
    for n in BIG + ("ev_conv_w",):
        shp = w[n].shape
        two_d = (-1, shp[-1])
        d_, m_, v_ = _adamw("adamw_" + n, w[n].reshape(two_d), grads[n].reshape(two_d), m[n].reshape(two_d),
                            v[n].reshape(two_d), 256)
        deltas[n], new_m[n], new_v[n] = d_.reshape(shp), m_.reshape(shp), v_.reshape(shp)
    packs = [_pack([src[n] for n in SMALL], F32) for src in (w, grads, m, v)]
    outs = _adamw("adamw_small", *packs, 1024)
    for dst, packed in zip((deltas, new_m, new_v), outs):
        dst.update(zip(SMALL, _unpack(packed, small_shapes)))
    return loss, grad_x, grads, deltas, new_m, new_v


def kernel(x, norm_w, final_norm_w, ev_w_in, ev_conv_w, ev_conv_b, ev_dt_bias, ev_a_log, ev_d_skip, ev_ssd_norm_w, ev_sgu_ln_w, ev_sgu_ln_b, ev_sgu_w, ev_sgu_b, ev_w_out, od_w_in, od_w_out, loss_target, m_norm_w, m_final_norm_w, m_ev_w_in, m_ev_conv_w, m_ev_conv_b, m_ev_dt_bias, m_ev_a_log, m_ev_d_skip, m_ev_ssd_norm_w, m_ev_sgu_ln_w, m_ev_sgu_ln_b, m_ev_sgu_w, m_ev_sgu_b, m_ev_w_out, m_od_w_in, m_od_w_out, v_norm_w, v_final_norm_w, v_ev_w_in, v_ev_conv_w, v_ev_conv_b, v_ev_dt_bias, v_ev_a_log, v_ev_d_skip, v_ev_ssd_norm_w, v_ev_sgu_ln_w, v_ev_sgu_ln_b, v_ev_sgu_w, v_ev_sgu_b, v_ev_w_out, v_od_w_in, v_od_w_out):
    w = dict(zip(WEIGHTS, (norm_w, final_norm_w, ev_w_in, ev_conv_w, ev_conv_b, ev_dt_bias, ev_a_log, ev_d_skip,
                           ev_ssd_norm_w, ev_sgu_ln_w, ev_sgu_ln_b, ev_sgu_w, ev_sgu_b, ev_w_out, od_w_in, od_w_out)))
    m = dict(zip(WEIGHTS, (m_norm_w, m_final_norm_w, m_ev_w_in, m_ev_conv_w, m_ev_conv_b, m_ev_dt_bias, m_ev_a_log,
                           m_ev_d_skip, m_ev_ssd_norm_w, m_ev_sgu_ln_w, m_ev_sgu_ln_b, m_ev_sgu_w, m_ev_sgu_b,
                           m_ev_w_out, m_od_w_in, m_od_w_out)))
    v = dict(zip(WEIGHTS, (v_norm_w, v_final_norm_w, v_ev_w_in, v_ev_conv_w, v_ev_conv_b, v_ev_dt_bias, v_ev_a_log,
                           v_ev_d_skip, v_ev_ssd_norm_w, v_ev_sgu_ln_w, v_ev_sgu_ln_b, v_ev_sgu_w, v_ev_sgu_b,
                           v_ev_w_out, v_od_w_in, v_od_w_out)))
    loss, grad_x, grads, deltas, new_m, new_v = _step(w, m, v, x, loss_target)
    return (loss, grad_x, *[grads[n] for n in WEIGHTS], *[deltas[n] for n in WEIGHTS],
            *[new_m[n] for n in WEIGHTS], *[new_v[n] for n in WEIGHTS])
```

```python
import jax
import jax.numpy as jnp
from jax import lax
from jax.experimental import pallas as pl
from jax.experimental.pallas import tpu as pltpu

F32, BF16 = jnp.float32, jnp.bfloat16

D_MODEL = 2048
SSD_HEADS = 32
SSD_HEAD_DIM = 64
SSD_GROUPS = 4
SSD_STATE = 128
CHUNK = 128
CONV_WIDTH = 4
CONV_DIM = D_MODEL + 2 * SSD_GROUPS * SSD_STATE
SGU_GROUPS = 16
SB_HEADS = 16
LANES = 128
N_PAIRS = SSD_HEADS // 2
PAIRS_PER_GROUP = N_PAIRS // SSD_GROUPS
NORM_EPS = 1e-5
N_DEV = 8

ADAM_LR, ADAM_B1, ADAM_B2, ADAM_EPS, ADAM_WD, ADAM_STEP = 0.001, 0.9, 0.999, 1e-08, 0.01, 10

VMEM_LIMIT_BYTES = 48 * 1024 * 1024

NN = ((1,), (0,))
NT = ((1,), (1,))
TN = ((0,), (0,))


def _cparams(*sem):
    return pltpu.CompilerParams(dimension_semantics=sem, vmem_limit_bytes=VMEM_LIMIT_BYTES)


def _dg(a, b, dims):
    return lax.dot_general(a, b, (dims, ((), ())), preferred_element_type=F32)


def _make_bdot(dims):
    @jax.custom_vjp
    def f(a, b):
        return _dg(a.astype(BF16), b.astype(BF16), dims)

    def fwd(a, b):
        return f(a, b), (a, b)

    def bwd(res, g):
        a, b = res
        a16, b16, g16 = a.astype(BF16), b.astype(BF16), g.astype(BF16)
        if dims == NN:
            da, db = _dg(g16, b16, NT), _dg(a16, g16, TN)
        elif dims == NT:
            da, db = _dg(g16, b16, NN), _dg(g16, a16, TN)
        else:
            da, db = _dg(b16, g16, NT), _dg(a16, g16, NN)
        return da.astype(a.dtype), db.astype(b.dtype)

    f.defvjp(fwd, bwd)
    return f


_bdot_nn, _bdot_nt, _bdot_tn = _make_bdot(NN), _make_bdot(NT), _make_bdot(TN)


def _iota2(shape, axis):
    return lax.broadcasted_iota(jnp.int32, shape, axis)


def _split3(x, axis):
    hi = x.astype(BF16)
    r = x - hi.astype(F32)
    mid = r.astype(BF16)
    return jnp.concatenate([hi, mid, (r - mid.astype(F32)).astype(BF16)], axis=axis)


def _make_onehot_dot(build, build_t, left):
    def apply(x, e):
        e = e.astype(BF16)
        if left:
            return _dg(jnp.concatenate([e, e, e], axis=1), _split3(x, 0), NN)
        return _dg(_split3(x, 1), jnp.concatenate([e, e, e], axis=0), NN)

    @jax.custom_vjp
    def f(x):
        return apply(x, build())

    f.defvjp(lambda x: (f(x), None), lambda _, g: (apply(g, build_t()),))
    return f


def _eq_div(shape, axis_div, axis_eq, div):
    return _iota2(shape, axis_div) // div == _iota2(shape, axis_eq)


N_CH = SSD_HEADS * SSD_HEAD_DIM
_to_channels = _make_onehot_dot(lambda: _eq_div((LANES, N_CH), 1, 0, SSD_HEAD_DIM),
                                lambda: _eq_div((N_CH, LANES), 0, 1, SSD_HEAD_DIM), False)
_to_head_blocks = _make_onehot_dot(lambda: _eq_div((LANES, SSD_HEADS * LANES), 1, 0, LANES),
                                   lambda: _eq_div((SSD_HEADS * LANES, LANES), 0, 1, LANES), False)
_cumsum_rows = _make_onehot_dot(lambda: _iota2((CHUNK, CHUNK), 1) <= _iota2((CHUNK, CHUNK), 0),
                                lambda: _iota2((CHUNK, CHUNK), 0) <= _iota2((CHUNK, CHUNK), 1), True)


def _softplus(x):
    return jnp.maximum(x, 0.0) + jnp.log1p(jnp.exp(-jnp.abs(x)))


def _silu(x):
    return x * jax.nn.sigmoid(x)


def _gelu(x):
    return 0.5 * x * (1.0 + jnp.tanh(0.7978845608028654 * (x + 0.044715 * (x * x * x))))


def _rms(x, w):
    return x * lax.rsqrt(jnp.mean(x * x, axis=-1, keepdims=True) + NORM_EPS) * w


def _layer_norm(x, w, b):
    xc = x - jnp.mean(x, axis=-1, keepdims=True)
    return xc * lax.rsqrt(jnp.mean(xc * xc, axis=-1, keepdims=True) + NORM_EPS) * w + b


def _row_spec(width, tm):
    return pl.BlockSpec((tm, width), lambda i: (i, 0))


def _full_spec(p):
    zeros = (0,) * p.ndim
    return pl.BlockSpec(p.shape, lambda i: zeros)


def _rows_fwd(name, fn, tiled, params, outs, tm):
    n_rows = tiled[0].shape[0]
    n_in = len(tiled) + len(params)

    def body(*refs):
        res = fn(*[r[...] for r in refs[:n_in]])
        for o_ref, o in zip(refs[n_in:], res):
            o_ref[...] = o.astype(o_ref.dtype)

    return pl.pallas_call(
        body, name=name, grid=(n_rows // tm,),
        in_specs=[_row_spec(a.shape[1], tm) for a in tiled] + [_full_spec(p) for p in params],
        out_specs=[_row_spec(w, tm) for w, _ in outs],
        out_shape=[jax.ShapeDtypeStruct((n_rows, w), d) for w, d in outs],
        compiler_params=_cparams("parallel"),
    )(*tiled, *params)


def _rows_bwd(name, fn, tiled, params, cots, grad_dtypes, tm, add=None):
    n_rows = tiled[0].shape[0]
    nt, npar, nc = len(tiled), len(params), len(cots)
    want = [k for k, d in enumerate(grad_dtypes) if d is not None]
    n_add = 0 if add is None else 1

    def body(*refs):
        ins = [r[...] for r in refs[:nt + npar]]
        c_refs = refs[nt + npar:nt + npar + nc]
        add_refs = refs[nt + npar + nc:nt + npar + nc + n_add]
        o_refs = refs[nt + npar + nc + n_add:]
        res, vjp = jax.vjp(fn, *ins)
        grads = vjp(tuple(c[...].astype(r.dtype) for c, r in zip(c_refs, res)))
        for pos, k in enumerate(want):
            gk = grads[k]
            if add is not None and add[0] == pos:
                gk = gk + add_refs[0][...]
            o_refs[pos][...] = gk.astype(o_refs[pos].dtype)
        p_refs = o_refs[len(want):]

        @pl.when(pl.program_id(0) == 0)
        def _():
            for r in p_refs:
                r[...] = jnp.zeros_like(r)

        for r, gp in zip(p_refs, grads[nt:]):
            r[...] += gp

    add_arrays = [] if add is None else [add[1]]
    out = pl.pallas_call(
        body, name=name, grid=(n_rows // tm,),
        in_specs=([_row_spec(a.shape[1], tm) for a in tiled] + [_full_spec(p) for p in params]
                  + [_row_spec(c.shape[1], tm) for c in cots] + [_row_spec(a.shape[1], tm) for a in add_arrays]),
        out_specs=([_row_spec(tiled[k].shape[1], tm) for k in want] + [_full_spec(p) for p in params]),
        out_shape=([jax.ShapeDtypeStruct(tiled[k].shape, grad_dtypes[k]) for k in want]
                   + [jax.ShapeDtypeStruct(p.shape, F32) for p in params]),
        compiler_params=_cparams("arbitrary"),
    )(*tiled, *params, *cots, *add_arrays)
    return out


def _matmul(name, a, b, dims, out_dtype, add=None, tm=1024, tn=1024, tk=2048):
    if dims == NN:
        (m, k), n = a.shape, b.shape[1]
    elif dims == NT:
        (m, k), n = a.shape, b.shape[0]
    else:
        (k, m), n = a.shape, b.shape[1]
    tm, tn, tk = min(tm, m), min(tn, n), min(tk, k)
    while k % tk:
        tk -= LANES
    assert m % tm == 0 and n % tn == 0 and k % tk == 0, (name, m, n, k)
    nk = k // tk
    a_spec = (pl.BlockSpec((tk, tm), lambda i, j, kk: (kk, i)) if dims == TN
              else pl.BlockSpec((tm, tk), lambda i, j, kk: (i, kk)))
    b_spec = (pl.BlockSpec((tn, tk), lambda i, j, kk: (j, kk)) if dims == NT
              else pl.BlockSpec((tk, tn), lambda i, j, kk: (kk, j)))
    o_spec = pl.BlockSpec((tm, tn), lambda i, j, kk: (i, j))
    has_add = add is not None

    def body(*refs):
        a_ref, b_ref = refs[0], refs[1]
        part = _dg(a_ref[...].astype(BF16), b_ref[...].astype(BF16), dims)
        if nk == 1:
            o_ref = refs[-1]
            if has_add:
                part = part + refs[2][...]
            o_ref[...] = part.astype(o_ref.dtype)
            return
        o_ref, acc = refs[-2], refs[-1]
        kk = pl.program_id(2)

        @pl.when(kk == 0)
        def _():
            acc[...] = part

        @pl.when(kk > 0)
        def _():
            acc[...] += part

        @pl.when(kk == nk - 1)
        def _():
            r = acc[...]
            if has_add:
                r = r + refs[2][...]
            o_ref[...] = r.astype(o_ref.dtype)

    return pl.pallas_call(
        body, name=name, grid=(m // tm, n // tn, nk),
        in_specs=[a_spec, b_spec] + ([o_spec] if has_add else []),
        out_specs=o_spec,
        out_shape=jax.ShapeDtypeStruct((m, n), out_dtype),
        scratch_shapes=[pltpu.VMEM((tm, tn), F32)] if nk > 1 else [],
        compiler_params=_cparams("parallel", "parallel", "arbitrary"),
    )(a, b, *([add] if has_add else []))


CONV_TM = 256
HALO = 8


def _shift_down(x, halo, j):
    if j == 0:
        return x, x[:HALO]
    xr = pltpu.roll(x, j, 0)
    hr = pltpu.roll(halo, j, 0)
    top = jnp.where(_iota2((HALO, x.shape[1]), 0) < j, hr, xr[:HALO])
    return xr, top


def _conv_fwd(name, x, w, b):
    t, c = x.shape
    tm = min(CONV_TM, t)
    hb = tm // HALO

    def body(x_ref, halo_ref, w_ref, b_ref, pre_ref, xs_ref, bm_ref, cm_ref):
        i = pl.program_id(0)
        xv = x_ref[...]
        halo = jnp.where(i > 0, halo_ref[...], 0.0)
        main = jnp.zeros_like(xv) + b_ref[...]
        top = jnp.zeros((HALO, c), F32) + b_ref[...]
        for kk in range(CONV_WIDTH):
            xr, tp = _shift_down(xv, halo, CONV_WIDTH - 1 - kk)
            main = main + w_ref[kk:kk + 1, :] * xr
            top = top + w_ref[kk:kk + 1, :] * tp
        pre = jnp.concatenate([top, main[HALO:]], axis=0)
        pre_ref[...] = pre
        act = _silu(pre)
        xs_ref[...] = act[:, :D_MODEL]
        bm_ref[...] = act[:, D_MODEL:D_MODEL + SSD_GROUPS * SSD_STATE]
        cm_ref[...] = act[:, D_MODEL + SSD_GROUPS * SSD_STATE:]

    gs = SSD_GROUPS * SSD_STATE
    return pl.pallas_call(
        body, name=name, grid=(t // tm,),
        in_specs=[_row_spec(c, tm),
                  pl.BlockSpec((HALO, c), lambda i: (jnp.maximum(i * hb - 1, 0), 0)),
                  _full_spec(w), _full_spec(b)],
        out_specs=[_row_spec(c, tm), _row_spec(D_MODEL, tm), _row_spec(gs, tm), _row_spec(gs, tm)],
        out_shape=[jax.ShapeDtypeStruct((t, c), F32), jax.ShapeDtypeStruct((t, D_MODEL), F32),
                   jax.ShapeDtypeStruct((t, gs), F32), jax.ShapeDtypeStruct((t, gs), F32)],
        compiler_params=_cparams("parallel"),
    )(x, x, w, b)


def _dsilu(pre, dact):
    s = jax.nn.sigmoid(pre)
    return dact * (s * (1.0 + pre * (1.0 - s)))


def _conv_bwd(name, x, pre, w, dxs, dbm, dcm):
    t, c = x.shape
    tm = min(CONV_TM, t)
    hb = tm // HALO
    last_hb = t // HALO - 1
    n_tiles = t // tm

    def body(x_ref, xh_ref, pre_ref, preh_ref, w_ref, dxs_ref, dbm_ref, dcm_ref,
             dxsh_ref, dbmh_ref, dcmh_ref, dx_ref, dw_ref, db_ref):
        i = pl.program_id(0)
        dact = jnp.concatenate([dxs_ref[...], dbm_ref[...], dcm_ref[...]], axis=1)
        dpre = _dsilu(pre_ref[...], dact)
        dact_h = jnp.concatenate([dxsh_ref[...], dbmh_ref[...], dcmh_ref[...]], axis=1)
        dpre_h = jnp.where(i < n_tiles - 1, _dsilu(preh_ref[...], dact_h), 0.0)
        xv = x_ref[...]
        xh = jnp.where(i > 0, xh_ref[...], 0.0)

        @pl.when(i == 0)
        def _():
            dw_ref[...] = jnp.zeros_like(dw_ref)
            db_ref[...] = jnp.zeros_like(db_ref)

        db_ref[...] += jnp.sum(dpre, axis=0, keepdims=True)
        dxm = jnp.zeros_like(xv)
        dxt = jnp.zeros((HALO, c), F32)
        row8 = _iota2((HALO, c), 0)
        for kk in range(CONV_WIDTH):
            j = CONV_WIDTH - 1 - kk
            wk = w_ref[kk:kk + 1, :]
            xr, tp = _shift_down(xv, xh, j)
            full = jnp.sum(dpre * xr, axis=0, keepdims=True)
            fix = jnp.sum(dpre[:HALO] * (tp - xr[:HALO]), axis=0, keepdims=True)
            dw_ref[kk:kk + 1, :] += full + fix
            if j == 0:
                dxm = dxm + wk * dpre
                dxt = dxt + wk * dpre[tm - HALO:]
            else:
                dr = pltpu.roll(dpre, tm - j, 0)
                hr = pltpu.roll(dpre_h, HALO - j, 0)
                dxm = dxm + wk * dr
                dxt = dxt + wk * jnp.where(row8 >= HALO - j, hr, dr[tm - HALO:])
        dx_ref[...] = jnp.concatenate([dxm[:tm - HALO], dxt], axis=0).astype(dx_ref.dtype)

    gs = SSD_GROUPS * SSD_STATE
    prev_halo = lambda i: (jnp.maximum(i * hb - 1, 0), 0)
    next_halo = lambda i: (jnp.minimum((i + 1) * hb, last_hb), 0)
    return pl.pallas_call(
        body, name=name, grid=(n_tiles,),
        in_specs=[_row_spec(c, tm), pl.BlockSpec((HALO, c), prev_halo),
                  _row_spec(c, tm), pl.BlockSpec((HALO, c), next_halo), _full_spec(w),
                  _row_spec(D_MODEL, tm), _row_spec(gs, tm), _row_spec(gs, tm),
                  pl.BlockSpec((HALO, D_MODEL), next_halo), pl.BlockSpec((HALO, gs), next_halo),
                  pl.BlockSpec((HALO, gs), next_halo)],
        out_specs=[_row_spec(c, tm), _full_spec(w), pl.BlockSpec((1, c), lambda i: (0, 0))],
        out_shape=[jax.ShapeDtypeStruct((t, c), BF16), jax.ShapeDtypeStruct(w.shape, F32),
                   jax.ShapeDtypeStruct((1, c), F32)],
        compiler_params=_cparams("arbitrary"),
    )(x, x, pre, pre, w, dxs, dbm, dcm, dxs, dbm, dcm)


def _ssd_stage1(dt_raw, dt_bias, a_log, d_skip8):
    dt = _softplus(dt_raw + dt_bias)
    cs = _cumsum_rows(dt * (-jnp.exp(a_log)))
    return _to_channels(dt), _to_channels(cs), _to_head_blocks(cs), _to_channels(d_skip8)


def _ssd_pair(xs, dtf, csf, cs_last, a0, a1, bg, cg, prev, dskf8):
    l = xs.shape[0]
    xc = xs * dtf
    scores = _bdot_nt(cg, bg)
    causal = _iota2((l, l), 0) >= _iota2((l, l), 1)

    def decay(a):
        return jnp.where(causal, jnp.exp(jnp.where(causal, a - a.T, 0.0)), 0.0)

    first = _iota2((l, LANES), 1) < SSD_HEAD_DIM
    y_diag = (_bdot_nn(scores * decay(a0), jnp.where(first, xc, 0.0))
              + _bdot_nn(scores * decay(a1), jnp.where(first, 0.0, xc)))
    states = _bdot_tn(bg, xc * jnp.exp(cs_last - csf))
    new_state = jnp.exp(cs_last) * prev + states
    y_off = _bdot_nn(cg, prev) * jnp.exp(csf)
    y = y_diag + y_off + xs * jnp.mean(dskf8, axis=0, keepdims=True)
    return y, new_state


def _ssd_specs(nc_rev=None):
    def ch(c):
        return c if nc_rev is None else nc_rev - 1 - c

    gs = SSD_GROUPS * SSD_STATE
    return dict(
        wide=pl.BlockSpec((CHUNK, D_MODEL), lambda c: (ch(c), 0)),
        group=pl.BlockSpec((CHUNK, gs), lambda c: (ch(c), 0)),
        chunk=pl.BlockSpec((CHUNK, LANES), lambda c: (ch(c), 0)),
        vec=pl.BlockSpec((1, LANES), lambda c: (0, 0)),
        state=pl.BlockSpec((1, N_PAIRS, LANES, LANES), lambda c: (ch(c), 0, 0, 0)),
    )


def _ssd_scratch():
    return [pltpu.VMEM((CHUNK, D_MODEL), F32), pltpu.VMEM((CHUNK, D_MODEL), F32),
            pltpu.VMEM((CHUNK, SSD_HEADS * LANES), F32), pltpu.VMEM((8, D_MODEL), F32)]


def _pair_slices(p):
    g = p // PAIRS_PER_GROUP
    return (slice(p * LANES, (p + 1) * LANES), slice(g * LANES, (g + 1) * LANES),
            slice(2 * p * LANES, (2 * p + 1) * LANES), slice((2 * p + 1) * LANES, (2 * p + 2) * LANES))


def _pair_inputs(p, xs_ref, b_ref, c_ref, prev, stage1):
    dtf_s, csf_s, csa_s, dsk_s = stage1
    sl, gsl, h0, h1 = _pair_slices(p)
    return (xs_ref[:, sl], dtf_s[:, sl], csf_s[:, sl], csf_s[CHUNK - 1:CHUNK, sl], csa_s[:, h0], csa_s[:, h1],
            b_ref[:, gsl], c_ref[:, gsl], prev, dsk_s[:, sl])


def _ssd_fwd(name, xs, bm, cm, dt_raw, dt_bias, a_log, d_skip, rider=None):
    t = xs.shape[0]
    nc = t // CHUNK

    def body(xs_ref, b_ref, c_ref, dt_ref, bias_ref, alog_ref, dsk_ref, y_ref, prev_ref,
             state, dtf_s, csf_s, csa_s, dsk_s):
        c = pl.program_id(0)
        stage1 = (dtf_s, csf_s, csa_s, dsk_s)
        d8 = jnp.broadcast_to(dsk_ref[...], (8, LANES))
        for ref, val in zip(stage1, _ssd_stage1(dt_ref[...], bias_ref[...], alog_ref[...], d8)):
            ref[...] = val

        @pl.when(c == 0)
        def _():
            state[...] = jnp.zeros_like(state)

        for p in range(N_PAIRS):
            prev = state[p]
            prev_ref[0, p] = prev
            y, new_state = _ssd_pair(*_pair_inputs(p, xs_ref, b_ref, c_ref, prev, stage1))
            y_ref[:, _pair_slices(p)[0]] = y
            state[p] = new_state

    sp = _ssd_specs()
    return _hosted(
        body, rider, name=name, grid=(nc,),
        in_specs=[sp["wide"], sp["group"], sp["group"], sp["chunk"], sp["vec"], sp["vec"], sp["vec"]],
        out_specs=[sp["wide"], sp["state"]],
        out_shape=[jax.ShapeDtypeStruct((t, D_MODEL), F32),
                   jax.ShapeDtypeStruct((nc, N_PAIRS, LANES, LANES), F32)],
        scratch_shapes=[pltpu.VMEM((N_PAIRS, LANES, LANES), F32)] + _ssd_scratch(),
        args=(xs, bm, cm, dt_raw, dt_bias, a_log, d_skip))


def _ssd_bwd(name, xs, bm, cm, dt_raw, dt_bias, a_log, d_skip, prev_states, dy, rider=None):
    t = xs.shape[0]
    nc = t // CHUNK

    def body(xs_ref, b_ref, c_ref, dt_ref, bias_ref, alog_ref, dsk_ref, prev_ref, dy_ref,
             dxs_ref, db_ref, dc_ref, ddt_ref, dbias_ref, dalog_ref, ddsk_ref,
             dstate, dtf_s, csf_s, csa_s, dsk_s, g_dtf, g_csf, g_csa, g_dsk):
        c = pl.program_id(0)
        stage1 = (dtf_s, csf_s, csa_s, dsk_s)
        d8 = jnp.broadcast_to(dsk_ref[...], (8, LANES))
        vals, vjp1 = jax.vjp(_ssd_stage1, dt_ref[...], bias_ref[...], alog_ref[...], d8)
        for ref, val in zip(stage1, vals):
            ref[...] = val

        @pl.when(c == 0)
        def _():
            dstate[...] = jnp.zeros_like(dstate)
            dbias_ref[...] = jnp.zeros_like(dbias_ref)
            dalog_ref[...] = jnp.zeros_like(dalog_ref)
            ddsk_ref[...] = jnp.zeros_like(ddsk_ref)

        last_row = _iota2((CHUNK, LANES), 0) == CHUNK - 1
        for p in range(N_PAIRS):
            sl, gsl, h0, h1 = _pair_slices(p)
            _, vjp = jax.vjp(_ssd_pair, *_pair_inputs(p, xs_ref, b_ref, c_ref, prev_ref[0, p], stage1))
            dxs, ddtf, dcsf, dlast, da0, da1, dbg, dcg, dprev, ddsk8 = vjp((dy_ref[:, sl], dstate[p]))
            dxs_ref[:, sl] = dxs
            g_dtf[:, sl] = ddtf
            g_csf[:, sl] = dcsf + jnp.where(last_row, dlast, 0.0)
            g_csa[:, h0] = da0
            g_csa[:, h1] = da1
            g_dsk[:, sl] = ddsk8
            dstate[p] = dprev
            if p % PAIRS_PER_GROUP == 0:
                db_ref[:, gsl] = dbg
                dc_ref[:, gsl] = dcg
            else:
                db_ref[:, gsl] += dbg
                dc_ref[:, gsl] += dcg

        ddt, dbias, dalog, dd8 = vjp1((g_dtf[...], g_csf[...], g_csa[...], g_dsk[...]))
        ddt_ref[...] = ddt
        dbias_ref[...] += dbias
        dalog_ref[...] += dalog
        ddsk_ref[...] += jnp.sum(dd8, axis=0, keepdims=True)

    sp = _ssd_specs(nc)
    gs = SSD_GROUPS * SSD_STATE
    return _hosted(
        body, rider, name=name, grid=(nc,),
        in_specs=[sp["wide"], sp["group"], sp["group"], sp["chunk"], sp["vec"], sp["vec"], sp["vec"],
                  sp["state"], sp["wide"]],
        out_specs=[sp["wide"], sp["group"], sp["group"], sp["chunk"], sp["vec"], sp["vec"], sp["vec"]],
        out_shape=[jax.ShapeDtypeStruct((t, D_MODEL), F32), jax.ShapeDtypeStruct((t, gs), F32),
                   jax.ShapeDtypeStruct((t, gs), F32), jax.ShapeDtypeStruct((t, LANES), F32),
                   jax.ShapeDtypeStruct((1, LANES), F32), jax.ShapeDtypeStruct((1, LANES), F32),
                   jax.ShapeDtypeStruct((1, LANES), F32)],
        scratch_shapes=[pltpu.VMEM((N_PAIRS, LANES, LANES), F32)] + _ssd_scratch() + _ssd_scratch(),
        args=(xs, bm, cm, dt_raw, dt_bias, a_log, d_skip, prev_states, dy))


def _ssd_gate(y, z, w):
    return (_rms(y * _silu(z), w),)


def _sgu_norm(v, w, b):
    return (_layer_norm(_gelu(v), w, b),)


def _sgu_group(u, gate, vn, w, bcol):
    l = u.shape[0]
    wc = jnp.where(_iota2((l, l), 0) >= _iota2((l, l), 1), w, 0.0)
    return _gelu(u) * (_bdot_nn(wc, vn) + bcol) * _silu(gate)


def _sgu_fwd(name, u, gate, vn, w, bcol):
    t = u.shape[0]
    blk = _row_spec(D_MODEL, CHUNK)

    def body(u_ref, g_ref, vn_ref, w_ref, b_ref, y_ref):
        for g in range(SGU_GROUPS):
            sl = slice(g * LANES, (g + 1) * LANES)
            y_ref[:, sl] = _sgu_group(u_ref[:, sl], g_ref[:, sl], vn_ref[:, sl], w_ref[g], b_ref[g]).astype(y_ref.dtype)

    return pl.pallas_call(
        body, name=name, grid=(t // CHUNK,),
        in_specs=[blk, blk, blk, _full_spec(w), _full_spec(bcol)], out_specs=blk,
        out_shape=jax.ShapeDtypeStruct((t, D_MODEL), BF16),
        compiler_params=_cparams("parallel"),
    )(u, gate, vn, w, bcol)


def _sgu_bwd(name, u, gate, vn, w, bcol, dy):
    t = u.shape[0]
    blk = _row_spec(D_MODEL, CHUNK)

    def body(u_ref, g_ref, vn_ref, w_ref, b_ref, dy_ref, du_ref, dg_ref, dvn_ref, dw_ref, db_ref):
        @pl.when(pl.program_id(0) == 0)
        def _():
            dw_ref[...] = jnp.zeros_like(dw_ref)
            db_ref[...] = jnp.zeros_like(db_ref)

        for g in range(SGU_GROUPS):
            sl = slice(g * LANES, (g + 1) * LANES)
            _, vjp = jax.vjp(_sgu_group, u_ref[:, sl], g_ref[:, sl], vn_ref[:, sl], w_ref[g], b_ref[g])
            du, dg, dvn, dw, db = vjp(dy_ref[:, sl])
            du_ref[:, sl] = du.astype(du_ref.dtype)
            dg_ref[:, sl] = dg.astype(dg_ref.dtype)
            dvn_ref[:, sl] = dvn
            dw_ref[g] += dw
            db_ref[g] += db

    return pl.pallas_call(
        body, name=name, grid=(t // CHUNK,),
        in_specs=[blk, blk, blk, _full_spec(w), _full_spec(bcol), blk],
        out_specs=[blk, blk, blk, _full_spec(w), _full_spec(bcol)],
        out_shape=[jax.ShapeDtypeStruct((t, D_MODEL), BF16), jax.ShapeDtypeStruct((t, D_MODEL), BF16),
                   jax.ShapeDtypeStruct((t, D_MODEL), F32), jax.ShapeDtypeStruct(w.shape, F32),
                   jax.ShapeDtypeStruct(bcol.shape, F32)],
        compiler_params=_cparams("arbitrary"),
    )(u, gate, vn, w, bcol, dy)


SB_SCALE = LANES ** -0.5
LOG2_E = 1.4426950408889634
SB_TQ = 256
SB_TS = 512


def _keep(mask, x):
    return x if mask is None else jnp.where(mask, x, 0.0)


def _sb_pieces(z2, mask):
    tl = jnp.log(1.0 + jnp.exp2(-jnp.abs(z2))) * LOG2_E
    lk = _keep(mask, -(jnp.maximum(z2, 0.0) + tl))
    ls = jnp.minimum(z2, 0.0) - tl
    return lk, ls


def _split2(x):
    hi = x.astype(BF16)
    return jnp.concatenate([hi, (x - hi.astype(F32)).astype(BF16)], axis=1)


def _tri2(cmp):
    sq = (CHUNK, CHUNK)
    m = cmp(_iota2(sq, 0), _iota2(sq, 1)).astype(BF16)
    return jnp.concatenate([m, m], axis=0)


def _tri_sums(blocks, tri2):
    tq = blocks[0].shape[0]
    res = _dg(jnp.concatenate([_split2(b) for b in blocks], axis=0), tri2, NN)
    return [res[b * tq:(b + 1) * tq] for b in range(len(blocks))]


def _sb_tiles(t):
    tq, ts = min(SB_TQ, t), min(SB_TS, t)
    assert t % tq == 0 and t % ts == 0 and ts % tq == 0 and ts % CHUNK == 0
    return tq, ts, ts // CHUNK


def _sb_logits(qb, ks, off, q_off, masked):
    tq = qb.shape[0]
    z2 = _dg(qb, ks, NT) * (SB_SCALE * LOG2_E)
    out = []
    for b in range(ks.shape[0] // CHUNK):
        mask = None
        if masked:
            mask = (_iota2((tq, CHUNK), 1) + (off + b * CHUNK)) < (_iota2((tq, CHUNK), 0) + q_off)
        out.append(_sb_pieces(z2[:, b * CHUNK:(b + 1) * CHUNK], mask) + (mask,))
    return out


def _attn_fwd(name, q, k, v, rider=None):
    t = q.shape[0]
    tq, ts, nb = _sb_tiles(t)

    def body(q_ref, k_ref, v_ref, y_ref, tot_ref):
        i = pl.program_id(1)
        qb = q_ref[...]
        later2 = _tri2(lambda r, c: r > c)
        last = (i * tq + tq - 1) // ts

        def span(j, carry, masked):
            acc, after = carry
            off = pl.multiple_of(j * ts, ts)
            pieces = _sb_logits(qb, k_ref[pl.ds(off, ts), :], off, i * tq, masked)
            inside = _tri_sums([lk for lk, _, _ in pieces], later2)
            ws = [None] * nb
            for b in reversed(range(nb)):
                lk, ls, mask = pieces[b]
                ws[b] = _keep(mask, jnp.exp2(ls + inside[b] + after)).astype(BF16)
                after = after + jnp.sum(lk, axis=1, keepdims=True)
            acc = acc + _dg(jnp.concatenate(ws, axis=1), v_ref[pl.ds(off, ts), :], NN)
            return acc, after

        carry = span(last, (jnp.zeros((tq, LANES), F32), jnp.zeros((tq, 1), F32)), True)
        acc, tot = lax.fori_loop(0, last, lambda n, c: span(last - 1 - n, c, False), carry)
        y_ref[...] = acc
        tot_ref[...] = jnp.broadcast_to(tot, (tq, LANES))

    qsp = pl.BlockSpec((tq, LANES), lambda h, i: (i, h))
    kvsp = pl.BlockSpec((t, LANES), lambda h, i: (0, h))
    return _hosted(
        body, rider, name=name, grid=(SB_HEADS, t // tq),
        in_specs=[qsp, kvsp, kvsp], out_specs=[qsp, qsp],
        out_shape=[jax.ShapeDtypeStruct((t, D_MODEL), F32), jax.ShapeDtypeStruct((t, D_MODEL), F32)],
        scratch_shapes=[], args=(q, k, v))


def _attn_bwd(name, q, k, v, tot, dy, rider=None):
    t = q.shape[0]
    tq, ts, nb = _sb_tiles(t)
    nq, ns = t // tq, t // ts

    def body(q_ref, k_ref, v_ref, tot_ref, dy_ref, dq_ref, dk_ref, dv_ref, dkt_acc, dvt_acc):
        i = pl.program_id(1)

        @pl.when(i == 0)
        def _():
            dkt_acc[...] = jnp.zeros_like(dkt_acc)
            dvt_acc[...] = jnp.zeros_like(dvt_acc)

        qb = q_ref[...]
        dy = dy_ref[...]
        dyb = dy.astype(BF16)
        q_t = qb.astype(F32).T.astype(BF16)
        dy_t = dy.T.astype(BF16)
        totb = tot_ref[...]
        upto2 = _tri2(lambda r, c: r <= c)
        before2 = _tri2(lambda r, c: r < c)
        last = (i * tq + tq - 1) // ts

        def span(j, carry, masked):
            dq, lk_seen, e_seen = carry
            off = pl.multiple_of(j * ts, ts)
            ks = k_ref[pl.ds(off, ts), :]
            pieces = _sb_logits(qb, ks, off, i * tq, masked)
            dw = _dg(dyb, v_ref[pl.ds(off, ts), :], NT)
            upto = _tri_sums([lk for lk, _, _ in pieces], upto2)
            ws, es = [], []
            for b in range(nb):
                lk, ls, mask = pieces[b]
                w = _keep(mask, jnp.exp2(ls + ((totb - upto[b]) - lk_seen)))
                ws.append(w.astype(BF16))
                es.append(dw[:, b * CHUNK:(b + 1) * CHUNK] * w)
                lk_seen = lk_seen + jnp.sum(lk, axis=1, keepdims=True)
            before = _tri_sums(es, before2)
            dzs = []
            for b in range(nb):
                _, ls, mask = pieces[b]
                sig = jnp.exp2(ls)
                dlk = e_seen + before[b]
                dzs.append((_keep(mask, es[b] - (es[b] + dlk) * sig) * SB_SCALE).astype(BF16))
                e_seen = e_seen + jnp.sum(es[b], axis=1, keepdims=True)
            dz = jnp.concatenate(dzs, axis=1)
            dq = dq + _dg(dz, ks, NN)
            dkt_acc[j] += _dg(q_t, dz, NN)
            dvt_acc[j] += _dg(dy_t, jnp.concatenate(ws, axis=1), NN)
            return dq, lk_seen, e_seen

        zero_col = jnp.zeros((tq, 1), F32)
        carry = lax.fori_loop(0, last, lambda j, c: span(j, c, False), (jnp.zeros((tq, LANES), F32), zero_col, zero_col))
        dq, _, _ = span(last, carry, True)
        dq_ref[...] = dq.astype(dq_ref.dtype)

        @pl.when(i == nq - 1)
        def _():
            for s in range(ns):
                dk_ref[s * ts:(s + 1) * ts, :] = dkt_acc[s].T.astype(dk_ref.dtype)
                dv_ref[s * ts:(s + 1) * ts, :] = dvt_acc[s].T.astype(dv_ref.dtype)

    qsp = pl.BlockSpec((tq, LANES), lambda h, i: (i, h))
    kvsp = pl.BlockSpec((t, LANES), lambda h, i: (0, h))
    return _hosted(
        body, rider, name=name, grid=(SB_HEADS, nq),
        in_specs=[qsp, kvsp, kvsp, qsp, qsp], out_specs=[qsp, kvsp, kvsp],
        out_shape=[jax.ShapeDtypeStruct((t, D_MODEL), BF16)] * 3,
        scratch_shapes=[pltpu.VMEM((ns, LANES, ts), F32), pltpu.VMEM((ns, LANES, ts), F32)],
        args=(q, k, v, tot, dy))


def _attn_gate(y, g):
    return (y * _silu(g),)


def _loss_head(name, h, target, w, tm):
    t, d = h.shape

    def body(h_ref, t_ref, w_ref, dh_ref, dw_ref, loss_ref):
        tgt = t_ref[...]

        def f(hv, wv):
            e = _rms(hv, wv) - tgt
            return 0.5 * jnp.mean(e * e, axis=-1, keepdims=True)

        row_loss, vjp = jax.vjp(f, h_ref[...], w_ref[...])
        dh, dw = vjp(jnp.ones_like(row_loss))
        dh_ref[...] = dh

        @pl.when(pl.program_id(0) == 0)
        def _():
            dw_ref[...] = jnp.zeros_like(dw_ref)
            loss_ref[...] = jnp.zeros_like(loss_ref)

        dw_ref[...] += dw
        loss_ref[...] += jnp.sum(row_loss, axis=0, keepdims=True)

    return pl.pallas_call(
        body, name=name, grid=(t // tm,),
        in_specs=[_row_spec(d, tm), _row_spec(d, tm), _full_spec(w)],
        out_specs=[_row_spec(d, tm), _full_spec(w), pl.BlockSpec((1, 1), lambda i: (0, 0))],
        out_shape=[jax.ShapeDtypeStruct((t, d), F32), jax.ShapeDtypeStruct(w.shape, F32),
                   jax.ShapeDtypeStruct((1, 1), F32)],
        compiler_params=_cparams("arbitrary"),
    )(h, target, w)


def _pick_tile(rows, cap):
    if rows <= cap:
        return rows
    for tm in range(cap - cap % 16, 0, -16):
        if rows % tm == 0:
            return tm
    raise ValueError(rows)


def _adamw(name, w, g, m, v, tm):
    n_l, r, c = w.shape
    tm = _pick_tile(r, tm)

    def body(w_ref, g_ref, m_ref, v_ref, d_ref, nm_ref, nv_ref):
        g_ = g_ref[...]
        m_ = ADAM_B1 * m_ref[...] + (1.0 - ADAM_B1) * g_
        v_ = ADAM_B2 * v_ref[...] + (1.0 - ADAM_B2) * (g_ * g_)
        m_hat = m_ / (1.0 - ADAM_B1 ** ADAM_STEP)
        v_hat = v_ / (1.0 - ADAM_B2 ** ADAM_STEP)
        d_ref[...] = -ADAM_LR * (m_hat / (jnp.sqrt(v_hat) + ADAM_EPS) + ADAM_WD * w_ref[...])
        nm_ref[...] = m_
        nv_ref[...] = v_

    spec = pl.BlockSpec((1, tm, c), lambda l, i: (l, i, 0))
    return pl.pallas_call(
        body, name=name, grid=(n_l, r // tm), in_specs=[spec] * 4, out_specs=[spec] * 3,
        out_shape=[jax.ShapeDtypeStruct((n_l, r, c), F32)] * 3, compiler_params=_cparams("parallel", "parallel"),
    )(w, g, m, v)


def _sum_parts(name, parts, tm):
    n, r, c = parts.shape
    tm = _pick_tile(r, tm)

    def body(p_ref, o_ref):
        s = p_ref[0].astype(F32)
        for d in range(1, n):
            s = s + p_ref[d].astype(F32)
        o_ref[...] = s

    return pl.pallas_call(
        body, name=name, grid=(r // tm,),
        in_specs=[pl.BlockSpec((n, tm, c), lambda i: (0, i, 0))], out_specs=_row_spec(c, tm),
        out_shape=jax.ShapeDtypeStruct((r, c), F32), compiler_params=_cparams("parallel"),
    )(parts)


def _peer(k):
    x, y, c = lax.axis_index("x"), lax.axis_index("y"), lax.axis_index("c")
    px, py, pc = x ^ ((k >> 2) & 1), y ^ ((k >> 1) & 1), c ^ (k & 1)
    return (px, py, pc), 4 * px + 2 * py + pc


def _exchange(name, xs, gather):
    rider = _Rider(xs, gather)
    n = rider.n

    def body(*refs):
        x_refs, out_refs, sems = refs[:n], refs[n:2 * n], refs[2 * n:]
        _exchange_start(x_refs, out_refs, sems, gather)
        _exchange_finish(x_refs, out_refs, sems, gather)

    return pl.pallas_call(
        body, name=name, in_specs=[ANY_SPEC] * n, out_specs=[ANY_SPEC] * n,
        out_shape=rider.out_shape(), scratch_shapes=rider.scratch(),
    )(*xs)


ANY_SPEC = pl.BlockSpec(memory_space=pl.ANY)
SAME_CORE = (2, 4, 6)


class _Rider:
    def __init__(self, xs, gather):
        self.xs, self.gather, self.n = list(xs), gather, len(xs)

    def out_shape(self):
        return [jax.ShapeDtypeStruct((N_DEV,) + tuple(x.shape if self.gather else x.shape[1:]), x.dtype)
                for x in self.xs]

    def scratch(self):
        return [pltpu.SemaphoreType.DMA((self.n, N_DEV - 1)), pltpu.SemaphoreType.DMA((self.n, N_DEV - 1)),
                pltpu.SemaphoreType.DMA((self.n,))]


def _remote(src, dst, sems, a, k, dev):
    return pltpu.make_async_remote_copy(
        src_ref=src, dst_ref=dst, send_sem=sems[0].at[a, k - 1], recv_sem=sems[1].at[a, k - 1],
        device_id=dev, device_id_type=pl.DeviceIdType.MESH)


def _exchange_start(x_refs, out_refs, sems, gather):
    _, me = _peer(0)
    for a, (x, out) in enumerate(zip(x_refs, out_refs)):
        pltpu.make_async_copy(x if gather else x.at[me], out.at[me], sems[2].at[a]).start()
    for k in ((1,) + SAME_CORE if gather else range(1, N_DEV)):
        dev, idx = _peer(k)
        for a, (x, out) in enumerate(zip(x_refs, out_refs)):
            _remote(x if gather else x.at[idx], out.at[me], sems, a, k, dev).start()


def _exchange_finish(x_refs, out_refs, sems, gather):
    _, me = _peer(0)
    sibling, _ = _peer(1)
    pairs = list(enumerate(zip(x_refs, out_refs)))
    waited = ()
    if gather:
        for k in SAME_CORE:
            dev, idx = _peer(k)
            for a, (x, out) in pairs:
                _remote(x, out.at[idx], sems, a, k, dev).wait_recv()
                _remote(out.at[idx], out.at[idx], sems, a, k + 1, sibling).start()
        waited = SAME_CORE
    for k in range(1, N_DEV):
        dev, idx = _peer(k)
        for a, (x, out) in pairs:
            cp = _remote(x if gather else x.at[idx], out.at[idx], sems, a, k, dev)
            if k not in waited:
                cp.wait_recv()
            cp.wait_send()
    for a, (x, out) in pairs:
        pltpu.make_async_copy(x if gather else x.at[me], out.at[me], sems[2].at[a]).wait()


def _hosted(body, rider, *, name, grid, in_specs, out_specs, out_shape, scratch_shapes, args):
    sem = ("arbitrary",) * len(grid)
    if rider is None:
        return pl.pallas_call(body, name=name, grid=grid, in_specs=in_specs, out_specs=out_specs, out_shape=out_shape,
                              scratch_shapes=scratch_shapes, compiler_params=_cparams(*sem))(*args), []
    n_in, n_out, n_scr, nr = len(in_specs), len(out_specs), len(scratch_shapes), rider.n

    def hosted(*refs):
        ins, refs = refs[:n_in], refs[n_in:]
        r_in, refs = refs[:nr], refs[nr:]
        outs, refs = refs[:n_out], refs[n_out:]
        r_out, refs = refs[:nr], refs[nr:]
        scr, sems = refs[:n_scr], refs[n_scr:]
        first = pl.program_id(0) == 0
        last = pl.program_id(0) == grid[0] - 1
        for ax in range(1, len(grid)):
            first = first & (pl.program_id(ax) == 0)
            last = last & (pl.program_id(ax) == grid[ax] - 1)

        @pl.when(first)
        def _():
            _exchange_start(r_in, r_out, sems, rider.gather)

        body(*ins, *outs, *scr)

        @pl.when(last)
        def _():
            _exchange_finish(r_in, r_out, sems, rider.gather)

    res = pl.pallas_call(
        hosted, name=name, grid=grid, in_specs=list(in_specs) + [ANY_SPEC] * nr,
        out_specs=list(out_specs) + [ANY_SPEC] * nr, out_shape=list(out_shape) + rider.out_shape(),
        scratch_shapes=list(scratch_shapes) + rider.scratch(), compiler_params=_cparams(*sem),
    )(*args, *rider.xs)
    return res[:n_out], res[n_out:]


PACK_ALIGN = 8 * LANES
PACK_ROWS = 512


def _pack(arrays, dtype):
    pieces, total = [], 0
    for a in arrays:
        f = a.reshape(-1).astype(dtype)
        pad = (-f.shape[0]) % PACK_ALIGN
        pieces.append(jnp.pad(f, (0, pad)) if pad else f)
        total += f.shape[0] + pad
    tail = (-total) % (PACK_ROWS * LANES)
    if tail:
        pieces.append(jnp.zeros((tail,), dtype))
    return jnp.concatenate(pieces).reshape(-1, LANES)


def _unpack(packed, shapes, lead=()):
    flat = packed.reshape(lead + (-1,))
    out, off = [], 0
    for s in shapes:
        n = 1
        for d in s:
            n *= d
        out.append(flat[..., off:off + n].reshape(lead + tuple(s)))
        off += n + ((-n) % PACK_ALIGN)
    return out


def _pad_lanes(a):
    return jnp.pad(a, (0, LANES - a.shape[0])).reshape(1, LANES)


ROW_TM = 256
EVEN_SEGS = (("z", D_MODEL), ("xbc", CONV_DIM), ("dt", SSD_HEADS), ("g", D_MODEL), ("u", D_MODEL), ("v", D_MODEL))
ODD_SEGS = (("q", D_MODEL), ("k", D_MODEL), ("v", D_MODEL), ("g", D_MODEL))


def _split_cols(w, segs):
    out, off = {}, 0
    for nm, n in segs:
        out[nm] = w[:, off:off + n]
        off += n
    return out


def _rms_fn(h, w):
    return (_rms(h, w),)


def _even_fwd(tag, h, p, rider=None, late=None):
    hn, = _rows_fwd(tag + "_norm", _rms_fn, [h], [p["norm_w"]], [(D_MODEL, BF16)], ROW_TM)
    proj = {nm: _matmul(f"{tag}_in_{nm}", hn, p["w_in"][nm], NN, F32) for nm, _ in EVEN_SEGS}
    pre, xs, bm, cm = _conv_fwd(tag + "_conv", proj["xbc"], p["conv_w"], p["conv_b"])
    (y_ssd, states), got = _ssd_fwd(tag + "_ssd", xs, bm, cm, proj["dt"], p["dt_bias"], p["a_log"], p["d_skip"], rider)
    if late is not None:
        p = dict(p, **late(got))
    ya, = _rows_fwd(tag + "_ssdgate", _ssd_gate, [y_ssd, proj["z"]], [p["ssd_norm_w"]], [(D_MODEL, BF16)], ROW_TM)
    vn, = _rows_fwd(tag + "_sgunorm", _sgu_norm, [proj["v"]], [p["sgu_ln_w"], p["sgu_ln_b"]], [(D_MODEL, F32)], ROW_TM)
    yb = _sgu_fwd(tag + "_sgu", proj["u"], proj["g"], vn, p["sgu_w"], p["sgu_b"])
    h1 = _matmul(tag + "_out_a", ya, p["w_out_a"], NN, F32, add=h)
    h2 = _matmul(tag + "_out_b", yb, p["w_out_b"], NN, F32, add=h1)
    saved = dict(h=h, hn=hn, proj=proj, pre=pre, xs=xs, bm=bm, cm=cm, y_ssd=y_ssd, states=states,
                 ya=ya, vn=vn, yb=yb)
    return h2, saved, p, got


def _even_bwd(tag, dh, s, p, make_rider=None):
    g = {}
    dh16 = dh.astype(BF16)
    proj = s["proj"]
    dya = _matmul(tag + "_dya", dh16, p["w_out_a"], NT, F32)
    dyb = _matmul(tag + "_dyb", dh16, p["w_out_b"], NT, F32)
    g["w_out_a"] = _matmul(tag + "_dwout_a", s["ya"], dh16, TN, BF16)
    g["w_out_b"] = _matmul(tag + "_dwout_b", s["yb"], dh16, TN, BF16)
    du, dg, dvn, g["sgu_w"], g["sgu_b"] = _sgu_bwd(tag + "_sgu_b", proj["u"], proj["g"], s["vn"],
                                                   p["sgu_w"], p["sgu_b"], dyb)
    dv, g["sgu_ln_w"], g["sgu_ln_b"] = _rows_bwd(tag + "_sgunorm_b", _sgu_norm, [proj["v"]],
                                                 [p["sgu_ln_w"], p["sgu_ln_b"]], [dvn], [BF16], ROW_TM // 2)
    dy_ssd, dz, g["ssd_norm_w"] = _rows_bwd(tag + "_ssdgate_b", _ssd_gate, [s["y_ssd"], proj["z"]],
                                            [p["ssd_norm_w"]], [dya], [F32, BF16], ROW_TM // 2)
    (dxs, dbm, dcm, ddt, g["dt_bias"], g["a_log"], g["d_skip"]), got = _ssd_bwd(
        tag + "_ssd_b", s["xs"], s["bm"], s["cm"], proj["dt"], p["dt_bias"], p["a_log"], p["d_skip"],
        s["states"], dy_ssd, None if make_rider is None else make_rider(g))
    dxbc, g["conv_w"], g["conv_b"] = _conv_bwd(tag + "_conv_b", proj["xbc"], s["pre"], p["conv_w"], dxs, dbm, dcm)
    dproj = dict(z=dz, xbc=dxbc, dt=ddt.astype(BF16), g=dg, u=du, v=dv)
    dhn = None
    g["w_in"] = {}
    for nm, _ in EVEN_SEGS:
        dhn = _matmul(f"{tag}_dhn_{nm}", dproj[nm], p["w_in"][nm], NT, F32, add=dhn)
        g["w_in"][nm] = _matmul(f"{tag}_dwin_{nm}", s["hn"], dproj[nm], TN, BF16)
    dh_in, g["norm_w"] = _rows_bwd(tag + "_norm_b", _rms_fn, [s["h"]], [p["norm_w"]], [dhn], [F32],
                                   ROW_TM // 2, add=(0, dh))
    return dh_in, g, got


def _odd_fwd(tag, h, p, rider=None):
    hn, = _rows_fwd(tag + "_norm", _rms_fn, [h], [p["norm_w"]], [(D_MODEL, BF16)], ROW_TM)
    q = _matmul(tag + "_in_q", hn, p["w_in"]["q"], NN, BF16)
    k = _matmul(tag + "_in_k", hn, p["w_in"]["k"], NN, BF16)
    v = _matmul(tag + "_in_v", hn, p["w_in"]["v"], NN, BF16)
    gate = _matmul(tag + "_in_g", hn, p["w_in"]["g"], NN, F32)
    (y, tot), got = _attn_fwd(tag + "_attn", q, k, v, rider)
    yg, = _rows_fwd(tag + "_gate", _attn_gate, [y, gate], [], [(D_MODEL, BF16)], ROW_TM)
    h1 = _matmul(tag + "_out", yg, p["w_out"], NN, F32, add=h)
    return h1, dict(h=h, hn=hn, q=q, k=k, v=v, gate=gate, y=y, tot=tot, yg=yg), got


def _odd_bwd(tag, dh, s, p, rider=None):
    g = {}
    dh16 = dh.astype(BF16)
    dyg = _matmul(tag + "_dyg", dh16, p["w_out"], NT, F32)
    g["w_out"] = _matmul(tag + "_dwout", s["yg"], dh16, TN, BF16)
    dy, dgate = _rows_bwd(tag + "_gate_b", _attn_gate, [s["y"], s["gate"]], [], [dyg], [F32, BF16], ROW_TM)
    (dq, dk, dv), got = _attn_bwd(tag + "_attn_b", s["q"], s["k"], s["v"], s["tot"], dy, rider)
    dproj = dict(q=dq, k=dk, v=dv, g=dgate)
    dhn = None
    g["w_in"] = {}
    for nm, _ in ODD_SEGS:
        dhn = _matmul(f"{tag}_dhn_{nm}", dproj[nm], p["w_in"][nm], NT, F32, add=dhn)
        g["w_in"][nm] = _matmul(f"{tag}_dwin_{nm}", s["hn"], dproj[nm], TN, BF16)
    dh_in, g["norm_w"] = _rows_bwd(tag + "_norm_b", _rms_fn, [s["h"]], [p["norm_w"]], [dhn], [F32],
                                   ROW_TM // 2, add=(0, dh))
    return dh_in, g, got


BIG = ("ev_w_in", "ev_w_out", "od_w_in", "od_w_out")
SMALL = ("norm_w", "final_norm_w", "ev_conv_b", "ev_dt_bias", "ev_a_log", "ev_d_skip", "ev_ssd_norm_w",
         "ev_sgu_ln_w", "ev_sgu_ln_b", "ev_sgu_w", "ev_sgu_b")
WEIGHTS = ("norm_w", "final_norm_w", "ev_w_in", "ev_conv_w", "ev_conv_b", "ev_dt_bias", "ev_a_log", "ev_d_skip",
           "ev_ssd_norm_w", "ev_sgu_ln_w", "ev_sgu_ln_b", "ev_sgu_w", "ev_sgu_b", "ev_w_out", "od_w_in", "od_w_out")


def _step(w, m, v, x, loss_target):
    h = x[0]
    tgt = loss_target[0]
    n_even, n_odd = w["ev_w_in"].shape[0], w["od_w_in"].shape[0]
    depth = n_even + n_odd
    assert (n_even, n_odd) == (2, 2), "the exchange schedule below is written for the four-layer trunk"

    def shard(n, i):
        return w[n][i].astype(BF16)

    def cols(gathered):
        return jnp.moveaxis(gathered, 0, 1).reshape(gathered.shape[1], -1)

    def rows(gathered):
        return gathered.reshape(-1, gathered.shape[2])

    def by_owner_cols(full):
        kk, n8 = full.shape
        return jnp.moveaxis(full.reshape(kk, N_DEV, n8 // N_DEV), 1, 0)

    def by_owner_rows(full):
        return full.reshape(N_DEV, full.shape[0] // N_DEV, full.shape[1])

    def even_params(layer, ev_in):
        i = layer // 2
        w_in = _split_cols(cols(ev_in), EVEN_SEGS)
        w_in["dt"] = jnp.pad(w_in["dt"], ((0, 0), (0, LANES - SSD_HEADS)))
        return dict(norm_w=w["norm_w"][layer][None], w_in=w_in, conv_w=conv_w[i], conv_b=w["ev_conv_b"][i][None],
                    dt_bias=_pad_lanes(w["ev_dt_bias"][i]), a_log=_pad_lanes(w["ev_a_log"][i]),
                    d_skip=_pad_lanes(w["ev_d_skip"][i]), ssd_norm_w=w["ev_ssd_norm_w"][i][None],
                    sgu_ln_w=w["ev_sgu_ln_w"][i][None], sgu_ln_b=w["ev_sgu_ln_b"][i][None],
                    sgu_w=w["ev_sgu_w"][i], sgu_b=w["ev_sgu_b"][i][:, :, None])

    def even_out_params(ev_out):
        full = rows(ev_out)
        return dict(w_out_a=full[:D_MODEL], w_out_b=full[D_MODEL:])

    def odd_params(layer, od_in, od_out):
        return dict(norm_w=w["norm_w"][layer][None], w_in=_split_cols(cols(od_in), ODD_SEGS), w_out=rows(od_out))

    def even_in_grads(g):
        return by_owner_cols(jnp.concatenate([g["w_in"][nm][:, :n] for nm, n in EVEN_SEGS], axis=1))

    def even_out_grads(g):
        return by_owner_rows(jnp.concatenate([g["w_out_a"], g["w_out_b"]], axis=0))

    def odd_grads(g):
        return [by_owner_cols(jnp.concatenate([g["w_in"][nm] for nm, _ in ODD_SEGS], axis=1)),
                by_owner_rows(g["w_out"])]

    ev_in0, conv_w = _exchange("gather_first", [shard("ev_w_in", 0), w["ev_conv_w"]], gather=True)
    conv_w = jnp.moveaxis(conv_w, 0, 2).reshape(n_even, CONV_WIDTH, CONV_DIM)
    h, s0, p0, got = _even_fwd(
        "l0", h, even_params(0, ev_in0),
        _Rider([shard("ev_w_out", 0), shard("od_w_in", 0), shard("od_w_out", 0)], True),
        late=lambda arrived: even_out_params(arrived[0]))
    p1 = odd_params(1, got[1], got[2])
    h, s1, got = _odd_fwd("l1", h, p1, _Rider([shard("ev_w_in", 1), shard("ev_w_out", 1)], True))
    p2 = dict(even_params(2, got[0]), **even_out_params(got[1]))
    h, s2, p2, got = _even_fwd("l2", h, p2, _Rider([shard("od_w_in", 1), shard("od_w_out", 1)], True))
    p3 = odd_params(3, got[0], got[1])
    h, s3, _ = _odd_fwd("l3", h, p3)

    dh, d_final, loss_part = _loss_head("loss_head", h, tgt, w["final_norm_w"][None], ROW_TM // 2)
    loss = lax.psum(loss_part[0, 0], ("x", "y", "c"))

    dh, g3, _ = _odd_bwd("l3", dh, s3, p3)
    dh, g2, from3 = _even_bwd("l2", dh, s2, p2, lambda g: _Rider(odd_grads(g3), False))
    dh, g1, from2 = _odd_bwd("l1", dh, s1, p1, _Rider([even_in_grads(g2), even_out_grads(g2)], False))
    dh, g0, from1 = _even_bwd("l0", dh, s0, p0, lambda g: _Rider(odd_grads(g1) + [even_out_grads(g)], False))
    from0, = _exchange("scatter_last", [even_in_grads(g0)], gather=False)
    grad_x = dh[None]
    lg = [g0, g1, g2, g3]
    ev, od = [g0, g2], [g1, g3]

    def total(tag, parts):
        return _sum_parts("sum_" + tag, parts, 256)

    big_grads = {
        "ev_w_in": jnp.stack([total("ev_in0", from0), total("ev_in1", from2[0])]),
        "ev_w_out": jnp.stack([total("ev_out0", from1[2]), total("ev_out1", from2[1])]),
        "od_w_in": jnp.stack([total("od_in0", from1[0]), total("od_in1", from3[0])]),
        "od_w_out": jnp.stack([total("od_out0", from1[1]), total("od_out1", from3[1])]),
    }

    small_g = {
        "norm_w": jnp.concatenate([lg[l]["norm_w"] for l in range(depth)], axis=0),
        "final_norm_w": d_final[0],
        "ev_conv_b": jnp.concatenate([e["conv_b"] for e in ev], axis=0),
        "ev_dt_bias": jnp.concatenate([e["dt_bias"][:, :SSD_HEADS] for e in ev], axis=0),
        "ev_a_log": jnp.concatenate([e["a_log"][:, :SSD_HEADS] for e in ev], axis=0),
        "ev_d_skip": jnp.concatenate([e["d_skip"][:, :SSD_HEADS] for e in ev], axis=0),
        "ev_ssd_norm_w": jnp.concatenate([e["ssd_norm_w"] for e in ev], axis=0),
        "ev_sgu_ln_w": jnp.concatenate([e["sgu_ln_w"] for e in ev], axis=0),
        "ev_sgu_ln_b": jnp.concatenate([e["sgu_ln_b"] for e in ev], axis=0),
        "ev_sgu_w": jnp.stack([e["sgu_w"] for e in ev]),
        "ev_sgu_b": jnp.stack([e["sgu_b"][:, :, 0] for e in ev]),
    }
    conv_g = jnp.stack([e["conv_w"] for e in ev])
    small_shapes = [w[n].shape for n in SMALL]
    small_parts, = _exchange("gather_small", [_pack([small_g[n] for n in SMALL] + [conv_g], F32)], gather=True)
    small_sum = _sum_parts("sum_small", small_parts, 1024)
    *small_list, conv_full = _unpack(small_sum, small_shapes + [conv_g.shape])
    grads = dict(zip(SMALL, small_list))
    grads.update(big_grads)
    me = 4 * lax.axis_index("x") + 2 * lax.axis_index("y") + lax.axis_index("c")
    n_cv = w["ev_conv_w"].shape[2]
    grads["ev_conv_w"] = lax.dynamic_slice_in_dim(conv_full, me * n_cv, n_cv, axis=2)

    deltas, new_m, new_v = {}, {}, {}
    for n in BIG + ("ev_conv_w",):
        deltas[n], new_m[n], new_v[n] = _adamw("adamw_" + n, w[n], grads[n], m[n], v[n], 256)
    packs = [_pack([src[n] for n in SMALL], F32)[None] for src in (w, grads, m, v)]
    outs = _adamw("adamw_small", *packs, 1024)
    for dst, packed in zip((deltas, new_m, new_v), outs):
        dst.update(zip(SMALL, _unpack(packed[0], small_shapes)))
    return loss, grad_x, grads, deltas, new_m, new_v


def kernel(x, norm_w, final_norm_w, ev_w_in, ev_conv_w, ev_conv_b, ev_dt_bias, ev_a_log, ev_d_skip, ev_ssd_norm_w, ev_sgu_ln_w, ev_sgu_ln_b, ev_sgu_w, ev_sgu_b, ev_w_out, od_w_in, od_w_out, loss_target, m_norm_w, m_final_norm_w, m_ev_w_in, m_ev_conv_w, m_ev_conv_b, m_ev_dt_bias, m_ev_a_log, m_ev_d_skip, m_ev_ssd_norm_w, m_ev_sgu_ln_w, m_ev_sgu_ln_b, m_ev_sgu_w, m_ev_sgu_b, m_ev_w_out, m_od_w_in, m_od_w_out, v_norm_w, v_final_norm_w, v_ev_w_in, v_ev_conv_w, v_ev_conv_b, v_ev_dt_bias, v_ev_a_log, v_ev_d_skip, v_ev_ssd_norm_w, v_ev_sgu_ln_w, v_ev_sgu_ln_b, v_ev_sgu_w, v_ev_sgu_b, v_ev_w_out, v_od_w_in, v_od_w_out):
    w = dict(zip(WEIGHTS, (norm_w, final_norm_w, ev_w_in, ev_conv_w, ev_conv_b, ev_dt_bias, ev_a_log, ev_d_skip,
                           ev_ssd_norm_w, ev_sgu_ln_w, ev_sgu_ln_b, ev_sgu_w, ev_sgu_b, ev_w_out, od_w_in, od_w_out)))
    m = dict(zip(WEIGHTS, (m_norm_w, m_final_norm_w, m_ev_w_in, m_ev_conv_w, m_ev_conv_b, m_ev_dt_bias, m_ev_a_log,
                           m_ev_d_skip, m_ev_ssd_norm_w, m_ev_sgu_ln_w, m_ev_sgu_ln_b, m_ev_sgu_w, m_ev_sgu_b,
                           m_ev_w_out, m_od_w_in, m_od_w_out)))
    v = dict(zip(WEIGHTS, (v_norm_w, v_final_norm_w, v_ev_w_in, v_ev_conv_w, v_ev_conv_b, v_ev_dt_bias, v_ev_a_log,
                           v_ev_d_skip, v_ev_ssd_norm_w, v_ev_sgu_ln_w, v_ev_sgu_ln_b, v_ev_sgu_w, v_ev_sgu_b,
                           v_ev_w_out, v_od_w_in, v_od_w_out)))
    loss, grad_x, grads, deltas, new_m, new_v = _step(w, m, v, x, loss_target)
    return (loss, grad_x, *[grads[n] for n in WEIGHTS], *[deltas[n] for n in WEIGHTS],
            *[new_m[n] for n in WEIGHTS], *[new_v[n] for n in WEIGHTS])
```

```python
import jax
import jax.numpy as jnp
from jax import lax
from jax.experimental import pallas as pl
from jax.experimental.pallas import tpu as pltpu

F32, BF16 = jnp.float32, jnp.bfloat16

D_MODEL = 2048
SSD_HEADS = 32
SSD_HEAD_DIM = 64
SSD_GROUPS = 4
SSD_STATE = 128
CHUNK = 128
CONV_WIDTH = 4
CONV_DIM = D_MODEL + 2 * SSD_GROUPS * SSD_STATE
SGU_GROUPS = 16
SB_HEADS = 16
LANES = 128
N_PAIRS = SSD_HEADS // 2
PAIRS_PER_GROUP = N_PAIRS // SSD_GROUPS
NORM_EPS = 1e-5
N_DEV = 8

ADAM_LR, ADAM_B1, ADAM_B2, ADAM_EPS, ADAM_WD, ADAM_STEP = 0.001, 0.9, 0.999, 1e-08, 0.01, 10

VMEM_LIMIT_BYTES = 48 * 1024 * 1024

NN = ((1,), (0,))
NT = ((1,), (1,))
TN = ((0,), (0,))


def _cparams(*sem):
    return pltpu.CompilerParams(dimension_semantics=sem, vmem_limit_bytes=VMEM_LIMIT_BYTES)


def _dg(a, b, dims):
    return lax.dot_general(a, b, (dims, ((), ())), preferred_element_type=F32)


def _make_bdot(dims):
    @jax.custom_vjp
    def f(a, b):
        return _dg(a.astype(BF16), b.astype(BF16), dims)

    def fwd(a, b):
        return f(a, b), (a, b)

    def bwd(res, g):
        a, b = res
        a16, b16, g16 = a.astype(BF16), b.astype(BF16), g.astype(BF16)
        if dims == NN:
            da, db = _dg(g16, b16, NT), _dg(a16, g16, TN)
        elif dims == NT:
            da, db = _dg(g16, b16, NN), _dg(g16, a16, TN)
        else:
            da, db = _dg(b16, g16, NT), _dg(a16, g16, NN)
        return da.astype(a.dtype), db.astype(b.dtype)

    f.defvjp(fwd, bwd)
    return f


_bdot_nn, _bdot_nt, _bdot_tn = _make_bdot(NN), _make_bdot(NT), _make_bdot(TN)


def _iota2(shape, axis):
    return lax.broadcasted_iota(jnp.int32, shape, axis)


def _split3(x, axis):
    hi = x.astype(BF16)
    r = x - hi.astype(F32)
    mid = r.astype(BF16)
    return jnp.concatenate([hi, mid, (r - mid.astype(F32)).astype(BF16)], axis=axis)


def _make_onehot_dot(build, build_t, left):
    def apply(x, e):
        e = e.astype(BF16)
        if left:
            return _dg(jnp.concatenate([e, e, e], axis=1), _split3(x, 0), NN)
        return _dg(_split3(x, 1), jnp.concatenate([e, e, e], axis=0), NN)

    @jax.custom_vjp
    def f(x):
        return apply(x, build())

    f.defvjp(lambda x: (f(x), None), lambda _, g: (apply(g, build_t()),))
    return f


def _eq_div(shape, axis_div, axis_eq, div):
    return _iota2(shape, axis_div) // div == _iota2(shape, axis_eq)


N_CH = SSD_HEADS * SSD_HEAD_DIM
_to_channels = _make_onehot_dot(lambda: _eq_div((LANES, N_CH), 1, 0, SSD_HEAD_DIM),
                                lambda: _eq_div((N_CH, LANES), 0, 1, SSD_HEAD_DIM), False)
_to_head_blocks = _make_onehot_dot(lambda: _eq_div((LANES, SSD_HEADS * LANES), 1, 0, LANES),
                                   lambda: _eq_div((SSD_HEADS * LANES, LANES), 0, 1, LANES), False)
_cumsum_rows = _make_onehot_dot(lambda: _iota2((CHUNK, CHUNK), 1) <= _iota2((CHUNK, CHUNK), 0),
                                lambda: _iota2((CHUNK, CHUNK), 0) <= _iota2((CHUNK, CHUNK), 1), True)


def _softplus(x):
    return jnp.maximum(x, 0.0) + jnp.log1p(jnp.exp(-jnp.abs(x)))


def _silu(x):
    return x * jax.nn.sigmoid(x)


def _gelu(x):
    return 0.5 * x * (1.0 + jnp.tanh(0.7978845608028654 * (x + 0.044715 * (x * x * x))))


def _rms(x, w):
    return x * lax.rsqrt(jnp.mean(x * x, axis=-1, keepdims=True) + NORM_EPS) * w


def _layer_norm(x, w, b):
    xc = x - jnp.mean(x, axis=-1, keepdims=True)
    return xc * lax.rsqrt(jnp.mean(xc * xc, axis=-1, keepdims=True) + NORM_EPS) * w + b


def _row_spec(width, tm):
    return pl.BlockSpec((tm, width), lambda i: (i, 0))


def _full_spec(p):
    zeros = (0,) * p.ndim
    return pl.BlockSpec(p.shape, lambda i: zeros)


def _rows_fwd(name, fn, tiled, params, outs, tm):
    n_rows = tiled[0].shape[0]
    n_in = len(tiled) + len(params)

    def body(*refs):
        res = fn(*[r[...] for r in refs[:n_in]])
        for o_ref, o in zip(refs[n_in:], res):
            o_ref[...] = o.astype(o_ref.dtype)

    return pl.pallas_call(
        body, name=name, grid=(n_rows // tm,),
        in_specs=[_row_spec(a.shape[1], tm) for a in tiled] + [_full_spec(p) for p in params],
        out_specs=[_row_spec(w, tm) for w, _ in outs],
        out_shape=[jax.ShapeDtypeStruct((n_rows, w), d) for w, d in outs],
        compiler_params=_cparams("parallel"),
    )(*tiled, *params)


def _rows_bwd(name, fn, tiled, params, cots, grad_dtypes, tm, add=None):
    n_rows = tiled[0].shape[0]
    nt, npar, nc = len(tiled), len(params), len(cots)
    want = [k for k, d in enumerate(grad_dtypes) if d is not None]
    n_add = 0 if add is None else 1

    def body(*refs):
        ins = [r[...] for r in refs[:nt + npar]]
        c_refs = refs[nt + npar:nt + npar + nc]
        add_refs = refs[nt + npar + nc:nt + npar + nc + n_add]
        o_refs = refs[nt + npar + nc + n_add:]
        res, vjp = jax.vjp(fn, *ins)
        grads = vjp(tuple(c[...].astype(r.dtype) for c, r in zip(c_refs, res)))
        for pos, k in enumerate(want):
            gk = grads[k]
            if add is not None and add[0] == pos:
                gk = gk + add_refs[0][...]
            o_refs[pos][...] = gk.astype(o_refs[pos].dtype)
        p_refs = o_refs[len(want):]

        @pl.when(pl.program_id(0) == 0)
        def _():
            for r in p_refs:
                r[...] = jnp.zeros_like(r)

        for r, gp in zip(p_refs, grads[nt:]):
            r[...] += gp

    add_arrays = [] if add is None else [add[1]]
    out = pl.pallas_call(
        body, name=name, grid=(n_rows // tm,),
        in_specs=([_row_spec(a.shape[1], tm) for a in tiled] + [_full_spec(p) for p in params]
                  + [_row_spec(c.shape[1], tm) for c in cots] + [_row_spec(a.shape[1], tm) for a in add_arrays]),
        out_specs=([_row_spec(tiled[k].shape[1], tm) for k in want] + [_full_spec(p) for p in params]),
        out_shape=([jax.ShapeDtypeStruct(tiled[k].shape, grad_dtypes[k]) for k in want]
                   + [jax.ShapeDtypeStruct(p.shape, F32) for p in params]),
        compiler_params=_cparams("arbitrary"),
    )(*tiled, *params, *cots, *add_arrays)
    return out


def _matmul(name, a, b, dims, out_dtype, add=None, tm=1024, tn=1024, tk=2048):
    if dims == NN:
        (m, k), n = a.shape, b.shape[1]
    elif dims == NT:
        (m, k), n = a.shape, b.shape[0]
    else:
        (k, m), n = a.shape, b.shape[1]
    tm, tn, tk = min(tm, m), min(tn, n), min(tk, k)
    while k % tk:
        tk -= LANES
    assert m % tm == 0 and n % tn == 0 and k % tk == 0, (name, m, n, k)
    nk = k // tk
    a_spec = (pl.BlockSpec((tk, tm), lambda i, j, kk: (kk, i)) if dims == TN
              else pl.BlockSpec((tm, tk), lambda i, j, kk: (i, kk)))
    b_spec = (pl.BlockSpec((tn, tk), lambda i, j, kk: (j, kk)) if dims == NT
              else pl.BlockSpec((tk, tn), lambda i, j, kk: (kk, j)))
    o_spec = pl.BlockSpec((tm, tn), lambda i, j, kk: (i, j))
    has_add = add is not None

    def body(*refs):
        a_ref, b_ref = refs[0], refs[1]
        part = _dg(a_ref[...].astype(BF16), b_ref[...].astype(BF16), dims)
        if nk == 1:
            o_ref = refs[-1]
            if has_add:
                part = part + refs[2][...]
            o_ref[...] = part.astype(o_ref.dtype)
            return
        o_ref, acc = refs[-2], refs[-1]
        kk = pl.program_id(2)

        @pl.when(kk == 0)
        def _():
            acc[...] = part

        @pl.when(kk > 0)
        def _():
            acc[...] += part

        @pl.when(kk == nk - 1)
        def _():
            r = acc[...]
            if has_add:
                r = r + refs[2][...]
            o_ref[...] = r.astype(o_ref.dtype)

    return pl.pallas_call(
        body, name=name, grid=(m // tm, n // tn, nk),
        in_specs=[a_spec, b_spec] + ([o_spec] if has_add else []),
        out_specs=o_spec,
        out_shape=jax.ShapeDtypeStruct((m, n), out_dtype),
        scratch_shapes=[pltpu.VMEM((tm, tn), F32)] if nk > 1 else [],
        compiler_params=_cparams("parallel", "parallel", "arbitrary"),
    )(a, b, *([add] if has_add else []))


CONV_TM = 256
HALO = 8


def _shift_down(x, halo, j):
    if j == 0:
        return x, x[:HALO]
    xr = pltpu.roll(x, j, 0)
    hr = pltpu.roll(halo, j, 0)
    top = jnp.where(_iota2((HALO, x.shape[1]), 0) < j, hr, xr[:HALO])
    return xr, top


def _conv_fwd(name, x, w, b):
    t, c = x.shape
    tm = min(CONV_TM, t)
    hb = tm // HALO

    def body(x_ref, halo_ref, w_ref, b_ref, pre_ref, xs_ref, bm_ref, cm_ref):
        i = pl.program_id(0)
        xv = x_ref[...]
        halo = jnp.where(i > 0, halo_ref[...], 0.0)
        main = jnp.zeros_like(xv) + b_ref[...]
        top = jnp.zeros((HALO, c), F32) + b_ref[...]
        for kk in range(CONV_WIDTH):
            xr, tp = _shift_down(xv, halo, CONV_WIDTH - 1 - kk)
            main = main + w_ref[kk:kk + 1, :] * xr
            top = top + w_ref[kk:kk + 1, :] * tp
        pre = jnp.concatenate([top, main[HALO:]], axis=0)
        pre_ref[...] = pre
        act = _silu(pre)
        xs_ref[...] = act[:, :D_MODEL]
        bm_ref[...] = act[:, D_MODEL:D_MODEL + SSD_GROUPS * SSD_STATE]
        cm_ref[...] = act[:, D_MODEL + SSD_GROUPS * SSD_STATE:]

    gs = SSD_GROUPS * SSD_STATE
    return pl.pallas_call(
        body, name=name, grid=(t // tm,),
        in_specs=[_row_spec(c, tm),
                  pl.BlockSpec((HALO, c), lambda i: (jnp.maximum(i * hb - 1, 0), 0)),
                  _full_spec(w), _full_spec(b)],
        out_specs=[_row_spec(c, tm), _row_spec(D_MODEL, tm), _row_spec(gs, tm), _row_spec(gs, tm)],
        out_shape=[jax.ShapeDtypeStruct((t, c), F32), jax.ShapeDtypeStruct((t, D_MODEL), F32),
                   jax.ShapeDtypeStruct((t, gs), F32), jax.ShapeDtypeStruct((t, gs), F32)],
        compiler_params=_cparams("parallel"),
    )(x, x, w, b)


def _dsilu(pre, dact):
    s = jax.nn.sigmoid(pre)
    return dact * (s * (1.0 + pre * (1.0 - s)))


def _conv_bwd(name, x, pre, w, dxs, dbm, dcm):
    t, c = x.shape
    tm = min(CONV_TM, t)
    hb = tm // HALO
    last_hb = t // HALO - 1
    n_tiles = t // tm

    def body(x_ref, xh_ref, pre_ref, preh_ref, w_ref, dxs_ref, dbm_ref, dcm_ref,
             dxsh_ref, dbmh_ref, dcmh_ref, dx_ref, dw_ref, db_ref):
        i = pl.program_id(0)
        dact = jnp.concatenate([dxs_ref[...], dbm_ref[...], dcm_ref[...]], axis=1)
        dpre = _dsilu(pre_ref[...], dact)
        dact_h = jnp.concatenate([dxsh_ref[...], dbmh_ref[...], dcmh_ref[...]], axis=1)
        dpre_h = jnp.where(i < n_tiles - 1, _dsilu(preh_ref[...], dact_h), 0.0)
        xv = x_ref[...]
        xh = jnp.where(i > 0, xh_ref[...], 0.0)

        @pl.when(i == 0)
        def _():
            dw_ref[...] = jnp.zeros_like(dw_ref)
            db_ref[...] = jnp.zeros_like(db_ref)

        db_ref[...] += jnp.sum(dpre, axis=0, keepdims=True)
        dxm = jnp.zeros_like(xv)
        dxt = jnp.zeros((HALO, c), F32)
        row8 = _iota2((HALO, c), 0)
        for kk in range(CONV_WIDTH):
            j = CONV_WIDTH - 1 - kk
            wk = w_ref[kk:kk + 1, :]
            xr, tp = _shift_down(xv, xh, j)
            full = jnp.sum(dpre * xr, axis=0, keepdims=True)
            fix = jnp.sum(dpre[:HALO] * (tp - xr[:HALO]), axis=0, keepdims=True)
            dw_ref[kk:kk + 1, :] += full + fix
            if j == 0:
                dxm = dxm + wk * dpre
                dxt = dxt + wk * dpre[tm - HALO:]
            else:
                dr = pltpu.roll(dpre, tm - j, 0)
                hr = pltpu.roll(dpre_h, HALO - j, 0)
                dxm = dxm + wk * dr
                dxt = dxt + wk * jnp.where(row8 >= HALO - j, hr, dr[tm - HALO:])
        dx_ref[...] = jnp.concatenate([dxm[:tm - HALO], dxt], axis=0).astype(dx_ref.dtype)

    gs = SSD_GROUPS * SSD_STATE
    prev_halo = lambda i: (jnp.maximum(i * hb - 1, 0), 0)
    next_halo = lambda i: (jnp.minimum((i + 1) * hb, last_hb), 0)
    return pl.pallas_call(
        body, name=name, grid=(n_tiles,),
        in_specs=[_row_spec(c, tm), pl.BlockSpec((HALO, c), prev_halo),
                  _row_spec(c, tm), pl.BlockSpec((HALO, c), next_halo), _full_spec(w),
                  _row_spec(D_MODEL, tm), _row_spec(gs, tm), _row_spec(gs, tm),
                  pl.BlockSpec((HALO, D_MODEL), next_halo), pl.BlockSpec((HALO, gs), next_halo),
                  pl.BlockSpec((HALO, gs), next_halo)],
        out_specs=[_row_spec(c, tm), _full_spec(w), pl.BlockSpec((1, c), lambda i: (0, 0))],
        out_shape=[jax.ShapeDtypeStruct((t, c), BF16), jax.ShapeDtypeStruct(w.shape, F32),
                   jax.ShapeDtypeStruct((1, c), F32)],
        compiler_params=_cparams("arbitrary"),
    )(x, x, pre, pre, w, dxs, dbm, dcm, dxs, dbm, dcm)


def _ssd_stage1(dt_raw, dt_bias, a_log, d_skip8):
    dt = _softplus(dt_raw + dt_bias)
    cs = _cumsum_rows(dt * (-jnp.exp(a_log)))
    return _to_channels(dt), _to_channels(cs), _to_head_blocks(cs), _to_channels(d_skip8)


def _ssd_pair(xs, dtf, csf, cs_last, a0, a1, bg, cg, prev, dskf8):
    l = xs.shape[0]
    xc = xs * dtf
    scores = _bdot_nt(cg, bg)
    causal = _iota2((l, l), 0) >= _iota2((l, l), 1)

    def decay(a):
        return jnp.where(causal, jnp.exp(jnp.where(causal, a - a.T, 0.0)), 0.0)

    first = _iota2((l, LANES), 1) < SSD_HEAD_DIM
    y_diag = (_bdot_nn(scores * decay(a0), jnp.where(first, xc, 0.0))
              + _bdot_nn(scores * decay(a1), jnp.where(first, 0.0, xc)))
    states = _bdot_tn(bg, xc * jnp.exp(cs_last - csf))
    new_state = jnp.exp(cs_last) * prev + states
    y_off = _bdot_nn(cg, prev) * jnp.exp(csf)
    y = y_diag + y_off + xs * jnp.mean(dskf8, axis=0, keepdims=True)
    return y, new_state


def _ssd_specs(nc_rev=None):
    def ch(c):
        return c if nc_rev is None else nc_rev - 1 - c

    gs = SSD_GROUPS * SSD_STATE
    return dict(
        wide=pl.BlockSpec((CHUNK, D_MODEL), lambda c: (ch(c), 0)),
        group=pl.BlockSpec((CHUNK, gs), lambda c: (ch(c), 0)),
        chunk=pl.BlockSpec((CHUNK, LANES), lambda c: (ch(c), 0)),
        vec=pl.BlockSpec((1, LANES), lambda c: (0, 0)),
        state=pl.BlockSpec((1, N_PAIRS, LANES, LANES), lambda c: (ch(c), 0, 0, 0)),
    )


def _ssd_scratch():
    return [pltpu.VMEM((CHUNK, D_MODEL), F32), pltpu.VMEM((CHUNK, D_MODEL), F32),
            pltpu.VMEM((CHUNK, SSD_HEADS * LANES), F32), pltpu.VMEM((8, D_MODEL), F32)]


def _pair_slices(p):
    g = p // PAIRS_PER_GROUP
    return (slice(p * LANES, (p + 1) * LANES), slice(g * LANES, (g + 1) * LANES),
            slice(2 * p * LANES, (2 * p + 1) * LANES), slice((2 * p + 1) * LANES, (2 * p + 2) * LANES))


def _pair_inputs(p, xs_ref, b_ref, c_ref, prev, stage1):
    dtf_s, csf_s, csa_s, dsk_s = stage1
    sl, gsl, h0, h1 = _pair_slices(p)
    return (xs_ref[:, sl], dtf_s[:, sl], csf_s[:, sl], csf_s[CHUNK - 1:CHUNK, sl], csa_s[:, h0], csa_s[:, h1],
            b_ref[:, gsl], c_ref[:, gsl], prev, dsk_s[:, sl])


def _ssd_fwd(name, xs, bm, cm, dt_raw, dt_bias, a_log, d_skip, rider=None):
    t = xs.shape[0]
    nc = t // CHUNK

    def body(xs_ref, b_ref, c_ref, dt_ref, bias_ref, alog_ref, dsk_ref, y_ref, prev_ref,
             state, dtf_s, csf_s, csa_s, dsk_s):
        c = pl.program_id(0)
        stage1 = (dtf_s, csf_s, csa_s, dsk_s)
        d8 = jnp.broadcast_to(dsk_ref[...], (8, LANES))
        for ref, val in zip(stage1, _ssd_stage1(dt_ref[...], bias_ref[...], alog_ref[...], d8)):
            ref[...] = val

        @pl.when(c == 0)
        def _():
            state[...] = jnp.zeros_like(state)

        for p in range(N_PAIRS):
            prev = state[p]
            prev_ref[0, p] = prev
            y, new_state = _ssd_pair(*_pair_inputs(p, xs_ref, b_ref, c_ref, prev, stage1))
            y_ref[:, _pair_slices(p)[0]] = y
            state[p] = new_state

    sp = _ssd_specs()
    return _hosted(
        body, rider, name=name, grid=(nc,),
        in_specs=[sp["wide"], sp["group"], sp["group"], sp["chunk"], sp["vec"], sp["vec"], sp["vec"]],
        out_specs=[sp["wide"], sp["state"]],
        out_shape=[jax.ShapeDtypeStruct((t, D_MODEL), F32),
                   jax.ShapeDtypeStruct((nc, N_PAIRS, LANES, LANES), F32)],
        scratch_shapes=[pltpu.VMEM((N_PAIRS, LANES, LANES), F32)] + _ssd_scratch(),
        args=(xs, bm, cm, dt_raw, dt_bias, a_log, d_skip))


def _ssd_bwd(name, xs, bm, cm, dt_raw, dt_bias, a_log, d_skip, prev_states, dy, rider=None):
    t = xs.shape[0]
    nc = t // CHUNK

    def body(xs_ref, b_ref, c_ref, dt_ref, bias_ref, alog_ref, dsk_ref, prev_ref, dy_ref,
             dxs_ref, db_ref, dc_ref, ddt_ref, dbias_ref, dalog_ref, ddsk_ref,
             dstate, dtf_s, csf_s, csa_s, dsk_s, g_dtf, g_csf, g_csa, g_dsk):
        c = pl.program_id(0)
        stage1 = (dtf_s, csf_s, csa_s, dsk_s)
        d8 = jnp.broadcast_to(dsk_ref[...], (8, LANES))
        vals, vjp1 = jax.vjp(_ssd_stage1, dt_ref[...], bias_ref[...], alog_ref[...], d8)
        for ref, val in zip(stage1, vals):
            ref[...] = val

        @pl.when(c == 0)
        def _():
            dstate[...] = jnp.zeros_like(dstate)
            dbias_ref[...] = jnp.zeros_like(dbias_ref)
            dalog_ref[...] = jnp.zeros_like(dalog_ref)
            ddsk_ref[...] = jnp.zeros_like(ddsk_ref)

        last_row = _iota2((CHUNK, LANES), 0) == CHUNK - 1
        for p in range(N_PAIRS):
            sl, gsl, h0, h1 = _pair_slices(p)
            _, vjp = jax.vjp(_ssd_pair, *_pair_inputs(p, xs_ref, b_ref, c_ref, prev_ref[0, p], stage1))
            dxs, ddtf, dcsf, dlast, da0, da1, dbg, dcg, dprev, ddsk8 = vjp((dy_ref[:, sl], dstate[p]))
            dxs_ref[:, sl] = dxs
            g_dtf[:, sl] = ddtf
            g_csf[:, sl] = dcsf + jnp.where(last_row, dlast, 0.0)
            g_csa[:, h0] = da0
            g_csa[:, h1] = da1
            g_dsk[:, sl] = ddsk8
            dstate[p] = dprev
            if p % PAIRS_PER_GROUP == 0:
                db_ref[:, gsl] = dbg
                dc_ref[:, gsl] = dcg
            else:
                db_ref[:, gsl] += dbg
                dc_ref[:, gsl] += dcg

        ddt, dbias, dalog, dd8 = vjp1((g_dtf[...], g_csf[...], g_csa[...], g_dsk[...]))
        ddt_ref[...] = ddt
        dbias_ref[...] += dbias
        dalog_ref[...] += dalog
        ddsk_ref[...] += jnp.sum(dd8, axis=0, keepdims=True)

    sp = _ssd_specs(nc)
    gs = SSD_GROUPS * SSD_STATE
    return _hosted(
        body, rider, name=name, grid=(nc,),
        in_specs=[sp["wide"], sp["group"], sp["group"], sp["chunk"], sp["vec"], sp["vec"], sp["vec"],
                  sp["state"], sp["wide"]],
        out_specs=[sp["wide"], sp["group"], sp["group"], sp["chunk"], sp["vec"], sp["vec"], sp["vec"]],
        out_shape=[jax.ShapeDtypeStruct((t, D_MODEL), F32), jax.ShapeDtypeStruct((t, gs), F32),
                   jax.ShapeDtypeStruct((t, gs), F32), jax.ShapeDtypeStruct((t, LANES), F32),
                   jax.ShapeDtypeStruct((1, LANES), F32), jax.ShapeDtypeStruct((1, LANES), F32),
                   jax.ShapeDtypeStruct((1, LANES), F32)],
        scratch_shapes=[pltpu.VMEM((N_PAIRS, LANES, LANES), F32)] + _ssd_scratch() + _ssd_scratch(),
        args=(xs, bm, cm, dt_raw, dt_bias, a_log, d_skip, prev_states, dy))


def _ssd_gate(y, z, w):
    return (_rms(y * _silu(z), w),)


def _sgu_norm(v, w, b):
    return (_layer_norm(_gelu(v), w, b),)


def _sgu_group(u, gate, vn, w, bcol):
    l = u.shape[0]
    wc = jnp.where(_iota2((l, l), 0) >= _iota2((l, l), 1), w, 0.0)
    return _gelu(u) * (_bdot_nn(wc, vn) + bcol) * _silu(gate)


def _sgu_fwd(name, u, gate, vn, w, bcol):
    t = u.shape[0]
    blk = _row_spec(D_MODEL, CHUNK)

    def body(u_ref, g_ref, vn_ref, w_ref, b_ref, y_ref):
        for g in range(SGU_GROUPS):
            sl = slice(g * LANES, (g + 1) * LANES)
            y_ref[:, sl] = _sgu_group(u_ref[:, sl], g_ref[:, sl], vn_ref[:, sl], w_ref[g], b_ref[g]).astype(y_ref.dtype)

    return pl.pallas_call(
        body, name=name, grid=(t // CHUNK,),
        in_specs=[blk, blk, blk, _full_spec(w), _full_spec(bcol)], out_specs=blk,
        out_shape=jax.ShapeDtypeStruct((t, D_MODEL), BF16),
        compiler_params=_cparams("parallel"),
    )(u, gate, vn, w, bcol)


def _sgu_bwd(name, u, gate, vn, w, bcol, dy):
    t = u.shape[0]
    blk = _row_spec(D_MODEL, CHUNK)

    def body(u_ref, g_ref, vn_ref, w_ref, b_ref, dy_ref, du_ref, dg_ref, dvn_ref, dw_ref, db_ref):
        @pl.when(pl.program_id(0) == 0)
        def _():
            dw_ref[...] = jnp.zeros_like(dw_ref)
            db_ref[...] = jnp.zeros_like(db_ref)

        for g in range(SGU_GROUPS):
            sl = slice(g * LANES, (g + 1) * LANES)
            _, vjp = jax.vjp(_sgu_group, u_ref[:, sl], g_ref[:, sl], vn_ref[:, sl], w_ref[g], b_ref[g])
            du, dg, dvn, dw, db = vjp(dy_ref[:, sl])
            du_ref[:, sl] = du.astype(du_ref.dtype)
            dg_ref[:, sl] = dg.astype(dg_ref.dtype)
            dvn_ref[:, sl] = dvn
            dw_ref[g] += dw
            db_ref[g] += db

    return pl.pallas_call(
        body, name=name, grid=(t // CHUNK,),
        in_specs=[blk, blk, blk, _full_spec(w), _full_spec(bcol), blk],
        out_specs=[blk, blk, blk, _full_spec(w), _full_spec(bcol)],
        out_shape=[jax.ShapeDtypeStruct((t, D_MODEL), BF16), jax.ShapeDtypeStruct((t, D_MODEL), BF16),
                   jax.ShapeDtypeStruct((t, D_MODEL), F32), jax.ShapeDtypeStruct(w.shape, F32),
                   jax.ShapeDtypeStruct(bcol.shape, F32)],
        compiler_params=_cparams("arbitrary"),
    )(u, gate, vn, w, bcol, dy)


SB_SCALE = LANES ** -0.5
LOG2_E = 1.4426950408889634
SB_TQ = 512
SB_TS = 512


def _keep(mask, x):
    return x if mask is None else jnp.where(mask, x, 0.0)


def _sb_pieces(z2, mask):
    tl = jnp.log(1.0 + jnp.exp2(-jnp.abs(z2))) * LOG2_E
    lk = _keep(mask, -(jnp.maximum(z2, 0.0) + tl))
    ls = jnp.minimum(z2, 0.0) - tl
    return lk, ls


def _split2(x):
    hi = x.astype(BF16)
    return jnp.concatenate([hi, (x - hi.astype(F32)).astype(BF16)], axis=1)


def _tri2(cmp):
    sq = (CHUNK, CHUNK)
    m = cmp(_iota2(sq, 0), _iota2(sq, 1)).astype(BF16)
    return jnp.concatenate([m, m], axis=0)


def _tri_sums(blocks, tri2):
    tq = blocks[0].shape[0]
    res = _dg(jnp.concatenate([_split2(b) for b in blocks], axis=0), tri2, NN)
    return [res[b * tq:(b + 1) * tq] for b in range(len(blocks))]


def _sb_tiles(t):
    tq, ts = min(SB_TQ, t), min(SB_TS, t)
    assert t % tq == 0 and t % ts == 0 and ts % tq == 0 and ts % CHUNK == 0
    return tq, ts, ts // CHUNK


def _sb_logits(qb, ks, off, q_off, masked):
    tq = qb.shape[0]
    z2 = _dg(qb, ks, NT) * (SB_SCALE * LOG2_E)
    out = []
    for b in range(ks.shape[0] // CHUNK):
        mask = None
        if masked:
            mask = (_iota2((tq, CHUNK), 1) + (off + b * CHUNK)) < (_iota2((tq, CHUNK), 0) + q_off)
        out.append(_sb_pieces(z2[:, b * CHUNK:(b + 1) * CHUNK], mask) + (mask,))
    return out


def _attn_fwd(name, q, k, v, rider=None):
    t = q.shape[0]
    tq, ts, nb = _sb_tiles(t)

    def body(q_ref, k_ref, v_ref, y_ref, tot_ref):
        i = pl.program_id(1)
        qb = q_ref[...]
        later2 = _tri2(lambda r, c: r > c)
        last = (i * tq + tq - 1) // ts

        def span(j, carry, masked):
            acc, after = carry
            off = pl.multiple_of(j * ts, ts)
            pieces = _sb_logits(qb, k_ref[pl.ds(off, ts), :], off, i * tq, masked)
            inside = _tri_sums([lk for lk, _, _ in pieces], later2)
            ws = [None] * nb
            for b in reversed(range(nb)):
                lk, ls, mask = pieces[b]
                ws[b] = _keep(mask, jnp.exp2(ls + inside[b] + after)).astype(BF16)
                after = after + jnp.sum(lk, axis=1, keepdims=True)
            acc = acc + _dg(jnp.concatenate(ws, axis=1), v_ref[pl.ds(off, ts), :], NN)
            return acc, after

        carry = span(last, (jnp.zeros((tq, LANES), F32), jnp.zeros((tq, 1), F32)), True)
        acc, tot = lax.fori_loop(0, last, lambda n, c: span(last - 1 - n, c, False), carry)
        y_ref[...] = acc
        tot_ref[...] = jnp.broadcast_to(tot, (tq, LANES))

    qsp = pl.BlockSpec((tq, LANES), lambda h, i: (i, h))
    kvsp = pl.BlockSpec((t, LANES), lambda h, i: (0, h))
    return _hosted(
        body, rider, name=name, grid=(SB_HEADS, t // tq),
        in_specs=[qsp, kvsp, kvsp], out_specs=[qsp, qsp],
        out_shape=[jax.ShapeDtypeStruct((t, D_MODEL), F32), jax.ShapeDtypeStruct((t, D_MODEL), F32)],
        scratch_shapes=[], args=(q, k, v))


def _attn_bwd(name, q, k, v, tot, dy, rider=None):
    t = q.shape[0]
    tq, ts, nb = _sb_tiles(t)
    nq, ns = t // tq, t // ts

    def body(q_ref, k_ref, v_ref, tot_ref, dy_ref, dq_ref, dk_ref, dv_ref, dkt_acc, dvt_acc):
        i = pl.program_id(1)

        @pl.when(i == 0)
        def _():
            dkt_acc[...] = jnp.zeros_like(dkt_acc)
            dvt_acc[...] = jnp.zeros_like(dvt_acc)

        qb = q_ref[...]
        dy = dy_ref[...]
        dyb = dy.astype(BF16)
        q_t = qb.astype(F32).T.astype(BF16)
        dy_t = dy.T.astype(BF16)
        totb = tot_ref[...]
        upto2 = _tri2(lambda r, c: r <= c)
        before2 = _tri2(lambda r, c: r < c)
        last = (i * tq + tq - 1) // ts

        def span(j, carry, masked):
            dq, lk_seen, e_seen = carry
            off = pl.multiple_of(j * ts, ts)
            ks = k_ref[pl.ds(off, ts), :]
            pieces = _sb_logits(qb, ks, off, i * tq, masked)
            dw = _dg(dyb, v_ref[pl.ds(off, ts), :], NT)
            upto = _tri_sums([lk for lk, _, _ in pieces], upto2)
            ws, es = [], []
            for b in range(nb):
                lk, ls, mask = pieces[b]
                w = _keep(mask, jnp.exp2(ls + ((totb - upto[b]) - lk_seen)))
                ws.append(w.astype(BF16))
                es.append(dw[:, b * CHUNK:(b + 1) * CHUNK] * w)
                lk_seen = lk_seen + jnp.sum(lk, axis=1, keepdims=True)
            before = _tri_sums(es, before2)
            dzs = []
            for b in range(nb):
                _, ls, mask = pieces[b]
                sig = jnp.exp2(ls)
                dlk = e_seen + before[b]
                dzs.append((_keep(mask, es[b] - (es[b] + dlk) * sig) * SB_SCALE).astype(BF16))
                e_seen = e_seen + jnp.sum(es[b], axis=1, keepdims=True)
            dz = jnp.concatenate(dzs, axis=1)
            dq = dq + _dg(dz, ks, NN)
            dkt_acc[j] += _dg(q_t, dz, NN)
            dvt_acc[j] += _dg(dy_t, jnp.concatenate(ws, axis=1), NN)
            return dq, lk_seen, e_seen

        zero_col = jnp.zeros((tq, 1), F32)
        carry = lax.fori_loop(0, last, lambda j, c: span(j, c, False), (jnp.zeros((tq, LANES), F32), zero_col, zero_col))
        dq, _, _ = span(last, carry, True)
        dq_ref[...] = dq.astype(dq_ref.dtype)

        @pl.when(i == nq - 1)
        def _():
            for s in range(ns):
                dk_ref[s * ts:(s + 1) * ts, :] = dkt_acc[s].T.astype(dk_ref.dtype)
                dv_ref[s * ts:(s + 1) * ts, :] = dvt_acc[s].T.astype(dv_ref.dtype)

    qsp = pl.BlockSpec((tq, LANES), lambda h, i: (i, h))
    kvsp = pl.BlockSpec((t, LANES), lambda h, i: (0, h))
    return _hosted(
        body, rider, name=name, grid=(SB_HEADS, nq),
        in_specs=[qsp, kvsp, kvsp, qsp, qsp], out_specs=[qsp, kvsp, kvsp],
        out_shape=[jax.ShapeDtypeStruct((t, D_MODEL), BF16)] * 3,
        scratch_shapes=[pltpu.VMEM((ns, LANES, ts), F32), pltpu.VMEM((ns, LANES, ts), F32)],
        args=(q, k, v, tot, dy))


def _attn_gate(y, g):
    return (y * _silu(g),)


def _loss_head(name, h, target, w, tm):
    t, d = h.shape

    def body(h_ref, t_ref, w_ref, dh_ref, dw_ref, loss_ref):
        tgt = t_ref[...]

        def f(hv, wv):
            e = _rms(hv, wv) - tgt
            return 0.5 * jnp.mean(e * e, axis=-1, keepdims=True)

        row_loss, vjp = jax.vjp(f, h_ref[...], w_ref[...])
        dh, dw = vjp(jnp.ones_like(row_loss))
        dh_ref[...] = dh

        @pl.when(pl.program_id(0) == 0)
        def _():
            dw_ref[...] = jnp.zeros_like(dw_ref)
            loss_ref[...] = jnp.zeros_like(loss_ref)

        dw_ref[...] += dw
        loss_ref[...] += jnp.sum(row_loss, axis=0, keepdims=True)

    return pl.pallas_call(
        body, name=name, grid=(t // tm,),
        in_specs=[_row_spec(d, tm), _row_spec(d, tm), _full_spec(w)],
        out_specs=[_row_spec(d, tm), _full_spec(w), pl.BlockSpec((1, 1), lambda i: (0, 0))],
        out_shape=[jax.ShapeDtypeStruct((t, d), F32), jax.ShapeDtypeStruct(w.shape, F32),
                   jax.ShapeDtypeStruct((1, 1), F32)],
        compiler_params=_cparams("arbitrary"),
    )(h, target, w)


def _pick_tile(rows, cap):
    if rows <= cap:
        return rows
    for tm in range(cap - cap % 16, 0, -16):
        if rows % tm == 0:
            return tm
    raise ValueError(rows)


def _adamw(name, w, g, m, v, tm):
    n_l, r, c = w.shape
    tm = _pick_tile(r, tm)

    def body(w_ref, g_ref, m_ref, v_ref, d_ref, nm_ref, nv_ref):
        g_ = g_ref[...]
        m_ = ADAM_B1 * m_ref[...] + (1.0 - ADAM_B1) * g_
        v_ = ADAM_B2 * v_ref[...] + (1.0 - ADAM_B2) * (g_ * g_)
        m_hat = m_ / (1.0 - ADAM_B1 ** ADAM_STEP)
        v_hat = v_ / (1.0 - ADAM_B2 ** ADAM_STEP)
        d_ref[...] = -ADAM_LR * (m_hat / (jnp.sqrt(v_hat) + ADAM_EPS) + ADAM_WD * w_ref[...])
        nm_ref[...] = m_
        nv_ref[...] = v_

    spec = pl.BlockSpec((1, tm, c), lambda l, i: (l, i, 0))
    return pl.pallas_call(
        body, name=name, grid=(n_l, r // tm), in_specs=[spec] * 4, out_specs=[spec] * 3,
        out_shape=[jax.ShapeDtypeStruct((n_l, r, c), F32)] * 3, compiler_params=_cparams("parallel", "parallel"),
    )(w, g, m, v)


def _sum_parts(name, layers, tm):
    n, r, c = layers[0].shape
    tm = _pick_tile(r, tm)
    nblk = r // tm

    def body(*refs):
        o_ref = refs[-1]
        for li, p_ref in enumerate(refs[:-1]):
            @pl.when(pl.program_id(0) == li)
            def _(p_ref=p_ref):
                s = p_ref[0].astype(F32)
                for d in range(1, n):
                    s = s + p_ref[d].astype(F32)
                o_ref[0] = s

    def spec(li):
        return pl.BlockSpec((n, tm, c), lambda l, i: (0, jnp.where(l == li, i, jnp.where(l < li, 0, nblk - 1)), 0))

    return pl.pallas_call(
        body, name=name, grid=(len(layers), nblk),
        in_specs=[spec(li) for li in range(len(layers))],
        out_specs=pl.BlockSpec((1, tm, c), lambda l, i: (l, i, 0)),
        out_shape=jax.ShapeDtypeStruct((len(layers), r, c), F32), compiler_params=_cparams("arbitrary", "arbitrary"),
    )(*layers)


def _peer(k):
    x, y, c = lax.axis_index("x"), lax.axis_index("y"), lax.axis_index("c")
    px, py, pc = x ^ ((k >> 2) & 1), y ^ ((k >> 1) & 1), c ^ (k & 1)
    return (px, py, pc), 4 * px + 2 * py + pc


def _exchange(name, xs, gather):
    rider = _Rider(xs, gather)
    n = rider.n

    def body(*refs):
        x_refs, out_refs, sems = refs[:n], refs[n:2 * n], refs[2 * n:]
        _exchange_start(x_refs, out_refs, sems, gather)
        _exchange_finish(x_refs, out_refs, sems, gather)

    return pl.pallas_call(
        body, name=name, in_specs=[ANY_SPEC] * n, out_specs=[ANY_SPEC] * n,
        out_shape=rider.out_shape(), scratch_shapes=rider.scratch(),
    )(*xs)


ANY_SPEC = pl.BlockSpec(memory_space=pl.ANY)
SAME_CORE = (2, 4, 6)


class _Rider:
    def __init__(self, xs, gather):
        self.xs, self.gather, self.n = list(xs), gather, len(xs)

    def out_shape(self):
        return [jax.ShapeDtypeStruct((N_DEV,) + tuple(x.shape if self.gather else x.shape[1:]), x.dtype)
                for x in self.xs]

    def scratch(self):
        return [pltpu.SemaphoreType.DMA((self.n, N_DEV - 1)), pltpu.SemaphoreType.DMA((self.n, N_DEV - 1)),
                pltpu.SemaphoreType.DMA((self.n,))]


def _remote(src, dst, sems, a, k, dev):
    return pltpu.make_async_remote_copy(
        src_ref=src, dst_ref=dst, send_sem=sems[0].at[a, k - 1], recv_sem=sems[1].at[a, k - 1],
        device_id=dev, device_id_type=pl.DeviceIdType.MESH)


def _exchange_start(x_refs, out_refs, sems, gather):
    _, me = _peer(0)
    for a, (x, out) in enumerate(zip(x_refs, out_refs)):
        pltpu.make_async_copy(x if gather else x.at[me], out.at[me], sems[2].at[a]).start()
    for k in ((1,) + SAME_CORE if gather else range(1, N_DEV)):
        dev, idx = _peer(k)
        for a, (x, out) in enumerate(zip(x_refs, out_refs)):
            _remote(x if gather else x.at[idx], out.at[me], sems, a, k, dev).start()


def _exchange_finish(x_refs, out_refs, sems, gather):
    _, me = _peer(0)
    sibling, _ = _peer(1)
    pairs = list(enumerate(zip(x_refs, out_refs)))
    waited = ()
    if gather:
        for k in SAME_CORE:
            dev, idx = _peer(k)
            for a, (x, out) in pairs:
                _remote(x, out.at[idx], sems, a, k, dev).wait_recv()
                _remote(out.at[idx], out.at[idx], sems, a, k + 1, sibling).start()
        waited = SAME_CORE
    for k in range(1, N_DEV):
        dev, idx = _peer(k)
        for a, (x, out) in pairs:
            cp = _remote(x if gather else x.at[idx], out.at[idx], sems, a, k, dev)
            if k not in waited:
                cp.wait_recv()
            cp.wait_send()
    for a, (x, out) in pairs:
        pltpu.make_async_copy(x if gather else x.at[me], out.at[me], sems[2].at[a]).wait()


def _hosted(body, rider, *, name, grid, in_specs, out_specs, out_shape, scratch_shapes, args):
    sem = ("arbitrary",) * len(grid)
    if rider is None:
        return pl.pallas_call(body, name=name, grid=grid, in_specs=in_specs, out_specs=out_specs, out_shape=out_shape,
                              scratch_shapes=scratch_shapes, compiler_params=_cparams(*sem))(*args), []
    n_in, n_out, n_scr, nr = len(in_specs), len(out_specs), len(scratch_shapes), rider.n

    def hosted(*refs):
        ins, refs = refs[:n_in], refs[n_in:]
        r_in, refs = refs[:nr], refs[nr:]
        outs, refs = refs[:n_out], refs[n_out:]
        r_out, refs = refs[:nr], refs[nr:]
        scr, sems = refs[:n_scr], refs[n_scr:]
        first = pl.program_id(0) == 0
        last = pl.program_id(0) == grid[0] - 1
        for ax in range(1, len(grid)):
            first = first & (pl.program_id(ax) == 0)
            last = last & (pl.program_id(ax) == grid[ax] - 1)

        @pl.when(first)
        def _():
            _exchange_start(r_in, r_out, sems, rider.gather)

        body(*ins, *outs, *scr)

        @pl.when(last)
        def _():
            _exchange_finish(r_in, r_out, sems, rider.gather)

    res = pl.pallas_call(
        hosted, name=name, grid=grid, in_specs=list(in_specs) + [ANY_SPEC] * nr,
        out_specs=list(out_specs) + [ANY_SPEC] * nr, out_shape=list(out_shape) + rider.out_shape(),
        scratch_shapes=list(scratch_shapes) + rider.scratch(), compiler_params=_cparams(*sem),
    )(*args, *rider.xs)
    return res[:n_out], res[n_out:]


PACK_ALIGN = 8 * LANES
PACK_ROWS = 512


def _pack(arrays, dtype):
    pieces, total = [], 0
    for a in arrays:
        f = a.reshape(-1).astype(dtype)
        pad = (-f.shape[0]) % PACK_ALIGN
        pieces.append(jnp.pad(f, (0, pad)) if pad else f)
        total += f.shape[0] + pad
    tail = (-total) % (PACK_ROWS * LANES)
    if tail:
        pieces.append(jnp.zeros((tail,), dtype))
    return jnp.concatenate(pieces).reshape(-1, LANES)


def _unpack(packed, shapes, lead=()):
    flat = packed.reshape(lead + (-1,))
    out, off = [], 0
    for s in shapes:
        n = 1
        for d in s:
            n *= d
        out.append(flat[..., off:off + n].reshape(lead + tuple(s)))
        off += n + ((-n) % PACK_ALIGN)
    return out


def _pad_lanes(a):
    return jnp.pad(a, (0, LANES - a.shape[0])).reshape(1, LANES)


ROW_TM = 256
EVEN_SEGS = (("z", D_MODEL), ("xbc", CONV_DIM), ("dt", SSD_HEADS), ("g", D_MODEL), ("u", D_MODEL), ("v", D_MODEL))
ODD_SEGS = (("q", D_MODEL), ("k", D_MODEL), ("v", D_MODEL), ("g", D_MODEL))


def _split_cols(w, segs):
    out, off = {}, 0
    for nm, n in segs:
        out[nm] = w[:, off:off + n]
        off += n
    return out


def _rms_fn(h, w):
    return (_rms(h, w),)


def _even_fwd(tag, h, p, rider=None, late=None):
    hn, = _rows_fwd(tag + "_norm", _rms_fn, [h], [p["norm_w"]], [(D_MODEL, BF16)], ROW_TM)
    proj = {nm: _matmul(f"{tag}_in_{nm}", hn, p["w_in"][nm], NN, F32) for nm, _ in EVEN_SEGS}
    pre, xs, bm, cm = _conv_fwd(tag + "_conv", proj["xbc"], p["conv_w"], p["conv_b"])
    (y_ssd, states), got = _ssd_fwd(tag + "_ssd", xs, bm, cm, proj["dt"], p["dt_bias"], p["a_log"], p["d_skip"], rider)
    if late is not None:
        p = dict(p, **late(got))
    ya, = _rows_fwd(tag + "_ssdgate", _ssd_gate, [y_ssd, proj["z"]], [p["ssd_norm_w"]], [(D_MODEL, BF16)], ROW_TM)
    vn, = _rows_fwd(tag + "_sgunorm", _sgu_norm, [proj["v"]], [p["sgu_ln_w"], p["sgu_ln_b"]], [(D_MODEL, F32)], ROW_TM)
    yb = _sgu_fwd(tag + "_sgu", proj["u"], proj["g"], vn, p["sgu_w"], p["sgu_b"])
    h1 = _matmul(tag + "_out_a", ya, p["w_out_a"], NN, F32, add=h)
    h2 = _matmul(tag + "_out_b", yb, p["w_out_b"], NN, F32, add=h1)
    saved = dict(h=h, hn=hn, proj=proj, pre=pre, xs=xs, bm=bm, cm=cm, y_ssd=y_ssd, states=states,
                 ya=ya, vn=vn, yb=yb)
    return h2, saved, p, got


def _even_bwd(tag, dh, s, p, make_rider=None):
    g = {}
    dh16 = dh.astype(BF16)
    proj = s["proj"]
    dya = _matmul(tag + "_dya", dh16, p["w_out_a"], NT, F32)
    dyb = _matmul(tag + "_dyb", dh16, p["w_out_b"], NT, F32)
    g["w_out_a"] = _matmul(tag + "_dwout_a", s["ya"], dh16, TN, BF16)
    g["w_out_b"] = _matmul(tag + "_dwout_b", s["yb"], dh16, TN, BF16)
    du, dg, dvn, g["sgu_w"], g["sgu_b"] = _sgu_bwd(tag + "_sgu_b", proj["u"], proj["g"], s["vn"],
                                                   p["sgu_w"], p["sgu_b"], dyb)
    dv, g["sgu_ln_w"], g["sgu_ln_b"] = _rows_bwd(tag + "_sgunorm_b", _sgu_norm, [proj["v"]],
                                                 [p["sgu_ln_w"], p["sgu_ln_b"]], [dvn], [BF16], ROW_TM // 2)
    dy_ssd, dz, g["ssd_norm_w"] = _rows_bwd(tag + "_ssdgate_b", _ssd_gate, [s["y_ssd"], proj["z"]],
                                            [p["ssd_norm_w"]], [dya], [F32, BF16], ROW_TM // 2)
    (dxs, dbm, dcm, ddt, g["dt_bias"], g["a_log"], g["d_skip"]), got = _ssd_bwd(
        tag + "_ssd_b", s["xs"], s["bm"], s["cm"], proj["dt"], p["dt_bias"], p["a_log"], p["d_skip"],
        s["states"], dy_ssd, None if make_rider is None else make_rider(g))
    dxbc, g["conv_w"], g["conv_b"] = _conv_bwd(tag + "_conv_b", proj["xbc"], s["pre"], p["conv_w"], dxs, dbm, dcm)
    dproj = dict(z=dz, xbc=dxbc, dt=ddt.astype(BF16), g=dg, u=du, v=dv)
    dhn = None
    g["w_in"] = {}
    for nm, _ in EVEN_SEGS:
        dhn = _matmul(f"{tag}_dhn_{nm}", dproj[nm], p["w_in"][nm], NT, F32, add=dhn)
        g["w_in"][nm] = _matmul(f"{tag}_dwin_{nm}", s["hn"], dproj[nm], TN, BF16)
    dh_in, g["norm_w"] = _rows_bwd(tag + "_norm_b", _rms_fn, [s["h"]], [p["norm_w"]], [dhn], [F32],
                                   ROW_TM // 2, add=(0, dh))
    return dh_in, g, got


def _odd_fwd(tag, h, p, rider=None):
    hn, = _rows_fwd(tag + "_norm", _rms_fn, [h], [p["norm_w"]], [(D_MODEL, BF16)], ROW_TM)
    q = _matmul(tag + "_in_q", hn, p["w_in"]["q"], NN, BF16)
    k = _matmul(tag + "_in_k", hn, p["w_in"]["k"], NN, BF16)
    v = _matmul(tag + "_in_v", hn, p["w_in"]["v"], NN, BF16)
    gate = _matmul(tag + "_in_g", hn, p["w_in"]["g"], NN, F32)
    (y, tot), got = _attn_fwd(tag + "_attn", q, k, v, rider)
    yg, = _rows_fwd(tag + "_gate", _attn_gate, [y, gate], [], [(D_MODEL, BF16)], ROW_TM)
    h1 = _matmul(tag + "_out", yg, p["w_out"], NN, F32, add=h)
    return h1, dict(h=h, hn=hn, q=q, k=k, v=v, gate=gate, y=y, tot=tot, yg=yg), got


def _odd_bwd(tag, dh, s, p, rider=None):
    g = {}
    dh16 = dh.astype(BF16)
    dyg = _matmul(tag + "_dyg", dh16, p["w_out"], NT, F32)
    g["w_out"] = _matmul(tag + "_dwout", s["yg"], dh16, TN, BF16)
    dy, dgate = _rows_bwd(tag + "_gate_b", _attn_gate, [s["y"], s["gate"]], [], [dyg], [F32, BF16], ROW_TM)
    (dq, dk, dv), got = _attn_bwd(tag + "_attn_b", s["q"], s["k"], s["v"], s["tot"], dy, rider)
    dproj = dict(q=dq, k=dk, v=dv, g=dgate)
    dhn = None
    g["w_in"] = {}
    for nm, _ in ODD_SEGS:
        dhn = _matmul(f"{tag}_dhn_{nm}", dproj[nm], p["w_in"][nm], NT, F32, add=dhn)
        g["w_in"][nm] = _matmul(f"{tag}_dwin_{nm}", s["hn"], dproj[nm], TN, BF16)
    dh_in, g["norm_w"] = _rows_bwd(tag + "_norm_b", _rms_fn, [s["h"]], [p["norm_w"]], [dhn], [F32],
                                   ROW_TM // 2, add=(0, dh))
    return dh_in, g, got


BIG = ("ev_w_in", "ev_w_out", "od_w_in", "od_w_out")
SMALL = ("norm_w", "final_norm_w", "ev_conv_b", "ev_dt_bias", "ev_a_log", "ev_d_skip", "ev_ssd_norm_w",
         "ev_sgu_ln_w", "ev_sgu_ln_b", "ev_sgu_w", "ev_sgu_b")
WEIGHTS = ("norm_w", "final_norm_w", "ev_w_in", "ev_conv_w", "ev_conv_b", "ev_dt_bias", "ev_a_log", "ev_d_skip",
           "ev_ssd_norm_w", "ev_sgu_ln_w", "ev_sgu_ln_b", "ev_sgu_w", "ev_sgu_b", "ev_w_out", "od_w_in", "od_w_out")


def _step(w, m, v, x, loss_target):
    h = x[0]
    tgt = loss_target[0]
    n_even, n_odd = w["ev_w_in"].shape[0], w["od_w_in"].shape[0]
    depth = n_even + n_odd
    assert (n_even, n_odd) == (2, 2), "the exchange schedule below is written for the four-layer trunk"

    def shard(n, i):
        return w[n][i].astype(BF16)

    def take_cols(blocks, widths, lo, hi):
        pieces, start = [], 0
        for blk, wd in zip(blocks, widths):
            a, b = max(lo - start, 0), min(hi - start, wd)
            if a < b:
                pieces.append(blk[:, a:b])
            start += wd
        return pieces[0] if len(pieces) == 1 else jnp.concatenate(pieces, axis=1)

    def segments(gathered, segs):
        n = gathered.shape[2]
        out, lo = {}, 0
        for nm, wd in segs:
            out[nm] = take_cols([gathered[d] for d in range(N_DEV)], [n] * N_DEV, lo, lo + wd)
            lo += wd
        return out

    def rows(gathered):
        return gathered.reshape(-1, gathered.shape[2])

    def by_owner_cols(g_in, segs):
        blocks, widths = [g_in[nm] for nm, _ in segs], [wd for _, wd in segs]
        n = sum(widths) // N_DEV
        return jnp.stack([take_cols(blocks, widths, d * n, (d + 1) * n) for d in range(N_DEV)])

    def by_owner_rows(full):
        return full.reshape(N_DEV, full.shape[0] // N_DEV, full.shape[1])

    def even_params(layer, ev_in):
        i = layer // 2
        w_in = segments(ev_in, EVEN_SEGS)
        w_in["dt"] = jnp.pad(w_in["dt"], ((0, 0), (0, LANES - SSD_HEADS)))
        return dict(norm_w=w["norm_w"][layer][None], w_in=w_in, conv_w=conv_w[i], conv_b=w["ev_conv_b"][i][None],
                    dt_bias=_pad_lanes(w["ev_dt_bias"][i]), a_log=_pad_lanes(w["ev_a_log"][i]),
                    d_skip=_pad_lanes(w["ev_d_skip"][i]), ssd_norm_w=w["ev_ssd_norm_w"][i][None],
                    sgu_ln_w=w["ev_sgu_ln_w"][i][None], sgu_ln_b=w["ev_sgu_ln_b"][i][None],
                    sgu_w=w["ev_sgu_w"][i], sgu_b=w["ev_sgu_b"][i][:, :, None])

    def even_out_params(ev_out):
        full = rows(ev_out)
        return dict(w_out_a=full[:D_MODEL], w_out_b=full[D_MODEL:])

    def odd_params(layer, od_in, od_out):
        return dict(norm_w=w["norm_w"][layer][None], w_in=segments(od_in, ODD_SEGS), w_out=rows(od_out))

    def even_in_grads(g):
        return by_owner_cols(g["w_in"], EVEN_SEGS)

    def even_out_grads(g):
        return by_owner_rows(jnp.concatenate([g["w_out_a"], g["w_out_b"]], axis=0))

    def odd_grads(g):
        return [by_owner_cols(g["w_in"], ODD_SEGS), by_owner_rows(g["w_out"])]

    ev_in0, conv_w = _exchange("gather_first", [shard("ev_w_in", 0), w["ev_conv_w"]], gather=True)
    conv_w = jnp.moveaxis(conv_w, 0, 2).reshape(n_even, CONV_WIDTH, CONV_DIM)
    h, s0, p0, got = _even_fwd(
        "l0", h, even_params(0, ev_in0),
        _Rider([shard("ev_w_out", 0), shard("od_w_in", 0), shard("od_w_out", 0)], True),
        late=lambda arrived: even_out_params(arrived[0]))
    p1 = odd_params(1, got[1], got[2])
    h, s1, got = _odd_fwd("l1", h, p1, _Rider([shard("ev_w_in", 1), shard("ev_w_out", 1)], True))
    p2 = dict(even_params(2, got[0]), **even_out_params(got[1]))
    h, s2, p2, got = _even_fwd("l2", h, p2, _Rider([shard("od_w_in", 1), shard("od_w_out", 1)], True))
    p3 = odd_params(3, got[0], got[1])
    h, s3, _ = _odd_fwd("l3", h, p3)

    dh, d_final, loss_part = _loss_head("loss_head", h, tgt, w["final_norm_w"][None], ROW_TM // 2)
    loss = lax.psum(loss_part[0, 0], ("x", "y", "c"))

    dh, g3, _ = _odd_bwd("l3", dh, s3, p3)
    dh, g2, from3 = _even_bwd("l2", dh, s2, p2, lambda g: _Rider(odd_grads(g3), False))
    dh, g1, from2 = _odd_bwd("l1", dh, s1, p1, _Rider([even_in_grads(g2), even_out_grads(g2)], False))
    dh, g0, from1 = _even_bwd("l0", dh, s0, p0, lambda g: _Rider(odd_grads(g1) + [even_out_grads(g)], False))
    from0, = _exchange("scatter_last", [even_in_grads(g0)], gather=False)
    grad_x = dh[None]
    lg = [g0, g1, g2, g3]
    ev, od = [g0, g2], [g1, g3]

    def total(tag, parts):
        return _sum_parts("sum_" + tag, parts, 256)

    big_grads = {
        "ev_w_in": total("ev_w_in", [from0, from2[0]]),
        "ev_w_out": total("ev_w_out", [from1[2], from2[1]]),
        "od_w_in": total("od_w_in", [from1[0], from3[0]]),
        "od_w_out": total("od_w_out", [from1[1], from3[1]]),
    }

    small_g = {
        "norm_w": jnp.concatenate([lg[l]["norm_w"] for l in range(depth)], axis=0),
        "final_norm_w": d_final[0],
        "ev_conv_b": jnp.concatenate([e["conv_b"] for e in ev], axis=0),
        "ev_dt_bias": jnp.concatenate([e["dt_bias"][:, :SSD_HEADS] for e in ev], axis=0),
        "ev_a_log": jnp.concatenate([e["a_log"][:, :SSD_HEADS] for e in ev], axis=0),
        "ev_d_skip": jnp.concatenate([e["d_skip"][:, :SSD_HEADS] for e in ev], axis=0),
        "ev_ssd_norm_w": jnp.concatenate([e["ssd_norm_w"] for e in ev], axis=0),
        "ev_sgu_ln_w": jnp.concatenate([e["sgu_ln_w"] for e in ev], axis=0),
        "ev_sgu_ln_b": jnp.concatenate([e["sgu_ln_b"] for e in ev], axis=0),
        "ev_sgu_w": jnp.stack([e["sgu_w"] for e in ev]),
        "ev_sgu_b": jnp.stack([e["sgu_b"][:, :, 0] for e in ev]),
    }
    conv_g = jnp.stack([e["conv_w"] for e in ev])
    small_shapes = [w[n].shape for n in SMALL]
    small_parts, = _exchange("gather_small", [_pack([small_g[n] for n in SMALL] + [conv_g], F32)], gather=True)
    small_sum = _sum_parts("sum_small", [small_parts], 1024)[0]
    *small_list, conv_full = _unpack(small_sum, small_shapes + [conv_g.shape])
    grads = dict(zip(SMALL, small_list))
    grads.update(big_grads)
    me = 4 * lax.axis_index("x") + 2 * lax.axis_index("y") + lax.axis_index("c")
    n_cv = w["ev_conv_w"].shape[2]
    grads["ev_conv_w"] = lax.dynamic_slice_in_dim(conv_full, me * n_cv, n_cv, axis=2)

    deltas, new_m, new_v = {}, {}, {}
    for n in BIG + ("ev_conv_w",):
        deltas[n], new_m[n], new_v[n] = _adamw("adamw_" + n, w[n], grads[n], m[n], v[n], 256)
    packs = [_pack([src[n] for n in SMALL], F32)[None] for src in (w, grads, m, v)]
    outs = _adamw("adamw_small", *packs, 1024)
    for dst, packed in zip((deltas, new_m, new_v), outs):
        dst.update(zip(SMALL, _unpack(packed[0], small_shapes)))
    return loss, grad_x, grads, deltas, new_m, new_v


def kernel(x, norm_w, final_norm_w, ev_w_in, ev_conv_w, ev_conv_b, ev_dt_bias, ev_a_log, ev_d_skip, ev_ssd_norm_w, ev_sgu_ln_w, ev_sgu_ln_b, ev_sgu_w, ev_sgu_b, ev_w_out, od_w_in, od_w_out, loss_target, m_norm_w, m_final_norm_w, m_ev_w_in, m_ev_conv_w, m_ev_conv_b, m_ev_dt_bias, m_ev_a_log, m_ev_d_skip, m_ev_ssd_norm_w, m_ev_sgu_ln_w, m_ev_sgu_ln_b, m_ev_sgu_w, m_ev_sgu_b, m_ev_w_out, m_od_w_in, m_od_w_out, v_norm_w, v_final_norm_w, v_ev_w_in, v_ev_conv_w, v_ev_conv_b, v_ev_dt_bias, v_ev_a_log, v_ev_d_skip, v_ev_ssd_norm_w, v_ev_sgu_ln_w, v_ev_sgu_ln_b, v_ev_sgu_w, v_ev_sgu_b, v_ev_w_out, v_od_w_in, v_od_w_out):
    w = dict(zip(WEIGHTS, (norm_w, final_norm_w, ev_w_in, ev_conv_w, ev_conv_b, ev_dt_bias, ev_a_log, ev_d_skip,
                           ev_ssd_norm_w, ev_sgu_ln_w, ev_sgu_ln_b, ev_sgu_w, ev_sgu_b, ev_w_out, od_w_in, od_w_out)))
    m = dict(zip(WEIGHTS, (m_norm_w, m_final_norm_w, m_ev_w_in, m_ev_conv_w, m_ev_conv_b, m_ev_dt_bias, m_ev_a_log,
                           m_ev_d_skip, m_ev_ssd_norm_w, m_ev_sgu_ln_w, m_ev_sgu_ln_b, m_ev_sgu_w, m_ev_sgu_b,
                           m_ev_w_out, m_od_w_in, m_od_w_out)))
    v = dict(zip(WEIGHTS, (v_norm_w, v_final_norm_w, v_ev_w_in, v_ev_conv_w, v_ev_conv_b, v_ev_dt_bias, v_ev_a_log,
                           v_ev_d_skip, v_ev_ssd_norm_w, v_ev_sgu_ln_w, v_ev_sgu_ln_b, v_ev_sgu_w, v_ev_sgu_b,
                           v_ev_w_out, v_od_w_in, v_od_w_out)))
    loss, grad_x, grads, deltas, new_m, new_v = _step(w, m, v, x, loss_target)
    return (loss, grad_x, *[grads[n] for n in WEIGHTS], *[deltas[n] for n in WEIGHTS],
            *[new_m[n] for n in WEIGHTS], *[new_v[n] for n in WEIGHTS])
```

```python
import jax
import jax.numpy as jnp
from jax import lax
from jax.experimental import pallas as pl
from jax.experimental.pallas import tpu as pltpu

F32, BF16 = jnp.float32, jnp.bfloat16

D_MODEL = 2048
SSD_HEADS = 32
SSD_HEAD_DIM = 64
SSD_GROUPS = 4
SSD_STATE = 128
CHUNK = 128
CONV_WIDTH = 4
CONV_DIM = D_MODEL + 2 * SSD_GROUPS * SSD_STATE
SGU_GROUPS = 16
SB_HEADS = 16
LANES = 128
N_PAIRS = SSD_HEADS // 2
PAIRS_PER_GROUP = N_PAIRS // SSD_GROUPS
NORM_EPS = 1e-5
N_DEV = 8

ADAM_LR, ADAM_B1, ADAM_B2, ADAM_EPS, ADAM_WD, ADAM_STEP = 0.001, 0.9, 0.999, 1e-08, 0.01, 10

VMEM_LIMIT_BYTES = 48 * 1024 * 1024

NN = ((1,), (0,))
NT = ((1,), (1,))
TN = ((0,), (0,))


def _cparams(*sem):
    return pltpu.CompilerParams(dimension_semantics=sem, vmem_limit_bytes=VMEM_LIMIT_BYTES)


def _dg(a, b, dims):
    return lax.dot_general(a, b, (dims, ((), ())), preferred_element_type=F32)


def _make_bdot(dims):
    @jax.custom_vjp
    def f(a, b):
        return _dg(a.astype(BF16), b.astype(BF16), dims)

    def fwd(a, b):
        return f(a, b), (a, b)

    def bwd(res, g):
        a, b = res
        a16, b16, g16 = a.astype(BF16), b.astype(BF16), g.astype(BF16)
        if dims == NN:
            da, db = _dg(g16, b16, NT), _dg(a16, g16, TN)
        elif dims == NT:
            da, db = _dg(g16, b16, NN), _dg(g16, a16, TN)
        else:
            da, db = _dg(b16, g16, NT), _dg(a16, g16, NN)
        return da.astype(a.dtype), db.astype(b.dtype)

    f.defvjp(fwd, bwd)
    return f


_bdot_nn, _bdot_nt, _bdot_tn = _make_bdot(NN), _make_bdot(NT), _make_bdot(TN)


def _iota2(shape, axis):
    return lax.broadcasted_iota(jnp.int32, shape, axis)


def _split3(x, axis):
    hi = x.astype(BF16)
    r = x - hi.astype(F32)
    mid = r.astype(BF16)
    return jnp.concatenate([hi, mid, (r - mid.astype(F32)).astype(BF16)], axis=axis)


def _make_onehot_dot(build, build_t, left):
    def apply(x, e):
        e = e.astype(BF16)
        if left:
            return _dg(jnp.concatenate([e, e, e], axis=1), _split3(x, 0), NN)
        return _dg(_split3(x, 1), jnp.concatenate([e, e, e], axis=0), NN)

    @jax.custom_vjp
    def f(x):
        return apply(x, build())

    f.defvjp(lambda x: (f(x), None), lambda _, g: (apply(g, build_t()),))
    return f


_cumsum_rows = _make_onehot_dot(lambda: _iota2((CHUNK, CHUNK), 1) <= _iota2((CHUNK, CHUNK), 0),
                                lambda: _iota2((CHUNK, CHUNK), 0) <= _iota2((CHUNK, CHUNK), 1), True)


def _softplus(x):
    return jnp.maximum(x, 0.0) + jnp.log1p(jnp.exp(-jnp.abs(x)))


def _silu(x):
    return x * jax.nn.sigmoid(x)


def _gelu(x):
    return 0.5 * x * (1.0 + jnp.tanh(0.7978845608028654 * (x + 0.044715 * (x * x * x))))


def _rms(x, w):
    return x * lax.rsqrt(jnp.mean(x * x, axis=-1, keepdims=True) + NORM_EPS) * w


def _layer_norm(x, w, b):
    xc = x - jnp.mean(x, axis=-1, keepdims=True)
    return xc * lax.rsqrt(jnp.mean(xc * xc, axis=-1, keepdims=True) + NORM_EPS) * w + b


def _row_spec(width, tm):
    return pl.BlockSpec((tm, width), lambda i: (i, 0))


def _full_spec(p):
    zeros = (0,) * p.ndim
    return pl.BlockSpec(p.shape, lambda i: zeros)


def _rows_fwd(name, fn, tiled, params, outs, tm):
    n_rows = tiled[0].shape[0]
    n_in = len(tiled) + len(params)

    def body(*refs):
        res = fn(*[r[...] for r in refs[:n_in]])
        for o_ref, o in zip(refs[n_in:], res):
            o_ref[...] = o.astype(o_ref.dtype)

    return pl.pallas_call(
        body, name=name, grid=(n_rows // tm,),
        in_specs=[_row_spec(a.shape[1], tm) for a in tiled] + [_full_spec(p) for p in params],
        out_specs=[_row_spec(w, tm) for w, _ in outs],
        out_shape=[jax.ShapeDtypeStruct((n_rows, w), d) for w, d in outs],
        compiler_params=_cparams("parallel"),
    )(*tiled, *params)


def _rows_bwd(name, fn, tiled, params, cots, grad_dtypes, tm, add=None):
    n_rows = tiled[0].shape[0]
    nt, npar, nc = len(tiled), len(params), len(cots)
    want = [k for k, d in enumerate(grad_dtypes) if d is not None]
    n_add = 0 if add is None else 1

    def body(*refs):
        ins = [r[...] for r in refs[:nt + npar]]
        c_refs = refs[nt + npar:nt + npar + nc]
        add_refs = refs[nt + npar + nc:nt + npar + nc + n_add]
        o_refs = refs[nt + npar + nc + n_add:]
        res, vjp = jax.vjp(fn, *ins)
        grads = vjp(tuple(c[...].astype(r.dtype) for c, r in zip(c_refs, res)))
        for pos, k in enumerate(want):
            gk = grads[k]
            if add is not None and add[0] == pos:
                gk = gk + add_refs[0][...]
            o_refs[pos][...] = gk.astype(o_refs[pos].dtype)
        p_refs = o_refs[len(want):]

        @pl.when(pl.program_id(0) == 0)
        def _():
            for r in p_refs:
                r[...] = jnp.zeros_like(r)

        for r, gp in zip(p_refs, grads[nt:]):
            r[...] += gp

    add_arrays = [] if add is None else [add[1]]
    out = pl.pallas_call(
        body, name=name, grid=(n_rows // tm,),
        in_specs=([_row_spec(a.shape[1], tm) for a in tiled] + [_full_spec(p) for p in params]
                  + [_row_spec(c.shape[1], tm) for c in cots] + [_row_spec(a.shape[1], tm) for a in add_arrays]),
        out_specs=([_row_spec(tiled[k].shape[1], tm) for k in want] + [_full_spec(p) for p in params]),
        out_shape=([jax.ShapeDtypeStruct(tiled[k].shape, grad_dtypes[k]) for k in want]
                   + [jax.ShapeDtypeStruct(p.shape, F32) for p in params]),
        compiler_params=_cparams("arbitrary"),
    )(*tiled, *params, *cots, *add_arrays)
    return out


def _matmul(name, a, b, dims, out_dtype, add=None, rider=None, tm=1024, tn=1024, tk=2048):
    if dims == NN:
        (m, k), n = a.shape, b.shape[1]
    elif dims == NT:
        (m, k), n = a.shape, b.shape[0]
    else:
        (k, m), n = a.shape, b.shape[1]
    tm, tn, tk = min(tm, m), min(tn, n), min(tk, k)
    while k % tk:
        tk -= LANES
    assert m % tm == 0 and n % tn == 0 and k % tk == 0, (name, m, n, k)
    nk = k // tk
    a_spec = (pl.BlockSpec((tk, tm), lambda i, j, kk: (kk, i)) if dims == TN
              else pl.BlockSpec((tm, tk), lambda i, j, kk: (i, kk)))
    b_spec = (pl.BlockSpec((tn, tk), lambda i, j, kk: (j, kk)) if dims == NT
              else pl.BlockSpec((tk, tn), lambda i, j, kk: (kk, j)))
    o_spec = pl.BlockSpec((tm, tn), lambda i, j, kk: (i, j))
    has_add = add is not None

    def body(*refs):
        a_ref, b_ref = refs[0], refs[1]
        part = _dg(a_ref[...].astype(BF16), b_ref[...].astype(BF16), dims)
        if nk == 1:
            o_ref = refs[-1]
            if has_add:
                part = part + refs[2][...]
            o_ref[...] = part.astype(o_ref.dtype)
            return
        o_ref, acc = refs[-2], refs[-1]
        kk = pl.program_id(2)

        @pl.when(kk == 0)
        def _():
            acc[...] = part

        @pl.when(kk > 0)
        def _():
            acc[...] += part

        @pl.when(kk == nk - 1)
        def _():
            r = acc[...]
            if has_add:
                r = r + refs[2][...]
            o_ref[...] = r.astype(o_ref.dtype)

    in_specs = [a_spec, b_spec] + ([o_spec] if has_add else [])
    scratch = [pltpu.VMEM((tm, tn), F32)] if nk > 1 else []
    args = (a, b) + ((add,) if has_add else ())
    if rider is not None:
        (out,), got = _hosted(body, rider, name=name, grid=(m // tm, n // tn, nk), in_specs=in_specs,
                              out_specs=[o_spec], out_shape=[jax.ShapeDtypeStruct((m, n), out_dtype)],
                              scratch_shapes=scratch, args=args)
        return out, got
    return pl.pallas_call(
        body, name=name, grid=(m // tm, n // tn, nk), in_specs=in_specs, out_specs=o_spec,
        out_shape=jax.ShapeDtypeStruct((m, n), out_dtype), scratch_shapes=scratch,
        compiler_params=_cparams("parallel", "parallel", "arbitrary"),
    )(*args)


CONV_TM = 256
HALO = 8


def _shift_down(x, halo, j):
    if j == 0:
        return x, x[:HALO]
    xr = pltpu.roll(x, j, 0)
    hr = pltpu.roll(halo, j, 0)
    top = jnp.where(_iota2((HALO, x.shape[1]), 0) < j, hr, xr[:HALO])
    return xr, top


def _conv_fwd(name, x, w, b):
    t, c = x.shape
    tm = min(CONV_TM, t)
    hb = tm // HALO

    def body(x_ref, halo_ref, w_ref, b_ref, pre_ref, xs_ref, bm_ref, cm_ref):
        i = pl.program_id(0)
        xv = x_ref[...]
        halo = jnp.where(i > 0, halo_ref[...], 0.0)
        main = jnp.zeros_like(xv) + b_ref[...]
        top = jnp.zeros((HALO, c), F32) + b_ref[...]
        for kk in range(CONV_WIDTH):
            xr, tp = _shift_down(xv, halo, CONV_WIDTH - 1 - kk)
            main = main + w_ref[kk:kk + 1, :] * xr
            top = top + w_ref[kk:kk + 1, :] * tp
        pre = jnp.concatenate([top, main[HALO:]], axis=0)
        pre_ref[...] = pre
        act = _silu(pre)
        xs_ref[...] = act[:, :D_MODEL]
        bm_ref[...] = act[:, D_MODEL:D_MODEL + SSD_GROUPS * SSD_STATE]
        cm_ref[...] = act[:, D_MODEL + SSD_GROUPS * SSD_STATE:]

    gs = SSD_GROUPS * SSD_STATE
    return pl.pallas_call(
        body, name=name, grid=(t // tm,),
        in_specs=[_row_spec(c, tm),
                  pl.BlockSpec((HALO, c), lambda i: (jnp.maximum(i * hb - 1, 0), 0)),
                  _full_spec(w), _full_spec(b)],
        out_specs=[_row_spec(c, tm), _row_spec(D_MODEL, tm), _row_spec(gs, tm), _row_spec(gs, tm)],
        out_shape=[jax.ShapeDtypeStruct((t, c), F32), jax.ShapeDtypeStruct((t, D_MODEL), F32),
                   jax.ShapeDtypeStruct((t, gs), F32), jax.ShapeDtypeStruct((t, gs), F32)],
        compiler_params=_cparams("parallel"),
    )(x, x, w, b)


def _dsilu(pre, dact):
    s = jax.nn.sigmoid(pre)
    return dact * (s * (1.0 + pre * (1.0 - s)))


def _conv_bwd(name, x, pre, w, dxs, dbm, dcm):
    t, c = x.shape
    tm = min(CONV_TM, t)
    hb = tm // HALO
    last_hb = t // HALO - 1
    n_tiles = t // tm

    def body(x_ref, xh_ref, pre_ref, preh_ref, w_ref, dxs_ref, dbm_ref, dcm_ref,
             dxsh_ref, dbmh_ref, dcmh_ref, dx_ref, dw_ref, db_ref):
        i = pl.program_id(0)
        dact = jnp.concatenate([dxs_ref[...], dbm_ref[...], dcm_ref[...]], axis=1)
        dpre = _dsilu(pre_ref[...], dact)
        dact_h = jnp.concatenate([dxsh_ref[...], dbmh_ref[...], dcmh_ref[...]], axis=1)
        dpre_h = jnp.where(i < n_tiles - 1, _dsilu(preh_ref[...], dact_h), 0.0)
        xv = x_ref[...]
        xh = jnp.where(i > 0, xh_ref[...], 0.0)

        @pl.when(i == 0)
        def _():
            dw_ref[...] = jnp.zeros_like(dw_ref)
            db_ref[...] = jnp.zeros_like(db_ref)

        db_ref[...] += jnp.sum(dpre, axis=0, keepdims=True)
        dxm = jnp.zeros_like(xv)
        dxt = jnp.zeros((HALO, c), F32)
        row8 = _iota2((HALO, c), 0)
        for kk in range(CONV_WIDTH):
            j = CONV_WIDTH - 1 - kk
            wk = w_ref[kk:kk + 1, :]
            xr, tp = _shift_down(xv, xh, j)
            full = jnp.sum(dpre * xr, axis=0, keepdims=True)
            fix = jnp.sum(dpre[:HALO] * (tp - xr[:HALO]), axis=0, keepdims=True)
            dw_ref[kk:kk + 1, :] += full + fix
            if j == 0:
                dxm = dxm + wk * dpre
                dxt = dxt + wk * dpre[tm - HALO:]
            else:
                dr = pltpu.roll(dpre, tm - j, 0)
                hr = pltpu.roll(dpre_h, HALO - j, 0)
                dxm = dxm + wk * dr
                dxt = dxt + wk * jnp.where(row8 >= HALO - j, hr, dr[tm - HALO:])
        dx_ref[...] = jnp.concatenate([dxm[:tm - HALO], dxt], axis=0).astype(dx_ref.dtype)

    gs = SSD_GROUPS * SSD_STATE
    prev_halo = lambda i: (jnp.maximum(i * hb - 1, 0), 0)
    next_halo = lambda i: (jnp.minimum((i + 1) * hb, last_hb), 0)
    return pl.pallas_call(
        body, name=name, grid=(n_tiles,),
        in_specs=[_row_spec(c, tm), pl.BlockSpec((HALO, c), prev_halo),
                  _row_spec(c, tm), pl.BlockSpec((HALO, c), next_halo), _full_spec(w),
                  _row_spec(D_MODEL, tm), _row_spec(gs, tm), _row_spec(gs, tm),
                  pl.BlockSpec((HALO, D_MODEL), next_halo), pl.BlockSpec((HALO, gs), next_halo),
                  pl.BlockSpec((HALO, gs), next_halo)],
        out_specs=[_row_spec(c, tm), _full_spec(w), pl.BlockSpec((1, c), lambda i: (0, 0))],
        out_shape=[jax.ShapeDtypeStruct((t, c), BF16), jax.ShapeDtypeStruct(w.shape, F32),
                   jax.ShapeDtypeStruct((1, c), F32)],
        compiler_params=_cparams("arbitrary"),
    )(x, x, pre, pre, w, dxs, dbm, dcm, dxs, dbm, dcm)


def _ssd_prep(dt_raw, dt_bias, a_log):
    dt = _softplus(dt_raw + dt_bias)
    return dt, _cumsum_rows(dt * (-jnp.exp(a_log)))


def _head_col(x, h):
    return jnp.sum(jnp.where(_iota2(x.shape, 1) == h, x, 0.0), axis=1, keepdims=True)


def _ssd_pair(p, xs, dt, cs, d_skip, bg, cg, prev):
    l = xs.shape[0]
    first = _iota2((l, LANES), 1) < SSD_HEAD_DIM
    c0, c1 = _head_col(cs, 2 * p), _head_col(cs, 2 * p + 1)
    csf = jnp.where(first, c0, c1)
    dtf = jnp.where(first, _head_col(dt, 2 * p), _head_col(dt, 2 * p + 1))
    dskf = jnp.where(_iota2((1, LANES), 1) < SSD_HEAD_DIM, _head_col(d_skip, 2 * p), _head_col(d_skip, 2 * p + 1))
    cs_last = jnp.sum(jnp.where(_iota2((l, LANES), 0) == l - 1, csf, 0.0), axis=0, keepdims=True)
    xc = xs * dtf
    scores = _bdot_nt(cg, bg)
    causal = _iota2((l, l), 0) >= _iota2((l, l), 1)

    def decay(col):
        a = jnp.broadcast_to(col, (l, l))
        return jnp.where(causal, jnp.exp(jnp.where(causal, a - a.T, 0.0)), 0.0)

    y_diag = (_bdot_nn(scores * decay(c0), jnp.where(first, xc, 0.0))
              + _bdot_nn(scores * decay(c1), jnp.where(first, 0.0, xc)))
    states = _bdot_tn(bg, xc * jnp.exp(cs_last - csf))
    new_state = jnp.exp(cs_last) * prev + states
    y_off = _bdot_nn(cg, prev) * jnp.exp(csf)
    return y_diag + y_off + xs * dskf, new_state


def _ssd_specs(nc_rev=None):
    def ch(c):
        return c if nc_rev is None else nc_rev - 1 - c

    gs = SSD_GROUPS * SSD_STATE
    return dict(
        wide=pl.BlockSpec((CHUNK, D_MODEL), lambda c: (ch(c), 0)),
        group=pl.BlockSpec((CHUNK, gs), lambda c: (ch(c), 0)),
        chunk=pl.BlockSpec((CHUNK, LANES), lambda c: (ch(c), 0)),
        vec=pl.BlockSpec((1, LANES), lambda c: (0, 0)),
        state=pl.BlockSpec((1, N_PAIRS, LANES, LANES), lambda c: (ch(c), 0, 0, 0)),
    )


def _pair_slices(p):
    g = p // PAIRS_PER_GROUP
    return slice(p * LANES, (p + 1) * LANES), slice(g * LANES, (g + 1) * LANES)


def _ssd_fwd(name, xs, bm, cm, dt_raw, dt_bias, a_log, d_skip, rider=None):
    t = xs.shape[0]
    nc = t // CHUNK

    def body(xs_ref, b_ref, c_ref, dt_ref, bias_ref, alog_ref, dsk_ref, y_ref, prev_ref, state, dt_s, cs_s):
        c = pl.program_id(0)
        dt_s[...], cs_s[...] = _ssd_prep(dt_ref[...], bias_ref[...], alog_ref[...])

        @pl.when(c == 0)
        def _():
            state[...] = jnp.zeros_like(state)

        for p in range(N_PAIRS):
            sl, gsl = _pair_slices(p)
            prev = state[p]
            prev_ref[0, p] = prev
            y, new_state = _ssd_pair(p, xs_ref[:, sl], dt_s[...], cs_s[...], dsk_ref[...],
                                     b_ref[:, gsl], c_ref[:, gsl], prev)
            y_ref[:, sl] = y
            state[p] = new_state

    sp = _ssd_specs()
    return _hosted(
        body, rider, name=name, grid=(nc,),
        in_specs=[sp["wide"], sp["group"], sp["group"], sp["chunk"], sp["vec"], sp["vec"], sp["vec"]],
        out_specs=[sp["wide"], sp["state"]],
        out_shape=[jax.ShapeDtypeStruct((t, D_MODEL), F32),
                   jax.ShapeDtypeStruct((nc, N_PAIRS, LANES, LANES), F32)],
        scratch_shapes=[pltpu.VMEM((N_PAIRS, LANES, LANES), F32), pltpu.VMEM((CHUNK, LANES), F32),
                        pltpu.VMEM((CHUNK, LANES), F32)],
        args=(xs, bm, cm, dt_raw, dt_bias, a_log, d_skip))


def _ssd_bwd(name, xs, bm, cm, dt_raw, dt_bias, a_log, d_skip, prev_states, dy, rider=None):
    t = xs.shape[0]
    nc = t // CHUNK

    def body(xs_ref, b_ref, c_ref, dt_ref, bias_ref, alog_ref, dsk_ref, prev_ref, dy_ref,
             dxs_ref, db_ref, dc_ref, ddt_ref, dbias_ref, dalog_ref, ddsk_ref, dstate, dt_s, cs_s, g_dt, g_cs):
        c = pl.program_id(0)
        (dt_s[...], cs_s[...]), vjp_prep = jax.vjp(_ssd_prep, dt_ref[...], bias_ref[...], alog_ref[...])

        @pl.when(c == 0)
        def _():
            dstate[...] = jnp.zeros_like(dstate)
            dbias_ref[...] = jnp.zeros_like(dbias_ref)
            dalog_ref[...] = jnp.zeros_like(dalog_ref)
            ddsk_ref[...] = jnp.zeros_like(ddsk_ref)

        g_dt[...] = jnp.zeros_like(g_dt)
        g_cs[...] = jnp.zeros_like(g_cs)
        for p in range(N_PAIRS):
            sl, gsl = _pair_slices(p)
            pair = lambda *a, p=p: _ssd_pair(p, *a)
            _, vjp = jax.vjp(pair, xs_ref[:, sl], dt_s[...], cs_s[...], dsk_ref[...],
                             b_ref[:, gsl], c_ref[:, gsl], prev_ref[0, p])
            dxs, ddt, dcs, ddsk, dbg, dcg, dprev = vjp((dy_ref[:, sl], dstate[p]))
            dxs_ref[:, sl] = dxs
            g_dt[...] += ddt
            g_cs[...] += dcs
            ddsk_ref[...] += ddsk
            dstate[p] = dprev
            if p % PAIRS_PER_GROUP == 0:
                db_ref[:, gsl] = dbg
                dc_ref[:, gsl] = dcg
            else:
                db_ref[:, gsl] += dbg
                dc_ref[:, gsl] += dcg

        ddt_raw, dbias, dalog = vjp_prep((g_dt[...], g_cs[...]))
        ddt_ref[...] = ddt_raw
        dbias_ref[...] += dbias
        dalog_ref[...] += dalog

    sp = _ssd_specs(nc)
    gs = SSD_GROUPS * SSD_STATE
    return _hosted(
        body, rider, name=name, grid=(nc,),
        in_specs=[sp["wide"], sp["group"], sp["group"], sp["chunk"], sp["vec"], sp["vec"], sp["vec"],
                  sp["state"], sp["wide"]],
        out_specs=[sp["wide"], sp["group"], sp["group"], sp["chunk"], sp["vec"], sp["vec"], sp["vec"]],
        out_shape=[jax.ShapeDtypeStruct((t, D_MODEL), F32), jax.ShapeDtypeStruct((t, gs), F32),
                   jax.ShapeDtypeStruct((t, gs), F32), jax.ShapeDtypeStruct((t, LANES), F32),
                   jax.ShapeDtypeStruct((1, LANES), F32), jax.ShapeDtypeStruct((1, LANES), F32),
                   jax.ShapeDtypeStruct((1, LANES), F32)],
        scratch_shapes=[pltpu.VMEM((N_PAIRS, LANES, LANES), F32)] + [pltpu.VMEM((CHUNK, LANES), F32)] * 4,
        args=(xs, bm, cm, dt_raw, dt_bias, a_log, d_skip, prev_states, dy))


def _ssd_gate(y, z, w):
    return (_rms(y * _silu(z), w),)


def _sgu_norm(v, w, b):
    return (_layer_norm(_gelu(v), w, b),)


def _sgu_group(u, gate, vn, w, bcol):
    l = u.shape[0]
    wc = jnp.where(_iota2((l, l), 0) >= _iota2((l, l), 1), w, 0.0)
    return _gelu(u) * (_bdot_nn(wc, vn) + bcol) * _silu(gate)


def _sgu_fwd(name, u, gate, vn, w, bcol):
    t = u.shape[0]
    blk = _row_spec(D_MODEL, CHUNK)

    def body(u_ref, g_ref, vn_ref, w_ref, b_ref, y_ref):
        for g in range(SGU_GROUPS):
            sl = slice(g * LANES, (g + 1) * LANES)
            y_ref[:, sl] = _sgu_group(u_ref[:, sl], g_ref[:, sl], vn_ref[:, sl], w_ref[g], b_ref[g]).astype(y_ref.dtype)

    return pl.pallas_call(
        body, name=name, grid=(t // CHUNK,),
        in_specs=[blk, blk, blk, _full_spec(w), _full_spec(bcol)], out_specs=blk,
        out_shape=jax.ShapeDtypeStruct((t, D_MODEL), BF16),
        compiler_params=_cparams("parallel"),
    )(u, gate, vn, w, bcol)


def _sgu_bwd(name, u, gate, vn, w, bcol, dy):
    t = u.shape[0]
    blk = _row_spec(D_MODEL, CHUNK)

    def body(u_ref, g_ref, vn_ref, w_ref, b_ref, dy_ref, du_ref, dg_ref, dvn_ref, dw_ref, db_ref):
        @pl.when(pl.program_id(0) == 0)
        def _():
            dw_ref[...] = jnp.zeros_like(dw_ref)
            db_ref[...] = jnp.zeros_like(db_ref)

        for g in range(SGU_GROUPS):
            sl = slice(g * LANES, (g + 1) * LANES)
            _, vjp = jax.vjp(_sgu_group, u_ref[:, sl], g_ref[:, sl], vn_ref[:, sl], w_ref[g], b_ref[g])
            du, dg, dvn, dw, db = vjp(dy_ref[:, sl])
            du_ref[:, sl] = du.astype(du_ref.dtype)
            dg_ref[:, sl] = dg.astype(dg_ref.dtype)
            dvn_ref[:, sl] = dvn
            dw_ref[g] += dw
            db_ref[g] += db

    return pl.pallas_call(
        body, name=name, grid=(t // CHUNK,),
        in_specs=[blk, blk, blk, _full_spec(w), _full_spec(bcol), blk],
        out_specs=[blk, blk, blk, _full_spec(w), _full_spec(bcol)],
        out_shape=[jax.ShapeDtypeStruct((t, D_MODEL), BF16), jax.ShapeDtypeStruct((t, D_MODEL), BF16),
                   jax.ShapeDtypeStruct((t, D_MODEL), F32), jax.ShapeDtypeStruct(w.shape, F32),
                   jax.ShapeDtypeStruct(bcol.shape, F32)],
        compiler_params=_cparams("arbitrary"),
    )(u, gate, vn, w, bcol, dy)


SB_SCALE = LANES ** -0.5
LOG2_E = 1.4426950408889634
SB_TQ = 512
SB_TS = 512


def _keep(mask, x):
    return x if mask is None else jnp.where(mask, x, 0.0)


def _sb_pieces(z2, mask):
    tl = jnp.log(1.0 + jnp.exp2(-jnp.abs(z2))) * LOG2_E
    lk = _keep(mask, -(jnp.maximum(z2, 0.0) + tl))
    ls = jnp.minimum(z2, 0.0) - tl
    return lk, ls


def _split2(x):
    hi = x.astype(BF16)
    return jnp.concatenate([hi, (x - hi.astype(F32)).astype(BF16)], axis=1)


def _tri2(cmp):
    sq = (CHUNK, CHUNK)
    m = cmp(_iota2(sq, 0), _iota2(sq, 1)).astype(BF16)
    return jnp.concatenate([m, m], axis=0)


def _tri_sums(blocks, tri2):
    tq = blocks[0].shape[0]
    res = _dg(jnp.concatenate([_split2(b) for b in blocks], axis=0), tri2, NN)
    return [res[b * tq:(b + 1) * tq] for b in range(len(blocks))]


def _sb_tiles(t):
    tq, ts = min(SB_TQ, t), min(SB_TS, t)
    assert t % tq == 0 and t % ts == 0 and ts % tq == 0 and ts % CHUNK == 0
    return tq, ts, ts // CHUNK


def _sb_logits(qb, ks, off, q_off, masked):
    tq = qb.shape[0]
    z2 = _dg(qb, ks, NT) * (SB_SCALE * LOG2_E)
    out = []
    for b in range(ks.shape[0] // CHUNK):
        mask = None
        if masked:
            mask = (_iota2((tq, CHUNK), 1) + (off + b * CHUNK)) < (_iota2((tq, CHUNK), 0) + q_off)
        out.append(_sb_pieces(z2[:, b * CHUNK:(b + 1) * CHUNK], mask) + (mask,))
    return out


def _attn_fwd(name, q, k, v, rider=None):
    t = q.shape[0]
    tq, ts, nb = _sb_tiles(t)

    def body(q_ref, k_ref, v_ref, y_ref, tot_ref):
        i = pl.program_id(1)
        qb = q_ref[...]
        later2 = _tri2(lambda r, c: r > c)
        last = (i * tq + tq - 1) // ts

        def span(j, carry, masked):
            acc, after = carry
            off = pl.multiple_of(j * ts, ts)
            pieces = _sb_logits(qb, k_ref[pl.ds(off, ts), :], off, i * tq, masked)
            inside = _tri_sums([lk for lk, _, _ in pieces], later2)
            ws = [None] * nb
            for b in reversed(range(nb)):
                lk, ls, mask = pieces[b]
                ws[b] = _keep(mask, jnp.exp2(ls + inside[b] + after)).astype(BF16)
                after = after + jnp.sum(lk, axis=1, keepdims=True)
            acc = acc + _dg(jnp.concatenate(ws, axis=1), v_ref[pl.ds(off, ts), :], NN)
            return acc, after

        carry = span(last, (jnp.zeros((tq, LANES), F32), jnp.zeros((tq, 1), F32)), True)
        acc, tot = lax.fori_loop(0, last, lambda n, c: span(last - 1 - n, c, False), carry)
        y_ref[...] = acc
        tot_ref[...] = jnp.broadcast_to(tot, (tq, LANES))

    qsp = pl.BlockSpec((tq, LANES), lambda h, i: (i, h))
    kvsp = pl.BlockSpec((t, LANES), lambda h, i: (0, h))
    return _hosted(
        body, rider, name=name, grid=(SB_HEADS, t // tq),
        in_specs=[qsp, kvsp, kvsp], out_specs=[qsp, qsp],
        out_shape=[jax.ShapeDtypeStruct((t, D_MODEL), F32), jax.ShapeDtypeStruct((t, D_MODEL), F32)],
        scratch_shapes=[], args=(q, k, v))


def _attn_bwd(name, q, k, v, tot, dy, rider=None):
    t = q.shape[0]
    tq, ts, nb = _sb_tiles(t)
    nq, ns = t // tq, t // ts

    def body(q_ref, k_ref, v_ref, tot_ref, dy_ref, dq_ref, dk_ref, dv_ref, dkt_acc, dvt_acc):
        i = pl.program_id(1)

        @pl.when(i == 0)
        def _():
            dkt_acc[...] = jnp.zeros_like(dkt_acc)
            dvt_acc[...] = jnp.zeros_like(dvt_acc)

        qb = q_ref[...]
        dy = dy_ref[...]
        dyb = dy.astype(BF16)
        q_t = qb.astype(F32).T.astype(BF16)
        dy_t = dy.T.astype(BF16)
        totb = tot_ref[...]
        upto2 = _tri2(lambda r, c: r <= c)
        before2 = _tri2(lambda r, c: r < c)
        last = (i * tq + tq - 1) // ts

        def span(j, carry, masked):
            dq, lk_seen, e_seen = carry
            off = pl.multiple_of(j * ts, ts)
            ks = k_ref[pl.ds(off, ts), :]
            pieces = _sb_logits(qb, ks, off, i * tq, masked)
            dw = _dg(dyb, v_ref[pl.ds(off, ts), :], NT)
            upto = _tri_sums([lk for lk, _, _ in pieces], upto2)
            ws, es = [], []
            for b in range(nb):
                lk, ls, mask = pieces[b]
                w = _keep(mask, jnp.exp2(ls + ((totb - upto[b]) - lk_seen)))
                ws.append(w.astype(BF16))
                es.append(dw[:, b * CHUNK:(b + 1) * CHUNK] * w)
                lk_seen = lk_seen + jnp.sum(lk, axis=1, keepdims=True)
            before = _tri_sums(es, before2)
            dzs = []
            for b in range(nb):
                _, ls, mask = pieces[b]
                sig = jnp.exp2(ls)
                dlk = e_seen + before[b]
                dzs.append((_keep(mask, es[b] - (es[b] + dlk) * sig) * SB_SCALE).astype(BF16))
                e_seen = e_seen + jnp.sum(es[b], axis=1, keepdims=True)
            dz = jnp.concatenate(dzs, axis=1)
            dq = dq + _dg(dz, ks, NN)
            dkt_acc[j] += _dg(q_t, dz, NN)
            dvt_acc[j] += _dg(dy_t, jnp.concatenate(ws, axis=1), NN)
            return dq, lk_seen, e_seen

        zero_col = jnp.zeros((tq, 1), F32)
        carry = lax.fori_loop(0, last, lambda j, c: span(j, c, False), (jnp.zeros((tq, LANES), F32), zero_col, zero_col))
        dq, _, _ = span(last, carry, True)
        dq_ref[...] = dq.astype(dq_ref.dtype)

        @pl.when(i == nq - 1)
        def _():
            for s in range(ns):
                dk_ref[s * ts:(s + 1) * ts, :] = dkt_acc[s].T.astype(dk_ref.dtype)
                dv_ref[s * ts:(s + 1) * ts, :] = dvt_acc[s].T.astype(dv_ref.dtype)

    qsp = pl.BlockSpec((tq, LANES), lambda h, i: (i, h))
    kvsp = pl.BlockSpec((t, LANES), lambda h, i: (0, h))
    return _hosted(
        body, rider, name=name, grid=(SB_HEADS, nq),
        in_specs=[qsp, kvsp, kvsp, qsp, qsp], out_specs=[qsp, kvsp, kvsp],
        out_shape=[jax.ShapeDtypeStruct((t, D_MODEL), BF16)] * 3,
        scratch_shapes=[pltpu.VMEM((ns, LANES, ts), F32), pltpu.VMEM((ns, LANES, ts), F32)],
        args=(q, k, v, tot, dy))


def _attn_gate(y, g):
    return (y * _silu(g),)


def _loss_head(name, h, target, w, tm):
    t, d = h.shape

    def body(h_ref, t_ref, w_ref, dh_ref, dw_ref, loss_ref):
        tgt = t_ref[...]

        def f(hv, wv):
            e = _rms(hv, wv) - tgt
            return 0.5 * jnp.mean(e * e, axis=-1, keepdims=True)

        row_loss, vjp = jax.vjp(f, h_ref[...], w_ref[...])
        dh, dw = vjp(jnp.ones_like(row_loss))
        dh_ref[...] = dh

        @pl.when(pl.program_id(0) == 0)
        def _():
            dw_ref[...] = jnp.zeros_like(dw_ref)
            loss_ref[...] = jnp.zeros_like(loss_ref)

        dw_ref[...] += dw
        loss_ref[...] += jnp.sum(row_loss, axis=0, keepdims=True)

    return pl.pallas_call(
        body, name=name, grid=(t // tm,),
        in_specs=[_row_spec(d, tm), _row_spec(d, tm), _full_spec(w)],
        out_specs=[_row_spec(d, tm), _full_spec(w), pl.BlockSpec((1, 1), lambda i: (0, 0))],
        out_shape=[jax.ShapeDtypeStruct((t, d), F32), jax.ShapeDtypeStruct(w.shape, F32),
                   jax.ShapeDtypeStruct((1, 1), F32)],
        compiler_params=_cparams("arbitrary"),
    )(h, target, w)


def _pick_tile(rows, cap):
    if rows <= cap:
        return rows
    for tm in range(cap - cap % 16, 0, -16):
        if rows % tm == 0:
            return tm
    raise ValueError(rows)


def _adamw(name, w, g, m, v, tm):
    n_l, r, c = w.shape
    tm = _pick_tile(r, tm)

    def body(w_ref, g_ref, m_ref, v_ref, d_ref, nm_ref, nv_ref):
        g_ = g_ref[...]
        m_ = ADAM_B1 * m_ref[...] + (1.0 - ADAM_B1) * g_
        v_ = ADAM_B2 * v_ref[...] + (1.0 - ADAM_B2) * (g_ * g_)
        m_hat = m_ / (1.0 - ADAM_B1 ** ADAM_STEP)
        v_hat = v_ / (1.0 - ADAM_B2 ** ADAM_STEP)
        d_ref[...] = -ADAM_LR * (m_hat / (jnp.sqrt(v_hat) + ADAM_EPS) + ADAM_WD * w_ref[...])
        nm_ref[...] = m_
        nv_ref[...] = v_

    spec = pl.BlockSpec((1, tm, c), lambda l, i: (l, i, 0))
    return pl.pallas_call(
        body, name=name, grid=(n_l, r // tm), in_specs=[spec] * 4, out_specs=[spec] * 3,
        out_shape=[jax.ShapeDtypeStruct((n_l, r, c), F32)] * 3, compiler_params=_cparams("parallel", "parallel"),
    )(w, g, m, v)


def _sum_parts(name, layers, tm):
    n, r, c = layers[0].shape
    tm = _pick_tile(r, tm)
    nblk = r // tm

    def body(*refs):
        o_ref = refs[-1]
        for li, p_ref in enumerate(refs[:-1]):
            @pl.when(pl.program_id(0) == li)
            def _(p_ref=p_ref):
                s = p_ref[0].astype(F32)
                for d in range(1, n):
                    s = s + p_ref[d].astype(F32)
                o_ref[0] = s

    def spec(li):
        return pl.BlockSpec((n, tm, c), lambda l, i: (0, jnp.where(l == li, i, jnp.where(l < li, 0, nblk - 1)), 0))

    return pl.pallas_call(
        body, name=name, grid=(len(layers), nblk),
        in_specs=[spec(li) for li in range(len(layers))],
        out_specs=pl.BlockSpec((1, tm, c), lambda l, i: (l, i, 0)),
        out_shape=jax.ShapeDtypeStruct((len(layers), r, c), F32), compiler_params=_cparams("arbitrary", "arbitrary"),
    )(*layers)


def _peer(k):
    x, y, c = lax.axis_index("x"), lax.axis_index("y"), lax.axis_index("c")
    px, py, pc = x ^ ((k >> 2) & 1), y ^ ((k >> 1) & 1), c ^ (k & 1)
    return (px, py, pc), 4 * px + 2 * py + pc


def _exchange(name, xs, gather):
    rider = _Rider(xs, gather)
    n = rider.n

    def body(*refs):
        x_refs, out_refs, sems = refs[:n], refs[n:2 * n], refs[2 * n:]
        _exchange_start(x_refs, out_refs, sems, gather)
        _exchange_finish(x_refs, out_refs, sems, gather)

    return pl.pallas_call(
        body, name=name, in_specs=[ANY_SPEC] * n, out_specs=[ANY_SPEC] * n,
        out_shape=rider.out_shape(), scratch_shapes=rider.scratch(),
    )(*xs)


ANY_SPEC = pl.BlockSpec(memory_space=pl.ANY)
SAME_CORE = (2, 4, 6)


class _Rider:
    def __init__(self, xs, gather):
        self.xs, self.gather, self.n = list(xs), gather, len(xs)

    def out_shape(self):
        return [jax.ShapeDtypeStruct((N_DEV,) + tuple(x.shape if self.gather else x.shape[1:]), x.dtype)
                for x in self.xs]

    def scratch(self):
        return [pltpu.SemaphoreType.DMA((self.n, N_DEV - 1)), pltpu.SemaphoreType.DMA((self.n, N_DEV - 1)),
                pltpu.SemaphoreType.DMA((self.n,))]


def _remote(src, dst, sems, a, k, dev):
    return pltpu.make_async_remote_copy(
        src_ref=src, dst_ref=dst, send_sem=sems[0].at[a, k - 1], recv_sem=sems[1].at[a, k - 1],
        device_id=dev, device_id_type=pl.DeviceIdType.MESH)


def _exchange_start(x_refs, out_refs, sems, gather):
    _, me = _peer(0)
    for a, (x, out) in enumerate(zip(x_refs, out_refs)):
        pltpu.make_async_copy(x if gather else x.at[me], out.at[me], sems[2].at[a]).start()
    for k in ((1,) + SAME_CORE if gather else range(1, N_DEV)):
        dev, idx = _peer(k)
        for a, (x, out) in enumerate(zip(x_refs, out_refs)):
            _remote(x if gather else x.at[idx], out.at[me], sems, a, k, dev).start()


def _exchange_finish(x_refs, out_refs, sems, gather):
    _, me = _peer(0)
    sibling, _ = _peer(1)
    pairs = list(enumerate(zip(x_refs, out_refs)))
    waited = ()
    if gather:
        for k in SAME_CORE:
            dev, idx = _peer(k)
            for a, (x, out) in pairs:
                _remote(x, out.at[idx], sems, a, k, dev).wait_recv()
                _remote(out.at[idx], out.at[idx], sems, a, k + 1, sibling).start()
        waited = SAME_CORE
    for k in range(1, N_DEV):
        dev, idx = _peer(k)
        for a, (x, out) in pairs:
            cp = _remote(x if gather else x.at[idx], out.at[idx], sems, a, k, dev)
            if k not in waited:
                cp.wait_recv()
            cp.wait_send()
    for a, (x, out) in pairs:
        pltpu.make_async_copy(x if gather else x.at[me], out.at[me], sems[2].at[a]).wait()


def _hosted(body, rider, *, name, grid, in_specs, out_specs, out_shape, scratch_shapes, args):
    sem = ("arbitrary",) * len(grid)
    if rider is None:
        return pl.pallas_call(body, name=name, grid=grid, in_specs=in_specs, out_specs=out_specs, out_shape=out_shape,
                              scratch_shapes=scratch_shapes, compiler_params=_cparams(*sem))(*args), []
    n_in, n_out, n_scr, nr = len(in_specs), len(out_specs), len(scratch_shapes), rider.n

    def hosted(*refs):
        ins, refs = refs[:n_in], refs[n_in:]
        r_in, refs = refs[:nr], refs[nr:]
        outs, refs = refs[:n_out], refs[n_out:]
        r_out, refs = refs[:nr], refs[nr:]
        scr, sems = refs[:n_scr], refs[n_scr:]
        first = pl.program_id(0) == 0
        last = pl.program_id(0) == grid[0] - 1
        for ax in range(1, len(grid)):
            first = first & (pl.program_id(ax) == 0)
            last = last & (pl.program_id(ax) == grid[ax] - 1)

        @pl.when(first)
        def _():
            _exchange_start(r_in, r_out, sems, rider.gather)

        body(*ins, *outs, *scr)

        @pl.when(last)
        def _():
            _exchange_finish(r_in, r_out, sems, rider.gather)

    res = pl.pallas_call(
        hosted, name=name, grid=grid, in_specs=list(in_specs) + [ANY_SPEC] * nr,
        out_specs=list(out_specs) + [ANY_SPEC] * nr, out_shape=list(out_shape) + rider.out_shape(),
        scratch_shapes=list(scratch_shapes) + rider.scratch(), compiler_params=_cparams(*sem),
    )(*args, *rider.xs)
    return res[:n_out], res[n_out:]


PACK_ALIGN = 8 * LANES
PACK_ROWS = 512


def _pack(arrays, dtype):
    pieces, total = [], 0
    for a in arrays:
        f = a.reshape(-1).astype(dtype)
        pad = (-f.shape[0]) % PACK_ALIGN
        pieces.append(jnp.pad(f, (0, pad)) if pad else f)
        total += f.shape[0] + pad
    tail = (-total) % (PACK_ROWS * LANES)
    if tail:
        pieces.append(jnp.zeros((tail,), dtype))
    return jnp.concatenate(pieces).reshape(-1, LANES)


def _unpack(packed, shapes, lead=()):
    flat = packed.reshape(lead + (-1,))
    out, off = [], 0
    for s in shapes:
        n = 1
        for d in s:
            n *= d
        out.append(flat[..., off:off + n].reshape(lead + tuple(s)))
        off += n + ((-n) % PACK_ALIGN)
    return out


def _pad_lanes(a):
    return jnp.pad(a, (0, LANES - a.shape[0])).reshape(1, LANES)


ROW_TM = 256
EVEN_SEGS = (("z", D_MODEL), ("xbc", CONV_DIM), ("dt", SSD_HEADS), ("g", D_MODEL), ("u", D_MODEL), ("v", D_MODEL))
ODD_SEGS = (("q", D_MODEL), ("k", D_MODEL), ("v", D_MODEL), ("g", D_MODEL))
TAIL_ROWS = (368, 512, 64, 368, 368, 368)


def _split_cols(w, segs):
    out, off = {}, 0
    for nm, n in segs:
        out[nm] = w[:, off:off + n]
        off += n
    return out


def _rms_fn(h, w):
    return (_rms(h, w),)


def _even_fwd(tag, h, p, rider=None, late=None):
    hn, = _rows_fwd(tag + "_norm", _rms_fn, [h], [p["norm_w"]], [(D_MODEL, BF16)], ROW_TM)
    proj = {nm: _matmul(f"{tag}_in_{nm}", hn, p["w_in"][nm], NN, F32) for nm, _ in EVEN_SEGS}
    pre, xs, bm, cm = _conv_fwd(tag + "_conv", proj["xbc"], p["conv_w"], p["conv_b"])
    (y_ssd, states), got = _ssd_fwd(tag + "_ssd", xs, bm, cm, proj["dt"], p["dt_bias"], p["a_log"], p["d_skip"], rider)
    if late is not None:
        p = dict(p, **late(got))
    ya, = _rows_fwd(tag + "_ssdgate", _ssd_gate, [y_ssd, proj["z"]], [p["ssd_norm_w"]], [(D_MODEL, BF16)], ROW_TM)
    vn, = _rows_fwd(tag + "_sgunorm", _sgu_norm, [proj["v"]], [p["sgu_ln_w"], p["sgu_ln_b"]], [(D_MODEL, F32)], ROW_TM)
    yb = _sgu_fwd(tag + "_sgu", proj["u"], proj["g"], vn, p["sgu_w"], p["sgu_b"])
    h1 = _matmul(tag + "_out_a", ya, p["w_out_a"], NN, F32, add=h)
    h2 = _matmul(tag + "_out_b", yb, p["w_out_b"], NN, F32, add=h1)
    saved = dict(h=h, hn=hn, proj=proj, pre=pre, xs=xs, bm=bm, cm=cm, y_ssd=y_ssd, states=states,
                 ya=ya, vn=vn, yb=yb)
    return h2, saved, p, got


def _even_bwd(tag, dh, s, p, make_rider=None, tail_riders=None):
    g = {}
    dh16 = dh.astype(BF16)
    proj = s["proj"]
    dya = _matmul(tag + "_dya", dh16, p["w_out_a"], NT, F32)
    dyb = _matmul(tag + "_dyb", dh16, p["w_out_b"], NT, F32)
    g["w_out_a"] = _matmul(tag + "_dwout_a", s["ya"], dh16, TN, BF16)
    g["w_out_b"] = _matmul(tag + "_dwout_b", s["yb"], dh16, TN, BF16)
    du, dg, dvn, g["sgu_w"], g["sgu_b"] = _sgu_bwd(tag + "_sgu_b", proj["u"], proj["g"], s["vn"],
                                                   p["sgu_w"], p["sgu_b"], dyb)
    dv, g["sgu_ln_w"], g["sgu_ln_b"] = _rows_bwd(tag + "_sgunorm_b", _sgu_norm, [proj["v"]],
                                                 [p["sgu_ln_w"], p["sgu_ln_b"]], [dvn], [BF16], ROW_TM // 2)
    dy_ssd, dz, g["ssd_norm_w"] = _rows_bwd(tag + "_ssdgate_b", _ssd_gate, [s["y_ssd"], proj["z"]],
                                            [p["ssd_norm_w"]], [dya], [F32, BF16], ROW_TM // 2)
    (dxs, dbm, dcm, ddt, g["dt_bias"], g["a_log"], g["d_skip"]), got = _ssd_bwd(
        tag + "_ssd_b", s["xs"], s["bm"], s["cm"], proj["dt"], p["dt_bias"], p["a_log"], p["d_skip"],
        s["states"], dy_ssd, None if make_rider is None else make_rider(g))
    dxbc, g["conv_w"], g["conv_b"] = _conv_bwd(tag + "_conv_b", proj["xbc"], s["pre"], p["conv_w"], dxs, dbm, dcm)
    dproj = dict(z=dz, xbc=dxbc, dt=ddt.astype(BF16), g=dg, u=du, v=dv)
    g["w_in"] = {nm: _matmul(f"{tag}_dwin_{nm}", s["hn"], dproj[nm], TN, BF16) for nm, _ in EVEN_SEGS}
    riders = [None] * len(EVEN_SEGS) if tail_riders is None else tail_riders(g)
    dhn, tail_got = None, []
    for (nm, _), rider in zip(EVEN_SEGS, riders):
        dhn = _matmul(f"{tag}_dhn_{nm}", dproj[nm], p["w_in"][nm], NT, F32, add=dhn, rider=rider)
        if rider is not None:
            dhn, arrived = dhn
            tail_got += arrived
    dh_in, g["norm_w"] = _rows_bwd(tag + "_norm_b", _rms_fn, [s["h"]], [p["norm_w"]], [dhn], [F32],
                                   ROW_TM // 2, add=(0, dh))
    return dh_in, g, got, tail_got


def _odd_fwd(tag, h, p, rider=None):
    hn, = _rows_fwd(tag + "_norm", _rms_fn, [h], [p["norm_w"]], [(D_MODEL, BF16)], ROW_TM)
    q = _matmul(tag + "_in_q", hn, p["w_in"]["q"], NN, BF16)
    k = _matmul(tag + "_in_k", hn, p["w_in"]["k"], NN, BF16)
    v = _matmul(tag + "_in_v", hn, p["w_in"]["v"], NN, BF16)
    gate = _matmul(tag + "_in_g", hn, p["w_in"]["g"], NN, F32)
    (y, tot), got = _attn_fwd(tag + "_attn", q, k, v, rider)
    yg, = _rows_fwd(tag + "_gate", _attn_gate, [y, gate], [], [(D_MODEL, BF16)], ROW_TM)
    h1 = _matmul(tag + "_out", yg, p["w_out"], NN, F32, add=h)
    return h1, dict(h=h, hn=hn, q=q, k=k, v=v, gate=gate, y=y, tot=tot, yg=yg), got


def _odd_bwd(tag, dh, s, p, rider=None):
    g = {}
    dh16 = dh.astype(BF16)
    dyg = _matmul(tag + "_dyg", dh16, p["w_out"], NT, F32)
    g["w_out"] = _matmul(tag + "_dwout", s["yg"], dh16, TN, BF16)
    dy, dgate = _rows_bwd(tag + "_gate_b", _attn_gate, [s["y"], s["gate"]], [], [dyg], [F32, BF16], ROW_TM)
    (dq, dk, dv), got = _attn_bwd(tag + "_attn_b", s["q"], s["k"], s["v"], s["tot"], dy, rider)
    dproj = dict(q=dq, k=dk, v=dv, g=dgate)
    dhn = None
    g["w_in"] = {}
    for nm, _ in ODD_SEGS:
        dhn = _matmul(f"{tag}_dhn_{nm}", dproj[nm], p["w_in"][nm], NT, F32, add=dhn)
        g["w_in"][nm] = _matmul(f"{tag}_dwin_{nm}", s["hn"], dproj[nm], TN, BF16)
    dh_in, g["norm_w"] = _rows_bwd(tag + "_norm_b", _rms_fn, [s["h"]], [p["norm_w"]], [dhn], [F32],
                                   ROW_TM // 2, add=(0, dh))
    return dh_in, g, got


BIG = ("ev_w_in", "ev_w_out", "od_w_in", "od_w_out")
SMALL = ("norm_w", "final_norm_w", "ev_conv_b", "ev_dt_bias", "ev_a_log", "ev_d_skip", "ev_ssd_norm_w",
         "ev_sgu_ln_w", "ev_sgu_ln_b", "ev_sgu_w", "ev_sgu_b")
WEIGHTS = ("norm_w", "final_norm_w", "ev_w_in", "ev_conv_w", "ev_conv_b", "ev_dt_bias", "ev_a_log", "ev_d_skip",
           "ev_ssd_norm_w", "ev_sgu_ln_w", "ev_sgu_ln_b", "ev_sgu_w", "ev_sgu_b", "ev_w_out", "od_w_in", "od_w_out")


def _step(w, m, v, x, loss_target):
    h = x[0]
    tgt = loss_target[0]
    n_even, n_odd = w["ev_w_in"].shape[0], w["od_w_in"].shape[0]
    depth = n_even + n_odd
    assert (n_even, n_odd) == (2, 2), "the exchange schedule below is written for the four-layer trunk"

    def shard(n, i):
        return w[n][i].astype(BF16)

    def take_cols(blocks, widths, lo, hi, r0=0, r1=None):
        pieces, start = [], 0
        for blk, wd in zip(blocks, widths):
            a, b = max(lo - start, 0), min(hi - start, wd)
            if a < b:
                pieces.append(blk[r0:r1, a:b])
            start += wd
        return pieces[0] if len(pieces) == 1 else jnp.concatenate(pieces, axis=1)

    def segments(gathered, segs):
        n = gathered.shape[2]
        out, lo = {}, 0
        for nm, wd in segs:
            out[nm] = take_cols([gathered[d] for d in range(N_DEV)], [n] * N_DEV, lo, lo + wd)
            lo += wd
        return out

    def rows(gathered):
        return gathered.reshape(-1, gathered.shape[2])

    def by_owner_cols(g_in, segs, r0=0, r1=None):
        blocks, widths = [g_in[nm] for nm, _ in segs], [wd for _, wd in segs]
        n = sum(widths) // N_DEV
        return jnp.stack([take_cols(blocks, widths, d * n, (d + 1) * n, r0, r1) for d in range(N_DEV)])

    def by_owner_rows(full):
        return full.reshape(N_DEV, full.shape[0] // N_DEV, full.shape[1])

    def even_params(layer, ev_in):
        i = layer // 2
        w_in = segments(ev_in, EVEN_SEGS)
        w_in["dt"] = jnp.pad(w_in["dt"], ((0, 0), (0, LANES - SSD_HEADS)))
        return dict(norm_w=w["norm_w"][layer][None], w_in=w_in, conv_w=conv_w[i], conv_b=w["ev_conv_b"][i][None],
                    dt_bias=_pad_lanes(w["ev_dt_bias"][i]), a_log=_pad_lanes(w["ev_a_log"][i]),
                    d_skip=_pad_lanes(w["ev_d_skip"][i]), ssd_norm_w=w["ev_ssd_norm_w"][i][None],
                    sgu_ln_w=w["ev_sgu_ln_w"][i][None], sgu_ln_b=w["ev_sgu_ln_b"][i][None],
                    sgu_w=w["ev_sgu_w"][i], sgu_b=w["ev_sgu_b"][i][:, :, None])

    def even_out_params(ev_out):
        full = rows(ev_out)
        return dict(w_out_a=full[:D_MODEL], w_out_b=full[D_MODEL:])

    def odd_params(layer, od_in, od_out):
        return dict(norm_w=w["norm_w"][layer][None], w_in=segments(od_in, ODD_SEGS), w_out=rows(od_out))

    def even_in_grads(g):
        return by_owner_cols(g["w_in"], EVEN_SEGS)

    def even_out_grads(g):
        return by_owner_rows(jnp.concatenate([g["w_out_a"], g["w_out_b"]], axis=0))

    def odd_grads(g):
        return [by_owner_cols(g["w_in"], ODD_SEGS), by_owner_rows(g["w_out"])]

    ev_in0, conv_w = _exchange("gather_first", [shard("ev_w_in", 0), w["ev_conv_w"]], gather=True)
    conv_w = jnp.moveaxis(conv_w, 0, 2).reshape(n_even, CONV_WIDTH, CONV_DIM)
    h, s0, p0, got = _even_fwd(
        "l0", h, even_params(0, ev_in0),
        _Rider([shard("ev_w_out", 0), shard("od_w_in", 0), shard("od_w_out", 0)], True),
        late=lambda arrived: even_out_params(arrived[0]))
    p1 = odd_params(1, got[1], got[2])
    h, s1, got = _odd_fwd("l1", h, p1, _Rider([shard("ev_w_in", 1), shard("ev_w_out", 1)], True))
    p2 = dict(even_params(2, got[0]), **even_out_params(got[1]))
    h, s2, p2, got = _even_fwd("l2", h, p2, _Rider([shard("od_w_in", 1), shard("od_w_out", 1)], True))
    p3 = odd_params(3, got[0], got[1])
    h, s3, _ = _odd_fwd("l3", h, p3)

    dh, d_final, loss_part = _loss_head("loss_head", h, tgt, w["final_norm_w"][None], ROW_TM // 2)
    loss = lax.psum(loss_part[0, 0], ("x", "y", "c"))

    def last_grads(g):
        bounds = [0]
        for rows_here in TAIL_ROWS:
            bounds.append(bounds[-1] + rows_here)
        assert bounds[-1] == D_MODEL
        return [_Rider([by_owner_cols(g["w_in"], EVEN_SEGS, r0, r1)], False) for r0, r1 in zip(bounds, bounds[1:])]

    dh, g3, _ = _odd_bwd("l3", dh, s3, p3)
    dh, g2, from3, _ = _even_bwd("l2", dh, s2, p2, lambda g: _Rider(odd_grads(g3), False))
    dh, g1, from2 = _odd_bwd("l1", dh, s1, p1, _Rider([even_in_grads(g2), even_out_grads(g2)], False))
    dh, g0, from1, from0 = _even_bwd("l0", dh, s0, p0, lambda g: _Rider(odd_grads(g1) + [even_out_grads(g)], False),
                                     last_grads)
    from0 = jnp.concatenate(from0, axis=1)
    grad_x = dh[None]
    lg = [g0, g1, g2, g3]
    ev, od = [g0, g2], [g1, g3]

    def total(tag, parts):
        return _sum_parts("sum_" + tag, parts, 256)

    big_grads = {
        "ev_w_in": total("ev_w_in", [from0, from2[0]]),
        "ev_w_out": total("ev_w_out", [from1[2], from2[1]]),
        "od_w_in": total("od_w_in", [from1[0], from3[0]]),
        "od_w_out": total("od_w_out", [from1[1], from3[1]]),
    }

    small_g = {
        "norm_w": jnp.concatenate([lg[l]["norm_w"] for l in range(depth)], axis=0),
        "final_norm_w": d_final[0],
        "ev_conv_b": jnp.concatenate([e["conv_b"] for e in ev], axis=0),
        "ev_dt_bias": jnp.concatenate([e["dt_bias"][:, :SSD_HEADS] for e in ev], axis=0),
        "ev_a_log": jnp.concatenate([e["a_log"][:, :SSD_HEADS] for e in ev], axis=0),
        "ev_d_skip": jnp.concatenate([e["d_skip"][:, :SSD_HEADS] for e in ev], axis=0),
        "ev_ssd_norm_w": jnp.concatenate([e["ssd_norm_w"] for e in ev], axis=0),
        "ev_sgu_ln_w": jnp.concatenate([e["sgu_ln_w"] for e in ev], axis=0),
        "ev_sgu_ln_b": jnp.concatenate([e["sgu_ln_b"] for e in ev], axis=0),
        "ev_sgu_w": jnp.stack([e["sgu_w"] for e in ev]),
        "ev_sgu_b": jnp.stack([e["sgu_b"][:, :, 0] for e in ev]),
    }
    conv_g = jnp.stack([e["conv_w"] for e in ev])
    small_shapes = [w[n].shape for n in SMALL]
    small_parts, = _exchange("gather_small", [_pack([small_g[n] for n in SMALL] + [conv_g], F32)], gather=True)
    small_sum = _sum_parts("sum_small", [small_parts], 1024)[0]
    *small_list, conv_full = _unpack(small_sum, small_shapes + [conv_g.shape])
    grads = dict(zip(SMALL, small_list))
    grads.update(big_grads)
    me = 4 * lax.axis_index("x") + 2 * lax.axis_index("y") + lax.axis_index("c")
    n_cv = w["ev_conv_w"].shape[2]
    grads["ev_conv_w"] = lax.dynamic_slice_in_dim(conv_full, me * n_cv, n_cv, axis=2)

    deltas, new_m, new_v = {}, {}, {}
    for n in BIG + ("ev_conv_w",):
        deltas[n], new_m[n], new_v[n] = _adamw("adamw_" + n, w[n], grads[n], m[n], v[n], 256)
    packs = [_pack([src[n] for n in SMALL], F32)[None] for src in (w, grads, m, v)]
    outs = _adamw("adamw_small", *packs, 1024)
    for dst, packed in zip((deltas, new_m, new_v), outs):
        dst.update(zip(SMALL, _unpack(packed[0], small_shapes)))
    return loss, grad_x, grads, deltas, new_m, new_v


def kernel(x, norm_w, final_norm_w, ev_w_in, ev_conv_w, ev_conv_b, ev_dt_bias, ev_a_log, ev_d_skip, ev_ssd_norm_w, ev_sgu_ln_w, ev_sgu_ln_b, ev_sgu_w, ev_sgu_b, ev_w_out, od_w_in, od_w_out, loss_target, m_norm_w, m_final_norm_w, m_ev_w_in, m_ev_conv_w, m_ev_conv_b, m_ev_dt_bias, m_ev_a_log, m_ev_d_skip, m_ev_ssd_norm_w, m_ev_sgu_ln_w, m_ev_sgu_ln_b, m_ev_sgu_w, m_ev_sgu_b, m_ev_w_out, m_od_w_in, m_od_w_out, v_norm_w, v_final_norm_w, v_ev_w_in, v_ev_conv_w, v_ev_conv_b, v_ev_dt_bias, v_ev_a_log, v_ev_d_skip, v_ev_ssd_norm_w, v_ev_sgu_ln_w, v_ev_sgu_ln_b, v_ev_sgu_w, v_ev_sgu_b, v_ev_w_out, v_od_w_in, v_od_w_out):
    w = dict(zip(WEIGHTS, (norm_w, final_norm_w, ev_w_in, ev_conv_w, ev_conv_b, ev_dt_bias, ev_a_log, ev_d_skip,
                           ev_ssd_norm_w, ev_sgu_ln_w, ev_sgu_ln_b, ev_sgu_w, ev_sgu_b, ev_w_out, od_w_in, od_w_out)))
    m = dict(zip(WEIGHTS, (m_norm_w, m_final_norm_w, m_ev_w_in, m_ev_conv_w, m_ev_conv_b, m_ev_dt_bias, m_ev_a_log,
                           m_ev_d_skip, m_ev_ssd_norm_w, m_ev_sgu_ln_w, m_ev_sgu_ln_b, m_ev_sgu_w, m_ev_sgu_b,
                           m_ev_w_out, m_od_w_in, m_od_w_out)))
    v = dict(zip(WEIGHTS, (v_norm_w, v_final_norm_w, v_ev_w_in, v_ev_conv_w, v_ev_conv_b, v_ev_dt_bias, v_ev_a_log,
                           v_ev_d_skip, v_ev_ssd_norm_w, v_ev_sgu_ln_w, v_ev_sgu_ln_b, v_ev_sgu_w, v_ev_sgu_b,
                           v_ev_w_out, v_od_w_in, v_od_w_out)))
    loss, grad_x, grads, deltas, new_m, new_v = _step(w, m, v, x, loss_target)
    return (loss, grad_x, *[grads[n] for n in WEIGHTS], *[deltas[n] for n in WEIGHTS],
            *[new_m[n] for n in WEIGHTS], *[new_v[n] for n in WEIGHTS])
```

```python
import jax
import jax.numpy as jnp
from jax import lax
from jax.experimental import pallas as pl
from jax.experimental.pallas import tpu as pltpu

F32, BF16 = jnp.float32, jnp.bfloat16

D_MODEL = 2048
SSD_HEADS = 32
SSD_HEAD_DIM = 64
SSD_GROUPS = 4
SSD_STATE = 128
CHUNK = 128
CONV_WIDTH = 4
CONV_DIM = D_MODEL + 2 * SSD_GROUPS * SSD_STATE
SGU_GROUPS = 16
SB_HEADS = 16
LANES = 128
N_PAIRS = SSD_HEADS // 2
PAIRS_PER_GROUP = N_PAIRS // SSD_GROUPS
NORM_EPS = 1e-5
N_DEV = 8

ADAM_LR, ADAM_B1, ADAM_B2, ADAM_EPS, ADAM_WD, ADAM_STEP = 0.001, 0.9, 0.999, 1e-08, 0.01, 10

VMEM_LIMIT_BYTES = 48 * 1024 * 1024

NN = ((1,), (0,))
NT = ((1,), (1,))
TN = ((0,), (0,))


def _cparams(*sem):
    return pltpu.CompilerParams(dimension_semantics=sem, vmem_limit_bytes=VMEM_LIMIT_BYTES)


def _dg(a, b, dims):
    return lax.dot_general(a, b, (dims, ((), ())), preferred_element_type=F32)


def _make_bdot(dims):
    @jax.custom_vjp
    def f(a, b):
        return _dg(a.astype(BF16), b.astype(BF16), dims)

    def fwd(a, b):
        return f(a, b), (a, b)

    def bwd(res, g):
        a, b = res
        a16, b16, g16 = a.astype(BF16), b.astype(BF16), g.astype(BF16)
        if dims == NN:
            da, db = _dg(g16, b16, NT), _dg(a16, g16, TN)
        elif dims == NT:
            da, db = _dg(g16, b16, NN), _dg(g16, a16, TN)
        else:
            da, db = _dg(b16, g16, NT), _dg(a16, g16, NN)
        return da.astype(a.dtype), db.astype(b.dtype)

    f.defvjp(fwd, bwd)
    return f


_bdot_nn, _bdot_nt, _bdot_tn = _make_bdot(NN), _make_bdot(NT), _make_bdot(TN)


def _iota2(shape, axis):
    return lax.broadcasted_iota(jnp.int32, shape, axis)


def _split3(x, axis):
    hi = x.astype(BF16)
    r = x - hi.astype(F32)
    mid = r.astype(BF16)
    return jnp.concatenate([hi, mid, (r - mid.astype(F32)).astype(BF16)], axis=axis)


def _make_onehot_dot(build, build_t, left):
    def apply(x, e):
        e = e.astype(BF16)
        if left:
            return _dg(jnp.concatenate([e, e, e], axis=1), _split3(x, 0), NN)
        return _dg(_split3(x, 1), jnp.concatenate([e, e, e], axis=0), NN)

    @jax.custom_vjp
    def f(x):
        return apply(x, build())

    f.defvjp(lambda x: (f(x), None), lambda _, g: (apply(g, build_t()),))
    return f


_cumsum_rows = _make_onehot_dot(lambda: _iota2((CHUNK, CHUNK), 1) <= _iota2((CHUNK, CHUNK), 0),
                                lambda: _iota2((CHUNK, CHUNK), 0) <= _iota2((CHUNK, CHUNK), 1), True)


def _softplus(x):
    return jnp.maximum(x, 0.0) + jnp.log1p(jnp.exp(-jnp.abs(x)))


def _silu(x):
    return x * jax.nn.sigmoid(x)


def _gelu(x):
    return 0.5 * x * (1.0 + jnp.tanh(0.7978845608028654 * (x + 0.044715 * (x * x * x))))


def _rms(x, w):
    return x * lax.rsqrt(jnp.mean(x * x, axis=-1, keepdims=True) + NORM_EPS) * w


def _layer_norm(x, w, b):
    xc = x - jnp.mean(x, axis=-1, keepdims=True)
    return xc * lax.rsqrt(jnp.mean(xc * xc, axis=-1, keepdims=True) + NORM_EPS) * w + b


def _row_spec(width, tm):
    return pl.BlockSpec((tm, width), lambda i: (i, 0))


def _full_spec(p):
    zeros = (0,) * p.ndim
    return pl.BlockSpec(p.shape, lambda i: zeros)


def _rows_fwd(name, fn, tiled, params, outs, tm):
    n_rows = tiled[0].shape[0]
    n_in = len(tiled) + len(params)

    def body(*refs):
        res = fn(*[r[...] for r in refs[:n_in]])
        for o_ref, o in zip(refs[n_in:], res):
            o_ref[...] = o.astype(o_ref.dtype)

    return pl.pallas_call(
        body, name=name, grid=(n_rows // tm,),
        in_specs=[_row_spec(a.shape[1], tm) for a in tiled] + [_full_spec(p) for p in params],
        out_specs=[_row_spec(w, tm) for w, _ in outs],
        out_shape=[jax.ShapeDtypeStruct((n_rows, w), d) for w, d in outs],
        compiler_params=_cparams("parallel"),
    )(*tiled, *params)


def _rows_bwd(name, fn, tiled, params, cots, grad_dtypes, tm, add=None):
    n_rows = tiled[0].shape[0]
    nt, npar, nc = len(tiled), len(params), len(cots)
    want = [k for k, d in enumerate(grad_dtypes) if d is not None]
    n_add = 0 if add is None else 1

    def body(*refs):
        ins = [r[...] for r in refs[:nt + npar]]
        c_refs = refs[nt + npar:nt + npar + nc]
        add_refs = refs[nt + npar + nc:nt + npar + nc + n_add]
        o_refs = refs[nt + npar + nc + n_add:]
        res, vjp = jax.vjp(fn, *ins)
        grads = vjp(tuple(c[...].astype(r.dtype) for c, r in zip(c_refs, res)))
        for pos, k in enumerate(want):
            gk = grads[k]
            if add is not None and add[0] == pos:
                gk = gk + add_refs[0][...]
            o_refs[pos][...] = gk.astype(o_refs[pos].dtype)
        p_refs = o_refs[len(want):]

        @pl.when(pl.program_id(0) == 0)
        def _():
            for r in p_refs:
                r[...] = jnp.zeros_like(r)

        for r, gp in zip(p_refs, grads[nt:]):
            r[...] += gp

    add_arrays = [] if add is None else [add[1]]
    out = pl.pallas_call(
        body, name=name, grid=(n_rows // tm,),
        in_specs=([_row_spec(a.shape[1], tm) for a in tiled] + [_full_spec(p) for p in params]
                  + [_row_spec(c.shape[1], tm) for c in cots] + [_row_spec(a.shape[1], tm) for a in add_arrays]),
        out_specs=([_row_spec(tiled[k].shape[1], tm) for k in want] + [_full_spec(p) for p in params]),
        out_shape=([jax.ShapeDtypeStruct(tiled[k].shape, grad_dtypes[k]) for k in want]
                   + [jax.ShapeDtypeStruct(p.shape, F32) for p in params]),
        compiler_params=_cparams("arbitrary"),
    )(*tiled, *params, *cots, *add_arrays)
    return out


def _matmul(name, a, b, dims, out_dtype, add=None, rider=None, tm=1024, tn=1024, tk=2048):
    if dims == NN:
        (m, k), n = a.shape, b.shape[1]
    elif dims == NT:
        (m, k), n = a.shape, b.shape[0]
    else:
        (k, m), n = a.shape, b.shape[1]
    tm, tn, tk = min(tm, m), min(tn, n), min(tk, k)
    while k % tk:
        tk -= LANES
    assert m % tm == 0 and n % tn == 0 and k % tk == 0, (name, m, n, k)
    nk = k // tk
    a_spec = (pl.BlockSpec((tk, tm), lambda i, j, kk: (kk, i)) if dims == TN
              else pl.BlockSpec((tm, tk), lambda i, j, kk: (i, kk)))
    b_spec = (pl.BlockSpec((tn, tk), lambda i, j, kk: (j, kk)) if dims == NT
              else pl.BlockSpec((tk, tn), lambda i, j, kk: (kk, j)))
    o_spec = pl.BlockSpec((tm, tn), lambda i, j, kk: (i, j))
    has_add = add is not None

    def body(*refs):
        a_ref, b_ref = refs[0], refs[1]
        part = _dg(a_ref[...].astype(BF16), b_ref[...].astype(BF16), dims)
        if nk == 1:
            o_ref = refs[-1]
            if has_add:
                part = part + refs[2][...]
            o_ref[...] = part.astype(o_ref.dtype)
            return
        o_ref, acc = refs[-2], refs[-1]
        kk = pl.program_id(2)

        @pl.when(kk == 0)
        def _():
            acc[...] = part

        @pl.when(kk > 0)
        def _():
            acc[...] += part

        @pl.when(kk == nk - 1)
        def _():
            r = acc[...]
            if has_add:
                r = r + refs[2][...]
            o_ref[...] = r.astype(o_ref.dtype)

    in_specs = [a_spec, b_spec] + ([o_spec] if has_add else [])
    scratch = [pltpu.VMEM((tm, tn), F32)] if nk > 1 else []
    args = (a, b) + ((add,) if has_add else ())
    if rider is not None:
        (out,), got = _hosted(body, rider, name=name, grid=(m // tm, n // tn, nk), in_specs=in_specs,
                              out_specs=[o_spec], out_shape=[jax.ShapeDtypeStruct((m, n), out_dtype)],
                              scratch_shapes=scratch, args=args)
        return out, got
    return pl.pallas_call(
        body, name=name, grid=(m // tm, n // tn, nk), in_specs=in_specs, out_specs=o_spec,
        out_shape=jax.ShapeDtypeStruct((m, n), out_dtype), scratch_shapes=scratch,
        compiler_params=_cparams("parallel", "parallel", "arbitrary"),
    )(*args)


CONV_TM = 256
HALO = 8


def _shift_down(x, halo, j):
    if j == 0:
        return x, x[:HALO]
    xr = pltpu.roll(x, j, 0)
    hr = pltpu.roll(halo, j, 0)
    top = jnp.where(_iota2((HALO, x.shape[1]), 0) < j, hr, xr[:HALO])
    return xr, top


def _conv_fwd(name, x, w, b, rider=None):
    t, c = x.shape
    tm = min(CONV_TM, t)
    hb = tm // HALO

    def body(x_ref, halo_ref, w_ref, b_ref, pre_ref, xs_ref, bm_ref, cm_ref):
        i = pl.program_id(0)
        xv = x_ref[...]
        halo = jnp.where(i > 0, halo_ref[...], 0.0)
        main = jnp.zeros_like(xv) + b_ref[...]
        top = jnp.zeros((HALO, c), F32) + b_ref[...]
        for kk in range(CONV_WIDTH):
            xr, tp = _shift_down(xv, halo, CONV_WIDTH - 1 - kk)
            main = main + w_ref[kk:kk + 1, :] * xr
            top = top + w_ref[kk:kk + 1, :] * tp
        pre = jnp.concatenate([top, main[HALO:]], axis=0)
        pre_ref[...] = pre
        act = _silu(pre)
        xs_ref[...] = act[:, :D_MODEL]
        bm_ref[...] = act[:, D_MODEL:D_MODEL + SSD_GROUPS * SSD_STATE]
        cm_ref[...] = act[:, D_MODEL + SSD_GROUPS * SSD_STATE:]

    gs = SSD_GROUPS * SSD_STATE
    return _hosted(
        body, rider, name=name, grid=(t // tm,),
        in_specs=[_row_spec(c, tm),
                  pl.BlockSpec((HALO, c), lambda i: (jnp.maximum(i * hb - 1, 0), 0)),
                  _full_spec(w), _full_spec(b)],
        out_specs=[_row_spec(c, tm), _row_spec(D_MODEL, tm), _row_spec(gs, tm), _row_spec(gs, tm)],
        out_shape=[jax.ShapeDtypeStruct((t, c), F32), jax.ShapeDtypeStruct((t, D_MODEL), F32),
                   jax.ShapeDtypeStruct((t, gs), F32), jax.ShapeDtypeStruct((t, gs), F32)],
        scratch_shapes=[], args=(x, x, w, b))


def _dsilu(pre, dact):
    s = jax.nn.sigmoid(pre)
    return dact * (s * (1.0 + pre * (1.0 - s)))


def _conv_bwd(name, x, pre, w, dxs, dbm, dcm, rider=None):
    t, c = x.shape
    tm = min(CONV_TM, t)
    hb = tm // HALO
    last_hb = t // HALO - 1
    n_tiles = t // tm

    def body(x_ref, xh_ref, pre_ref, preh_ref, w_ref, dxs_ref, dbm_ref, dcm_ref,
             dxsh_ref, dbmh_ref, dcmh_ref, dx_ref, dw_ref, db_ref):
        i = pl.program_id(0)
        dact = jnp.concatenate([dxs_ref[...], dbm_ref[...], dcm_ref[...]], axis=1)
        dpre = _dsilu(pre_ref[...], dact)
        dact_h = jnp.concatenate([dxsh_ref[...], dbmh_ref[...], dcmh_ref[...]], axis=1)
        dpre_h = jnp.where(i < n_tiles - 1, _dsilu(preh_ref[...], dact_h), 0.0)
        xv = x_ref[...]
        xh = jnp.where(i > 0, xh_ref[...], 0.0)

        @pl.when(i == 0)
        def _():
            dw_ref[...] = jnp.zeros_like(dw_ref)
            db_ref[...] = jnp.zeros_like(db_ref)

        db_ref[...] += jnp.sum(dpre, axis=0, keepdims=True)
        dxm = jnp.zeros_like(xv)
        dxt = jnp.zeros((HALO, c), F32)
        row8 = _iota2((HALO, c), 0)
        for kk in range(CONV_WIDTH):
            j = CONV_WIDTH - 1 - kk
            wk = w_ref[kk:kk + 1, :]
            xr, tp = _shift_down(xv, xh, j)
            full = jnp.sum(dpre * xr, axis=0, keepdims=True)
            fix = jnp.sum(dpre[:HALO] * (tp - xr[:HALO]), axis=0, keepdims=True)
            dw_ref[kk:kk + 1, :] += full + fix
            if j == 0:
                dxm = dxm + wk * dpre
                dxt = dxt + wk * dpre[tm - HALO:]
            else:
                dr = pltpu.roll(dpre, tm - j, 0)
                hr = pltpu.roll(dpre_h, HALO - j, 0)
                dxm = dxm + wk * dr
                dxt = dxt + wk * jnp.where(row8 >= HALO - j, hr, dr[tm - HALO:])
        dx_ref[...] = jnp.concatenate([dxm[:tm - HALO], dxt], axis=0).astype(dx_ref.dtype)

    gs = SSD_GROUPS * SSD_STATE
    prev_halo = lambda i: (jnp.maximum(i * hb - 1, 0), 0)
    next_halo = lambda i: (jnp.minimum((i + 1) * hb, last_hb), 0)
    return _hosted(
        body, rider, name=name, grid=(n_tiles,),
        in_specs=[_row_spec(c, tm), pl.BlockSpec((HALO, c), prev_halo),
                  _row_spec(c, tm), pl.BlockSpec((HALO, c), next_halo), _full_spec(w),
                  _row_spec(D_MODEL, tm), _row_spec(gs, tm), _row_spec(gs, tm),
                  pl.BlockSpec((HALO, D_MODEL), next_halo), pl.BlockSpec((HALO, gs), next_halo),
                  pl.BlockSpec((HALO, gs), next_halo)],
        out_specs=[_row_spec(c, tm), _full_spec(w), pl.BlockSpec((1, c), lambda i: (0, 0))],
        out_shape=[jax.ShapeDtypeStruct((t, c), BF16), jax.ShapeDtypeStruct(w.shape, F32),
                   jax.ShapeDtypeStruct((1, c), F32)],
        scratch_shapes=[], args=(x, x, pre, pre, w, dxs, dbm, dcm, dxs, dbm, dcm))


def _ssd_prep(dt_raw, dt_bias, a_log):
    dt = _softplus(dt_raw + dt_bias)
    return dt, _cumsum_rows(dt * (-jnp.exp(a_log)))


def _head_col(x, h):
    return jnp.sum(jnp.where(_iota2(x.shape, 1) == h, x, 0.0), axis=1, keepdims=True)


def _ssd_pair(p, xs, dt, cs, d_skip, bg, cg, prev):
    l = xs.shape[0]
    first = _iota2((l, LANES), 1) < SSD_HEAD_DIM
    c0, c1 = _head_col(cs, 2 * p), _head_col(cs, 2 * p + 1)
    csf = jnp.where(first, c0, c1)
    dtf = jnp.where(first, _head_col(dt, 2 * p), _head_col(dt, 2 * p + 1))
    dskf = jnp.where(_iota2((1, LANES), 1) < SSD_HEAD_DIM, _head_col(d_skip, 2 * p), _head_col(d_skip, 2 * p + 1))
    cs_last = jnp.sum(jnp.where(_iota2((l, LANES), 0) == l - 1, csf, 0.0), axis=0, keepdims=True)
    xc = xs * dtf
    scores = _bdot_nt(cg, bg)
    causal = _iota2((l, l), 0) >= _iota2((l, l), 1)

    def decay(col):
        a = jnp.broadcast_to(col, (l, l))
        return jnp.where(causal, jnp.exp(jnp.where(causal, a - a.T, 0.0)), 0.0)

    y_diag = (_bdot_nn(scores * decay(c0), jnp.where(first, xc, 0.0))
              + _bdot_nn(scores * decay(c1), jnp.where(first, 0.0, xc)))
    states = _bdot_tn(bg, xc * jnp.exp(cs_last - csf))
    new_state = jnp.exp(cs_last) * prev + states
    y_off = _bdot_nn(cg, prev) * jnp.exp(csf)
    return y_diag + y_off + xs * dskf, new_state


def _ssd_specs(nc_rev=None):
    def ch(c):
        return c if nc_rev is None else nc_rev - 1 - c

    gs = SSD_GROUPS * SSD_STATE
    return dict(
        wide=pl.BlockSpec((CHUNK, D_MODEL), lambda c: (ch(c), 0)),
        group=pl.BlockSpec((CHUNK, gs), lambda c: (ch(c), 0)),
        chunk=pl.BlockSpec((CHUNK, LANES), lambda c: (ch(c), 0)),
        vec=pl.BlockSpec((1, LANES), lambda c: (0, 0)),
        state=pl.BlockSpec((1, N_PAIRS, LANES, LANES), lambda c: (ch(c), 0, 0, 0)),
    )


def _pair_slices(p):
    g = p // PAIRS_PER_GROUP
    return slice(p * LANES, (p + 1) * LANES), slice(g * LANES, (g + 1) * LANES)


def _ssd_fwd(name, xs, bm, cm, dt_raw, dt_bias, a_log, d_skip, rider=None):
    t = xs.shape[0]
    nc = t // CHUNK

    def body(xs_ref, b_ref, c_ref, dt_ref, bias_ref, alog_ref, dsk_ref, y_ref, prev_ref, state, dt_s, cs_s):
        c = pl.program_id(0)
        dt_s[...], cs_s[...] = _ssd_prep(dt_ref[...], bias_ref[...], alog_ref[...])

        @pl.when(c == 0)
        def _():
            state[...] = jnp.zeros_like(state)

        for p in range(N_PAIRS):
            sl, gsl = _pair_slices(p)
            prev = state[p]
            prev_ref[0, p] = prev
            y, new_state = _ssd_pair(p, xs_ref[:, sl], dt_s[...], cs_s[...], dsk_ref[...],
                                     b_ref[:, gsl], c_ref[:, gsl], prev)
            y_ref[:, sl] = y
            state[p] = new_state

    sp = _ssd_specs()
    return _hosted(
        body, rider, name=name, grid=(nc,),
        in_specs=[sp["wide"], sp["group"], sp["group"], sp["chunk"], sp["vec"], sp["vec"], sp["vec"]],
        out_specs=[sp["wide"], sp["state"]],
        out_shape=[jax.ShapeDtypeStruct((t, D_MODEL), F32),
                   jax.ShapeDtypeStruct((nc, N_PAIRS, LANES, LANES), F32)],
        scratch_shapes=[pltpu.VMEM((N_PAIRS, LANES, LANES), F32), pltpu.VMEM((CHUNK, LANES), F32),
                        pltpu.VMEM((CHUNK, LANES), F32)],
        args=(xs, bm, cm, dt_raw, dt_bias, a_log, d_skip))


def _ssd_bwd(name, xs, bm, cm, dt_raw, dt_bias, a_log, d_skip, prev_states, dy, rider=None):
    t = xs.shape[0]
    nc = t // CHUNK

    def body(xs_ref, b_ref, c_ref, dt_ref, bias_ref, alog_ref, dsk_ref, prev_ref, dy_ref,
             dxs_ref, db_ref, dc_ref, ddt_ref, dbias_ref, dalog_ref, ddsk_ref, dstate, dt_s, cs_s, g_dt, g_cs):
        c = pl.program_id(0)
        (dt_s[...], cs_s[...]), vjp_prep = jax.vjp(_ssd_prep, dt_ref[...], bias_ref[...], alog_ref[...])

        @pl.when(c == 0)
        def _():
            dstate[...] = jnp.zeros_like(dstate)
            dbias_ref[...] = jnp.zeros_like(dbias_ref)
            dalog_ref[...] = jnp.zeros_like(dalog_ref)
            ddsk_ref[...] = jnp.zeros_like(ddsk_ref)

        g_dt[...] = jnp.zeros_like(g_dt)
        g_cs[...] = jnp.zeros_like(g_cs)
        for p in range(N_PAIRS):
            sl, gsl = _pair_slices(p)
            pair = lambda *a, p=p: _ssd_pair(p, *a)
            _, vjp = jax.vjp(pair, xs_ref[:, sl], dt_s[...], cs_s[...], dsk_ref[...],
                             b_ref[:, gsl], c_ref[:, gsl], prev_ref[0, p])
            dxs, ddt, dcs, ddsk, dbg, dcg, dprev = vjp((dy_ref[:, sl], dstate[p]))
            dxs_ref[:, sl] = dxs
            g_dt[...] += ddt
            g_cs[...] += dcs
            ddsk_ref[...] += ddsk
            dstate[p] = dprev
            if p % PAIRS_PER_GROUP == 0:
                db_ref[:, gsl] = dbg
                dc_ref[:, gsl] = dcg
            else:
                db_ref[:, gsl] += dbg
                dc_ref[:, gsl] += dcg

        ddt_raw, dbias, dalog = vjp_prep((g_dt[...], g_cs[...]))
        ddt_ref[...] = ddt_raw
        dbias_ref[...] += dbias
        dalog_ref[...] += dalog

    sp = _ssd_specs(nc)
    gs = SSD_GROUPS * SSD_STATE
    return _hosted(
        body, rider, name=name, grid=(nc,),
        in_specs=[sp["wide"], sp["group"], sp["group"], sp["chunk"], sp["vec"], sp["vec"], sp["vec"],
                  sp["state"], sp["wide"]],
        out_specs=[sp["wide"], sp["group"], sp["group"], sp["chunk"], sp["vec"], sp["vec"], sp["vec"]],
        out_shape=[jax.ShapeDtypeStruct((t, D_MODEL), F32), jax.ShapeDtypeStruct((t, gs), F32),
                   jax.ShapeDtypeStruct((t, gs), F32), jax.ShapeDtypeStruct((t, LANES), F32),
                   jax.ShapeDtypeStruct((1, LANES), F32), jax.ShapeDtypeStruct((1, LANES), F32),
                   jax.ShapeDtypeStruct((1, LANES), F32)],
        scratch_shapes=[pltpu.VMEM((N_PAIRS, LANES, LANES), F32)] + [pltpu.VMEM((CHUNK, LANES), F32)] * 4,
        args=(xs, bm, cm, dt_raw, dt_bias, a_log, d_skip, prev_states, dy))


def _ssd_gate(y, z, w):
    return (_rms(y * _silu(z), w),)


def _sgu_norm(v, w, b):
    return (_layer_norm(_gelu(v), w, b),)


def _sgu_group(u, gate, vn, w, bcol):
    l = u.shape[0]
    wc = jnp.where(_iota2((l, l), 0) >= _iota2((l, l), 1), w, 0.0)
    return _gelu(u) * (_bdot_nn(wc, vn) + bcol) * _silu(gate)


def _sgu_fwd(name, u, gate, vn, w, bcol):
    t = u.shape[0]
    blk = _row_spec(D_MODEL, CHUNK)

    def body(u_ref, g_ref, vn_ref, w_ref, b_ref, y_ref):
        for g in range(SGU_GROUPS):
            sl = slice(g * LANES, (g + 1) * LANES)
            y_ref[:, sl] = _sgu_group(u_ref[:, sl], g_ref[:, sl], vn_ref[:, sl], w_ref[g], b_ref[g]).astype(y_ref.dtype)

    return pl.pallas_call(
        body, name=name, grid=(t // CHUNK,),
        in_specs=[blk, blk, blk, _full_spec(w), _full_spec(bcol)], out_specs=blk,
        out_shape=jax.ShapeDtypeStruct((t, D_MODEL), BF16),
        compiler_params=_cparams("parallel"),
    )(u, gate, vn, w, bcol)


def _sgu_bwd(name, u, gate, vn, w, bcol, dy, rider=None):
    t = u.shape[0]
    blk = _row_spec(D_MODEL, CHUNK)

    def body(u_ref, g_ref, vn_ref, w_ref, b_ref, dy_ref, du_ref, dg_ref, dvn_ref, dw_ref, db_ref):
        @pl.when(pl.program_id(0) == 0)
        def _():
            dw_ref[...] = jnp.zeros_like(dw_ref)
            db_ref[...] = jnp.zeros_like(db_ref)

        for g in range(SGU_GROUPS):
            sl = slice(g * LANES, (g + 1) * LANES)
            _, vjp = jax.vjp(_sgu_group, u_ref[:, sl], g_ref[:, sl], vn_ref[:, sl], w_ref[g], b_ref[g])
            du, dg, dvn, dw, db = vjp(dy_ref[:, sl])
            du_ref[:, sl] = du.astype(du_ref.dtype)
            dg_ref[:, sl] = dg.astype(dg_ref.dtype)
            dvn_ref[:, sl] = dvn
            dw_ref[g] += dw
            db_ref[g] += db

    return _hosted(
        body, rider, name=name, grid=(t // CHUNK,),
        in_specs=[blk, blk, blk, _full_spec(w), _full_spec(bcol), blk],
        out_specs=[blk, blk, blk, _full_spec(w), _full_spec(bcol)],
        out_shape=[jax.ShapeDtypeStruct((t, D_MODEL), BF16), jax.ShapeDtypeStruct((t, D_MODEL), BF16),
                   jax.ShapeDtypeStruct((t, D_MODEL), F32), jax.ShapeDtypeStruct(w.shape, F32),
                   jax.ShapeDtypeStruct(bcol.shape, F32)],
        scratch_shapes=[], args=(u, gate, vn, w, bcol, dy))


SB_SCALE = LANES ** -0.5
LOG2_E = 1.4426950408889634
SB_TQ = 512
SB_TS = 512


def _keep(mask, x):
    return x if mask is None else jnp.where(mask, x, 0.0)


def _sb_pieces(z2, mask):
    tl = jnp.log(1.0 + jnp.exp2(-jnp.abs(z2))) * LOG2_E
    lk = _keep(mask, -(jnp.maximum(z2, 0.0) + tl))
    ls = jnp.minimum(z2, 0.0) - tl
    return lk, ls


def _split2(x):
    hi = x.astype(BF16)
    return jnp.concatenate([hi, (x - hi.astype(F32)).astype(BF16)], axis=1)


def _tri2(cmp):
    sq = (CHUNK, CHUNK)
    m = cmp(_iota2(sq, 0), _iota2(sq, 1)).astype(BF16)
    return jnp.concatenate([m, m], axis=0)


def _tri_sums(blocks, tri2):
    tq = blocks[0].shape[0]
    res = _dg(jnp.concatenate([_split2(b) for b in blocks], axis=0), tri2, NN)
    return [res[b * tq:(b + 1) * tq] for b in range(len(blocks))]


def _sb_tiles(t):
    tq, ts = min(SB_TQ, t), min(SB_TS, t)
    assert t % tq == 0 and t % ts == 0 and ts % tq == 0 and ts % CHUNK == 0
    return tq, ts, ts // CHUNK


def _sb_logits(qb, ks, off, q_off, masked):
    tq = qb.shape[0]
    z2 = _dg(qb, ks, NT) * (SB_SCALE * LOG2_E)
    out = []
    for b in range(ks.shape[0] // CHUNK):
        mask = None
        if masked:
            mask = (_iota2((tq, CHUNK), 1) + (off + b * CHUNK)) < (_iota2((tq, CHUNK), 0) + q_off)
        out.append(_sb_pieces(z2[:, b * CHUNK:(b + 1) * CHUNK], mask) + (mask,))
    return out


def _attn_fwd(name, q, k, v, rider=None):
    t = q.shape[0]
    tq, ts, nb = _sb_tiles(t)

    def body(q_ref, k_ref, v_ref, y_ref, tot_ref):
        i = pl.program_id(1)
        qb = q_ref[...]
        later2 = _tri2(lambda r, c: r > c)
        last = (i * tq + tq - 1) // ts

        def span(j, carry, masked):
            acc, after = carry
            off = pl.multiple_of(j * ts, ts)
            pieces = _sb_logits(qb, k_ref[pl.ds(off, ts), :], off, i * tq, masked)
            inside = _tri_sums([lk for lk, _, _ in pieces], later2)
            ws = [None] * nb
            for b in reversed(range(nb)):
                lk, ls, mask = pieces[b]
                ws[b] = _keep(mask, jnp.exp2(ls + inside[b] + after)).astype(BF16)
                after = after + jnp.sum(lk, axis=1, keepdims=True)
            acc = acc + _dg(jnp.concatenate(ws, axis=1), v_ref[pl.ds(off, ts), :], NN)
            return acc, after

        carry = span(last, (jnp.zeros((tq, LANES), F32), jnp.zeros((tq, 1), F32)), True)
        acc, tot = lax.fori_loop(0, last, lambda n, c: span(last - 1 - n, c, False), carry)
        y_ref[...] = acc
        tot_ref[...] = jnp.broadcast_to(tot, (tq, LANES))

    qsp = pl.BlockSpec((tq, LANES), lambda h, i: (i, h))
    kvsp = pl.BlockSpec((t, LANES), lambda h, i: (0, h))
    return _hosted(
        body, rider, name=name, grid=(SB_HEADS, t // tq),
        in_specs=[qsp, kvsp, kvsp], out_specs=[qsp, qsp],
        out_shape=[jax.ShapeDtypeStruct((t, D_MODEL), F32), jax.ShapeDtypeStruct((t, D_MODEL), F32)],
        scratch_shapes=[], args=(q, k, v))


def _attn_bwd(name, q, k, v, tot, dy, rider=None):
    t = q.shape[0]
    tq, ts, nb = _sb_tiles(t)
    nq, ns = t // tq, t // ts

    def body(q_ref, k_ref, v_ref, tot_ref, dy_ref, dq_ref, dk_ref, dv_ref, dkt_acc, dvt_acc):
        i = pl.program_id(1)

        @pl.when(i == 0)
        def _():
            dkt_acc[...] = jnp.zeros_like(dkt_acc)
            dvt_acc[...] = jnp.zeros_like(dvt_acc)

        qb = q_ref[...]
        dy = dy_ref[...]
        dyb = dy.astype(BF16)
        q_t = qb.astype(F32).T.astype(BF16)
        dy_t = dy.T.astype(BF16)
        totb = tot_ref[...]
        upto2 = _tri2(lambda r, c: r <= c)
        before2 = _tri2(lambda r, c: r < c)
        last = (i * tq + tq - 1) // ts

        def span(j, carry, masked):
            dq, lk_seen, e_seen = carry
            off = pl.multiple_of(j * ts, ts)
            ks = k_ref[pl.ds(off, ts), :]
            pieces = _sb_logits(qb, ks, off, i * tq, masked)
            dw = _dg(dyb, v_ref[pl.ds(off, ts), :], NT)
            upto = _tri_sums([lk for lk, _, _ in pieces], upto2)
            ws, es = [], []
            for b in range(nb):
                lk, ls, mask = pieces[b]
                w = _keep(mask, jnp.exp2(ls + ((totb - upto[b]) - lk_seen)))
                ws.append(w.astype(BF16))
                es.append(dw[:, b * CHUNK:(b + 1) * CHUNK] * w)
                lk_seen = lk_seen + jnp.sum(lk, axis=1, keepdims=True)
            before = _tri_sums(es, before2)
            dzs = []
            for b in range(nb):
                _, ls, mask = pieces[b]
                sig = jnp.exp2(ls)
                dlk = e_seen + before[b]
                dzs.append((_keep(mask, es[b] - (es[b] + dlk) * sig) * SB_SCALE).astype(BF16))
                e_seen = e_seen + jnp.sum(es[b], axis=1, keepdims=True)
            dz = jnp.concatenate(dzs, axis=1)
            dq = dq + _dg(dz, ks, NN)
            dkt_acc[j] += _dg(q_t, dz, NN)
            dvt_acc[j] += _dg(dy_t, jnp.concatenate(ws, axis=1), NN)
            return dq, lk_seen, e_seen

        zero_col = jnp.zeros((tq, 1), F32)
        carry = lax.fori_loop(0, last, lambda j, c: span(j, c, False), (jnp.zeros((tq, LANES), F32), zero_col, zero_col))
        dq, _, _ = span(last, carry, True)
        dq_ref[...] = dq.astype(dq_ref.dtype)

        @pl.when(i == nq - 1)
        def _():
            for s in range(ns):
                dk_ref[s * ts:(s + 1) * ts, :] = dkt_acc[s].T.astype(dk_ref.dtype)
                dv_ref[s * ts:(s + 1) * ts, :] = dvt_acc[s].T.astype(dv_ref.dtype)

    qsp = pl.BlockSpec((tq, LANES), lambda h, i: (i, h))
    kvsp = pl.BlockSpec((t, LANES), lambda h, i: (0, h))
    return _hosted(
        body, rider, name=name, grid=(SB_HEADS, nq),
        in_specs=[qsp, kvsp, kvsp, qsp, qsp], out_specs=[qsp, kvsp, kvsp],
        out_shape=[jax.ShapeDtypeStruct((t, D_MODEL), BF16)] * 3,
        scratch_shapes=[pltpu.VMEM((ns, LANES, ts), F32), pltpu.VMEM((ns, LANES, ts), F32)],
        args=(q, k, v, tot, dy))


def _attn_gate(y, g):
    return (y * _silu(g),)


def _loss_head(name, h, target, w, tm):
    t, d = h.shape

    def body(h_ref, t_ref, w_ref, dh_ref, dw_ref, loss_ref):
        tgt = t_ref[...]

        def f(hv, wv):
            e = _rms(hv, wv) - tgt
            return 0.5 * jnp.mean(e * e, axis=-1, keepdims=True)

        row_loss, vjp = jax.vjp(f, h_ref[...], w_ref[...])
        dh, dw = vjp(jnp.ones_like(row_loss))
        dh_ref[...] = dh

        @pl.when(pl.program_id(0) == 0)
        def _():
            dw_ref[...] = jnp.zeros_like(dw_ref)
            loss_ref[...] = jnp.zeros_like(loss_ref)

        dw_ref[...] += dw
        loss_ref[...] += jnp.sum(row_loss, axis=0, keepdims=True)

    return pl.pallas_call(
        body, name=name, grid=(t // tm,),
        in_specs=[_row_spec(d, tm), _row_spec(d, tm), _full_spec(w)],
        out_specs=[_row_spec(d, tm), _full_spec(w), pl.BlockSpec((1, 1), lambda i: (0, 0))],
        out_shape=[jax.ShapeDtypeStruct((t, d), F32), jax.ShapeDtypeStruct(w.shape, F32),
                   jax.ShapeDtypeStruct((1, 1), F32)],
        compiler_params=_cparams("arbitrary"),
    )(h, target, w)


def _pick_tile(rows, cap):
    if rows <= cap:
        return rows
    for tm in range(cap - cap % 16, 0, -16):
        if rows % tm == 0:
            return tm
    raise ValueError(rows)


def _adamw(name, w, g, m, v, tm):
    n_l, r, c = w.shape
    tm = _pick_tile(r, tm)

    def body(w_ref, g_ref, m_ref, v_ref, d_ref, nm_ref, nv_ref):
        g_ = g_ref[...]
        m_ = ADAM_B1 * m_ref[...] + (1.0 - ADAM_B1) * g_
        v_ = ADAM_B2 * v_ref[...] + (1.0 - ADAM_B2) * (g_ * g_)
        m_hat = m_ / (1.0 - ADAM_B1 ** ADAM_STEP)
        v_hat = v_ / (1.0 - ADAM_B2 ** ADAM_STEP)
        d_ref[...] = -ADAM_LR * (m_hat / (jnp.sqrt(v_hat) + ADAM_EPS) + ADAM_WD * w_ref[...])
        nm_ref[...] = m_
        nv_ref[...] = v_

    spec = pl.BlockSpec((1, tm, c), lambda l, i: (l, i, 0))
    return pl.pallas_call(
        body, name=name, grid=(n_l, r // tm), in_specs=[spec] * 4, out_specs=[spec] * 3,
        out_shape=[jax.ShapeDtypeStruct((n_l, r, c), F32)] * 3, compiler_params=_cparams("parallel", "parallel"),
    )(w, g, m, v)


def _sum_parts(name, layers, tm):
    n, r, c = layers[0].shape
    tm = _pick_tile(r, tm)
    nblk = r // tm

    def body(*refs):
        o_ref = refs[-1]
        for li, p_ref in enumerate(refs[:-1]):
            @pl.when(pl.program_id(0) == li)
            def _(p_ref=p_ref):
                s = p_ref[0].astype(F32)
                for d in range(1, n):
                    s = s + p_ref[d].astype(F32)
                o_ref[0] = s

    def spec(li):
        return pl.BlockSpec((n, tm, c), lambda l, i: (0, jnp.where(l == li, i, jnp.where(l < li, 0, nblk - 1)), 0))

    return pl.pallas_call(
        body, name=name, grid=(len(layers), nblk),
        in_specs=[spec(li) for li in range(len(layers))],
        out_specs=pl.BlockSpec((1, tm, c), lambda l, i: (l, i, 0)),
        out_shape=jax.ShapeDtypeStruct((len(layers), r, c), F32), compiler_params=_cparams("arbitrary", "arbitrary"),
    )(*layers)


def _peer(k):
    x, y, c = lax.axis_index("x"), lax.axis_index("y"), lax.axis_index("c")
    px, py, pc = x ^ ((k >> 2) & 1), y ^ ((k >> 1) & 1), c ^ (k & 1)
    return (px, py, pc), 4 * px + 2 * py + pc


def _exchange(name, xs, gather):
    rider = _Rider(xs, gather)
    n = rider.n

    def body(*refs):
        x_refs, out_refs, sems = refs[:n], refs[n:2 * n], refs[2 * n:]
        _exchange_start(x_refs, out_refs, sems, gather)
        _exchange_finish(x_refs, out_refs, sems, gather)

    return pl.pallas_call(
        body, name=name, in_specs=[ANY_SPEC] * n, out_specs=[ANY_SPEC] * n,
        out_shape=rider.out_shape(), scratch_shapes=rider.scratch(),
    )(*xs)


ANY_SPEC = pl.BlockSpec(memory_space=pl.ANY)
SAME_CORE = (2, 4, 6)


class _Rider:
    def __init__(self, xs, gather):
        self.xs, self.gather, self.n = list(xs), gather, len(xs)

    def out_shape(self):
        return [jax.ShapeDtypeStruct((N_DEV,) + tuple(x.shape if self.gather else x.shape[1:]), x.dtype)
                for x in self.xs]

    def scratch(self):
        return [pltpu.SemaphoreType.DMA((self.n, N_DEV - 1)), pltpu.SemaphoreType.DMA((self.n, N_DEV - 1)),
                pltpu.SemaphoreType.DMA((self.n,))]


def _remote(src, dst, sems, a, k, dev):
    return pltpu.make_async_remote_copy(
        src_ref=src, dst_ref=dst, send_sem=sems[0].at[a, k - 1], recv_sem=sems[1].at[a, k - 1],
        device_id=dev, device_id_type=pl.DeviceIdType.MESH)


def _exchange_start(x_refs, out_refs, sems, gather):
    _, me = _peer(0)
    for a, (x, out) in enumerate(zip(x_refs, out_refs)):
        pltpu.make_async_copy(x if gather else x.at[me], out.at[me], sems[2].at[a]).start()
    for k in ((1,) + SAME_CORE if gather else range(1, N_DEV)):
        dev, idx = _peer(k)
        for a, (x, out) in enumerate(zip(x_refs, out_refs)):
            _remote(x if gather else x.at[idx], out.at[me], sems, a, k, dev).start()


def _exchange_finish(x_refs, out_refs, sems, gather):
    _, me = _peer(0)
    sibling, _ = _peer(1)
    pairs = list(enumerate(zip(x_refs, out_refs)))
    waited = ()
    if gather:
        for k in SAME_CORE:
            dev, idx = _peer(k)
            for a, (x, out) in pairs:
                _remote(x, out.at[idx], sems, a, k, dev).wait_recv()
                _remote(out.at[idx], out.at[idx], sems, a, k + 1, sibling).start()
        waited = SAME_CORE
    for k in range(1, N_DEV):
        dev, idx = _peer(k)
        for a, (x, out) in pairs:
            cp = _remote(x if gather else x.at[idx], out.at[idx], sems, a, k, dev)
            if k not in waited:
                cp.wait_recv()
            cp.wait_send()
    for a, (x, out) in pairs:
        pltpu.make_async_copy(x if gather else x.at[me], out.at[me], sems[2].at[a]).wait()


def _hosted(body, rider, *, name, grid, in_specs, out_specs, out_shape, scratch_shapes, args):
    sem = ("arbitrary",) * len(grid)
    if rider is None:
        return pl.pallas_call(body, name=name, grid=grid, in_specs=in_specs, out_specs=out_specs, out_shape=out_shape,
                              scratch_shapes=scratch_shapes, compiler_params=_cparams(*sem))(*args), []
    n_in, n_out, n_scr, nr = len(in_specs), len(out_specs), len(scratch_shapes), rider.n

    def hosted(*refs):
        ins, refs = refs[:n_in], refs[n_in:]
        r_in, refs = refs[:nr], refs[nr:]
        outs, refs = refs[:n_out], refs[n_out:]
        r_out, refs = refs[:nr], refs[nr:]
        scr, sems = refs[:n_scr], refs[n_scr:]
        first = pl.program_id(0) == 0
        last = pl.program_id(0) == grid[0] - 1
        for ax in range(1, len(grid)):
            first = first & (pl.program_id(ax) == 0)
            last = last & (pl.program_id(ax) == grid[ax] - 1)

        @pl.when(first)
        def _():
            _exchange_start(r_in, r_out, sems, rider.gather)

        body(*ins, *outs, *scr)

        @pl.when(last)
        def _():
            _exchange_finish(r_in, r_out, sems, rider.gather)

    res = pl.pallas_call(
        hosted, name=name, grid=grid, in_specs=list(in_specs) + [ANY_SPEC] * nr,
        out_specs=list(out_specs) + [ANY_SPEC] * nr, out_shape=list(out_shape) + rider.out_shape(),
        scratch_shapes=list(scratch_shapes) + rider.scratch(), compiler_params=_cparams(*sem),
    )(*args, *rider.xs)
    return list(res[:n_out]), list(res[n_out:])


PACK_ALIGN = 8 * LANES
PACK_ROWS = 512


def _pack(arrays, dtype):
    pieces, total = [], 0
    for a in arrays:
        f = a.reshape(-1).astype(dtype)
        pad = (-f.shape[0]) % PACK_ALIGN
        pieces.append(jnp.pad(f, (0, pad)) if pad else f)
        total += f.shape[0] + pad
    tail = (-total) % (PACK_ROWS * LANES)
    if tail:
        pieces.append(jnp.zeros((tail,), dtype))
    return jnp.concatenate(pieces).reshape(-1, LANES)


def _unpack(packed, shapes, lead=()):
    flat = packed.reshape(lead + (-1,))
    out, off = [], 0
    for s in shapes:
        n = 1
        for d in s:
            n *= d
        out.append(flat[..., off:off + n].reshape(lead + tuple(s)))
        off += n + ((-n) % PACK_ALIGN)
    return out


def _pad_lanes(a):
    return jnp.pad(a, (0, LANES - a.shape[0])).reshape(1, LANES)


ROW_TM = 256
EVEN_SEGS = (("z", D_MODEL), ("xbc", CONV_DIM), ("dt", SSD_HEADS), ("g", D_MODEL), ("u", D_MODEL), ("v", D_MODEL))
ODD_SEGS = (("q", D_MODEL), ("k", D_MODEL), ("v", D_MODEL), ("g", D_MODEL))
TAIL_ROWS = (368, 512, 64, 368, 368, 368)


def _split_cols(w, segs):
    out, off = {}, 0
    for nm, n in segs:
        out[nm] = w[:, off:off + n]
        off += n
    return out


def _rms_fn(h, w):
    return (_rms(h, w),)


def _even_fwd(tag, h, p, riders=None, late=None):
    riders = riders or {}
    got = {}
    hn, = _rows_fwd(tag + "_norm", _rms_fn, [h], [p["norm_w"]], [(D_MODEL, BF16)], ROW_TM)
    proj = {nm: _matmul(f"{tag}_in_{nm}", hn, p["w_in"][nm], NN, F32) for nm, _ in EVEN_SEGS}
    (pre, xs, bm, cm), got["conv"] = _conv_fwd(tag + "_conv", proj["xbc"], p["conv_w"], p["conv_b"], riders.get("conv"))
    (y_ssd, states), got["ssd"] = _ssd_fwd(tag + "_ssd", xs, bm, cm, proj["dt"], p["dt_bias"], p["a_log"],
                                           p["d_skip"], riders.get("ssd"))
    if late is not None:
        p = dict(p, **late(got))
    ya, = _rows_fwd(tag + "_ssdgate", _ssd_gate, [y_ssd, proj["z"]], [p["ssd_norm_w"]], [(D_MODEL, BF16)], ROW_TM)
    vn, = _rows_fwd(tag + "_sgunorm", _sgu_norm, [proj["v"]], [p["sgu_ln_w"], p["sgu_ln_b"]], [(D_MODEL, F32)], ROW_TM)
    yb = _sgu_fwd(tag + "_sgu", proj["u"], proj["g"], vn, p["sgu_w"], p["sgu_b"])
    h1 = _matmul(tag + "_out_a", ya, p["w_out_a"], NN, F32, add=h)
    h2 = _matmul(tag + "_out_b", yb, p["w_out_b"], NN, F32, add=h1)
    saved = dict(h=h, hn=hn, proj=proj, pre=pre, xs=xs, bm=bm, cm=cm, y_ssd=y_ssd, states=states,
                 ya=ya, vn=vn, yb=yb)
    return h2, saved, p, got


def _even_bwd(tag, dh, s, p, riders=None):
    riders = riders or {}
    g, got = {}, {}

    def rider_for(host):
        return riders[host](g) if host in riders else None

    dh16 = dh.astype(BF16)
    proj = s["proj"]
    dya = _matmul(tag + "_dya", dh16, p["w_out_a"], NT, F32)
    dyb = _matmul(tag + "_dyb", dh16, p["w_out_b"], NT, F32)
    g["w_out_a"] = _matmul(tag + "_dwout_a", s["ya"], dh16, TN, BF16)
    g["w_out_b"] = _matmul(tag + "_dwout_b", s["yb"], dh16, TN, BF16)
    (du, dg, dvn, g["sgu_w"], g["sgu_b"]), got["sgu_b"] = _sgu_bwd(
        tag + "_sgu_b", proj["u"], proj["g"], s["vn"], p["sgu_w"], p["sgu_b"], dyb, rider_for("sgu_b"))
    dv, g["sgu_ln_w"], g["sgu_ln_b"] = _rows_bwd(tag + "_sgunorm_b", _sgu_norm, [proj["v"]],
                                                 [p["sgu_ln_w"], p["sgu_ln_b"]], [dvn], [BF16], ROW_TM // 2)
    dy_ssd, dz, g["ssd_norm_w"] = _rows_bwd(tag + "_ssdgate_b", _ssd_gate, [s["y_ssd"], proj["z"]],
                                            [p["ssd_norm_w"]], [dya], [F32, BF16], ROW_TM // 2)
    (dxs, dbm, dcm, ddt, g["dt_bias"], g["a_log"], g["d_skip"]), got["ssd_b"] = _ssd_bwd(
        tag + "_ssd_b", s["xs"], s["bm"], s["cm"], proj["dt"], p["dt_bias"], p["a_log"], p["d_skip"],
        s["states"], dy_ssd, rider_for("ssd_b"))
    (dxbc, g["conv_w"], g["conv_b"]), got["conv_b"] = _conv_bwd(
        tag + "_conv_b", proj["xbc"], s["pre"], p["conv_w"], dxs, dbm, dcm, rider_for("conv_b"))
    dproj = dict(z=dz, xbc=dxbc, dt=ddt.astype(BF16), g=dg, u=du, v=dv)
    g["w_in"] = {nm: _matmul(f"{tag}_dwin_{nm}", s["hn"], dproj[nm], TN, BF16) for nm, _ in EVEN_SEGS}
    tail = rider_for("tail") or [None] * len(EVEN_SEGS)
    dhn, got["tail"] = None, []
    for (nm, _), rider in zip(EVEN_SEGS, tail):
        dhn = _matmul(f"{tag}_dhn_{nm}", dproj[nm], p["w_in"][nm], NT, F32, add=dhn, rider=rider)
        if rider is not None:
            dhn, arrived = dhn
            got["tail"] += arrived
    dh_in, g["norm_w"] = _rows_bwd(tag + "_norm_b", _rms_fn, [s["h"]], [p["norm_w"]], [dhn], [F32],
                                   ROW_TM // 2, add=(0, dh))
    return dh_in, g, got


def _odd_fwd(tag, h, p, rider=None):
    hn, = _rows_fwd(tag + "_norm", _rms_fn, [h], [p["norm_w"]], [(D_MODEL, BF16)], ROW_TM)
    q = _matmul(tag + "_in_q", hn, p["w_in"]["q"], NN, BF16)
    k = _matmul(tag + "_in_k", hn, p["w_in"]["k"], NN, BF16)
    v = _matmul(tag + "_in_v", hn, p["w_in"]["v"], NN, BF16)
    gate = _matmul(tag + "_in_g", hn, p["w_in"]["g"], NN, F32)
    (y, tot), got = _attn_fwd(tag + "_attn", q, k, v, rider)
    yg, = _rows_fwd(tag + "_gate", _attn_gate, [y, gate], [], [(D_MODEL, BF16)], ROW_TM)
    h1 = _matmul(tag + "_out", yg, p["w_out"], NN, F32, add=h)
    return h1, dict(h=h, hn=hn, q=q, k=k, v=v, gate=gate, y=y, tot=tot, yg=yg), got


def _odd_bwd(tag, dh, s, p, rider=None):
    g = {}
    dh16 = dh.astype(BF16)
    dyg = _matmul(tag + "_dyg", dh16, p["w_out"], NT, F32)
    g["w_out"] = _matmul(tag + "_dwout", s["yg"], dh16, TN, BF16)
    dy, dgate = _rows_bwd(tag + "_gate_b", _attn_gate, [s["y"], s["gate"]], [], [dyg], [F32, BF16], ROW_TM)
    (dq, dk, dv), got = _attn_bwd(tag + "_attn_b", s["q"], s["k"], s["v"], s["tot"], dy, rider)
    dproj = dict(q=dq, k=dk, v=dv, g=dgate)
    dhn = None
    g["w_in"] = {}
    for nm, _ in ODD_SEGS:
        dhn = _matmul(f"{tag}_dhn_{nm}", dproj[nm], p["w_in"][nm], NT, F32, add=dhn)
        g["w_in"][nm] = _matmul(f"{tag}_dwin_{nm}", s["hn"], dproj[nm], TN, BF16)
    dh_in, g["norm_w"] = _rows_bwd(tag + "_norm_b", _rms_fn, [s["h"]], [p["norm_w"]], [dhn], [F32],
                                   ROW_TM // 2, add=(0, dh))
    return dh_in, g, got


BIG = ("ev_w_in", "ev_w_out", "od_w_in", "od_w_out")
SMALL = ("norm_w", "final_norm_w", "ev_conv_b", "ev_dt_bias", "ev_a_log", "ev_d_skip", "ev_ssd_norm_w",
         "ev_sgu_ln_w", "ev_sgu_ln_b", "ev_sgu_w", "ev_sgu_b")
WEIGHTS = ("norm_w", "final_norm_w", "ev_w_in", "ev_conv_w", "ev_conv_b", "ev_dt_bias", "ev_a_log", "ev_d_skip",
           "ev_ssd_norm_w", "ev_sgu_ln_w", "ev_sgu_ln_b", "ev_sgu_w", "ev_sgu_b", "ev_w_out", "od_w_in", "od_w_out")


def _step(w, m, v, x, loss_target):
    h = x[0]
    tgt = loss_target[0]
    n_even, n_odd = w["ev_w_in"].shape[0], w["od_w_in"].shape[0]
    depth = n_even + n_odd
    assert (n_even, n_odd) == (2, 2), "the exchange schedule below is written for the four-layer trunk"

    def shard(n, i):
        return w[n][i].astype(BF16)

    def take_cols(blocks, widths, lo, hi, r0=0, r1=None):
        pieces, start = [], 0
        for blk, wd in zip(blocks, widths):
            a, b = max(lo - start, 0), min(hi - start, wd)
            if a < b:
                pieces.append(blk[r0:r1, a:b])
            start += wd
        return pieces[0] if len(pieces) == 1 else jnp.concatenate(pieces, axis=1)

    def segments(gathered, segs):
        n = gathered.shape[2]
        out, lo = {}, 0
        for nm, wd in segs:
            out[nm] = take_cols([gathered[d] for d in range(N_DEV)], [n] * N_DEV, lo, lo + wd)
            lo += wd
        return out

    def rows(gathered):
        return gathered.reshape(-1, gathered.shape[2])

    def by_owner_cols(g_in, segs, r0=0, r1=None):
        blocks, widths = [g_in[nm] for nm, _ in segs], [wd for _, wd in segs]
        n = sum(widths) // N_DEV
        return jnp.stack([take_cols(blocks, widths, d * n, (d + 1) * n, r0, r1) for d in range(N_DEV)])

    def by_owner_rows(full):
        return full.reshape(N_DEV, full.shape[0] // N_DEV, full.shape[1])

    def even_params(layer, ev_in):
        i = layer // 2
        w_in = segments(ev_in, EVEN_SEGS)
        w_in["dt"] = jnp.pad(w_in["dt"], ((0, 0), (0, LANES - SSD_HEADS)))
        return dict(norm_w=w["norm_w"][layer][None], w_in=w_in, conv_w=conv_w[i], conv_b=w["ev_conv_b"][i][None],
                    dt_bias=_pad_lanes(w["ev_dt_bias"][i]), a_log=_pad_lanes(w["ev_a_log"][i]),
                    d_skip=_pad_lanes(w["ev_d_skip"][i]), ssd_norm_w=w["ev_ssd_norm_w"][i][None],
                    sgu_ln_w=w["ev_sgu_ln_w"][i][None], sgu_ln_b=w["ev_sgu_ln_b"][i][None],
                    sgu_w=w["ev_sgu_w"][i], sgu_b=w["ev_sgu_b"][i][:, :, None])

    def even_out_params(ev_out):
        full = rows(ev_out)
        return dict(w_out_a=full[:D_MODEL], w_out_b=full[D_MODEL:])

    def odd_params(layer, od_in, od_out):
        return dict(norm_w=w["norm_w"][layer][None], w_in=segments(od_in, ODD_SEGS), w_out=rows(od_out))

    def even_in_grads(g):
        return by_owner_cols(g["w_in"], EVEN_SEGS)

    def even_out_grads(g):
        return by_owner_rows(jnp.concatenate([g["w_out_a"], g["w_out_b"]], axis=0))

    def odd_grads(g):
        return [by_owner_cols(g["w_in"], ODD_SEGS), by_owner_rows(g["w_out"])]

    ev_in0, conv_w = _exchange("gather_first", [shard("ev_w_in", 0), w["ev_conv_w"]], gather=True)
    conv_w = jnp.moveaxis(conv_w, 0, 2).reshape(n_even, CONV_WIDTH, CONV_DIM)
    h, s0, p0, got = _even_fwd(
        "l0", h, even_params(0, ev_in0),
        {"conv": _Rider([shard("ev_w_out", 0)], True),
         "ssd": _Rider([shard("od_w_in", 0), shard("od_w_out", 0)], True)},
        late=lambda arrived: even_out_params(arrived["conv"][0]))
    p1 = odd_params(1, *got["ssd"])
    h, s1, got = _odd_fwd("l1", h, p1, _Rider([shard("ev_w_in", 1), shard("ev_w_out", 1)], True))
    p2 = dict(even_params(2, got[0]), **even_out_params(got[1]))
    h, s2, p2, got = _even_fwd("l2", h, p2, {"ssd": _Rider([shard("od_w_in", 1), shard("od_w_out", 1)], True)})
    p3 = odd_params(3, *got["ssd"])
    h, s3, _ = _odd_fwd("l3", h, p3)

    dh, d_final, loss_part = _loss_head("loss_head", h, tgt, w["final_norm_w"][None], ROW_TM // 2)
    loss = lax.psum(loss_part[0, 0], ("x", "y", "c"))

    def last_grads(g):
        bounds = [0]
        for rows_here in TAIL_ROWS:
            bounds.append(bounds[-1] + rows_here)
        assert bounds[-1] == D_MODEL
        return [_Rider([by_owner_cols(g["w_in"], EVEN_SEGS, r0, r1)], False) for r0, r1 in zip(bounds, bounds[1:])]

    half = w["ev_w_out"].shape[1] // 2

    dh, g3, _ = _odd_bwd("l3", dh, s3, p3)
    dh, g2, got = _even_bwd("l2", dh, s2, p2, {"ssd_b": lambda g: _Rider(odd_grads(g3), False)})
    from3 = got["ssd_b"]
    dh, g1, from2 = _odd_bwd("l1", dh, s1, p1, _Rider([even_in_grads(g2), even_out_grads(g2)], False))
    dh, g0, got = _even_bwd("l0", dh, s0, p0, {
        "sgu_b": lambda g: _Rider([even_out_grads(g)[:, :half]], False),
        "ssd_b": lambda g: _Rider(odd_grads(g1), False),
        "conv_b": lambda g: _Rider([even_out_grads(g)[:, half:]], False),
        "tail": last_grads})
    from1 = got["ssd_b"] + [jnp.concatenate([got["sgu_b"][0], got["conv_b"][0]], axis=1)]
    from0 = jnp.concatenate(got["tail"], axis=1)
    grad_x = dh[None]
    lg = [g0, g1, g2, g3]
    ev, od = [g0, g2], [g1, g3]

    def total(tag, parts):
        return _sum_parts("sum_" + tag, parts, 256)

    big_grads = {
        "ev_w_in": total("ev_w_in", [from0, from2[0]]),
        "ev_w_out": total("ev_w_out", [from1[2], from2[1]]),
        "od_w_in": total("od_w_in", [from1[0], from3[0]]),
        "od_w_out": total("od_w_out", [from1[1], from3[1]]),
    }

    small_g = {
        "norm_w": jnp.concatenate([lg[l]["norm_w"] for l in range(depth)], axis=0),
        "final_norm_w": d_final[0],
        "ev_conv_b": jnp.concatenate([e["conv_b"] for e in ev], axis=0),
        "ev_dt_bias": jnp.concatenate([e["dt_bias"][:, :SSD_HEADS] for e in ev], axis=0),
        "ev_a_log": jnp.concatenate([e["a_log"][:, :SSD_HEADS] for e in ev], axis=0),
        "ev_d_skip": jnp.concatenate([e["d_skip"][:, :SSD_HEADS] for e in ev], axis=0),
        "ev_ssd_norm_w": jnp.concatenate([e["ssd_norm_w"] for e in ev], axis=0),
        "ev_sgu_ln_w": jnp.concatenate([e["sgu_ln_w"] for e in ev], axis=0),
        "ev_sgu_ln_b": jnp.concatenate([e["sgu_ln_b"] for e in ev], axis=0),
        "ev_sgu_w": jnp.stack([e["sgu_w"] for e in ev]),
        "ev_sgu_b": jnp.stack([e["sgu_b"][:, :, 0] for e in ev]),
    }
    conv_g = jnp.stack([e["conv_w"] for e in ev])
    small_shapes = [w[n].shape for n in SMALL]
    small_parts, = _exchange("gather_small", [_pack([small_g[n] for n in SMALL] + [conv_g], F32)], gather=True)
    small_sum = _sum_parts("sum_small", [small_parts], 1024)[0]
    *small_list, conv_full = _unpack(small_sum, small_shapes + [conv_g.shape])
    grads = dict(zip(SMALL, small_list))
    grads.update(big_grads)
    me = 4 * lax.axis_index("x") + 2 * lax.axis_index("y") + lax.axis_index("c")
    n_cv = w["ev_conv_w"].shape[2]
    grads["ev_conv_w"] = lax.dynamic_slice_in_dim(conv_full, me * n_cv, n_cv, axis=2)

    deltas, new_m, new_v = {}, {}, {}
    for n in BIG + ("ev_conv_w",):
        deltas[n], new_m[n], new_v[n] = _adamw("adamw_" + n, w[n], grads[n], m[n], v[n], 256)
    packs = [_pack([src[n] for n in SMALL], F32)[None] for src in (w, grads, m, v)]
    outs = _adamw("adamw_small", *packs, 1024)
    for dst, packed in zip((deltas, new_m, new_v), outs):
        dst.update(zip(SMALL, _unpack(packed[0], small_shapes)))
    return loss, grad_x, grads, deltas, new_m, new_v


def kernel(x, norm_w, final_norm_w, ev_w_in, ev_conv_w, ev_conv_b, ev_dt_bias, ev_a_log, ev_d_skip, ev_ssd_norm_w, ev_sgu_ln_w, ev_sgu_ln_b, ev_sgu_w, ev_sgu_b, ev_w_out, od_w_in, od_w_out, loss_target, m_norm_w, m_final_norm_w, m_ev_w_in, m_ev_conv_w, m_ev_conv_b, m_ev_dt_bias, m_ev_a_log, m_ev_d_skip, m_ev_ssd_norm_w, m_ev_sgu_ln_w, m_ev_sgu_ln_b, m_ev_sgu_w, m_ev_sgu_b, m_ev_w_out, m_od_w_in, m_od_w_out, v_norm_w, v_final_norm_w, v_ev_w_in, v_ev_conv_w, v_ev_conv_b, v_ev_dt_bias, v_ev_a_log, v_ev_d_skip, v_ev_ssd_norm_w, v_ev_sgu_ln_w, v_ev_sgu_ln_b, v_ev_sgu_w, v_ev_sgu_b, v_ev_w_out, v_od_w_in, v_od_w_out):
    w = dict(zip(WEIGHTS, (norm_w, final_norm_w, ev_w_in, ev_conv_w, ev_conv_b, ev_dt_bias, ev_a_log, ev_d_skip,
                           ev_ssd_norm_w, ev_sgu_ln_w, ev_sgu_ln_b, ev_sgu_w, ev_sgu_b, ev_w_out, od_w_in, od_w_out)))
    m = dict(zip(WEIGHTS, (m_norm_w, m_final_norm_w, m_ev_w_in, m_ev_conv_w, m_ev_conv_b, m_ev_dt_bias, m_ev_a_log,
                           m_ev_d_skip, m_ev_ssd_norm_w, m_ev_sgu_ln_w, m_ev_sgu_ln_b, m_ev_sgu_w, m_ev_sgu_b,
                           m_ev_w_out, m_od_w_in, m_od_w_out)))
    v = dict(zip(WEIGHTS, (v_norm_w, v_final_norm_w, v_ev_w_in, v_ev_conv_w, v_ev_conv_b, v_ev_dt_bias, v_ev_a_log,
                           v_ev_d_skip, v_ev_ssd_norm_w, v_ev_sgu_ln_w, v_ev_sgu_ln_b, v_ev_sgu_w, v_ev_sgu_b,
                           v_ev_w_out, v_od_w_in, v_od_w_out)))
    loss, grad_x, grads, deltas, new_m, new_v = _step(w, m, v, x, loss_target)
    return (loss, grad_x, *[grads[n] for n in WEIGHTS], *[deltas[n] for n in WEIGHTS],
            *[new_m[n] for n in WEIGHTS], *[new_v[n] for n in WEIGHTS])
```

```python
import jax
import jax.numpy as jnp
from jax import lax
from jax.experimental import pallas as pl
from jax.experimental.pallas import tpu as pltpu

F32, BF16 = jnp.float32, jnp.bfloat16

D_MODEL = 2048
SSD_HEADS = 32
SSD_HEAD_DIM = 64
SSD_GROUPS = 4
SSD_STATE = 128
CHUNK = 128
CONV_WIDTH = 4
CONV_DIM = D_MODEL + 2 * SSD_GROUPS * SSD_STATE
SGU_GROUPS = 16
SB_HEADS = 16
LANES = 128
N_PAIRS = SSD_HEADS // 2
PAIRS_PER_GROUP = N_PAIRS // SSD_GROUPS
NORM_EPS = 1e-5
N_DEV = 8

ADAM_LR, ADAM_B1, ADAM_B2, ADAM_EPS, ADAM_WD, ADAM_STEP = 0.001, 0.9, 0.999, 1e-08, 0.01, 10

VMEM_LIMIT_BYTES = 48 * 1024 * 1024

NN = ((1,), (0,))
NT = ((1,), (1,))
TN = ((0,), (0,))


def _cparams(*sem):
    return pltpu.CompilerParams(dimension_semantics=sem, vmem_limit_bytes=VMEM_LIMIT_BYTES)


def _dg(a, b, dims):
    return lax.dot_general(a, b, (dims, ((), ())), preferred_element_type=F32)


def _make_bdot(dims):
    @jax.custom_vjp
    def f(a, b):
        return _dg(a.astype(BF16), b.astype(BF16), dims)

    def fwd(a, b):
        return f(a, b), (a, b)

    def bwd(res, g):
        a, b = res
        a16, b16, g16 = a.astype(BF16), b.astype(BF16), g.astype(BF16)
        if dims == NN:
            da, db = _dg(g16, b16, NT), _dg(a16, g16, TN)
        elif dims == NT:
            da, db = _dg(g16, b16, NN), _dg(g16, a16, TN)
        else:
            da, db = _dg(b16, g16, NT), _dg(a16, g16, NN)
        return da.astype(a.dtype), db.astype(b.dtype)

    f.defvjp(fwd, bwd)
    return f


_bdot_nn, _bdot_nt, _bdot_tn = _make_bdot(NN), _make_bdot(NT), _make_bdot(TN)


def _iota2(shape, axis):
    return lax.broadcasted_iota(jnp.int32, shape, axis)


def _split3(x, axis):
    hi = x.astype(BF16)
    r = x - hi.astype(F32)
    mid = r.astype(BF16)
    return jnp.concatenate([hi, mid, (r - mid.astype(F32)).astype(BF16)], axis=axis)


def _make_onehot_dot(build, build_t, left):
    def apply(x, e):
        e = e.astype(BF16)
        if left:
            return _dg(jnp.concatenate([e, e, e], axis=1), _split3(x, 0), NN)
        return _dg(_split3(x, 1), jnp.concatenate([e, e, e], axis=0), NN)

    @jax.custom_vjp
    def f(x):
        return apply(x, build())

    f.defvjp(lambda x: (f(x), None), lambda _, g: (apply(g, build_t()),))
    return f


_cumsum_rows = _make_onehot_dot(lambda: _iota2((CHUNK, CHUNK), 1) <= _iota2((CHUNK, CHUNK), 0),
                                lambda: _iota2((CHUNK, CHUNK), 0) <= _iota2((CHUNK, CHUNK), 1), True)


def _softplus(x):
    return jnp.maximum(x, 0.0) + jnp.log1p(jnp.exp(-jnp.abs(x)))


def _silu(x):
    return x * jax.nn.sigmoid(x)


def _gelu(x):
    return 0.5 * x * (1.0 + jnp.tanh(0.7978845608028654 * (x + 0.044715 * (x * x * x))))


def _rms(x, w):
    return x * lax.rsqrt(jnp.mean(x * x, axis=-1, keepdims=True) + NORM_EPS) * w


def _layer_norm(x, w, b):
    xc = x - jnp.mean(x, axis=-1, keepdims=True)
    return xc * lax.rsqrt(jnp.mean(xc * xc, axis=-1, keepdims=True) + NORM_EPS) * w + b


def _row_spec(width, tm):
    return pl.BlockSpec((tm, width), lambda i: (i, 0))


def _full_spec(p):
    zeros = (0,) * p.ndim
    return pl.BlockSpec(p.shape, lambda i: zeros)


def _rows_fwd(name, fn, tiled, params, outs, tm):
    n_rows = tiled[0].shape[0]
    n_in = len(tiled) + len(params)

    def body(*refs):
        res = fn(*[r[...] for r in refs[:n_in]])
        for o_ref, o in zip(refs[n_in:], res):
            o_ref[...] = o.astype(o_ref.dtype)

    return pl.pallas_call(
        body, name=name, grid=(n_rows // tm,),
        in_specs=[_row_spec(a.shape[1], tm) for a in tiled] + [_full_spec(p) for p in params],
        out_specs=[_row_spec(w, tm) for w, _ in outs],
        out_shape=[jax.ShapeDtypeStruct((n_rows, w), d) for w, d in outs],
        compiler_params=_cparams("parallel"),
    )(*tiled, *params)


def _rows_bwd(name, fn, tiled, params, cots, grad_dtypes, tm, add=None):
    n_rows = tiled[0].shape[0]
    nt, npar, nc = len(tiled), len(params), len(cots)
    want = [k for k, d in enumerate(grad_dtypes) if d is not None]
    n_add = 0 if add is None else 1

    def body(*refs):
        ins = [r[...] for r in refs[:nt + npar]]
        c_refs = refs[nt + npar:nt + npar + nc]
        add_refs = refs[nt + npar + nc:nt + npar + nc + n_add]
        o_refs = refs[nt + npar + nc + n_add:]
        res, vjp = jax.vjp(fn, *ins)
        grads = vjp(tuple(c[...].astype(r.dtype) for c, r in zip(c_refs, res)))
        for pos, k in enumerate(want):
            gk = grads[k]
            if add is not None and add[0] == pos:
                gk = gk + add_refs[0][...]
            o_refs[pos][...] = gk.astype(o_refs[pos].dtype)
        p_refs = o_refs[len(want):]

        @pl.when(pl.program_id(0) == 0)
        def _():
            for r in p_refs:
                r[...] = jnp.zeros_like(r)

        for r, gp in zip(p_refs, grads[nt:]):
            r[...] += gp

    add_arrays = [] if add is None else [add[1]]
    out = pl.pallas_call(
        body, name=name, grid=(n_rows // tm,),
        in_specs=([_row_spec(a.shape[1], tm) for a in tiled] + [_full_spec(p) for p in params]
                  + [_row_spec(c.shape[1], tm) for c in cots] + [_row_spec(a.shape[1], tm) for a in add_arrays]),
        out_specs=([_row_spec(tiled[k].shape[1], tm) for k in want] + [_full_spec(p) for p in params]),
        out_shape=([jax.ShapeDtypeStruct(tiled[k].shape, grad_dtypes[k]) for k in want]
                   + [jax.ShapeDtypeStruct(p.shape, F32) for p in params]),
        compiler_params=_cparams("arbitrary"),
    )(*tiled, *params, *cots, *add_arrays)
    return out


def _matmul(name, a, b, dims, out_dtype, add=None, rider=None, tm=1024, tn=1024, tk=2048):
    if dims == NN:
        (m, k), n = a.shape, b.shape[1]
    elif dims == NT:
        (m, k), n = a.shape, b.shape[0]
    else:
        (k, m), n = a.shape, b.shape[1]
    tm, tn, tk = min(tm, m), min(tn, n), min(tk, k)
    while k % tk:
        tk -= LANES
    assert m % tm == 0 and n % tn == 0 and k % tk == 0, (name, m, n, k)
    nk = k // tk
    a_spec = (pl.BlockSpec((tk, tm), lambda i, j, kk: (kk, i)) if dims == TN
              else pl.BlockSpec((tm, tk), lambda i, j, kk: (i, kk)))
    b_spec = (pl.BlockSpec((tn, tk), lambda i, j, kk: (j, kk)) if dims == NT
              else pl.BlockSpec((tk, tn), lambda i, j, kk: (kk, j)))
    o_spec = pl.BlockSpec((tm, tn), lambda i, j, kk: (i, j))
    has_add = add is not None

    def body(*refs):
        a_ref, b_ref = refs[0], refs[1]
        part = _dg(a_ref[...].astype(BF16), b_ref[...].astype(BF16), dims)
        if nk == 1:
            o_ref = refs[-1]
            if has_add:
                part = part + refs[2][...]
            o_ref[...] = part.astype(o_ref.dtype)
            return
        o_ref, acc = refs[-2], refs[-1]
        kk = pl.program_id(2)

        @pl.when(kk == 0)
        def _():
            acc[...] = part

        @pl.when(kk > 0)
        def _():
            acc[...] += part

        @pl.when(kk == nk - 1)
        def _():
            r = acc[...]
            if has_add:
                r = r + refs[2][...]
            o_ref[...] = r.astype(o_ref.dtype)

    in_specs = [a_spec, b_spec] + ([o_spec] if has_add else [])
    scratch = [pltpu.VMEM((tm, tn), F32)] if nk > 1 else []
    args = (a, b) + ((add,) if has_add else ())
    if rider is not None:
        (out,), got = _hosted(body, rider, name=name, grid=(m // tm, n // tn, nk), in_specs=in_specs,
                              out_specs=[o_spec], out_shape=[jax.ShapeDtypeStruct((m, n), out_dtype)],
                              scratch_shapes=scratch, args=args)
        return out, got
    return pl.pallas_call(
        body, name=name, grid=(m // tm, n // tn, nk), in_specs=in_specs, out_specs=o_spec,
        out_shape=jax.ShapeDtypeStruct((m, n), out_dtype), scratch_shapes=scratch,
        compiler_params=_cparams("parallel", "parallel", "arbitrary"),
    )(*args)


CONV_TM = 256
HALO = 8


def _shift_down(x, halo, j):
    if j == 0:
        return x, x[:HALO]
    xr = pltpu.roll(x, j, 0)
    hr = pltpu.roll(halo, j, 0)
    top = jnp.where(_iota2((HALO, x.shape[1]), 0) < j, hr, xr[:HALO])
    return xr, top


def _conv_fwd(name, x, w, b, rider=None):
    t, c = x.shape
    tm = min(CONV_TM, t)
    hb = tm // HALO

    def body(x_ref, halo_ref, w_ref, b_ref, pre_ref, xs_ref, bm_ref, cm_ref):
        i = pl.program_id(0)
        xv = x_ref[...]
        halo = jnp.where(i > 0, halo_ref[...], 0.0)
        main = jnp.zeros_like(xv) + b_ref[...]
        top = jnp.zeros((HALO, c), F32) + b_ref[...]
        for kk in range(CONV_WIDTH):
            xr, tp = _shift_down(xv, halo, CONV_WIDTH - 1 - kk)
            main = main + w_ref[kk:kk + 1, :] * xr
            top = top + w_ref[kk:kk + 1, :] * tp
        pre = jnp.concatenate([top, main[HALO:]], axis=0)
        pre_ref[...] = pre
        act = _silu(pre)
        xs_ref[...] = act[:, :D_MODEL]
        bm_ref[...] = act[:, D_MODEL:D_MODEL + SSD_GROUPS * SSD_STATE]
        cm_ref[...] = act[:, D_MODEL + SSD_GROUPS * SSD_STATE:]

    gs = SSD_GROUPS * SSD_STATE
    return _hosted(
        body, rider, name=name, grid=(t // tm,),
        in_specs=[_row_spec(c, tm),
                  pl.BlockSpec((HALO, c), lambda i: (jnp.maximum(i * hb - 1, 0), 0)),
                  _full_spec(w), _full_spec(b)],
        out_specs=[_row_spec(c, tm), _row_spec(D_MODEL, tm), _row_spec(gs, tm), _row_spec(gs, tm)],
        out_shape=[jax.ShapeDtypeStruct((t, c), F32), jax.ShapeDtypeStruct((t, D_MODEL), F32),
                   jax.ShapeDtypeStruct((t, gs), F32), jax.ShapeDtypeStruct((t, gs), F32)],
        scratch_shapes=[], args=(x, x, w, b))


def _dsilu(pre, dact):
    s = jax.nn.sigmoid(pre)
    return dact * (s * (1.0 + pre * (1.0 - s)))


def _conv_bwd(name, x, pre, w, dxs, dbm, dcm, rider=None):
    t, c = x.shape
    tm = min(CONV_TM, t)
    hb = tm // HALO
    last_hb = t // HALO - 1
    n_tiles = t // tm

    def body(x_ref, xh_ref, pre_ref, preh_ref, w_ref, dxs_ref, dbm_ref, dcm_ref,
             dxsh_ref, dbmh_ref, dcmh_ref, dx_ref, dw_ref, db_ref):
        i = pl.program_id(0)
        dact = jnp.concatenate([dxs_ref[...], dbm_ref[...], dcm_ref[...]], axis=1)
        dpre = _dsilu(pre_ref[...], dact)
        dact_h = jnp.concatenate([dxsh_ref[...], dbmh_ref[...], dcmh_ref[...]], axis=1)
        dpre_h = jnp.where(i < n_tiles - 1, _dsilu(preh_ref[...], dact_h), 0.0)
        xv = x_ref[...]
        xh = jnp.where(i > 0, xh_ref[...], 0.0)

        @pl.when(i == 0)
        def _():
            dw_ref[...] = jnp.zeros_like(dw_ref)
            db_ref[...] = jnp.zeros_like(db_ref)

        db_ref[...] += jnp.sum(dpre, axis=0, keepdims=True)
        dxm = jnp.zeros_like(xv)
        dxt = jnp.zeros((HALO, c), F32)
        row8 = _iota2((HALO, c), 0)
        for kk in range(CONV_WIDTH):
            j = CONV_WIDTH - 1 - kk
            wk = w_ref[kk:kk + 1, :]
            xr, tp = _shift_down(xv, xh, j)
            full = jnp.sum(dpre * xr, axis=0, keepdims=True)
            fix = jnp.sum(dpre[:HALO] * (tp - xr[:HALO]), axis=0, keepdims=True)
            dw_ref[kk:kk + 1, :] += full + fix
            if j == 0:
                dxm = dxm + wk * dpre
                dxt = dxt + wk * dpre[tm - HALO:]
            else:
                dr = pltpu.roll(dpre, tm - j, 0)
                hr = pltpu.roll(dpre_h, HALO - j, 0)
                dxm = dxm + wk * dr
                dxt = dxt + wk * jnp.where(row8 >= HALO - j, hr, dr[tm - HALO:])
        dx_ref[...] = jnp.concatenate([dxm[:tm - HALO], dxt], axis=0).astype(dx_ref.dtype)

    gs = SSD_GROUPS * SSD_STATE
    prev_halo = lambda i: (jnp.maximum(i * hb - 1, 0), 0)
    next_halo = lambda i: (jnp.minimum((i + 1) * hb, last_hb), 0)
    return _hosted(
        body, rider, name=name, grid=(n_tiles,),
        in_specs=[_row_spec(c, tm), pl.BlockSpec((HALO, c), prev_halo),
                  _row_spec(c, tm), pl.BlockSpec((HALO, c), next_halo), _full_spec(w),
                  _row_spec(D_MODEL, tm), _row_spec(gs, tm), _row_spec(gs, tm),
                  pl.BlockSpec((HALO, D_MODEL), next_halo), pl.BlockSpec((HALO, gs), next_halo),
                  pl.BlockSpec((HALO, gs), next_halo)],
        out_specs=[_row_spec(c, tm), _full_spec(w), pl.BlockSpec((1, c), lambda i: (0, 0))],
        out_shape=[jax.ShapeDtypeStruct((t, c), BF16), jax.ShapeDtypeStruct(w.shape, F32),
                   jax.ShapeDtypeStruct((1, c), F32)],
        scratch_shapes=[], args=(x, x, pre, pre, w, dxs, dbm, dcm, dxs, dbm, dcm))


def _ssd_prep(dt_raw, dt_bias, a_log):
    dt = _softplus(dt_raw + dt_bias)
    return dt, _cumsum_rows(dt * (-jnp.exp(a_log)))


def _head_col(x, h):
    return jnp.sum(jnp.where(_iota2(x.shape, 1) == h, x, 0.0), axis=1, keepdims=True)


def _ssd_pair(p, xs, dt, cs, d_skip, bg, cg, prev):
    l = xs.shape[0]
    first = _iota2((l, LANES), 1) < SSD_HEAD_DIM
    c0, c1 = _head_col(cs, 2 * p), _head_col(cs, 2 * p + 1)
    csf = jnp.where(first, c0, c1)
    dtf = jnp.where(first, _head_col(dt, 2 * p), _head_col(dt, 2 * p + 1))
    dskf = jnp.where(_iota2((1, LANES), 1) < SSD_HEAD_DIM, _head_col(d_skip, 2 * p), _head_col(d_skip, 2 * p + 1))
    cs_last = jnp.sum(jnp.where(_iota2((l, LANES), 0) == l - 1, csf, 0.0), axis=0, keepdims=True)
    xc = xs * dtf
    scores = _bdot_nt(cg, bg)
    causal = _iota2((l, l), 0) >= _iota2((l, l), 1)

    def decay(col):
        a = jnp.broadcast_to(col, (l, l))
        return jnp.where(causal, jnp.exp(jnp.where(causal, a - a.T, 0.0)), 0.0)

    y_diag = (_bdot_nn(scores * decay(c0), jnp.where(first, xc, 0.0))
              + _bdot_nn(scores * decay(c1), jnp.where(first, 0.0, xc)))
    states = _bdot_tn(bg, xc * jnp.exp(cs_last - csf))
    new_state = jnp.exp(cs_last) * prev + states
    y_off = _bdot_nn(cg, prev) * jnp.exp(csf)
    return y_diag + y_off + xs * dskf, new_state


def _ssd_specs(nc_rev=None):
    def ch(c):
        return c if nc_rev is None else nc_rev - 1 - c

    gs = SSD_GROUPS * SSD_STATE
    return dict(
        wide=pl.BlockSpec((CHUNK, D_MODEL), lambda c: (ch(c), 0)),
        group=pl.BlockSpec((CHUNK, gs), lambda c: (ch(c), 0)),
        chunk=pl.BlockSpec((CHUNK, LANES), lambda c: (ch(c), 0)),
        vec=pl.BlockSpec((1, LANES), lambda c: (0, 0)),
        state=pl.BlockSpec((1, N_PAIRS, LANES, LANES), lambda c: (ch(c), 0, 0, 0)),
    )


def _pair_slices(p):
    g = p // PAIRS_PER_GROUP
    return slice(p * LANES, (p + 1) * LANES), slice(g * LANES, (g + 1) * LANES)


def _ssd_fwd(name, xs, bm, cm, dt_raw, dt_bias, a_log, d_skip, rider=None):
    t = xs.shape[0]
    nc = t // CHUNK

    def body(xs_ref, b_ref, c_ref, dt_ref, bias_ref, alog_ref, dsk_ref, y_ref, prev_ref, state, dt_s, cs_s):
        c = pl.program_id(0)
        dt_s[...], cs_s[...] = _ssd_prep(dt_ref[...], bias_ref[...], alog_ref[...])

        @pl.when(c == 0)
        def _():
            state[...] = jnp.zeros_like(state)

        for p in range(N_PAIRS):
            sl, gsl = _pair_slices(p)
            prev = state[p]
            prev_ref[0, p] = prev
            y, new_state = _ssd_pair(p, xs_ref[:, sl], dt_s[...], cs_s[...], dsk_ref[...],
                                     b_ref[:, gsl], c_ref[:, gsl], prev)
            y_ref[:, sl] = y
            state[p] = new_state

    sp = _ssd_specs()
    return _hosted(
        body, rider, name=name, grid=(nc,),
        in_specs=[sp["wide"], sp["group"], sp["group"], sp["chunk"], sp["vec"], sp["vec"], sp["vec"]],
        out_specs=[sp["wide"], sp["state"]],
        out_shape=[jax.ShapeDtypeStruct((t, D_MODEL), F32),
                   jax.ShapeDtypeStruct((nc, N_PAIRS, LANES, LANES), F32)],
        scratch_shapes=[pltpu.VMEM((N_PAIRS, LANES, LANES), F32), pltpu.VMEM((CHUNK, LANES), F32),
                        pltpu.VMEM((CHUNK, LANES), F32)],
        args=(xs, bm, cm, dt_raw, dt_bias, a_log, d_skip))


def _ssd_bwd(name, xs, bm, cm, dt_raw, dt_bias, a_log, d_skip, prev_states, dy, rider=None):
    t = xs.shape[0]
    nc = t // CHUNK

    def body(xs_ref, b_ref, c_ref, dt_ref, bias_ref, alog_ref, dsk_ref, prev_ref, dy_ref,
             dxs_ref, db_ref, dc_ref, ddt_ref, dbias_ref, dalog_ref, ddsk_ref, dstate, dt_s, cs_s, g_dt, g_cs):
        c = pl.program_id(0)
        (dt_s[...], cs_s[...]), vjp_prep = jax.vjp(_ssd_prep, dt_ref[...], bias_ref[...], alog_ref[...])

        @pl.when(c == 0)
        def _():
            dstate[...] = jnp.zeros_like(dstate)
            dbias_ref[...] = jnp.zeros_like(dbias_ref)
            dalog_ref[...] = jnp.zeros_like(dalog_ref)
            ddsk_ref[...] = jnp.zeros_like(ddsk_ref)

        g_dt[...] = jnp.zeros_like(g_dt)
        g_cs[...] = jnp.zeros_like(g_cs)
        for p in range(N_PAIRS):
            sl, gsl = _pair_slices(p)
            pair = lambda *a, p=p: _ssd_pair(p, *a)
            _, vjp = jax.vjp(pair, xs_ref[:, sl], dt_s[...], cs_s[...], dsk_ref[...],
                             b_ref[:, gsl], c_ref[:, gsl], prev_ref[0, p])
            dxs, ddt, dcs, ddsk, dbg, dcg, dprev = vjp((dy_ref[:, sl], dstate[p]))
            dxs_ref[:, sl] = dxs
            g_dt[...] += ddt
            g_cs[...] += dcs
            ddsk_ref[...] += ddsk
            dstate[p] = dprev
            if p % PAIRS_PER_GROUP == 0:
                db_ref[:, gsl] = dbg
                dc_ref[:, gsl] = dcg
            else:
                db_ref[:, gsl] += dbg
                dc_ref[:, gsl] += dcg

        ddt_raw, dbias, dalog = vjp_prep((g_dt[...], g_cs[...]))
        ddt_ref[...] = ddt_raw
        dbias_ref[...] += dbias
        dalog_ref[...] += dalog

    sp = _ssd_specs(nc)
    gs = SSD_GROUPS * SSD_STATE
    return _hosted(
        body, rider, name=name, grid=(nc,),
        in_specs=[sp["wide"], sp["group"], sp["group"], sp["chunk"], sp["vec"], sp["vec"], sp["vec"],
                  sp["state"], sp["wide"]],
        out_specs=[sp["wide"], sp["group"], sp["group"], sp["chunk"], sp["vec"], sp["vec"], sp["vec"]],
        out_shape=[jax.ShapeDtypeStruct((t, D_MODEL), F32), jax.ShapeDtypeStruct((t, gs), F32),
                   jax.ShapeDtypeStruct((t, gs), F32), jax.ShapeDtypeStruct((t, LANES), F32),
                   jax.ShapeDtypeStruct((1, LANES), F32), jax.ShapeDtypeStruct((1, LANES), F32),
                   jax.ShapeDtypeStruct((1, LANES), F32)],
        scratch_shapes=[pltpu.VMEM((N_PAIRS, LANES, LANES), F32)] + [pltpu.VMEM((CHUNK, LANES), F32)] * 4,
        args=(xs, bm, cm, dt_raw, dt_bias, a_log, d_skip, prev_states, dy))


def _ssd_gate(y, z, w):
    return (_rms(y * _silu(z), w),)


def _sgu_norm(v, w, b):
    return (_layer_norm(_gelu(v), w, b),)


def _sgu_group(u, gate, vn, w, bcol):
    l = u.shape[0]
    wc = jnp.where(_iota2((l, l), 0) >= _iota2((l, l), 1), w, 0.0)
    return _gelu(u) * (_bdot_nn(wc, vn) + bcol) * _silu(gate)


def _sgu_fwd(name, u, gate, vn, w, bcol):
    t = u.shape[0]
    blk = _row_spec(D_MODEL, CHUNK)

    def body(u_ref, g_ref, vn_ref, w_ref, b_ref, y_ref):
        for g in range(SGU_GROUPS):
            sl = slice(g * LANES, (g + 1) * LANES)
            y_ref[:, sl] = _sgu_group(u_ref[:, sl], g_ref[:, sl], vn_ref[:, sl], w_ref[g], b_ref[g]).astype(y_ref.dtype)

    return pl.pallas_call(
        body, name=name, grid=(t // CHUNK,),
        in_specs=[blk, blk, blk, _full_spec(w), _full_spec(bcol)], out_specs=blk,
        out_shape=jax.ShapeDtypeStruct((t, D_MODEL), BF16),
        compiler_params=_cparams("parallel"),
    )(u, gate, vn, w, bcol)


def _sgu_bwd(name, u, gate, vn, w, bcol, dy, rider=None):
    t = u.shape[0]
    blk = _row_spec(D_MODEL, CHUNK)

    def body(u_ref, g_ref, vn_ref, w_ref, b_ref, dy_ref, du_ref, dg_ref, dvn_ref, dw_ref, db_ref):
        @pl.when(pl.program_id(0) == 0)
        def _():
            dw_ref[...] = jnp.zeros_like(dw_ref)
            db_ref[...] = jnp.zeros_like(db_ref)

        for g in range(SGU_GROUPS):
            sl = slice(g * LANES, (g + 1) * LANES)
            _, vjp = jax.vjp(_sgu_group, u_ref[:, sl], g_ref[:, sl], vn_ref[:, sl], w_ref[g], b_ref[g])
            du, dg, dvn, dw, db = vjp(dy_ref[:, sl])
            du_ref[:, sl] = du.astype(du_ref.dtype)
            dg_ref[:, sl] = dg.astype(dg_ref.dtype)
            dvn_ref[:, sl] = dvn
            dw_ref[g] += dw
            db_ref[g] += db

    return _hosted(
        body, rider, name=name, grid=(t // CHUNK,),
        in_specs=[blk, blk, blk, _full_spec(w), _full_spec(bcol), blk],
        out_specs=[blk, blk, blk, _full_spec(w), _full_spec(bcol)],
        out_shape=[jax.ShapeDtypeStruct((t, D_MODEL), BF16), jax.ShapeDtypeStruct((t, D_MODEL), BF16),
                   jax.ShapeDtypeStruct((t, D_MODEL), F32), jax.ShapeDtypeStruct(w.shape, F32),
                   jax.ShapeDtypeStruct(bcol.shape, F32)],
        scratch_shapes=[], args=(u, gate, vn, w, bcol, dy))


SB_SCALE = LANES ** -0.5
LOG2_E = 1.4426950408889634
SB_TQ = 512
SB_TS = 512


def _keep(mask, x):
    return x if mask is None else jnp.where(mask, x, 0.0)


def _sb_pieces(z2, mask):
    tl = jnp.log(1.0 + jnp.exp2(-jnp.abs(z2))) * LOG2_E
    lk = _keep(mask, -(jnp.maximum(z2, 0.0) + tl))
    ls = jnp.minimum(z2, 0.0) - tl
    return lk, ls


def _split2(x):
    hi = x.astype(BF16)
    return jnp.concatenate([hi, (x - hi.astype(F32)).astype(BF16)], axis=1)


def _tri2(cmp):
    sq = (CHUNK, CHUNK)
    m = cmp(_iota2(sq, 0), _iota2(sq, 1)).astype(BF16)
    return jnp.concatenate([m, m], axis=0)


def _tri_sums(blocks, tri2):
    res = _dg(jnp.concatenate([_split2(b) for b in blocks], axis=0), tri2, NN)
    out, at = [], 0
    for b in blocks:
        out.append(res[at:at + b.shape[0]])
        at += b.shape[0]
    return out


def _pad_top(x, start):
    return x if start == 0 else jnp.concatenate([jnp.zeros((start, x.shape[1]), x.dtype), x], axis=0)


def _rows_from(x, start):
    return x if start == 0 else x[start:]


def _sb_tiles(t):
    tq, ts = min(SB_TQ, t), min(SB_TS, t)
    assert t % tq == 0 and t % ts == 0 and ts % tq == 0 and ts % CHUNK == 0
    return tq, ts, ts // CHUNK


def _sb_logits(qb, ks, off, q_off, masked, trim=False):
    tq, ts = qb.shape[0], ks.shape[0]
    z2 = _dg(qb, ks, NT) * (SB_SCALE * LOG2_E)
    out = []
    for b in range(ts // CHUNK):
        mask, start = None, 0
        if masked:
            start = b * CHUNK if trim and tq == ts else 0
            mask = ((_iota2((tq - start, CHUNK), 1) + (off + b * CHUNK))
                    < (_iota2((tq - start, CHUNK), 0) + (q_off + start)))
        out.append(_sb_pieces(_rows_from(z2, start)[:, b * CHUNK:(b + 1) * CHUNK], mask) + (mask, start))
    return out


def _attn_fwd(name, q, k, v, rider=None):
    t = q.shape[0]
    tq, ts, nb = _sb_tiles(t)

    def body(q_ref, k_ref, v_ref, y_ref, tot_ref):
        i = pl.program_id(1)
        qb = q_ref[...]
        later2 = _tri2(lambda r, c: r > c)
        last = (i * tq + tq - 1) // ts

        def span(j, carry, masked):
            acc, after = carry
            off = pl.multiple_of(j * ts, ts)
            pieces = _sb_logits(qb, k_ref[pl.ds(off, ts), :], off, i * tq, masked)
            inside = _tri_sums([pc[0] for pc in pieces], later2)
            ws = [None] * nb
            for b in reversed(range(nb)):
                lk, ls, mask, start = pieces[b]
                ws[b] = _pad_top(_keep(mask, jnp.exp2(ls + inside[b] + _rows_from(after, start))).astype(BF16), start)
                after = after + _pad_top(jnp.sum(lk, axis=1, keepdims=True), start)
            acc = acc + _dg(jnp.concatenate(ws, axis=1), v_ref[pl.ds(off, ts), :], NN)
            return acc, after

        carry = span(last, (jnp.zeros((tq, LANES), F32), jnp.zeros((tq, 1), F32)), True)
        acc, tot = lax.fori_loop(0, last, lambda n, c: span(last - 1 - n, c, False), carry)
        y_ref[...] = acc
        tot_ref[...] = jnp.broadcast_to(tot, (tq, LANES))

    qsp = pl.BlockSpec((tq, LANES), lambda h, i: (i, h))
    kvsp = pl.BlockSpec((t, LANES), lambda h, i: (0, h))
    return _hosted(
        body, rider, name=name, grid=(SB_HEADS, t // tq),
        in_specs=[qsp, kvsp, kvsp], out_specs=[qsp, qsp],
        out_shape=[jax.ShapeDtypeStruct((t, D_MODEL), F32), jax.ShapeDtypeStruct((t, D_MODEL), F32)],
        scratch_shapes=[], args=(q, k, v))


def _attn_bwd(name, q, k, v, tot, dy, rider=None):
    t = q.shape[0]
    tq, ts, nb = _sb_tiles(t)
    nq, ns = t // tq, t // ts

    def body(q_ref, k_ref, v_ref, tot_ref, dy_ref, dq_ref, dk_ref, dv_ref, dkt_acc, dvt_acc):
        i = pl.program_id(1)

        @pl.when(i == 0)
        def _():
            dkt_acc[...] = jnp.zeros_like(dkt_acc)
            dvt_acc[...] = jnp.zeros_like(dvt_acc)

        qb = q_ref[...]
        dy = dy_ref[...]
        dyb = dy.astype(BF16)
        q_t = qb.astype(F32).T.astype(BF16)
        dy_t = dy.T.astype(BF16)
        totb = tot_ref[...]
        upto2 = _tri2(lambda r, c: r <= c)
        before2 = _tri2(lambda r, c: r < c)
        last = (i * tq + tq - 1) // ts

        def span(j, carry, masked):
            dq, lk_seen, e_seen = carry
            off = pl.multiple_of(j * ts, ts)
            ks = k_ref[pl.ds(off, ts), :]
            pieces = _sb_logits(qb, ks, off, i * tq, masked, trim=True)
            dw = _dg(dyb, v_ref[pl.ds(off, ts), :], NT)
            upto = _tri_sums([pc[0] for pc in pieces], upto2)
            ws, es = [], []
            for b in range(nb):
                lk, ls, mask, start = pieces[b]
                w = _keep(mask, jnp.exp2(ls + ((_rows_from(totb, start) - upto[b]) - _rows_from(lk_seen, start))))
                ws.append(_pad_top(w.astype(BF16), start))
                es.append(_rows_from(dw, start)[:, b * CHUNK:(b + 1) * CHUNK] * w)
                lk_seen = lk_seen + _pad_top(jnp.sum(lk, axis=1, keepdims=True), start)
            before = _tri_sums(es, before2)
            dzs = []
            for b in range(nb):
                _, ls, mask, start = pieces[b]
                sig = jnp.exp2(ls)
                dlk = _rows_from(e_seen, start) + before[b]
                dzs.append(_pad_top((_keep(mask, es[b] - (es[b] + dlk) * sig) * SB_SCALE).astype(BF16), start))
                e_seen = e_seen + _pad_top(jnp.sum(es[b], axis=1, keepdims=True), start)
            dz = jnp.concatenate(dzs, axis=1)
            dq = dq + _dg(dz, ks, NN)
            dkt_acc[j] += _dg(q_t, dz, NN)
            dvt_acc[j] += _dg(dy_t, jnp.concatenate(ws, axis=1), NN)
            return dq, lk_seen, e_seen

        zero_col = jnp.zeros((tq, 1), F32)
        carry = lax.fori_loop(0, last, lambda j, c: span(j, c, False), (jnp.zeros((tq, LANES), F32), zero_col, zero_col))
        dq, _, _ = span(last, carry, True)
        dq_ref[...] = dq.astype(dq_ref.dtype)

        @pl.when(i == nq - 1)
        def _():
            for s in range(ns):
                dk_ref[s * ts:(s + 1) * ts, :] = dkt_acc[s].T.astype(dk_ref.dtype)
                dv_ref[s * ts:(s + 1) * ts, :] = dvt_acc[s].T.astype(dv_ref.dtype)

    qsp = pl.BlockSpec((tq, LANES), lambda h, i: (i, h))
    kvsp = pl.BlockSpec((t, LANES), lambda h, i: (0, h))
    return _hosted(
        body, rider, name=name, grid=(SB_HEADS, nq),
        in_specs=[qsp, kvsp, kvsp, qsp, qsp], out_specs=[qsp, kvsp, kvsp],
        out_shape=[jax.ShapeDtypeStruct((t, D_MODEL), BF16)] * 3,
        scratch_shapes=[pltpu.VMEM((ns, LANES, ts), F32), pltpu.VMEM((ns, LANES, ts), F32)],
        args=(q, k, v, tot, dy))


def _attn_gate(y, g):
    return (y * _silu(g),)


def _loss_head(name, h, target, w, tm):
    t, d = h.shape

    def body(h_ref, t_ref, w_ref, dh_ref, dw_ref, loss_ref):
        tgt = t_ref[...]

        def f(hv, wv):
            e = _rms(hv, wv) - tgt
            return 0.5 * jnp.mean(e * e, axis=-1, keepdims=True)

        row_loss, vjp = jax.vjp(f, h_ref[...], w_ref[...])
        dh, dw = vjp(jnp.ones_like(row_loss))
        dh_ref[...] = dh

        @pl.when(pl.program_id(0) == 0)
        def _():
            dw_ref[...] = jnp.zeros_like(dw_ref)
            loss_ref[...] = jnp.zeros_like(loss_ref)

        dw_ref[...] += dw
        loss_ref[...] += jnp.sum(row_loss, axis=0, keepdims=True)

    return pl.pallas_call(
        body, name=name, grid=(t // tm,),
        in_specs=[_row_spec(d, tm), _row_spec(d, tm), _full_spec(w)],
        out_specs=[_row_spec(d, tm), _full_spec(w), pl.BlockSpec((1, 1), lambda i: (0, 0))],
        out_shape=[jax.ShapeDtypeStruct((t, d), F32), jax.ShapeDtypeStruct(w.shape, F32),
                   jax.ShapeDtypeStruct((1, 1), F32)],
        compiler_params=_cparams("arbitrary"),
    )(h, target, w)


def _pick_tile(rows, cap):
    if rows <= cap:
        return rows
    for tm in range(cap - cap % 16, 0, -16):
        if rows % tm == 0:
            return tm
    raise ValueError(rows)


def _adamw(name, w, g, m, v, tm):
    n_l, r, c = w.shape
    tm = _pick_tile(r, tm)

    def body(w_ref, g_ref, m_ref, v_ref, d_ref, nm_ref, nv_ref):
        g_ = g_ref[...]
        m_ = ADAM_B1 * m_ref[...] + (1.0 - ADAM_B1) * g_
        v_ = ADAM_B2 * v_ref[...] + (1.0 - ADAM_B2) * (g_ * g_)
        m_hat = m_ / (1.0 - ADAM_B1 ** ADAM_STEP)
        v_hat = v_ / (1.0 - ADAM_B2 ** ADAM_STEP)
        d_ref[...] = -ADAM_LR * (m_hat / (jnp.sqrt(v_hat) + ADAM_EPS) + ADAM_WD * w_ref[...])
        nm_ref[...] = m_
        nv_ref[...] = v_

    spec = pl.BlockSpec((1, tm, c), lambda l, i: (l, i, 0))
    return pl.pallas_call(
        body, name=name, grid=(n_l, r // tm), in_specs=[spec] * 4, out_specs=[spec] * 3,
        out_shape=[jax.ShapeDtypeStruct((n_l, r, c), F32)] * 3, compiler_params=_cparams("parallel", "parallel"),
    )(w, g, m, v)


def _sum_parts(name, layers, tm):
    n, r, c = layers[0].shape
    tm = _pick_tile(r, tm)
    nblk = r // tm

    def body(*refs):
        o_ref = refs[-1]
        for li, p_ref in enumerate(refs[:-1]):
            @pl.when(pl.program_id(0) == li)
            def _(p_ref=p_ref):
                s = p_ref[0].astype(F32)
                for d in range(1, n):
                    s = s + p_ref[d].astype(F32)
                o_ref[0] = s

    def spec(li):
        return pl.BlockSpec((n, tm, c), lambda l, i: (0, jnp.where(l == li, i, jnp.where(l < li, 0, nblk - 1)), 0))

    return pl.pallas_call(
        body, name=name, grid=(len(layers), nblk),
        in_specs=[spec(li) for li in range(len(layers))],
        out_specs=pl.BlockSpec((1, tm, c), lambda l, i: (l, i, 0)),
        out_shape=jax.ShapeDtypeStruct((len(layers), r, c), F32), compiler_params=_cparams("arbitrary", "arbitrary"),
    )(*layers)


def _peer(k):
    x, y, c = lax.axis_index("x"), lax.axis_index("y"), lax.axis_index("c")
    px, py, pc = x ^ ((k >> 2) & 1), y ^ ((k >> 1) & 1), c ^ (k & 1)
    return (px, py, pc), 4 * px + 2 * py + pc


def _exchange(name, xs, gather):
    rider = _Rider(xs, gather)
    n = rider.n

    def body(*refs):
        x_refs, out_refs, sems = refs[:n], refs[n:2 * n], refs[2 * n:]
        _exchange_start(x_refs, out_refs, sems, gather)
        _exchange_finish(x_refs, out_refs, sems, gather)

    return pl.pallas_call(
        body, name=name, in_specs=[ANY_SPEC] * n, out_specs=[ANY_SPEC] * n,
        out_shape=rider.out_shape(), scratch_shapes=rider.scratch(),
    )(*xs)


ANY_SPEC = pl.BlockSpec(memory_space=pl.ANY)
SAME_CORE = (2, 4, 6)


class _Rider:
    def __init__(self, xs, gather):
        self.xs, self.gather, self.n = list(xs), gather, len(xs)

    def out_shape(self):
        return [jax.ShapeDtypeStruct((N_DEV,) + tuple(x.shape if self.gather else x.shape[1:]), x.dtype)
                for x in self.xs]

    def scratch(self):
        return [pltpu.SemaphoreType.DMA((self.n, N_DEV - 1)), pltpu.SemaphoreType.DMA((self.n, N_DEV - 1)),
                pltpu.SemaphoreType.DMA((self.n,))]


def _remote(src, dst, sems, a, k, dev):
    return pltpu.make_async_remote_copy(
        src_ref=src, dst_ref=dst, send_sem=sems[0].at[a, k - 1], recv_sem=sems[1].at[a, k - 1],
        device_id=dev, device_id_type=pl.DeviceIdType.MESH)


def _exchange_start(x_refs, out_refs, sems, gather):
    _, me = _peer(0)
    for a, (x, out) in enumerate(zip(x_refs, out_refs)):
        pltpu.make_async_copy(x if gather else x.at[me], out.at[me], sems[2].at[a]).start()
    for k in ((1,) + SAME_CORE if gather else range(1, N_DEV)):
        dev, idx = _peer(k)
        for a, (x, out) in enumerate(zip(x_refs, out_refs)):
            _remote(x if gather else x.at[idx], out.at[me], sems, a, k, dev).start()


def _exchange_finish(x_refs, out_refs, sems, gather):
    _, me = _peer(0)
    sibling, _ = _peer(1)
    pairs = list(enumerate(zip(x_refs, out_refs)))
    waited = ()
    if gather:
        for k in SAME_CORE:
            dev, idx = _peer(k)
            for a, (x, out) in pairs:
                _remote(x, out.at[idx], sems, a, k, dev).wait_recv()
                _remote(out.at[idx], out.at[idx], sems, a, k + 1, sibling).start()
        waited = SAME_CORE
    for k in range(1, N_DEV):
        dev, idx = _peer(k)
        for a, (x, out) in pairs:
            cp = _remote(x if gather else x.at[idx], out.at[idx], sems, a, k, dev)
            if k not in waited:
                cp.wait_recv()
            cp.wait_send()
    for a, (x, out) in pairs:
        pltpu.make_async_copy(x if gather else x.at[me], out.at[me], sems[2].at[a]).wait()


def _hosted(body, rider, *, name, grid, in_specs, out_specs, out_shape, scratch_shapes, args):
    sem = ("arbitrary",) * len(grid)
    if rider is None:
        return pl.pallas_call(body, name=name, grid=grid, in_specs=in_specs, out_specs=out_specs, out_shape=out_shape,
                              scratch_shapes=scratch_shapes, compiler_params=_cparams(*sem))(*args), []
    n_in, n_out, n_scr, nr = len(in_specs), len(out_specs), len(scratch_shapes), rider.n

    def hosted(*refs):
        ins, refs = refs[:n_in], refs[n_in:]
        r_in, refs = refs[:nr], refs[nr:]
        outs, refs = refs[:n_out], refs[n_out:]
        r_out, refs = refs[:nr], refs[nr:]
        scr, sems = refs[:n_scr], refs[n_scr:]
        first = pl.program_id(0) == 0
        last = pl.program_id(0) == grid[0] - 1
        for ax in range(1, len(grid)):
            first = first & (pl.program_id(ax) == 0)
            last = last & (pl.program_id(ax) == grid[ax] - 1)

        @pl.when(first)
        def _():
            _exchange_start(r_in, r_out, sems, rider.gather)

        body(*ins, *outs, *scr)

        @pl.when(last)
        def _():
            _exchange_finish(r_in, r_out, sems, rider.gather)

    res = pl.pallas_call(
        hosted, name=name, grid=grid, in_specs=list(in_specs) + [ANY_SPEC] * nr,
        out_specs=list(out_specs) + [ANY_SPEC] * nr, out_shape=list(out_shape) + rider.out_shape(),
        scratch_shapes=list(scratch_shapes) + rider.scratch(), compiler_params=_cparams(*sem),
    )(*args, *rider.xs)
    return list(res[:n_out]), list(res[n_out:])


PACK_ALIGN = 8 * LANES
PACK_ROWS = 512


def _pack(arrays, dtype):
    pieces, total = [], 0
    for a in arrays:
        f = a.reshape(-1).astype(dtype)
        pad = (-f.shape[0]) % PACK_ALIGN
        pieces.append(jnp.pad(f, (0, pad)) if pad else f)
        total += f.shape[0] + pad
    tail = (-total) % (PACK_ROWS * LANES)
    if tail:
        pieces.append(jnp.zeros((tail,), dtype))
    return jnp.concatenate(pieces).reshape(-1, LANES)


def _unpack(packed, shapes, lead=()):
    flat = packed.reshape(lead + (-1,))
    out, off = [], 0
    for s in shapes:
        n = 1
        for d in s:
            n *= d
        out.append(flat[..., off:off + n].reshape(lead + tuple(s)))
        off += n + ((-n) % PACK_ALIGN)
    return out


def _pad_lanes(a):
    return jnp.pad(a, (0, LANES - a.shape[0])).reshape(1, LANES)


ROW_TM = 256
EVEN_SEGS = (("z", D_MODEL), ("xbc", CONV_DIM), ("dt", SSD_HEADS), ("g", D_MODEL), ("u", D_MODEL), ("v", D_MODEL))
ODD_SEGS = (("q", D_MODEL), ("k", D_MODEL), ("v", D_MODEL), ("g", D_MODEL))
TAIL_ROWS = (368, 512, 64, 368, 368, 368)


def _split_cols(w, segs):
    out, off = {}, 0
    for nm, n in segs:
        out[nm] = w[:, off:off + n]
        off += n
    return out


def _rms_fn(h, w):
    return (_rms(h, w),)


def _even_fwd(tag, h, p, riders=None, late=None):
    riders = riders or {}
    got = {}
    hn, = _rows_fwd(tag + "_norm", _rms_fn, [h], [p["norm_w"]], [(D_MODEL, BF16)], ROW_TM)
    proj = {nm: _matmul(f"{tag}_in_{nm}", hn, p["w_in"][nm], NN, F32, rider=riders.get("in_" + nm))
            for nm, _ in EVEN_SEGS}
    for nm, _ in EVEN_SEGS:
        if "in_" + nm in riders:
            proj[nm], got["in_" + nm] = proj[nm]
    (pre, xs, bm, cm), got["conv"] = _conv_fwd(tag + "_conv", proj["xbc"], p["conv_w"], p["conv_b"], riders.get("conv"))
    (y_ssd, states), got["ssd"] = _ssd_fwd(tag + "_ssd", xs, bm, cm, proj["dt"], p["dt_bias"], p["a_log"],
                                           p["d_skip"], riders.get("ssd"))
    if late is not None:
        p = dict(p, **late(got))
    ya, = _rows_fwd(tag + "_ssdgate", _ssd_gate, [y_ssd, proj["z"]], [p["ssd_norm_w"]], [(D_MODEL, BF16)], ROW_TM)
    vn, = _rows_fwd(tag + "_sgunorm", _sgu_norm, [proj["v"]], [p["sgu_ln_w"], p["sgu_ln_b"]], [(D_MODEL, F32)], ROW_TM)
    yb = _sgu_fwd(tag + "_sgu", proj["u"], proj["g"], vn, p["sgu_w"], p["sgu_b"])
    h1 = _matmul(tag + "_out_a", ya, p["w_out_a"], NN, F32, add=h)
    h2 = _matmul(tag + "_out_b", yb, p["w_out_b"], NN, F32, add=h1)
    saved = dict(h=h, hn=hn, proj=proj, pre=pre, xs=xs, bm=bm, cm=cm, y_ssd=y_ssd, states=states,
                 ya=ya, vn=vn, yb=yb)
    return h2, saved, p, got


def _even_bwd(tag, dh, s, p, riders=None):
    riders = riders or {}
    g, got = {}, {}

    def rider_for(host):
        return riders[host](g) if host in riders else None

    dh16 = dh.astype(BF16)
    proj = s["proj"]
    dya = _matmul(tag + "_dya", dh16, p["w_out_a"], NT, F32)
    dyb = _matmul(tag + "_dyb", dh16, p["w_out_b"], NT, F32)
    g["w_out_a"] = _matmul(tag + "_dwout_a", s["ya"], dh16, TN, BF16)
    g["w_out_b"] = _matmul(tag + "_dwout_b", s["yb"], dh16, TN, BF16)
    (du, dg, dvn, g["sgu_w"], g["sgu_b"]), got["sgu_b"] = _sgu_bwd(
        tag + "_sgu_b", proj["u"], proj["g"], s["vn"], p["sgu_w"], p["sgu_b"], dyb, rider_for("sgu_b"))
    dv, g["sgu_ln_w"], g["sgu_ln_b"] = _rows_bwd(tag + "_sgunorm_b", _sgu_norm, [proj["v"]],
                                                 [p["sgu_ln_w"], p["sgu_ln_b"]], [dvn], [BF16], ROW_TM // 2)
    dy_ssd, dz, g["ssd_norm_w"] = _rows_bwd(tag + "_ssdgate_b", _ssd_gate, [s["y_ssd"], proj["z"]],
                                            [p["ssd_norm_w"]], [dya], [F32, BF16], ROW_TM // 2)
    (dxs, dbm, dcm, ddt, g["dt_bias"], g["a_log"], g["d_skip"]), got["ssd_b"] = _ssd_bwd(
        tag + "_ssd_b", s["xs"], s["bm"], s["cm"], proj["dt"], p["dt_bias"], p["a_log"], p["d_skip"],
        s["states"], dy_ssd, rider_for("ssd_b"))
    (dxbc, g["conv_w"], g["conv_b"]), got["conv_b"] = _conv_bwd(
        tag + "_conv_b", proj["xbc"], s["pre"], p["conv_w"], dxs, dbm, dcm, rider_for("conv_b"))
    dproj = dict(z=dz, xbc=dxbc, dt=ddt.astype(BF16), g=dg, u=du, v=dv)
    g["w_in"] = {nm: _matmul(f"{tag}_dwin_{nm}", s["hn"], dproj[nm], TN, BF16) for nm, _ in EVEN_SEGS}
    tail = rider_for("tail") or [None] * len(EVEN_SEGS)
    dhn, got["tail"] = None, []
    for (nm, _), rider in zip(EVEN_SEGS, tail):
        dhn = _matmul(f"{tag}_dhn_{nm}", dproj[nm], p["w_in"][nm], NT, F32, add=dhn, rider=rider)
        if rider is not None:
            dhn, arrived = dhn
            got["tail"] += arrived
    dh_in, g["norm_w"] = _rows_bwd(tag + "_norm_b", _rms_fn, [s["h"]], [p["norm_w"]], [dhn], [F32],
                                   ROW_TM // 2, add=(0, dh))
    return dh_in, g, got


def _odd_fwd(tag, h, p, rider=None):
    hn, = _rows_fwd(tag + "_norm", _rms_fn, [h], [p["norm_w"]], [(D_MODEL, BF16)], ROW_TM)
    q = _matmul(tag + "_in_q", hn, p["w_in"]["q"], NN, BF16)
    k = _matmul(tag + "_in_k", hn, p["w_in"]["k"], NN, BF16)
    v = _matmul(tag + "_in_v", hn, p["w_in"]["v"], NN, BF16)
    gate = _matmul(tag + "_in_g", hn, p["w_in"]["g"], NN, F32)
    (y, tot), got = _attn_fwd(tag + "_attn", q, k, v, rider)
    yg, = _rows_fwd(tag + "_gate", _attn_gate, [y, gate], [], [(D_MODEL, BF16)], ROW_TM)
    h1 = _matmul(tag + "_out", yg, p["w_out"], NN, F32, add=h)
    return h1, dict(h=h, hn=hn, q=q, k=k, v=v, gate=gate, y=y, tot=tot, yg=yg), got


def _odd_bwd(tag, dh, s, p, rider=None):
    g = {}
    dh16 = dh.astype(BF16)
    dyg = _matmul(tag + "_dyg", dh16, p["w_out"], NT, F32)
    g["w_out"] = _matmul(tag + "_dwout", s["yg"], dh16, TN, BF16)
    dy, dgate = _rows_bwd(tag + "_gate_b", _attn_gate, [s["y"], s["gate"]], [], [dyg], [F32, BF16], ROW_TM)
    (dq, dk, dv), got = _attn_bwd(tag + "_attn_b", s["q"], s["k"], s["v"], s["tot"], dy, rider)
    dproj = dict(q=dq, k=dk, v=dv, g=dgate)
    dhn = None
    g["w_in"] = {}
    for nm, _ in ODD_SEGS:
        dhn = _matmul(f"{tag}_dhn_{nm}", dproj[nm], p["w_in"][nm], NT, F32, add=dhn)
        g["w_in"][nm] = _matmul(f"{tag}_dwin_{nm}", s["hn"], dproj[nm], TN, BF16)
    dh_in, g["norm_w"] = _rows_bwd(tag + "_norm_b", _rms_fn, [s["h"]], [p["norm_w"]], [dhn], [F32],
                                   ROW_TM // 2, add=(0, dh))
    return dh_in, g, got


BIG = ("ev_w_in", "ev_w_out", "od_w_in", "od_w_out")
SMALL = ("norm_w", "final_norm_w", "ev_conv_b", "ev_dt_bias", "ev_a_log", "ev_d_skip", "ev_ssd_norm_w",
         "ev_sgu_ln_w", "ev_sgu_ln_b", "ev_sgu_w", "ev_sgu_b")
WEIGHTS = ("norm_w", "final_norm_w", "ev_w_in", "ev_conv_w", "ev_conv_b", "ev_dt_bias", "ev_a_log", "ev_d_skip",
           "ev_ssd_norm_w", "ev_sgu_ln_w", "ev_sgu_ln_b", "ev_sgu_w", "ev_sgu_b", "ev_w_out", "od_w_in", "od_w_out")


def _step(w, m, v, x, loss_target):
    h = x[0]
    tgt = loss_target[0]
    n_even, n_odd = w["ev_w_in"].shape[0], w["od_w_in"].shape[0]
    depth = n_even + n_odd
    assert (n_even, n_odd) == (2, 2), "the exchange schedule below is written for the four-layer trunk"

    def shard(n, i):
        return w[n][i].astype(BF16)

    def take_cols(blocks, widths, lo, hi, r0=0, r1=None):
        pieces, start = [], 0
        for blk, wd in zip(blocks, widths):
            a, b = max(lo - start, 0), min(hi - start, wd)
            if a < b:
                pieces.append(blk[r0:r1, a:b])
            start += wd
        return pieces[0] if len(pieces) == 1 else jnp.concatenate(pieces, axis=1)

    def segments(gathered, segs):
        n = gathered.shape[2]
        out, lo = {}, 0
        for nm, wd in segs:
            out[nm] = take_cols([gathered[d] for d in range(N_DEV)], [n] * N_DEV, lo, lo + wd)
            lo += wd
        return out

    def rows(gathered):
        return gathered.reshape(-1, gathered.shape[2])

    def by_owner_cols(g_in, segs, r0=0, r1=None):
        blocks, widths = [g_in[nm] for nm, _ in segs], [wd for _, wd in segs]
        n = sum(widths) // N_DEV
        return jnp.stack([take_cols(blocks, widths, d * n, (d + 1) * n, r0, r1) for d in range(N_DEV)])

    def by_owner_rows(full):
        return full.reshape(N_DEV, full.shape[0] // N_DEV, full.shape[1])

    def even_params(layer, ev_in):
        i = layer // 2
        w_in = segments(ev_in, EVEN_SEGS)
        w_in["dt"] = jnp.pad(w_in["dt"], ((0, 0), (0, LANES - SSD_HEADS)))
        return dict(norm_w=w["norm_w"][layer][None], w_in=w_in, conv_w=conv_w[i], conv_b=w["ev_conv_b"][i][None],
                    dt_bias=_pad_lanes(w["ev_dt_bias"][i]), a_log=_pad_lanes(w["ev_a_log"][i]),
                    d_skip=_pad_lanes(w["ev_d_skip"][i]), ssd_norm_w=w["ev_ssd_norm_w"][i][None],
                    sgu_ln_w=w["ev_sgu_ln_w"][i][None], sgu_ln_b=w["ev_sgu_ln_b"][i][None],
                    sgu_w=w["ev_sgu_w"][i], sgu_b=w["ev_sgu_b"][i][:, :, None])

    def even_out_params(ev_out):
        full = rows(ev_out)
        return dict(w_out_a=full[:D_MODEL], w_out_b=full[D_MODEL:])

    def odd_params(layer, od_in, od_out):
        return dict(norm_w=w["norm_w"][layer][None], w_in=segments(od_in, ODD_SEGS), w_out=rows(od_out))

    def even_in_grads(g):
        return by_owner_cols(g["w_in"], EVEN_SEGS)

    def even_out_grads(g):
        return by_owner_rows(jnp.concatenate([g["w_out_a"], g["w_out_b"]], axis=0))

    def odd_grads(g):
        return [by_owner_cols(g["w_in"], ODD_SEGS), by_owner_rows(g["w_out"])]

    half = w["ev_w_out"].shape[1] // 2
    ev_in0, conv_w = _exchange("gather_first", [shard("ev_w_in", 0), w["ev_conv_w"]], gather=True)
    conv_w = jnp.moveaxis(conv_w, 0, 2).reshape(n_even, CONV_WIDTH, CONV_DIM)
    h, s0, p0, got = _even_fwd(
        "l0", h, even_params(0, ev_in0),
        {"in_xbc": _Rider([shard("ev_w_out", 0)[:half]], True),
         "conv": _Rider([shard("ev_w_out", 0)[half:]], True),
         "ssd": _Rider([shard("od_w_in", 0), shard("od_w_out", 0)], True)},
        late=lambda arrived: even_out_params(jnp.concatenate([arrived["in_xbc"][0], arrived["conv"][0]], axis=1)))
    p1 = odd_params(1, *got["ssd"])
    h, s1, got = _odd_fwd("l1", h, p1, _Rider([shard("ev_w_in", 1), shard("ev_w_out", 1)], True))
    p2 = dict(even_params(2, got[0]), **even_out_params(got[1]))
    h, s2, p2, got = _even_fwd("l2", h, p2, {"ssd": _Rider([shard("od_w_in", 1), shard("od_w_out", 1)], True)})
    p3 = odd_params(3, *got["ssd"])
    h, s3, _ = _odd_fwd("l3", h, p3)

    dh, d_final, loss_part = _loss_head("loss_head", h, tgt, w["final_norm_w"][None], ROW_TM // 2)
    loss = lax.psum(loss_part[0, 0], ("x", "y", "c"))

    def last_grads(g):
        bounds = [0]
        for rows_here in TAIL_ROWS:
            bounds.append(bounds[-1] + rows_here)
        assert bounds[-1] == D_MODEL
        return [_Rider([by_owner_cols(g["w_in"], EVEN_SEGS, r0, r1)], False) for r0, r1 in zip(bounds, bounds[1:])]

    dh, g3, _ = _odd_bwd("l3", dh, s3, p3)
    dh, g2, got = _even_bwd("l2", dh, s2, p2, {"ssd_b": lambda g: _Rider(odd_grads(g3), False)})
    from3 = got["ssd_b"]
    dh, g1, from2 = _odd_bwd("l1", dh, s1, p1, _Rider([even_in_grads(g2), even_out_grads(g2)], False))
    dh, g0, got = _even_bwd("l0", dh, s0, p0, {
        "sgu_b": lambda g: _Rider([even_out_grads(g)[:, :half]], False),
        "ssd_b": lambda g: _Rider(odd_grads(g1), False),
        "conv_b": lambda g: _Rider([even_out_grads(g)[:, half:]], False),
        "tail": last_grads})
    from1 = got["ssd_b"] + [jnp.concatenate([got["sgu_b"][0], got["conv_b"][0]], axis=1)]
    from0 = jnp.concatenate(got["tail"], axis=1)
    grad_x = dh[None]
    lg = [g0, g1, g2, g3]
    ev, od = [g0, g2], [g1, g3]

    def total(tag, parts):
        return _sum_parts("sum_" + tag, parts, 256)

    big_grads = {
        "ev_w_in": total("ev_w_in", [from0, from2[0]]),
        "ev_w_out": total("ev_w_out", [from1[2], from2[1]]),
        "od_w_in": total("od_w_in", [from1[0], from3[0]]),
        "od_w_out": total("od_w_out", [from1[1], from3[1]]),
    }

    small_g = {
        "norm_w": jnp.concatenate([lg[l]["norm_w"] for l in range(depth)], axis=0),
        "final_norm_w": d_final[0],
        "ev_conv_b": jnp.concatenate([e["conv_b"] for e in ev], axis=0),
        "ev_dt_bias": jnp.concatenate([e["dt_bias"][:, :SSD_HEADS] for e in ev], axis=0),
        "ev_a_log": jnp.concatenate([e["a_log"][:, :SSD_HEADS] for e in ev], axis=0),
        "ev_d_skip": jnp.concatenate([e["d_skip"][:, :SSD_HEADS] for e in ev], axis=0),
        "ev_ssd_norm_w": jnp.concatenate([e["ssd_norm_w"] for e in ev], axis=0),
        "ev_sgu_ln_w": jnp.concatenate([e["sgu_ln_w"] for e in ev], axis=0),
        "ev_sgu_ln_b": jnp.concatenate([e["sgu_ln_b"] for e in ev], axis=0),
        "ev_sgu_w": jnp.stack([e["sgu_w"] for e in ev]),
        "ev_sgu_b": jnp.stack([e["sgu_b"][:, :, 0] for e in ev]),
    }
    conv_g = jnp.stack([e["conv_w"] for e in ev])
    small_shapes = [w[n].shape for n in SMALL]
    small_parts, = _exchange("gather_small", [_pack([small_g[n] for n in SMALL] + [conv_g], F32)], gather=True)
    small_sum = _sum_parts("sum_small", [small_parts], 1024)[0]
    *small_list, conv_full = _unpack(small_sum, small_shapes + [conv_g.shape])
    grads = dict(zip(SMALL, small_list))
    grads.update(big_grads)
    me = 4 * lax.axis_index("x") + 2 * lax.axis_index("y") + lax.axis_index("c")
    n_cv = w["ev_conv_w"].shape[2]
    grads["ev_conv_w"] = lax.dynamic_slice_in_dim(conv_full, me * n_cv, n_cv, axis=2)

    deltas, new_m, new_v = {}, {}, {}
    for n in BIG + ("ev_conv_w",):
        deltas[n], new_m[n], new_v[n] = _adamw("adamw_" + n, w[n], grads[n], m[n], v[n], 256)
    packs = [_pack([src[n] for n in SMALL], F32)[None] for src in (w, grads, m, v)]
    outs = _adamw("adamw_small", *packs, 1024)
    for dst, packed in zip((deltas, new_m, new_v), outs):
        dst.update(zip(SMALL, _unpack(packed[0], small_shapes)))
    return loss, grad_x, grads, deltas, new_m, new_v


def kernel(x, norm_w, final_norm_w, ev_w_in, ev_conv_w, ev_conv_b, ev_dt_bias, ev_a_log, ev_d_skip, ev_ssd_norm_w, ev_sgu_ln_w, ev_sgu_ln_b, ev_sgu_w, ev_sgu_b, ev_w_out, od_w_in, od_w_out, loss_target, m_norm_w, m_final_norm_w, m_ev_w_in, m_ev_conv_w, m_ev_conv_b, m_ev_dt_bias, m_ev_a_log, m_ev_d_skip, m_ev_ssd_norm_w, m_ev_sgu_ln_w, m_ev_sgu_ln_b, m_ev_sgu_w, m_ev_sgu_b, m_ev_w_out, m_od_w_in, m_od_w_out, v_norm_w, v_final_norm_w, v_ev_w_in, v_ev_conv_w, v_ev_conv_b, v_ev_dt_bias, v_ev_a_log, v_ev_d_skip, v_ev_ssd_norm_w, v_ev_sgu_ln_w, v_ev_sgu_ln_b, v_ev_sgu_w, v_ev_sgu_b, v_ev_w_out, v_od_w_in, v_od_w_out):
    w = dict(zip(WEIGHTS, (norm_w, final_norm_w, ev_w_in, ev_conv_w, ev_conv_b, ev_dt_bias, ev_a_log, ev_d_skip,
                           ev_ssd_norm_w, ev_sgu_ln_w, ev_sgu_ln_b, ev_sgu_w, ev_sgu_b, ev_w_out, od_w_in, od_w_out)))
    m = dict(zip(WEIGHTS, (m_norm_w, m_final_norm_w, m_ev_w_in, m_ev_conv_w, m_ev_conv_b, m_ev_dt_bias, m_ev_a_log,
                           m_ev_d_skip, m_ev_ssd_norm_w, m_ev_sgu_ln_w, m_ev_sgu_ln_b, m_ev_sgu_w, m_ev_sgu_b,
                           m_ev_w_out, m_od_w_in, m_od_w_out)))
    v = dict(zip(WEIGHTS, (v_norm_w, v_final_norm_w, v_ev_w_in, v_ev_conv_w, v_ev_conv_b, v_ev_dt_bias, v_ev_a_log,
                           v_ev_d_skip, v_ev_ssd_norm_w, v_ev_sgu_ln_w, v_ev_sgu_ln_b, v_ev_sgu_w, v_ev_sgu_b,
                           v_ev_w_out, v_od_w_in, v_od_w_out)))
    loss, grad_x, grads, deltas, new_m, new_v = _step(w, m, v, x, loss_target)
    return (loss, grad_x, *[grads[n] for n in WEIGHTS], *[deltas[n] for n in WEIGHTS],
            *[new_m[n] for n in WEIGHTS], *[new_v[n] for n in WEIGHTS])
```

```python
import jax
import jax.numpy as jnp
from jax import lax
from jax.experimental import pallas as pl
from jax.experimental.pallas import tpu as pltpu

F32, BF16 = jnp.float32, jnp.bfloat16

D_MODEL = 2048
SSD_HEADS = 32
SSD_HEAD_DIM = 64
SSD_GROUPS = 4
SSD_STATE = 128
CHUNK = 128
CONV_WIDTH = 4
CONV_DIM = D_MODEL + 2 * SSD_GROUPS * SSD_STATE
SGU_GROUPS = 16
SB_HEADS = 16
LANES = 128
N_PAIRS = SSD_HEADS // 2
PAIRS_PER_GROUP = N_PAIRS // SSD_GROUPS
NORM_EPS = 1e-5
N_DEV = 8

ADAM_LR, ADAM_B1, ADAM_B2, ADAM_EPS, ADAM_WD, ADAM_STEP = 0.001, 0.9, 0.999, 1e-08, 0.01, 10

VMEM_LIMIT_BYTES = 48 * 1024 * 1024

NN = ((1,), (0,))
NT = ((1,), (1,))
TN = ((0,), (0,))


def _cparams(*sem):
    return pltpu.CompilerParams(dimension_semantics=sem, vmem_limit_bytes=VMEM_LIMIT_BYTES)


def _dg(a, b, dims):
    return lax.dot_general(a, b, (dims, ((), ())), preferred_element_type=F32)


def _make_bdot(dims):
    @jax.custom_vjp
    def f(a, b):
        return _dg(a.astype(BF16), b.astype(BF16), dims)

    def fwd(a, b):
        return f(a, b), (a, b)

    def bwd(res, g):
        a, b = res
        a16, b16, g16 = a.astype(BF16), b.astype(BF16), g.astype(BF16)
        if dims == NN:
            da, db = _dg(g16, b16, NT), _dg(a16, g16, TN)
        elif dims == NT:
            da, db = _dg(g16, b16, NN), _dg(g16, a16, TN)
        else:
            da, db = _dg(b16, g16, NT), _dg(a16, g16, NN)
        return da.astype(a.dtype), db.astype(b.dtype)

    f.defvjp(fwd, bwd)
    return f


_bdot_nn, _bdot_nt, _bdot_tn = _make_bdot(NN), _make_bdot(NT), _make_bdot(TN)


def _iota2(shape, axis):
    return lax.broadcasted_iota(jnp.int32, shape, axis)


def _split3(x, axis):
    hi = x.astype(BF16)
    r = x - hi.astype(F32)
    mid = r.astype(BF16)
    return jnp.concatenate([hi, mid, (r - mid.astype(F32)).astype(BF16)], axis=axis)


def _make_onehot_dot(build, build_t, left):
    def apply(x, e):
        e = e.astype(BF16)
        if left:
            return _dg(jnp.concatenate([e, e, e], axis=1), _split3(x, 0), NN)
        return _dg(_split3(x, 1), jnp.concatenate([e, e, e], axis=0), NN)

    @jax.custom_vjp
    def f(x):
        return apply(x, build())

    f.defvjp(lambda x: (f(x), None), lambda _, g: (apply(g, build_t()),))
    return f


_cumsum_rows = _make_onehot_dot(lambda: _iota2((CHUNK, CHUNK), 1) <= _iota2((CHUNK, CHUNK), 0),
                                lambda: _iota2((CHUNK, CHUNK), 0) <= _iota2((CHUNK, CHUNK), 1), True)


def _softplus(x):
    return jnp.maximum(x, 0.0) + jnp.log1p(jnp.exp(-jnp.abs(x)))


def _silu(x):
    return x * jax.nn.sigmoid(x)


def _gelu(x):
    return 0.5 * x * (1.0 + jnp.tanh(0.7978845608028654 * (x + 0.044715 * (x * x * x))))


def _rms(x, w):
    return x * lax.rsqrt(jnp.mean(x * x, axis=-1, keepdims=True) + NORM_EPS) * w


def _layer_norm(x, w, b):
    xc = x - jnp.mean(x, axis=-1, keepdims=True)
    return xc * lax.rsqrt(jnp.mean(xc * xc, axis=-1, keepdims=True) + NORM_EPS) * w + b


def _row_spec(width, tm):
    return pl.BlockSpec((tm, width), lambda i: (i, 0))


def _full_spec(p):
    zeros = (0,) * p.ndim
    return pl.BlockSpec(p.shape, lambda i: zeros)


def _rows_fwd(name, fn, tiled, params, outs, tm):
    n_rows = tiled[0].shape[0]
    n_in = len(tiled) + len(params)

    def body(*refs):
        res = fn(*[r[...] for r in refs[:n_in]])
        for o_ref, o in zip(refs[n_in:], res):
            o_ref[...] = o.astype(o_ref.dtype)

    return pl.pallas_call(
        body, name=name, grid=(n_rows // tm,),
        in_specs=[_row_spec(a.shape[1], tm) for a in tiled] + [_full_spec(p) for p in params],
        out_specs=[_row_spec(w, tm) for w, _ in outs],
        out_shape=[jax.ShapeDtypeStruct((n_rows, w), d) for w, d in outs],
        compiler_params=_cparams("parallel"),
    )(*tiled, *params)


def _rows_bwd(name, fn, tiled, params, cots, grad_dtypes, tm, add=None):
    n_rows = tiled[0].shape[0]
    nt, npar, nc = len(tiled), len(params), len(cots)
    want = [k for k, d in enumerate(grad_dtypes) if d is not None]
    n_add = 0 if add is None else 1

    def body(*refs):
        ins = [r[...] for r in refs[:nt + npar]]
        c_refs = refs[nt + npar:nt + npar + nc]
        add_refs = refs[nt + npar + nc:nt + npar + nc + n_add]
        o_refs = refs[nt + npar + nc + n_add:]
        res, vjp = jax.vjp(fn, *ins)
        grads = vjp(tuple(c[...].astype(r.dtype) for c, r in zip(c_refs, res)))
        for pos, k in enumerate(want):
            gk = grads[k]
            if add is not None and add[0] == pos:
                gk = gk + add_refs[0][...]
            o_refs[pos][...] = gk.astype(o_refs[pos].dtype)
        p_refs = o_refs[len(want):]

        @pl.when(pl.program_id(0) == 0)
        def _():
            for r in p_refs:
                r[...] = jnp.zeros_like(r)

        for r, gp in zip(p_refs, grads[nt:]):
            r[...] += gp

    add_arrays = [] if add is None else [add[1]]
    out = pl.pallas_call(
        body, name=name, grid=(n_rows // tm,),
        in_specs=([_row_spec(a.shape[1], tm) for a in tiled] + [_full_spec(p) for p in params]
                  + [_row_spec(c.shape[1], tm) for c in cots] + [_row_spec(a.shape[1], tm) for a in add_arrays]),
        out_specs=([_row_spec(tiled[k].shape[1], tm) for k in want] + [_full_spec(p) for p in params]),
        out_shape=([jax.ShapeDtypeStruct(tiled[k].shape, grad_dtypes[k]) for k in want]
                   + [jax.ShapeDtypeStruct(p.shape, F32) for p in params]),
        compiler_params=_cparams("arbitrary"),
    )(*tiled, *params, *cots, *add_arrays)
    return out


def _matmul(name, a, b, dims, out_dtype, add=None, rider=None, tm=1024, tn=1024, tk=2048):
    if dims == NN:
        (m, k), n = a.shape, b.shape[1]
    elif dims == NT:
        (m, k), n = a.shape, b.shape[0]
    else:
        (k, m), n = a.shape, b.shape[1]
    tm, tn, tk = min(tm, m), min(tn, n), min(tk, k)
    while k % tk:
        tk -= LANES
    assert m % tm == 0 and n % tn == 0 and k % tk == 0, (name, m, n, k)
    nk = k // tk
    a_spec = (pl.BlockSpec((tk, tm), lambda i, j, kk: (kk, i)) if dims == TN
              else pl.BlockSpec((tm, tk), lambda i, j, kk: (i, kk)))
    b_spec = (pl.BlockSpec((tn, tk), lambda i, j, kk: (j, kk)) if dims == NT
              else pl.BlockSpec((tk, tn), lambda i, j, kk: (kk, j)))
    o_spec = pl.BlockSpec((tm, tn), lambda i, j, kk: (i, j))
    has_add = add is not None

    def body(*refs):
        a_ref, b_ref = refs[0], refs[1]
        part = _dg(a_ref[...].astype(BF16), b_ref[...].astype(BF16), dims)
        if nk == 1:
            o_ref = refs[-1]
            if has_add:
                part = part + refs[2][...]
            o_ref[...] = part.astype(o_ref.dtype)
            return
        o_ref, acc = refs[-2], refs[-1]
        kk = pl.program_id(2)

        @pl.when(kk == 0)
        def _():
            acc[...] = part

        @pl.when(kk > 0)
        def _():
            acc[...] += part

        @pl.when(kk == nk - 1)
        def _():
            r = acc[...]
            if has_add:
                r = r + refs[2][...]
            o_ref[...] = r.astype(o_ref.dtype)

    in_specs = [a_spec, b_spec] + ([o_spec] if has_add else [])
    scratch = [pltpu.VMEM((tm, tn), F32)] if nk > 1 else []
    args = (a, b) + ((add,) if has_add else ())
    if rider is not None:
        (out,), got = _hosted(body, rider, name=name, grid=(m // tm, n // tn, nk), in_specs=in_specs,
                              out_specs=[o_spec], out_shape=[jax.ShapeDtypeStruct((m, n), out_dtype)],
                              scratch_shapes=scratch, args=args)
        return out, got
    return pl.pallas_call(
        body, name=name, grid=(m // tm, n // tn, nk), in_specs=in_specs, out_specs=o_spec,
        out_shape=jax.ShapeDtypeStruct((m, n), out_dtype), scratch_shapes=scratch,
        compiler_params=_cparams("parallel", "parallel", "arbitrary"),
    )(*args)


CONV_TM = 256
HALO = 8


def _shift_down(x, halo, j):
    if j == 0:
        return x, x[:HALO]
    xr = pltpu.roll(x, j, 0)
    hr = pltpu.roll(halo, j, 0)
    top = jnp.where(_iota2((HALO, x.shape[1]), 0) < j, hr, xr[:HALO])
    return xr, top


def _conv_fwd(name, x, w, b, rider=None):
    t, c = x.shape
    tm = min(CONV_TM, t)
    hb = tm // HALO

    def body(x_ref, halo_ref, w_ref, b_ref, pre_ref, xs_ref, bm_ref, cm_ref):
        i = pl.program_id(0)
        xv = x_ref[...]
        halo = jnp.where(i > 0, halo_ref[...], 0.0)
        main = jnp.zeros_like(xv) + b_ref[...]
        top = jnp.zeros((HALO, c), F32) + b_ref[...]
        for kk in range(CONV_WIDTH):
            xr, tp = _shift_down(xv, halo, CONV_WIDTH - 1 - kk)
            main = main + w_ref[kk:kk + 1, :] * xr
            top = top + w_ref[kk:kk + 1, :] * tp
        pre = jnp.concatenate([top, main[HALO:]], axis=0)
        pre_ref[...] = pre
        act = _silu(pre)
        xs_ref[...] = act[:, :D_MODEL]
        bm_ref[...] = act[:, D_MODEL:D_MODEL + SSD_GROUPS * SSD_STATE]
        cm_ref[...] = act[:, D_MODEL + SSD_GROUPS * SSD_STATE:]

    gs = SSD_GROUPS * SSD_STATE
    return _hosted(
        body, rider, name=name, grid=(t // tm,),
        in_specs=[_row_spec(c, tm),
                  pl.BlockSpec((HALO, c), lambda i: (jnp.maximum(i * hb - 1, 0), 0)),
                  _full_spec(w), _full_spec(b)],
        out_specs=[_row_spec(c, tm), _row_spec(D_MODEL, tm), _row_spec(gs, tm), _row_spec(gs, tm)],
        out_shape=[jax.ShapeDtypeStruct((t, c), F32), jax.ShapeDtypeStruct((t, D_MODEL), F32),
                   jax.ShapeDtypeStruct((t, gs), F32), jax.ShapeDtypeStruct((t, gs), F32)],
        scratch_shapes=[], args=(x, x, w, b))


def _dsilu(pre, dact):
    s = jax.nn.sigmoid(pre)
    return dact * (s * (1.0 + pre * (1.0 - s)))


def _conv_bwd(name, x, pre, w, dxs, dbm, dcm, rider=None):
    t, c = x.shape
    tm = min(CONV_TM, t)
    hb = tm // HALO
    last_hb = t // HALO - 1
    n_tiles = t // tm

    def body(x_ref, xh_ref, pre_ref, preh_ref, w_ref, dxs_ref, dbm_ref, dcm_ref,
             dxsh_ref, dbmh_ref, dcmh_ref, dx_ref, dw_ref, db_ref):
        i = pl.program_id(0)
        dact = jnp.concatenate([dxs_ref[...], dbm_ref[...], dcm_ref[...]], axis=1)
        dpre = _dsilu(pre_ref[...], dact)
        dact_h = jnp.concatenate([dxsh_ref[...], dbmh_ref[...], dcmh_ref[...]], axis=1)
        dpre_h = jnp.where(i < n_tiles - 1, _dsilu(preh_ref[...], dact_h), 0.0)
        xv = x_ref[...]
        xh = jnp.where(i > 0, xh_ref[...], 0.0)

        @pl.when(i == 0)
        def _():
            dw_ref[...] = jnp.zeros_like(dw_ref)
            db_ref[...] = jnp.zeros_like(db_ref)

        db_ref[...] += jnp.sum(dpre, axis=0, keepdims=True)
        dxm = jnp.zeros_like(xv)
        dxt = jnp.zeros((HALO, c), F32)
        row8 = _iota2((HALO, c), 0)
        for kk in range(CONV_WIDTH):
            j = CONV_WIDTH - 1 - kk
            wk = w_ref[kk:kk + 1, :]
            xr, tp = _shift_down(xv, xh, j)
            full = jnp.sum(dpre * xr, axis=0, keepdims=True)
            fix = jnp.sum(dpre[:HALO] * (tp - xr[:HALO]), axis=0, keepdims=True)
            dw_ref[kk:kk + 1, :] += full + fix
            if j == 0:
                dxm = dxm + wk * dpre
                dxt = dxt + wk * dpre[tm - HALO:]
            else:
                dr = pltpu.roll(dpre, tm - j, 0)
                hr = pltpu.roll(dpre_h, HALO - j, 0)
                dxm = dxm + wk * dr
                dxt = dxt + wk * jnp.where(row8 >= HALO - j, hr, dr[tm - HALO:])
        dx_ref[...] = jnp.concatenate([dxm[:tm - HALO], dxt], axis=0).astype(dx_ref.dtype)

    gs = SSD_GROUPS * SSD_STATE
    prev_halo = lambda i: (jnp.maximum(i * hb - 1, 0), 0)
    next_halo = lambda i: (jnp.minimum((i + 1) * hb, last_hb), 0)
    return _hosted(
        body, rider, name=name, grid=(n_tiles,),
        in_specs=[_row_spec(c, tm), pl.BlockSpec((HALO, c), prev_halo),
                  _row_spec(c, tm), pl.BlockSpec((HALO, c), next_halo), _full_spec(w),
                  _row_spec(D_MODEL, tm), _row_spec(gs, tm), _row_spec(gs, tm),
                  pl.BlockSpec((HALO, D_MODEL), next_halo), pl.BlockSpec((HALO, gs), next_halo),
                  pl.BlockSpec((HALO, gs), next_halo)],
        out_specs=[_row_spec(c, tm), _full_spec(w), pl.BlockSpec((1, c), lambda i: (0, 0))],
        out_shape=[jax.ShapeDtypeStruct((t, c), BF16), jax.ShapeDtypeStruct(w.shape, F32),
                   jax.ShapeDtypeStruct((1, c), F32)],
        scratch_shapes=[], args=(x, x, pre, pre, w, dxs, dbm, dcm, dxs, dbm, dcm))


def _ssd_prep(dt_raw, dt_bias, a_log):
    dt = _softplus(dt_raw + dt_bias)
    return dt, _cumsum_rows(dt * (-jnp.exp(a_log)))


def _head_col(x, h):
    return jnp.sum(jnp.where(_iota2(x.shape, 1) == h, x, 0.0), axis=1, keepdims=True)


def _ssd_pair(p, xs, dt, cs, d_skip, bg, cg, prev):
    l = xs.shape[0]
    first = _iota2((l, LANES), 1) < SSD_HEAD_DIM
    c0, c1 = _head_col(cs, 2 * p), _head_col(cs, 2 * p + 1)
    csf = jnp.where(first, c0, c1)
    dtf = jnp.where(first, _head_col(dt, 2 * p), _head_col(dt, 2 * p + 1))
    dskf = jnp.where(_iota2((1, LANES), 1) < SSD_HEAD_DIM, _head_col(d_skip, 2 * p), _head_col(d_skip, 2 * p + 1))
    cs_last = jnp.sum(jnp.where(_iota2((l, LANES), 0) == l - 1, csf, 0.0), axis=0, keepdims=True)
    xc = xs * dtf
    scores = _bdot_nt(cg, bg)
    causal = _iota2((l, l), 0) >= _iota2((l, l), 1)

    def decay(col):
        a = jnp.broadcast_to(col, (l, l))
        return jnp.where(causal, jnp.exp(jnp.where(causal, a - a.T, 0.0)), 0.0)

    y_diag = (_bdot_nn(scores * decay(c0), jnp.where(first, xc, 0.0))
              + _bdot_nn(scores * decay(c1), jnp.where(first, 0.0, xc)))
    states = _bdot_tn(bg, xc * jnp.exp(cs_last - csf))
    new_state = jnp.exp(cs_last) * prev + states
    y_off = _bdot_nn(cg, prev) * jnp.exp(csf)
    return y_diag + y_off + xs * dskf, new_state


def _ssd_specs(nc_rev=None):
    def ch(c):
        return c if nc_rev is None else nc_rev - 1 - c

    gs = SSD_GROUPS * SSD_STATE
    return dict(
        wide=pl.BlockSpec((CHUNK, D_MODEL), lambda c: (ch(c), 0)),
        group=pl.BlockSpec((CHUNK, gs), lambda c: (ch(c), 0)),
        chunk=pl.BlockSpec((CHUNK, LANES), lambda c: (ch(c), 0)),
        vec=pl.BlockSpec((1, LANES), lambda c: (0, 0)),
        state=pl.BlockSpec((1, N_PAIRS, LANES, LANES), lambda c: (ch(c), 0, 0, 0)),
    )


def _pair_slices(p):
    g = p // PAIRS_PER_GROUP
    return slice(p * LANES, (p + 1) * LANES), slice(g * LANES, (g + 1) * LANES)


def _ssd_fwd(name, xs, bm, cm, dt_raw, dt_bias, a_log, d_skip, rider=None):
    t = xs.shape[0]
    nc = t // CHUNK

    def body(xs_ref, b_ref, c_ref, dt_ref, bias_ref, alog_ref, dsk_ref, y_ref, prev_ref, state, dt_s, cs_s):
        c = pl.program_id(0)
        dt_s[...], cs_s[...] = _ssd_prep(dt_ref[...], bias_ref[...], alog_ref[...])

        @pl.when(c == 0)
        def _():
            state[...] = jnp.zeros_like(state)

        for p in range(N_PAIRS):
            sl, gsl = _pair_slices(p)
            prev = state[p]
            prev_ref[0, p] = prev
            y, new_state = _ssd_pair(p, xs_ref[:, sl], dt_s[...], cs_s[...], dsk_ref[...],
                                     b_ref[:, gsl], c_ref[:, gsl], prev)
            y_ref[:, sl] = y
            state[p] = new_state

    sp = _ssd_specs()
    return _hosted(
        body, rider, name=name, grid=(nc,),
        in_specs=[sp["wide"], sp["group"], sp["group"], sp["chunk"], sp["vec"], sp["vec"], sp["vec"]],
        out_specs=[sp["wide"], sp["state"]],
        out_shape=[jax.ShapeDtypeStruct((t, D_MODEL), F32),
                   jax.ShapeDtypeStruct((nc, N_PAIRS, LANES, LANES), F32)],
        scratch_shapes=[pltpu.VMEM((N_PAIRS, LANES, LANES), F32), pltpu.VMEM((CHUNK, LANES), F32),
                        pltpu.VMEM((CHUNK, LANES), F32)],
        args=(xs, bm, cm, dt_raw, dt_bias, a_log, d_skip))


def _ssd_bwd(name, xs, bm, cm, dt_raw, dt_bias, a_log, d_skip, prev_states, dy, rider=None):
    t = xs.shape[0]
    nc = t // CHUNK

    def body(xs_ref, b_ref, c_ref, dt_ref, bias_ref, alog_ref, dsk_ref, prev_ref, dy_ref,
             dxs_ref, db_ref, dc_ref, ddt_ref, dbias_ref, dalog_ref, ddsk_ref, dstate, dt_s, cs_s, g_dt, g_cs):
        c = pl.program_id(0)
        (dt_s[...], cs_s[...]), vjp_prep = jax.vjp(_ssd_prep, dt_ref[...], bias_ref[...], alog_ref[...])

        @pl.when(c == 0)
        def _():
            dstate[...] = jnp.zeros_like(dstate)
            dbias_ref[...] = jnp.zeros_like(dbias_ref)
            dalog_ref[...] = jnp.zeros_like(dalog_ref)
            ddsk_ref[...] = jnp.zeros_like(ddsk_ref)

        g_dt[...] = jnp.zeros_like(g_dt)
        g_cs[...] = jnp.zeros_like(g_cs)
        for p in range(N_PAIRS):
            sl, gsl = _pair_slices(p)
            pair = lambda *a, p=p: _ssd_pair(p, *a)
            _, vjp = jax.vjp(pair, xs_ref[:, sl], dt_s[...], cs_s[...], dsk_ref[...],
                             b_ref[:, gsl], c_ref[:, gsl], prev_ref[0, p])
            dxs, ddt, dcs, ddsk, dbg, dcg, dprev = vjp((dy_ref[:, sl], dstate[p]))
            dxs_ref[:, sl] = dxs
            g_dt[...] += ddt
            g_cs[...] += dcs
            ddsk_ref[...] += ddsk
            dstate[p] = dprev
            if p % PAIRS_PER_GROUP == 0:
                db_ref[:, gsl] = dbg
                dc_ref[:, gsl] = dcg
            else:
                db_ref[:, gsl] += dbg
                dc_ref[:, gsl] += dcg

        ddt_raw, dbias, dalog = vjp_prep((g_dt[...], g_cs[...]))
        ddt_ref[...] = ddt_raw
        dbias_ref[...] += dbias
        dalog_ref[...] += dalog

    sp = _ssd_specs(nc)
    gs = SSD_GROUPS * SSD_STATE
    return _hosted(
        body, rider, name=name, grid=(nc,),
        in_specs=[sp["wide"], sp["group"], sp["group"], sp["chunk"], sp["vec"], sp["vec"], sp["vec"],
                  sp["state"], sp["wide"]],
        out_specs=[sp["wide"], sp["group"], sp["group"], sp["chunk"], sp["vec"], sp["vec"], sp["vec"]],
        out_shape=[jax.ShapeDtypeStruct((t, D_MODEL), F32), jax.ShapeDtypeStruct((t, gs), F32),
                   jax.ShapeDtypeStruct((t, gs), F32), jax.ShapeDtypeStruct((t, LANES), F32),
                   jax.ShapeDtypeStruct((1, LANES), F32), jax.ShapeDtypeStruct((1, LANES), F32),
                   jax.ShapeDtypeStruct((1, LANES), F32)],
        scratch_shapes=[pltpu.VMEM((N_PAIRS, LANES, LANES), F32)] + [pltpu.VMEM((CHUNK, LANES), F32)] * 4,
        args=(xs, bm, cm, dt_raw, dt_bias, a_log, d_skip, prev_states, dy))


def _ssd_gate(y, z, w):
    return (_rms(y * _silu(z), w),)


def _sgu_norm(v, w, b):
    return (_layer_norm(_gelu(v), w, b),)


def _sgu_group(u, gate, vn, w, bcol):
    l = u.shape[0]
    wc = jnp.where(_iota2((l, l), 0) >= _iota2((l, l), 1), w, 0.0)
    return _gelu(u) * (_bdot_nn(wc, vn) + bcol) * _silu(gate)


def _sgu_fwd(name, u, gate, vn, w, bcol):
    t = u.shape[0]
    blk = _row_spec(D_MODEL, CHUNK)

    def body(u_ref, g_ref, vn_ref, w_ref, b_ref, y_ref):
        for g in range(SGU_GROUPS):
            sl = slice(g * LANES, (g + 1) * LANES)
            y_ref[:, sl] = _sgu_group(u_ref[:, sl], g_ref[:, sl], vn_ref[:, sl], w_ref[g], b_ref[g]).astype(y_ref.dtype)

    return pl.pallas_call(
        body, name=name, grid=(t // CHUNK,),
        in_specs=[blk, blk, blk, _full_spec(w), _full_spec(bcol)], out_specs=blk,
        out_shape=jax.ShapeDtypeStruct((t, D_MODEL), BF16),
        compiler_params=_cparams("parallel"),
    )(u, gate, vn, w, bcol)


def _sgu_bwd(name, u, gate, vn, w, bcol, dy, rider=None):
    t = u.shape[0]
    blk = _row_spec(D_MODEL, CHUNK)

    def body(u_ref, g_ref, vn_ref, w_ref, b_ref, dy_ref, du_ref, dg_ref, dvn_ref, dw_ref, db_ref):
        @pl.when(pl.program_id(0) == 0)
        def _():
            dw_ref[...] = jnp.zeros_like(dw_ref)
            db_ref[...] = jnp.zeros_like(db_ref)

        for g in range(SGU_GROUPS):
            sl = slice(g * LANES, (g + 1) * LANES)
            _, vjp = jax.vjp(_sgu_group, u_ref[:, sl], g_ref[:, sl], vn_ref[:, sl], w_ref[g], b_ref[g])
            du, dg, dvn, dw, db = vjp(dy_ref[:, sl])
            du_ref[:, sl] = du.astype(du_ref.dtype)
            dg_ref[:, sl] = dg.astype(dg_ref.dtype)
            dvn_ref[:, sl] = dvn
            dw_ref[g] += dw
            db_ref[g] += db

    return _hosted(
        body, rider, name=name, grid=(t // CHUNK,),
        in_specs=[blk, blk, blk, _full_spec(w), _full_spec(bcol), blk],
        out_specs=[blk, blk, blk, _full_spec(w), _full_spec(bcol)],
        out_shape=[jax.ShapeDtypeStruct((t, D_MODEL), BF16), jax.ShapeDtypeStruct((t, D_MODEL), BF16),
                   jax.ShapeDtypeStruct((t, D_MODEL), F32), jax.ShapeDtypeStruct(w.shape, F32),
                   jax.ShapeDtypeStruct(bcol.shape, F32)],
        scratch_shapes=[], args=(u, gate, vn, w, bcol, dy))


SB_SCALE = LANES ** -0.5
LOG2_E = 1.4426950408889634
SB_TQ = 512
SB_TS = 512


def _keep(mask, x):
    return x if mask is None else jnp.where(mask, x, 0.0)


def _sb_pieces(z2, mask):
    sp = jnp.maximum(z2, 0.0) + jnp.log(1.0 + jnp.exp2(-jnp.abs(z2))) * LOG2_E
    return _keep(mask, sp), z2 - sp


def _split2(x):
    hi = x.astype(BF16)
    return jnp.concatenate([hi, (x - hi.astype(F32)).astype(BF16)], axis=1)


def _tri2(cmp):
    sq = (CHUNK, CHUNK)
    m = cmp(_iota2(sq, 0), _iota2(sq, 1)).astype(BF16)
    return jnp.concatenate([m, m], axis=0)


def _tri_sums(blocks, tri2):
    res = _dg(jnp.concatenate([_split2(b) for b in blocks], axis=0), tri2, NN)
    out, at = [], 0
    for b in blocks:
        out.append(res[at:at + b.shape[0]])
        at += b.shape[0]
    return out


def _pad_top(x, start):
    return x if start == 0 else jnp.concatenate([jnp.zeros((start, x.shape[1]), x.dtype), x], axis=0)


def _rows_from(x, start):
    return x if start == 0 else x[start:]


def _sb_tiles(t):
    tq, ts = min(SB_TQ, t), min(SB_TS, t)
    assert t % tq == 0 and t % ts == 0 and ts % tq == 0 and ts % CHUNK == 0
    return tq, ts, ts // CHUNK


def _sb_logits(qb, ks, off, q_off, masked, trim=False):
    tq, ts = qb.shape[0], ks.shape[0]
    z2 = _dg(qb, ks, NT) * (SB_SCALE * LOG2_E)
    out = []
    for b in range(ts // CHUNK):
        mask, start = None, 0
        if masked:
            start = b * CHUNK if trim and tq == ts else 0
            mask = ((_iota2((tq - start, CHUNK), 1) + (off + b * CHUNK))
                    < (_iota2((tq - start, CHUNK), 0) + (q_off + start)))
        out.append(_sb_pieces(_rows_from(z2, start)[:, b * CHUNK:(b + 1) * CHUNK], mask) + (mask, start))
    return out


def _attn_fwd(name, q, k, v, rider=None):
    t = q.shape[0]
    tq, ts, nb = _sb_tiles(t)

    def body(q_ref, k_ref, v_ref, y_ref, tot_ref):
        i = pl.program_id(1)
        qb = q_ref[...]
        later2 = _tri2(lambda r, c: r > c)
        last = (i * tq + tq - 1) // ts

        def span(j, carry, masked):
            acc, after = carry
            off = pl.multiple_of(j * ts, ts)
            pieces = _sb_logits(qb, k_ref[pl.ds(off, ts), :], off, i * tq, masked)
            inside = _tri_sums([pc[0] for pc in pieces], later2)
            ws = [None] * nb
            for b in reversed(range(nb)):
                drop, ls, mask, start = pieces[b]
                ws[b] = _pad_top(_keep(mask, jnp.exp2(ls - (inside[b] + _rows_from(after, start)))).astype(BF16), start)
                after = after + _pad_top(jnp.sum(drop, axis=1, keepdims=True), start)
            acc = acc + _dg(jnp.concatenate(ws, axis=1), v_ref[pl.ds(off, ts), :], NN)
            return acc, after

        carry = span(last, (jnp.zeros((tq, LANES), F32), jnp.zeros((tq, 1), F32)), True)
        acc, tot = lax.fori_loop(0, last, lambda n, c: span(last - 1 - n, c, False), carry)
        y_ref[...] = acc
        tot_ref[...] = jnp.broadcast_to(tot, (tq, LANES))

    qsp = pl.BlockSpec((tq, LANES), lambda h, i: (i, h))
    kvsp = pl.BlockSpec((t, LANES), lambda h, i: (0, h))
    return _hosted(
        body, rider, name=name, grid=(SB_HEADS, t // tq),
        in_specs=[qsp, kvsp, kvsp], out_specs=[qsp, qsp],
        out_shape=[jax.ShapeDtypeStruct((t, D_MODEL), F32), jax.ShapeDtypeStruct((t, D_MODEL), F32)],
        scratch_shapes=[], args=(q, k, v))


def _attn_bwd(name, q, k, v, tot, dy, rider=None):
    t = q.shape[0]
    tq, ts, nb = _sb_tiles(t)
    nq, ns = t // tq, t // ts

    def body(q_ref, k_ref, v_ref, tot_ref, dy_ref, dq_ref, dk_ref, dv_ref, dkt_acc, dvt_acc):
        i = pl.program_id(1)

        @pl.when(i == 0)
        def _():
            dkt_acc[...] = jnp.zeros_like(dkt_acc)
            dvt_acc[...] = jnp.zeros_like(dvt_acc)

        qb = q_ref[...]
        dy = dy_ref[...]
        dyb = dy.astype(BF16)
        q_t = qb.astype(F32).T.astype(BF16)
        dy_t = dy.T.astype(BF16)
        totb = tot_ref[...]
        upto2 = _tri2(lambda r, c: r <= c)
        before2 = _tri2(lambda r, c: r < c)
        last = (i * tq + tq - 1) // ts

        def span(j, carry, masked):
            dq, lk_seen, e_seen = carry
            off = pl.multiple_of(j * ts, ts)
            ks = k_ref[pl.ds(off, ts), :]
            pieces = _sb_logits(qb, ks, off, i * tq, masked, trim=True)
            dw = _dg(dyb, v_ref[pl.ds(off, ts), :], NT)
            upto = _tri_sums([pc[0] for pc in pieces], upto2)
            ws, es = [], []
            for b in range(nb):
                drop, ls, mask, start = pieces[b]
                later = (_rows_from(totb, start) - upto[b]) - _rows_from(lk_seen, start)
                w = _keep(mask, jnp.exp2(ls - later))
                ws.append(_pad_top(w.astype(BF16), start))
                es.append(_rows_from(dw, start)[:, b * CHUNK:(b + 1) * CHUNK] * w)
                lk_seen = lk_seen + _pad_top(jnp.sum(drop, axis=1, keepdims=True), start)
            before = _tri_sums(es, before2)
            dzs = []
            for b in range(nb):
                _, ls, mask, start = pieces[b]
                sig = jnp.exp2(ls)
                dlk = _rows_from(e_seen, start) + before[b]
                dzs.append(_pad_top(_keep(mask, es[b] - (es[b] + dlk) * sig).astype(BF16), start))
                e_seen = e_seen + _pad_top(jnp.sum(es[b], axis=1, keepdims=True), start)
            dz = jnp.concatenate(dzs, axis=1)
            dq = dq + _dg(dz, ks, NN)
            dkt_acc[j] += _dg(q_t, dz, NN)
            dvt_acc[j] += _dg(dy_t, jnp.concatenate(ws, axis=1), NN)
            return dq, lk_seen, e_seen

        zero_col = jnp.zeros((tq, 1), F32)
        carry = lax.fori_loop(0, last, lambda j, c: span(j, c, False), (jnp.zeros((tq, LANES), F32), zero_col, zero_col))
        dq, _, _ = span(last, carry, True)
        dq_ref[...] = (dq * SB_SCALE).astype(dq_ref.dtype)

        @pl.when(i == nq - 1)
        def _():
            for s in range(ns):
                dk_ref[s * ts:(s + 1) * ts, :] = (dkt_acc[s].T * SB_SCALE).astype(dk_ref.dtype)
                dv_ref[s * ts:(s + 1) * ts, :] = dvt_acc[s].T.astype(dv_ref.dtype)

    qsp = pl.BlockSpec((tq, LANES), lambda h, i: (i, h))
    kvsp = pl.BlockSpec((t, LANES), lambda h, i: (0, h))
    return _hosted(
        body, rider, name=name, grid=(SB_HEADS, nq),
        in_specs=[qsp, kvsp, kvsp, qsp, qsp], out_specs=[qsp, kvsp, kvsp],
        out_shape=[jax.ShapeDtypeStruct((t, D_MODEL), BF16)] * 3,
        scratch_shapes=[pltpu.VMEM((ns, LANES, ts), F32), pltpu.VMEM((ns, LANES, ts), F32)],
        args=(q, k, v, tot, dy))


def _attn_gate(y, g):
    return (y * _silu(g),)


def _loss_head(name, h, target, w, tm):
    t, d = h.shape

    def body(h_ref, t_ref, w_ref, dh_ref, dw_ref, loss_ref):
        tgt = t_ref[...]

        def f(hv, wv):
            e = _rms(hv, wv) - tgt
            return 0.5 * jnp.mean(e * e, axis=-1, keepdims=True)

        row_loss, vjp = jax.vjp(f, h_ref[...], w_ref[...])
        dh, dw = vjp(jnp.ones_like(row_loss))
        dh_ref[...] = dh

        @pl.when(pl.program_id(0) == 0)
        def _():
            dw_ref[...] = jnp.zeros_like(dw_ref)
            loss_ref[...] = jnp.zeros_like(loss_ref)

        dw_ref[...] += dw
        loss_ref[...] += jnp.sum(row_loss, axis=0, keepdims=True)

    return pl.pallas_call(
        body, name=name, grid=(t // tm,),
        in_specs=[_row_spec(d, tm), _row_spec(d, tm), _full_spec(w)],
        out_specs=[_row_spec(d, tm), _full_spec(w), pl.BlockSpec((1, 1), lambda i: (0, 0))],
        out_shape=[jax.ShapeDtypeStruct((t, d), F32), jax.ShapeDtypeStruct(w.shape, F32),
                   jax.ShapeDtypeStruct((1, 1), F32)],
        compiler_params=_cparams("arbitrary"),
    )(h, target, w)


def _pick_tile(rows, cap):
    if rows <= cap:
        return rows
    for tm in range(cap - cap % 16, 0, -16):
        if rows % tm == 0:
            return tm
    raise ValueError(rows)


def _adamw(name, w, g, m, v, tm):
    n_l, r, c = w.shape
    tm = _pick_tile(r, tm)

    def body(w_ref, g_ref, m_ref, v_ref, d_ref, nm_ref, nv_ref):
        g_ = g_ref[...]
        m_ = ADAM_B1 * m_ref[...] + (1.0 - ADAM_B1) * g_
        v_ = ADAM_B2 * v_ref[...] + (1.0 - ADAM_B2) * (g_ * g_)
        m_hat = m_ / (1.0 - ADAM_B1 ** ADAM_STEP)
        v_hat = v_ / (1.0 - ADAM_B2 ** ADAM_STEP)
        d_ref[...] = -ADAM_LR * (m_hat / (jnp.sqrt(v_hat) + ADAM_EPS) + ADAM_WD * w_ref[...])
        nm_ref[...] = m_
        nv_ref[...] = v_

    spec = pl.BlockSpec((1, tm, c), lambda l, i: (l, i, 0))
    return pl.pallas_call(
        body, name=name, grid=(n_l, r // tm), in_specs=[spec] * 4, out_specs=[spec] * 3,
        out_shape=[jax.ShapeDtypeStruct((n_l, r, c), F32)] * 3, compiler_params=_cparams("parallel", "parallel"),
    )(w, g, m, v)


def _sum_parts(name, layers, tm):
    n, r, c = layers[0].shape
    tm = _pick_tile(r, tm)
    nblk = r // tm

    def body(*refs):
        o_ref = refs[-1]
        for li, p_ref in enumerate(refs[:-1]):
            @pl.when(pl.program_id(0) == li)
            def _(p_ref=p_ref):
                s = p_ref[0].astype(F32)
                for d in range(1, n):
                    s = s + p_ref[d].astype(F32)
                o_ref[0] = s

    def spec(li):
        return pl.BlockSpec((n, tm, c), lambda l, i: (0, jnp.where(l == li, i, jnp.where(l < li, 0, nblk - 1)), 0))

    return pl.pallas_call(
        body, name=name, grid=(len(layers), nblk),
        in_specs=[spec(li) for li in range(len(layers))],
        out_specs=pl.BlockSpec((1, tm, c), lambda l, i: (l, i, 0)),
        out_shape=jax.ShapeDtypeStruct((len(layers), r, c), F32), compiler_params=_cparams("arbitrary", "arbitrary"),
    )(*layers)


def _peer(k):
    x, y, c = lax.axis_index("x"), lax.axis_index("y"), lax.axis_index("c")
    px, py, pc = x ^ ((k >> 2) & 1), y ^ ((k >> 1) & 1), c ^ (k & 1)
    return (px, py, pc), 4 * px + 2 * py + pc


def _exchange(name, xs, gather):
    rider = _Rider(xs, gather)
    n = rider.n

    def body(*refs):
        x_refs, out_refs, sems = refs[:n], refs[n:2 * n], refs[2 * n:]
        _exchange_start(x_refs, out_refs, sems, gather)
        _exchange_finish(x_refs, out_refs, sems, gather)

    return pl.pallas_call(
        body, name=name, in_specs=[ANY_SPEC] * n, out_specs=[ANY_SPEC] * n,
        out_shape=rider.out_shape(), scratch_shapes=rider.scratch(),
    )(*xs)


ANY_SPEC = pl.BlockSpec(memory_space=pl.ANY)
SAME_CORE = (2, 4, 6)


class _Rider:
    def __init__(self, xs, gather):
        self.xs, self.gather, self.n = list(xs), gather, len(xs)

    def out_shape(self):
        return [jax.ShapeDtypeStruct((N_DEV,) + tuple(x.shape if self.gather else x.shape[1:]), x.dtype)
                for x in self.xs]

    def scratch(self):
        return [pltpu.SemaphoreType.DMA((self.n, N_DEV - 1)), pltpu.SemaphoreType.DMA((self.n, N_DEV - 1)),
                pltpu.SemaphoreType.DMA((self.n,))]


def _remote(src, dst, sems, a, k, dev):
    return pltpu.make_async_remote_copy(
        src_ref=src, dst_ref=dst, send_sem=sems[0].at[a, k - 1], recv_sem=sems[1].at[a, k - 1],
        device_id=dev, device_id_type=pl.DeviceIdType.MESH)


def _exchange_start(x_refs, out_refs, sems, gather):
    _, me = _peer(0)
    for a, (x, out) in enumerate(zip(x_refs, out_refs)):
        pltpu.make_async_copy(x if gather else x.at[me], out.at[me], sems[2].at[a]).start()
    for k in ((1,) + SAME_CORE if gather else range(1, N_DEV)):
        dev, idx = _peer(k)
        for a, (x, out) in enumerate(zip(x_refs, out_refs)):
            _remote(x if gather else x.at[idx], out.at[me], sems, a, k, dev).start()


def _exchange_finish(x_refs, out_refs, sems, gather):
    _, me = _peer(0)
    sibling, _ = _peer(1)
    pairs = list(enumerate(zip(x_refs, out_refs)))
    waited = ()
    if gather:
        for k in SAME_CORE:
            dev, idx = _peer(k)
            for a, (x, out) in pairs:
                _remote(x, out.at[idx], sems, a, k, dev).wait_recv()
                _remote(out.at[idx], out.at[idx], sems, a, k + 1, sibling).start()
        waited = SAME_CORE
    for k in range(1, N_DEV):
        dev, idx = _peer(k)
        for a, (x, out) in pairs:
            cp = _remote(x if gather else x.at[idx], out.at[idx], sems, a, k, dev)
            if k not in waited:
                cp.wait_recv()
            cp.wait_send()
    for a, (x, out) in pairs:
        pltpu.make_async_copy(x if gather else x.at[me], out.at[me], sems[2].at[a]).wait()


def _hosted(body, rider, *, name, grid, in_specs, out_specs, out_shape, scratch_shapes, args):
    sem = ("arbitrary",) * len(grid)
    if rider is None:
        return pl.pallas_call(body, name=name, grid=grid, in_specs=in_specs, out_specs=out_specs, out_shape=out_shape,
                              scratch_shapes=scratch_shapes, compiler_params=_cparams(*sem))(*args), []
    n_in, n_out, n_scr, nr = len(in_specs), len(out_specs), len(scratch_shapes), rider.n

    def hosted(*refs):
        ins, refs = refs[:n_in], refs[n_in:]
        r_in, refs = refs[:nr], refs[nr:]
        outs, refs = refs[:n_out], refs[n_out:]
        r_out, refs = refs[:nr], refs[nr:]
        scr, sems = refs[:n_scr], refs[n_scr:]
        first = pl.program_id(0) == 0
        last = pl.program_id(0) == grid[0] - 1
        for ax in range(1, len(grid)):
            first = first & (pl.program_id(ax) == 0)
            last = last & (pl.program_id(ax) == grid[ax] - 1)

        @pl.when(first)
        def _():
            _exchange_start(r_in, r_out, sems, rider.gather)

        body(*ins, *outs, *scr)

        @pl.when(last)
        def _():
            _exchange_finish(r_in, r_out, sems, rider.gather)

    res = pl.pallas_call(
        hosted, name=name, grid=grid, in_specs=list(in_specs) + [ANY_SPEC] * nr,
        out_specs=list(out_specs) + [ANY_SPEC] * nr, out_shape=list(out_shape) + rider.out_shape(),
        scratch_shapes=list(scratch_shapes) + rider.scratch(), compiler_params=_cparams(*sem),
    )(*args, *rider.xs)
    return list(res[:n_out]), list(res[n_out:])


PACK_ALIGN = 8 * LANES
PACK_ROWS = 512


def _pack(arrays, dtype):
    pieces, total = [], 0
    for a in arrays:
        f = a.reshape(-1).astype(dtype)
        pad = (-f.shape[0]) % PACK_ALIGN
        pieces.append(jnp.pad(f, (0, pad)) if pad else f)
        total += f.shape[0] + pad
    tail = (-total) % (PACK_ROWS * LANES)
    if tail:
        pieces.append(jnp.zeros((tail,), dtype))
    return jnp.concatenate(pieces).reshape(-1, LANES)


def _unpack(packed, shapes, lead=()):
    flat = packed.reshape(lead + (-1,))
    out, off = [], 0
    for s in shapes:
        n = 1
        for d in s:
            n *= d
        out.append(flat[..., off:off + n].reshape(lead + tuple(s)))
        off += n + ((-n) % PACK_ALIGN)
    return out


def _pad_lanes(a):
    return jnp.pad(a, (0, LANES - a.shape[0])).reshape(1, LANES)


ROW_TM = 256
EVEN_SEGS = (("z", D_MODEL), ("xbc", CONV_DIM), ("dt", SSD_HEADS), ("g", D_MODEL), ("u", D_MODEL), ("v", D_MODEL))
ODD_SEGS = (("q", D_MODEL), ("k", D_MODEL), ("v", D_MODEL), ("g", D_MODEL))
TAIL_ROWS = (368, 512, 64, 368, 368, 368)


def _split_cols(w, segs):
    out, off = {}, 0
    for nm, n in segs:
        out[nm] = w[:, off:off + n]
        off += n
    return out


def _rms_fn(h, w):
    return (_rms(h, w),)


def _even_fwd(tag, h, p, riders=None, late=None):
    riders = riders or {}
    got = {}
    hn, = _rows_fwd(tag + "_norm", _rms_fn, [h], [p["norm_w"]], [(D_MODEL, BF16)], ROW_TM)
    proj = {nm: _matmul(f"{tag}_in_{nm}", hn, p["w_in"][nm], NN, F32, rider=riders.get("in_" + nm))
            for nm, _ in EVEN_SEGS}
    for nm, _ in EVEN_SEGS:
        if "in_" + nm in riders:
            proj[nm], got["in_" + nm] = proj[nm]
    (pre, xs, bm, cm), got["conv"] = _conv_fwd(tag + "_conv", proj["xbc"], p["conv_w"], p["conv_b"], riders.get("conv"))
    (y_ssd, states), got["ssd"] = _ssd_fwd(tag + "_ssd", xs, bm, cm, proj["dt"], p["dt_bias"], p["a_log"],
                                           p["d_skip"], riders.get("ssd"))
    if late is not None:
        p = dict(p, **late(got))
    ya, = _rows_fwd(tag + "_ssdgate", _ssd_gate, [y_ssd, proj["z"]], [p["ssd_norm_w"]], [(D_MODEL, BF16)], ROW_TM)
    vn, = _rows_fwd(tag + "_sgunorm", _sgu_norm, [proj["v"]], [p["sgu_ln_w"], p["sgu_ln_b"]], [(D_MODEL, F32)], ROW_TM)
    yb = _sgu_fwd(tag + "_sgu", proj["u"], proj["g"], vn, p["sgu_w"], p["sgu_b"])
    h1 = _matmul(tag + "_out_a", ya, p["w_out_a"], NN, F32, add=h)
    h2 = _matmul(tag + "_out_b", yb, p["w_out_b"], NN, F32, add=h1)
    saved = dict(h=h, hn=hn, proj=proj, pre=pre, xs=xs, bm=bm, cm=cm, y_ssd=y_ssd, states=states,
                 ya=ya, vn=vn, yb=yb)
    return h2, saved, p, got


def _even_bwd(tag, dh, s, p, riders=None):
    riders = riders or {}
    g, got = {}, {}

    def rider_for(host):
        return riders[host](g) if host in riders else None

    dh16 = dh.astype(BF16)
    proj = s["proj"]
    dya = _matmul(tag + "_dya", dh16, p["w_out_a"], NT, F32)
    dyb = _matmul(tag + "_dyb", dh16, p["w_out_b"], NT, F32)
    g["w_out_a"] = _matmul(tag + "_dwout_a", s["ya"], dh16, TN, BF16)
    g["w_out_b"] = _matmul(tag + "_dwout_b", s["yb"], dh16, TN, BF16)
    (du, dg, dvn, g["sgu_w"], g["sgu_b"]), got["sgu_b"] = _sgu_bwd(
        tag + "_sgu_b", proj["u"], proj["g"], s["vn"], p["sgu_w"], p["sgu_b"], dyb, rider_for("sgu_b"))
    dv, g["sgu_ln_w"], g["sgu_ln_b"] = _rows_bwd(tag + "_sgunorm_b", _sgu_norm, [proj["v"]],
                                                 [p["sgu_ln_w"], p["sgu_ln_b"]], [dvn], [BF16], ROW_TM // 2)
    dy_ssd, dz, g["ssd_norm_w"] = _rows_bwd(tag + "_ssdgate_b", _ssd_gate, [s["y_ssd"], proj["z"]],
                                            [p["ssd_norm_w"]], [dya], [F32, BF16], ROW_TM // 2)
    (dxs, dbm, dcm, ddt, g["dt_bias"], g["a_log"], g["d_skip"]), got["ssd_b"] = _ssd_bwd(
        tag + "_ssd_b", s["xs"], s["bm"], s["cm"], proj["dt"], p["dt_bias"], p["a_log"], p["d_skip"],
        s["states"], dy_ssd, rider_for("ssd_b"))
    (dxbc, g["conv_w"], g["conv_b"]), got["conv_b"] = _conv_bwd(
        tag + "_conv_b", proj["xbc"], s["pre"], p["conv_w"], dxs, dbm, dcm, rider_for("conv_b"))
    dproj = dict(z=dz, xbc=dxbc, dt=ddt.astype(BF16), g=dg, u=du, v=dv)
    g["w_in"] = {nm: _matmul(f"{tag}_dwin_{nm}", s["hn"], dproj[nm], TN, BF16) for nm, _ in EVEN_SEGS}
    tail = rider_for("tail") or [None] * len(EVEN_SEGS)
    dhn, got["tail"] = None, []
    for (nm, _), rider in zip(EVEN_SEGS, tail):
        dhn = _matmul(f"{tag}_dhn_{nm}", dproj[nm], p["w_in"][nm], NT, F32, add=dhn, rider=rider)
        if rider is not None:
            dhn, arrived = dhn
            got["tail"] += arrived
    dh_in, g["norm_w"] = _rows_bwd(tag + "_norm_b", _rms_fn, [s["h"]], [p["norm_w"]], [dhn], [F32],
                                   ROW_TM // 2, add=(0, dh))
    return dh_in, g, got


def _odd_fwd(tag, h, p, rider=None):
    hn, = _rows_fwd(tag + "_norm", _rms_fn, [h], [p["norm_w"]], [(D_MODEL, BF16)], ROW_TM)
    q = _matmul(tag + "_in_q", hn, p["w_in"]["q"], NN, BF16)
    k = _matmul(tag + "_in_k", hn, p["w_in"]["k"], NN, BF16)
    v = _matmul(tag + "_in_v", hn, p["w_in"]["v"], NN, BF16)
    gate = _matmul(tag + "_in_g", hn, p["w_in"]["g"], NN, F32)
    (y, tot), got = _attn_fwd(tag + "_attn", q, k, v, rider)
    yg, = _rows_fwd(tag + "_gate", _attn_gate, [y, gate], [], [(D_MODEL, BF16)], ROW_TM)
    h1 = _matmul(tag + "_out", yg, p["w_out"], NN, F32, add=h)
    return h1, dict(h=h, hn=hn, q=q, k=k, v=v, gate=gate, y=y, tot=tot, yg=yg), got


def _odd_bwd(tag, dh, s, p, rider=None):
    g = {}
    dh16 = dh.astype(BF16)
    dyg = _matmul(tag + "_dyg", dh16, p["w_out"], NT, F32)
    g["w_out"] = _matmul(tag + "_dwout", s["yg"], dh16, TN, BF16)
    dy, dgate = _rows_bwd(tag + "_gate_b", _attn_gate, [s["y"], s["gate"]], [], [dyg], [F32, BF16], ROW_TM)
    (dq, dk, dv), got = _attn_bwd(tag + "_attn_b", s["q"], s["k"], s["v"], s["tot"], dy, rider)
    dproj = dict(q=dq, k=dk, v=dv, g=dgate)
    dhn = None
    g["w_in"] = {}
    for nm, _ in ODD_SEGS:
        dhn = _matmul(f"{tag}_dhn_{nm}", dproj[nm], p["w_in"][nm], NT, F32, add=dhn)
        g["w_in"][nm] = _matmul(f"{tag}_dwin_{nm}", s["hn"], dproj[nm], TN, BF16)
    dh_in, g["norm_w"] = _rows_bwd(tag + "_norm_b", _rms_fn, [s["h"]], [p["norm_w"]], [dhn], [F32],
                                   ROW_TM // 2, add=(0, dh))
    return dh_in, g, got


BIG = ("ev_w_in", "ev_w_out", "od_w_in", "od_w_out")
SMALL = ("norm_w", "final_norm_w", "ev_conv_b", "ev_dt_bias", "ev_a_log", "ev_d_skip", "ev_ssd_norm_w",
         "ev_sgu_ln_w", "ev_sgu_ln_b", "ev_sgu_w", "ev_sgu_b")
WEIGHTS = ("norm_w", "final_norm_w", "ev_w_in", "ev_conv_w", "ev_conv_b", "ev_dt_bias", "ev_a_log", "ev_d_skip",
           "ev_ssd_norm_w", "ev_sgu_ln_w", "ev_sgu_ln_b", "ev_sgu_w", "ev_sgu_b", "ev_w_out", "od_w_in", "od_w_out")


def _step(w, m, v, x, loss_target):
    h = x[0]
    tgt = loss_target[0]
    n_even, n_odd = w["ev_w_in"].shape[0], w["od_w_in"].shape[0]
    depth = n_even + n_odd
    assert (n_even, n_odd) == (2, 2), "the exchange schedule below is written for the four-layer trunk"

    def shard(n, i):
        return w[n][i].astype(BF16)

    def take_cols(blocks, widths, lo, hi, r0=0, r1=None):
        pieces, start = [], 0
        for blk, wd in zip(blocks, widths):
            a, b = max(lo - start, 0), min(hi - start, wd)
            if a < b:
                pieces.append(blk[r0:r1, a:b])
            start += wd
        return pieces[0] if len(pieces) == 1 else jnp.concatenate(pieces, axis=1)

    def segments(gathered, segs):
        n = gathered.shape[2]
        out, lo = {}, 0
        for nm, wd in segs:
            out[nm] = take_cols([gathered[d] for d in range(N_DEV)], [n] * N_DEV, lo, lo + wd)
            lo += wd
        return out

    def rows(gathered):
        return gathered.reshape(-1, gathered.shape[2])

    def by_owner_cols(g_in, segs, r0=0, r1=None):
        blocks, widths = [g_in[nm] for nm, _ in segs], [wd for _, wd in segs]
        n = sum(widths) // N_DEV
        return jnp.stack([take_cols(blocks, widths, d * n, (d + 1) * n, r0, r1) for d in range(N_DEV)])

    def by_owner_rows(full):
        return full.reshape(N_DEV, full.shape[0] // N_DEV, full.shape[1])

    def even_params(layer, ev_in):
        i = layer // 2
        w_in = segments(ev_in, EVEN_SEGS)
        w_in["dt"] = jnp.pad(w_in["dt"], ((0, 0), (0, LANES - SSD_HEADS)))
        return dict(norm_w=w["norm_w"][layer][None], w_in=w_in, conv_w=conv_w[i], conv_b=w["ev_conv_b"][i][None],
                    dt_bias=_pad_lanes(w["ev_dt_bias"][i]), a_log=_pad_lanes(w["ev_a_log"][i]),
                    d_skip=_pad_lanes(w["ev_d_skip"][i]), ssd_norm_w=w["ev_ssd_norm_w"][i][None],
                    sgu_ln_w=w["ev_sgu_ln_w"][i][None], sgu_ln_b=w["ev_sgu_ln_b"][i][None],
                    sgu_w=w["ev_sgu_w"][i], sgu_b=w["ev_sgu_b"][i][:, :, None])

    def even_out_params(ev_out):
        full = rows(ev_out)
        return dict(w_out_a=full[:D_MODEL], w_out_b=full[D_MODEL:])

    def odd_params(layer, od_in, od_out):
        return dict(norm_w=w["norm_w"][layer][None], w_in=segments(od_in, ODD_SEGS), w_out=rows(od_out))

    def even_in_grads(g):
        return by_owner_cols(g["w_in"], EVEN_SEGS)

    def even_out_grads(g):
        return by_owner_rows(jnp.concatenate([g["w_out_a"], g["w_out_b"]], axis=0))

    def odd_grads(g):
        return [by_owner_cols(g["w_in"], ODD_SEGS), by_owner_rows(g["w_out"])]

    half = w["ev_w_out"].shape[1] // 2
    ev_in0, conv_w = _exchange("gather_first", [shard("ev_w_in", 0), w["ev_conv_w"]], gather=True)
    conv_w = jnp.moveaxis(conv_w, 0, 2).reshape(n_even, CONV_WIDTH, CONV_DIM)
    h, s0, p0, got = _even_fwd(
        "l0", h, even_params(0, ev_in0),
        {"in_xbc": _Rider([shard("ev_w_out", 0)[:half]], True),
         "conv": _Rider([shard("ev_w_out", 0)[half:]], True),
         "ssd": _Rider([shard("od_w_in", 0), shard("od_w_out", 0)], True)},
        late=lambda arrived: even_out_params(jnp.concatenate([arrived["in_xbc"][0], arrived["conv"][0]], axis=1)))
    p1 = odd_params(1, *got["ssd"])
    h, s1, got = _odd_fwd("l1", h, p1, _Rider([shard("ev_w_in", 1), shard("ev_w_out", 1)], True))
    p2 = dict(even_params(2, got[0]), **even_out_params(got[1]))
    h, s2, p2, got = _even_fwd("l2", h, p2, {"ssd": _Rider([shard("od_w_in", 1), shard("od_w_out", 1)], True)})
    p3 = odd_params(3, *got["ssd"])
    h, s3, _ = _odd_fwd("l3", h, p3)

    dh, d_final, loss_part = _loss_head("loss_head", h, tgt, w["final_norm_w"][None], ROW_TM // 2)
    loss = lax.psum(loss_part[0, 0], ("x", "y", "c"))

    def last_grads(g):
        bounds = [0]
        for rows_here in TAIL_ROWS:
            bounds.append(bounds[-1] + rows_here)
        assert bounds[-1] == D_MODEL
        return [_Rider([by_owner_cols(g["w_in"], EVEN_SEGS, r0, r1)], False) for r0, r1 in zip(bounds, bounds[1:])]

    dh, g3, _ = _odd_bwd("l3", dh, s3, p3)
    dh, g2, got = _even_bwd("l2", dh, s2, p2, {"ssd_b": lambda g: _Rider(odd_grads(g3), False)})
    from3 = got["ssd_b"]
    dh, g1, from2 = _odd_bwd("l1", dh, s1, p1, _Rider([even_in_grads(g2), even_out_grads(g2)], False))
    dh, g0, got = _even_bwd("l0", dh, s0, p0, {
        "sgu_b": lambda g: _Rider([even_out_grads(g)[:, :half]], False),
        "ssd_b": lambda g: _Rider(odd_grads(g1), False),
        "conv_b": lambda g: _Rider([even_out_grads(g)[:, half:]], False),
        "tail": last_grads})
    from1 = got["ssd_b"] + [jnp.concatenate([got["sgu_b"][0], got["conv_b"][0]], axis=1)]
    from0 = jnp.concatenate(got["tail"], axis=1)
    grad_x = dh[None]
    lg = [g0, g1, g2, g3]
    ev, od = [g0, g2], [g1, g3]

    def total(tag, parts):
        return _sum_parts("sum_" + tag, parts, 256)

    big_grads = {
        "ev_w_in": total("ev_w_in", [from0, from2[0]]),
        "ev_w_out": total("ev_w_out", [from1[2], from2[1]]),
        "od_w_in": total("od_w_in", [from1[0], from3[0]]),
        "od_w_out": total("od_w_out", [from1[1], from3[1]]),
    }

    small_g = {
        "norm_w": jnp.concatenate([lg[l]["norm_w"] for l in range(depth)], axis=0),
        "final_norm_w": d_final[0],
        "ev_conv_b": jnp.concatenate([e["conv_b"] for e in ev], axis=0),
        "ev_dt_bias": jnp.concatenate([e["dt_bias"][:, :SSD_HEADS] for e in ev], axis=0),
        "ev_a_log": jnp.concatenate([e["a_log"][:, :SSD_HEADS] for e in ev], axis=0),
        "ev_d_skip": jnp.concatenate([e["d_skip"][:, :SSD_HEADS] for e in ev], axis=0),
        "ev_ssd_norm_w": jnp.concatenate([e["ssd_norm_w"] for e in ev], axis=0),
        "ev_sgu_ln_w": jnp.concatenate([e["sgu_ln_w"] for e in ev], axis=0),
        "ev_sgu_ln_b": jnp.concatenate([e["sgu_ln_b"] for e in ev], axis=0),
        "ev_sgu_w": jnp.stack([e["sgu_w"] for e in ev]),
        "ev_sgu_b": jnp.stack([e["sgu_b"][:, :, 0] for e in ev]),
    }
    conv_g = jnp.stack([e["conv_w"] for e in ev])
    small_shapes = [w[n].shape for n in SMALL]
    small_parts, = _exchange("gather_small", [_pack([small_g[n] for n in SMALL] + [conv_g], F32)], gather=True)
    small_sum = _sum_parts("sum_small", [small_parts], 1024)[0]
    *small_list, conv_full = _unpack(small_sum, small_shapes + [conv_g.shape])
    grads = dict(zip(SMALL, small_list))
    grads.update(big_grads)
    me = 4 * lax.axis_index("x") + 2 * lax.axis_index("y") + lax.axis_index("c")
    n_cv = w["ev_conv_w"].shape[2]
    grads["ev_conv_w"] = lax.dynamic_slice_in_dim(conv_full, me * n_cv, n_cv, axis=2)

    deltas, new_m, new_v = {}, {}, {}
    for n in BIG + ("ev_conv_w",):
        deltas[n], new_m[n], new_v[n] = _adamw("adamw_" + n, w[n], grads[n], m[n], v[n], 256)
    packs = [_pack([src[n] for n in SMALL], F32)[None] for src in (w, grads, m, v)]
    outs = _adamw("adamw_small", *packs, 1024)
    for dst, packed in zip((deltas, new_m, new_v), outs):
        dst.update(zip(SMALL, _unpack(packed[0], small_shapes)))
    return loss, grad_x, grads, deltas, new_m, new_v


def kernel(x, norm_w, final_norm_w, ev_w_in, ev_conv_w, ev_conv_b, ev_dt_bias, ev_a_log, ev_d_skip, ev_ssd_norm_w, ev_sgu_ln_w, ev_sgu_ln_b, ev_sgu_w, ev_sgu_b, ev_w_out, od_w_in, od_w_out, loss_target, m_norm_w, m_final_norm_w, m_ev_w_in, m_ev_conv_w, m_ev_conv_b, m_ev_dt_bias, m_ev_a_log, m_ev_d_skip, m_ev_ssd_norm_w, m_ev_sgu_ln_w, m_ev_sgu_ln_b, m_ev_sgu_w, m_ev_sgu_b, m_ev_w_out, m_od_w_in, m_od_w_out, v_norm_w, v_final_norm_w, v_ev_w_in, v_ev_conv_w, v_ev_conv_b, v_ev_dt_bias, v_ev_a_log, v_ev_d_skip, v_ev_ssd_norm_w, v_ev_sgu_ln_w, v_ev_sgu_ln_b, v_ev_sgu_w, v_ev_sgu_b, v_ev_w_out, v_od_w_in, v_od_w_out):
    w = dict(zip(WEIGHTS, (norm_w, final_norm_w, ev_w_in, ev_conv_w, ev_conv_b, ev_dt_bias, ev_a_log, ev_d_skip,
                           ev_ssd_norm_w, ev_sgu_ln_w, ev_sgu_ln_b, ev_sgu_w, ev_sgu_b, ev_w_out, od_w_in, od_w_out)))
    m = dict(zip(WEIGHTS, (m_norm_w, m_final_norm_w, m_ev_w_in, m_ev_conv_w, m_ev_conv_b, m_ev_dt_bias, m_ev_a_log,
                           m_ev_d_skip, m_ev_ssd_norm_w, m_ev_sgu_ln_w, m_ev_sgu_ln_b, m_ev_sgu_w, m_ev_sgu_b,
                           m_ev_w_out, m_od_w_in, m_od_w_out)))
    v = dict(zip(WEIGHTS, (v_norm_w, v_final_norm_w, v_ev_w_in, v_ev_conv_w, v_ev_conv_b, v_ev_dt_bias, v_ev_a_log,
                           v_ev_d_skip, v_ev_ssd_norm_w, v_ev_sgu_ln_w, v_ev_sgu_ln_b, v_ev_sgu_w, v_ev_sgu_b,
                           v_ev_w_out, v_od_w_in, v_od_w_out)))
    loss, grad_x, grads, deltas, new_m, new_v = _step(w, m, v, x, loss_target)
    return (loss, grad_x, *[grads[n] for n in WEIGHTS], *[deltas[n] for n in WEIGHTS],
            *[new_m[n] for n in WEIGHTS], *[new_v[n] for n in WEIGHTS])
```

```python
import jax
import jax.numpy as jnp
from jax import lax
from jax.experimental import pallas as pl
from jax.experimental.pallas import tpu as pltpu

F32, BF16 = jnp.float32, jnp.bfloat16

D_MODEL = 2048
SSD_HEADS = 32
SSD_HEAD_DIM = 64
SSD_GROUPS = 4
SSD_STATE = 128
CHUNK = 128
CONV_WIDTH = 4
CONV_DIM = D_MODEL + 2 * SSD_GROUPS * SSD_STATE
SGU_GROUPS = 16
SB_HEADS = 16
LANES = 128
N_PAIRS = SSD_HEADS // 2
PAIRS_PER_GROUP = N_PAIRS // SSD_GROUPS
NORM_EPS = 1e-5
N_DEV = 8

ADAM_LR, ADAM_B1, ADAM_B2, ADAM_EPS, ADAM_WD, ADAM_STEP = 0.001, 0.9, 0.999, 1e-08, 0.01, 10

VMEM_LIMIT_BYTES = 48 * 1024 * 1024

NN = ((1,), (0,))
NT = ((1,), (1,))
TN = ((0,), (0,))


def _cparams(*sem):
    return pltpu.CompilerParams(dimension_semantics=sem, vmem_limit_bytes=VMEM_LIMIT_BYTES)


def _dg(a, b, dims):
    return lax.dot_general(a, b, (dims, ((), ())), preferred_element_type=F32)


def _make_bdot(dims):
    @jax.custom_vjp
    def f(a, b):
        return _dg(a.astype(BF16), b.astype(BF16), dims)

    def fwd(a, b):
        return f(a, b), (a, b)

    def bwd(res, g):
        a, b = res
        a16, b16, g16 = a.astype(BF16), b.astype(BF16), g.astype(BF16)
        if dims == NN:
            da, db = _dg(g16, b16, NT), _dg(a16, g16, TN)
        elif dims == NT:
            da, db = _dg(g16, b16, NN), _dg(g16, a16, TN)
        else:
            da, db = _dg(b16, g16, NT), _dg(a16, g16, NN)
        return da.astype(a.dtype), db.astype(b.dtype)

    f.defvjp(fwd, bwd)
    return f


_bdot_nn, _bdot_nt, _bdot_tn = _make_bdot(NN), _make_bdot(NT), _make_bdot(TN)


def _iota2(shape, axis):
    return lax.broadcasted_iota(jnp.int32, shape, axis)


def _split3(x, axis):
    hi = x.astype(BF16)
    r = x - hi.astype(F32)
    mid = r.astype(BF16)
    return jnp.concatenate([hi, mid, (r - mid.astype(F32)).astype(BF16)], axis=axis)


def _make_onehot_dot(build, build_t, left):
    def apply(x, e):
        e = e.astype(BF16)
        if left:
            return _dg(jnp.concatenate([e, e, e], axis=1), _split3(x, 0), NN)
        return _dg(_split3(x, 1), jnp.concatenate([e, e, e], axis=0), NN)

    @jax.custom_vjp
    def f(x):
        return apply(x, build())

    f.defvjp(lambda x: (f(x), None), lambda _, g: (apply(g, build_t()),))
    return f


_cumsum_rows = _make_onehot_dot(lambda: _iota2((CHUNK, CHUNK), 1) <= _iota2((CHUNK, CHUNK), 0),
                                lambda: _iota2((CHUNK, CHUNK), 0) <= _iota2((CHUNK, CHUNK), 1), True)


def _softplus(x):
    return jnp.maximum(x, 0.0) + jnp.log1p(jnp.exp(-jnp.abs(x)))


def _silu(x):
    return x * jax.nn.sigmoid(x)


def _gelu(x):
    return 0.5 * x * (1.0 + jnp.tanh(0.7978845608028654 * (x + 0.044715 * (x * x * x))))


def _rms(x, w):
    return x * lax.rsqrt(jnp.mean(x * x, axis=-1, keepdims=True) + NORM_EPS) * w


def _layer_norm(x, w, b):
    xc = x - jnp.mean(x, axis=-1, keepdims=True)
    return xc * lax.rsqrt(jnp.mean(xc * xc, axis=-1, keepdims=True) + NORM_EPS) * w + b


def _row_spec(width, tm):
    return pl.BlockSpec((tm, width), lambda i: (i, 0))


def _full_spec(p):
    zeros = (0,) * p.ndim
    return pl.BlockSpec(p.shape, lambda i: zeros)


def _rows_fwd(name, fn, tiled, params, outs, tm):
    n_rows = tiled[0].shape[0]
    n_in = len(tiled) + len(params)

    def body(*refs):
        res = fn(*[r[...] for r in refs[:n_in]])
        for o_ref, o in zip(refs[n_in:], res):
            o_ref[...] = o.astype(o_ref.dtype)

    return pl.pallas_call(
        body, name=name, grid=(n_rows // tm,),
        in_specs=[_row_spec(a.shape[1], tm) for a in tiled] + [_full_spec(p) for p in params],
        out_specs=[_row_spec(w, tm) for w, _ in outs],
        out_shape=[jax.ShapeDtypeStruct((n_rows, w), d) for w, d in outs],
        compiler_params=_cparams("parallel"),
    )(*tiled, *params)


def _rows_bwd(name, fn, tiled, params, cots, grad_dtypes, tm, add=None):
    n_rows = tiled[0].shape[0]
    nt, npar, nc = len(tiled), len(params), len(cots)
    want = [k for k, d in enumerate(grad_dtypes) if d is not None]
    n_add = 0 if add is None else 1

    def body(*refs):
        ins = [r[...] for r in refs[:nt + npar]]
        c_refs = refs[nt + npar:nt + npar + nc]
        add_refs = refs[nt + npar + nc:nt + npar + nc + n_add]
        o_refs = refs[nt + npar + nc + n_add:]
        res, vjp = jax.vjp(fn, *ins)
        grads = vjp(tuple(c[...].astype(r.dtype) for c, r in zip(c_refs, res)))
        for pos, k in enumerate(want):
            gk = grads[k]
            if add is not None and add[0] == pos:
                gk = gk + add_refs[0][...]
            o_refs[pos][...] = gk.astype(o_refs[pos].dtype)
        p_refs = o_refs[len(want):]

        @pl.when(pl.program_id(0) == 0)
        def _():
            for r in p_refs:
                r[...] = jnp.zeros_like(r)

        for r, gp in zip(p_refs, grads[nt:]):
            r[...] += gp

    add_arrays = [] if add is None else [add[1]]
    out = pl.pallas_call(
        body, name=name, grid=(n_rows // tm,),
        in_specs=([_row_spec(a.shape[1], tm) for a in tiled] + [_full_spec(p) for p in params]
                  + [_row_spec(c.shape[1], tm) for c in cots] + [_row_spec(a.shape[1], tm) for a in add_arrays]),
        out_specs=([_row_spec(tiled[k].shape[1], tm) for k in want] + [_full_spec(p) for p in params]),
        out_shape=([jax.ShapeDtypeStruct(tiled[k].shape, grad_dtypes[k]) for k in want]
                   + [jax.ShapeDtypeStruct(p.shape, F32) for p in params]),
        compiler_params=_cparams("arbitrary"),
    )(*tiled, *params, *cots, *add_arrays)
    return out


def _matmul(name, a, b, dims, out_dtype, add=None, rider=None, tm=1024, tn=1024, tk=2048):
    if dims == NN:
        (m, k), n = a.shape, b.shape[1]
    elif dims == NT:
        (m, k), n = a.shape, b.shape[0]
    else:
        (k, m), n = a.shape, b.shape[1]
    tm, tn, tk = min(tm, m), min(tn, n), min(tk, k)
    while k % tk:
        tk -= LANES
    assert m % tm == 0 and n % tn == 0 and k % tk == 0, (name, m, n, k)
    nk = k // tk
    a_spec = (pl.BlockSpec((tk, tm), lambda i, j, kk: (kk, i)) if dims == TN
              else pl.BlockSpec((tm, tk), lambda i, j, kk: (i, kk)))
    b_spec = (pl.BlockSpec((tn, tk), lambda i, j, kk: (j, kk)) if dims == NT
              else pl.BlockSpec((tk, tn), lambda i, j, kk: (kk, j)))
    o_spec = pl.BlockSpec((tm, tn), lambda i, j, kk: (i, j))
    has_add = add is not None

    def body(*refs):
        a_ref, b_ref = refs[0], refs[1]
        part = _dg(a_ref[...].astype(BF16), b_ref[...].astype(BF16), dims)
        if nk == 1:
            o_ref = refs[-1]
            if has_add:
                part = part + refs[2][...]
            o_ref[...] = part.astype(o_ref.dtype)
            return
        o_ref, acc = refs[-2], refs[-1]
        kk = pl.program_id(2)

        @pl.when(kk == 0)
        def _():
            acc[...] = part

        @pl.when(kk > 0)
        def _():
            acc[...] += part

        @pl.when(kk == nk - 1)
        def _():
            r = acc[...]
            if has_add:
                r = r + refs[2][...]
            o_ref[...] = r.astype(o_ref.dtype)

    in_specs = [a_spec, b_spec] + ([o_spec] if has_add else [])
    scratch = [pltpu.VMEM((tm, tn), F32)] if nk > 1 else []
    args = (a, b) + ((add,) if has_add else ())
    if rider is not None:
        (out,), got = _hosted(body, rider, name=name, grid=(m // tm, n // tn, nk), in_specs=in_specs,
                              out_specs=[o_spec], out_shape=[jax.ShapeDtypeStruct((m, n), out_dtype)],
                              scratch_shapes=scratch, args=args)
        return out, got
    return pl.pallas_call(
        body, name=name, grid=(m // tm, n // tn, nk), in_specs=in_specs, out_specs=o_spec,
        out_shape=jax.ShapeDtypeStruct((m, n), out_dtype), scratch_shapes=scratch,
        compiler_params=_cparams("parallel", "parallel", "arbitrary"),
    )(*args)


CONV_TM = 256
HALO = 8


def _shift_down(x, halo, j):
    if j == 0:
        return x, x[:HALO]
    xr = pltpu.roll(x, j, 0)
    hr = pltpu.roll(halo, j, 0)
    top = jnp.where(_iota2((HALO, x.shape[1]), 0) < j, hr, xr[:HALO])
    return xr, top


def _conv_fwd(name, x, w, b, rider=None):
    t, c = x.shape
    tm = min(CONV_TM, t)
    hb = tm // HALO

    def body(x_ref, halo_ref, w_ref, b_ref, pre_ref, xs_ref, bm_ref, cm_ref):
        i = pl.program_id(0)
        xv = x_ref[...]
        halo = jnp.where(i > 0, halo_ref[...], 0.0)
        main = jnp.zeros_like(xv) + b_ref[...]
        top = jnp.zeros((HALO, c), F32) + b_ref[...]
        for kk in range(CONV_WIDTH):
            xr, tp = _shift_down(xv, halo, CONV_WIDTH - 1 - kk)
            main = main + w_ref[kk:kk + 1, :] * xr
            top = top + w_ref[kk:kk + 1, :] * tp
        pre = jnp.concatenate([top, main[HALO:]], axis=0)
        pre_ref[...] = pre
        act = _silu(pre)
        xs_ref[...] = act[:, :D_MODEL]
        bm_ref[...] = act[:, D_MODEL:D_MODEL + SSD_GROUPS * SSD_STATE]
        cm_ref[...] = act[:, D_MODEL + SSD_GROUPS * SSD_STATE:]

    gs = SSD_GROUPS * SSD_STATE
    return _hosted(
        body, rider, name=name, grid=(t // tm,),
        in_specs=[_row_spec(c, tm),
                  pl.BlockSpec((HALO, c), lambda i: (jnp.maximum(i * hb - 1, 0), 0)),
                  _full_spec(w), _full_spec(b)],
        out_specs=[_row_spec(c, tm), _row_spec(D_MODEL, tm), _row_spec(gs, tm), _row_spec(gs, tm)],
        out_shape=[jax.ShapeDtypeStruct((t, c), F32), jax.ShapeDtypeStruct((t, D_MODEL), F32),
                   jax.ShapeDtypeStruct((t, gs), F32), jax.ShapeDtypeStruct((t, gs), F32)],
        scratch_shapes=[], args=(x, x, w, b))


def _dsilu(pre, dact):
    s = jax.nn.sigmoid(pre)
    return dact * (s * (1.0 + pre * (1.0 - s)))


def _conv_bwd(name, x, pre, w, dxs, dbm, dcm, rider=None):
    t, c = x.shape
    tm = min(CONV_TM, t)
    hb = tm // HALO
    last_hb = t // HALO - 1
    n_tiles = t // tm

    def body(x_ref, xh_ref, pre_ref, preh_ref, w_ref, dxs_ref, dbm_ref, dcm_ref,
             dxsh_ref, dbmh_ref, dcmh_ref, dx_ref, dw_ref, db_ref):
        i = pl.program_id(0)
        dact = jnp.concatenate([dxs_ref[...], dbm_ref[...], dcm_ref[...]], axis=1)
        dpre = _dsilu(pre_ref[...], dact)
        dact_h = jnp.concatenate([dxsh_ref[...], dbmh_ref[...], dcmh_ref[...]], axis=1)
        dpre_h = jnp.where(i < n_tiles - 1, _dsilu(preh_ref[...], dact_h), 0.0)
        xv = x_ref[...]
        xh = jnp.where(i > 0, xh_ref[...], 0.0)

        @pl.when(i == 0)
        def _():
            dw_ref[...] = jnp.zeros_like(dw_ref)
            db_ref[...] = jnp.zeros_like(db_ref)

        db_ref[...] += jnp.sum(dpre, axis=0, keepdims=True)
        dxm = jnp.zeros_like(xv)
        dxt = jnp.zeros((HALO, c), F32)
        row8 = _iota2((HALO, c), 0)
        for kk in range(CONV_WIDTH):
            j = CONV_WIDTH - 1 - kk
            wk = w_ref[kk:kk + 1, :]
            xr, tp = _shift_down(xv, xh, j)
            full = jnp.sum(dpre * xr, axis=0, keepdims=True)
            fix = jnp.sum(dpre[:HALO] * (tp - xr[:HALO]), axis=0, keepdims=True)
            dw_ref[kk:kk + 1, :] += full + fix
            if j == 0:
                dxm = dxm + wk * dpre
                dxt = dxt + wk * dpre[tm - HALO:]
            else:
                dr = pltpu.roll(dpre, tm - j, 0)
                hr = pltpu.roll(dpre_h, HALO - j, 0)
                dxm = dxm + wk * dr
                dxt = dxt + wk * jnp.where(row8 >= HALO - j, hr, dr[tm - HALO:])
        dx_ref[...] = jnp.concatenate([dxm[:tm - HALO], dxt], axis=0).astype(dx_ref.dtype)

    gs = SSD_GROUPS * SSD_STATE
    prev_halo = lambda i: (jnp.maximum(i * hb - 1, 0), 0)
    next_halo = lambda i: (jnp.minimum((i + 1) * hb, last_hb), 0)
    return _hosted(
        body, rider, name=name, grid=(n_tiles,),
        in_specs=[_row_spec(c, tm), pl.BlockSpec((HALO, c), prev_halo),
                  _row_spec(c, tm), pl.BlockSpec((HALO, c), next_halo), _full_spec(w),
                  _row_spec(D_MODEL, tm), _row_spec(gs, tm), _row_spec(gs, tm),
                  pl.BlockSpec((HALO, D_MODEL), next_halo), pl.BlockSpec((HALO, gs), next_halo),
                  pl.BlockSpec((HALO, gs), next_halo)],
        out_specs=[_row_spec(c, tm), _full_spec(w), pl.BlockSpec((1, c), lambda i: (0, 0))],
        out_shape=[jax.ShapeDtypeStruct((t, c), BF16), jax.ShapeDtypeStruct(w.shape, F32),
                   jax.ShapeDtypeStruct((1, c), F32)],
        scratch_shapes=[], args=(x, x, pre, pre, w, dxs, dbm, dcm, dxs, dbm, dcm))


def _ssd_prep(dt_raw, dt_bias, a_log):
    dt = _softplus(dt_raw + dt_bias)
    return dt, _cumsum_rows(dt * (-jnp.exp(a_log)))


def _head_col(x, h):
    return jnp.sum(jnp.where(_iota2(x.shape, 1) == h, x, 0.0), axis=1, keepdims=True)


def _ssd_pair(p, xs, dt, cs, d_skip, bg, cg, prev):
    l = xs.shape[0]
    first = _iota2((l, LANES), 1) < SSD_HEAD_DIM
    c0, c1 = _head_col(cs, 2 * p), _head_col(cs, 2 * p + 1)
    csf = jnp.where(first, c0, c1)
    dtf = jnp.where(first, _head_col(dt, 2 * p), _head_col(dt, 2 * p + 1))
    dskf = jnp.where(_iota2((1, LANES), 1) < SSD_HEAD_DIM, _head_col(d_skip, 2 * p), _head_col(d_skip, 2 * p + 1))
    cs_last = jnp.sum(jnp.where(_iota2((l, LANES), 0) == l - 1, csf, 0.0), axis=0, keepdims=True)
    xc = xs * dtf
    scores = _bdot_nt(cg, bg)
    causal = _iota2((l, l), 0) >= _iota2((l, l), 1)

    def decay(col):
        a = jnp.broadcast_to(col, (l, l))
        return jnp.where(causal, jnp.exp(jnp.where(causal, a - a.T, 0.0)), 0.0)

    y_diag = (_bdot_nn(scores * decay(c0), jnp.where(first, xc, 0.0))
              + _bdot_nn(scores * decay(c1), jnp.where(first, 0.0, xc)))
    states = _bdot_tn(bg, xc * jnp.exp(cs_last - csf))
    new_state = jnp.exp(cs_last) * prev + states
    y_off = _bdot_nn(cg, prev) * jnp.exp(csf)
    return y_diag + y_off + xs * dskf, new_state


def _ssd_specs(nc_rev=None):
    def ch(c):
        return c if nc_rev is None else nc_rev - 1 - c

    gs = SSD_GROUPS * SSD_STATE
    return dict(
        wide=pl.BlockSpec((CHUNK, D_MODEL), lambda c: (ch(c), 0)),
        group=pl.BlockSpec((CHUNK, gs), lambda c: (ch(c), 0)),
        chunk=pl.BlockSpec((CHUNK, LANES), lambda c: (ch(c), 0)),
        vec=pl.BlockSpec((1, LANES), lambda c: (0, 0)),
        state=pl.BlockSpec((1, N_PAIRS, LANES, LANES), lambda c: (ch(c), 0, 0, 0)),
    )


def _pair_slices(p):
    g = p // PAIRS_PER_GROUP
    return slice(p * LANES, (p + 1) * LANES), slice(g * LANES, (g + 1) * LANES)


def _ssd_fwd(name, xs, bm, cm, dt_raw, dt_bias, a_log, d_skip, rider=None):
    t = xs.shape[0]
    nc = t // CHUNK

    def body(xs_ref, b_ref, c_ref, dt_ref, bias_ref, alog_ref, dsk_ref, y_ref, prev_ref, state, dt_s, cs_s):
        c = pl.program_id(0)
        dt_s[...], cs_s[...] = _ssd_prep(dt_ref[...], bias_ref[...], alog_ref[...])

        @pl.when(c == 0)
        def _():
            state[...] = jnp.zeros_like(state)

        for p in range(N_PAIRS):
            sl, gsl = _pair_slices(p)
            prev = state[p]
            prev_ref[0, p] = prev
            y, new_state = _ssd_pair(p, xs_ref[:, sl], dt_s[...], cs_s[...], dsk_ref[...],
                                     b_ref[:, gsl], c_ref[:, gsl], prev)
            y_ref[:, sl] = y
            state[p] = new_state

    sp = _ssd_specs()
    return _hosted(
        body, rider, name=name, grid=(nc,),
        in_specs=[sp["wide"], sp["group"], sp["group"], sp["chunk"], sp["vec"], sp["vec"], sp["vec"]],
        out_specs=[sp["wide"], sp["state"]],
        out_shape=[jax.ShapeDtypeStruct((t, D_MODEL), F32),
                   jax.ShapeDtypeStruct((nc, N_PAIRS, LANES, LANES), F32)],
        scratch_shapes=[pltpu.VMEM((N_PAIRS, LANES, LANES), F32), pltpu.VMEM((CHUNK, LANES), F32),
                        pltpu.VMEM((CHUNK, LANES), F32)],
        args=(xs, bm, cm, dt_raw, dt_bias, a_log, d_skip))


def _ssd_bwd(name, xs, bm, cm, dt_raw, dt_bias, a_log, d_skip, prev_states, dy, rider=None):
    t = xs.shape[0]
    nc = t // CHUNK

    def body(xs_ref, b_ref, c_ref, dt_ref, bias_ref, alog_ref, dsk_ref, prev_ref, dy_ref,
             dxs_ref, db_ref, dc_ref, ddt_ref, dbias_ref, dalog_ref, ddsk_ref, dstate, dt_s, cs_s, g_dt, g_cs):
        c = pl.program_id(0)
        (dt_s[...], cs_s[...]), vjp_prep = jax.vjp(_ssd_prep, dt_ref[...], bias_ref[...], alog_ref[...])

        @pl.when(c == 0)
        def _():
            dstate[...] = jnp.zeros_like(dstate)
            dbias_ref[...] = jnp.zeros_like(dbias_ref)
            dalog_ref[...] = jnp.zeros_like(dalog_ref)
            ddsk_ref[...] = jnp.zeros_like(ddsk_ref)

        g_dt[...] = jnp.zeros_like(g_dt)
        g_cs[...] = jnp.zeros_like(g_cs)
        for p in range(N_PAIRS):
            sl, gsl = _pair_slices(p)
            pair = lambda *a, p=p: _ssd_pair(p, *a)
            _, vjp = jax.vjp(pair, xs_ref[:, sl], dt_s[...], cs_s[...], dsk_ref[...],
                             b_ref[:, gsl], c_ref[:, gsl], prev_ref[0, p])
            dxs, ddt, dcs, ddsk, dbg, dcg, dprev = vjp((dy_ref[:, sl], dstate[p]))
            dxs_ref[:, sl] = dxs
            g_dt[...] += ddt
            g_cs[...] += dcs
            ddsk_ref[...] += ddsk
            dstate[p] = dprev
            if p % PAIRS_PER_GROUP == 0:
                db_ref[:, gsl] = dbg
                dc_ref[:, gsl] = dcg
            else:
                db_ref[:, gsl] += dbg
                dc_ref[:, gsl] += dcg

        ddt_raw, dbias, dalog = vjp_prep((g_dt[...], g_cs[...]))
        ddt_ref[...] = ddt_raw
        dbias_ref[...] += dbias
        dalog_ref[...] += dalog

    sp = _ssd_specs(nc)
    gs = SSD_GROUPS * SSD_STATE
    return _hosted(
        body, rider, name=name, grid=(nc,),
        in_specs=[sp["wide"], sp["group"], sp["group"], sp["chunk"], sp["vec"], sp["vec"], sp["vec"],
                  sp["state"], sp["wide"]],
        out_specs=[sp["wide"], sp["group"], sp["group"], sp["chunk"], sp["vec"], sp["vec"], sp["vec"]],
        out_shape=[jax.ShapeDtypeStruct((t, D_MODEL), F32), jax.ShapeDtypeStruct((t, gs), F32),
                   jax.ShapeDtypeStruct((t, gs), F32), jax.ShapeDtypeStruct((t, LANES), F32),
                   jax.ShapeDtypeStruct((1, LANES), F32), jax.ShapeDtypeStruct((1, LANES), F32),
                   jax.ShapeDtypeStruct((1, LANES), F32)],
        scratch_shapes=[pltpu.VMEM((N_PAIRS, LANES, LANES), F32)] + [pltpu.VMEM((CHUNK, LANES), F32)] * 4,
        args=(xs, bm, cm, dt_raw, dt_bias, a_log, d_skip, prev_states, dy))


def _ssd_gate(y, z, w):
    return (_rms(y * _silu(z), w),)


def _sgu_norm(v, w, b):
    return (_layer_norm(_gelu(v), w, b),)


def _sgu_group(u, gate, vn, w, bcol):
    l = u.shape[0]
    wc = jnp.where(_iota2((l, l), 0) >= _iota2((l, l), 1), w, 0.0)
    return _gelu(u) * (_bdot_nn(wc, vn) + bcol) * _silu(gate)


def _sgu_fwd(name, u, gate, vn, w, bcol):
    t = u.shape[0]
    blk = _row_spec(D_MODEL, CHUNK)

    def body(u_ref, g_ref, vn_ref, w_ref, b_ref, y_ref):
        for g in range(SGU_GROUPS):
            sl = slice(g * LANES, (g + 1) * LANES)
            y_ref[:, sl] = _sgu_group(u_ref[:, sl], g_ref[:, sl], vn_ref[:, sl], w_ref[g], b_ref[g]).astype(y_ref.dtype)

    return pl.pallas_call(
        body, name=name, grid=(t // CHUNK,),
        in_specs=[blk, blk, blk, _full_spec(w), _full_spec(bcol)], out_specs=blk,
        out_shape=jax.ShapeDtypeStruct((t, D_MODEL), BF16),
        compiler_params=_cparams("parallel"),
    )(u, gate, vn, w, bcol)


def _sgu_bwd(name, u, gate, vn, w, bcol, dy, rider=None):
    t = u.shape[0]
    blk = _row_spec(D_MODEL, CHUNK)

    def body(u_ref, g_ref, vn_ref, w_ref, b_ref, dy_ref, du_ref, dg_ref, dvn_ref, dw_ref, db_ref):
        @pl.when(pl.program_id(0) == 0)
        def _():
            dw_ref[...] = jnp.zeros_like(dw_ref)
            db_ref[...] = jnp.zeros_like(db_ref)

        for g in range(SGU_GROUPS):
            sl = slice(g * LANES, (g + 1) * LANES)
            _, vjp = jax.vjp(_sgu_group, u_ref[:, sl], g_ref[:, sl], vn_ref[:, sl], w_ref[g], b_ref[g])
            du, dg, dvn, dw, db = vjp(dy_ref[:, sl])
            du_ref[:, sl] = du.astype(du_ref.dtype)
            dg_ref[:, sl] = dg.astype(dg_ref.dtype)
            dvn_ref[:, sl] = dvn
            dw_ref[g] += dw
            db_ref[g] += db

    return _hosted(
        body, rider, name=name, grid=(t // CHUNK,),
        in_specs=[blk, blk, blk, _full_spec(w), _full_spec(bcol), blk],
        out_specs=[blk, blk, blk, _full_spec(w), _full_spec(bcol)],
        out_shape=[jax.ShapeDtypeStruct((t, D_MODEL), BF16), jax.ShapeDtypeStruct((t, D_MODEL), BF16),
                   jax.ShapeDtypeStruct((t, D_MODEL), F32), jax.ShapeDtypeStruct(w.shape, F32),
                   jax.ShapeDtypeStruct(bcol.shape, F32)],
        scratch_shapes=[], args=(u, gate, vn, w, bcol, dy))


SB_SCALE = LANES ** -0.5
LOG2_E = 1.4426950408889634
SB_TQ = 512
SB_TS = 512


def _keep(mask, x):
    return x if mask is None else jnp.where(mask, x, 0.0)


def _sb_pieces(z2, mask):
    sp = jnp.maximum(z2, 0.0) + jnp.log(1.0 + jnp.exp2(-jnp.abs(z2))) * LOG2_E
    return _keep(mask, sp), z2 - sp


def _split2(x):
    hi = x.astype(BF16)
    return jnp.concatenate([hi, (x - hi.astype(F32)).astype(BF16)], axis=1)


def _tri2(cmp):
    sq = (CHUNK, CHUNK)
    m = cmp(_iota2(sq, 0), _iota2(sq, 1)).astype(BF16)
    return jnp.concatenate([m, m], axis=0)


def _tri_sums(blocks, tri2):
    res = _dg(jnp.concatenate([_split2(b) for b in blocks], axis=0), tri2, NN)
    out, at = [], 0
    for b in blocks:
        out.append(res[at:at + b.shape[0]])
        at += b.shape[0]
    return out


def _pad_top(x, start):
    return x if start == 0 else jnp.concatenate([jnp.zeros((start, x.shape[1]), x.dtype), x], axis=0)


def _rows_from(x, start):
    return x if start == 0 else x[start:]


def _sb_tiles(t):
    tq, ts = min(SB_TQ, t), min(SB_TS, t)
    assert t % tq == 0 and t % ts == 0 and ts % tq == 0 and ts % CHUNK == 0
    return tq, ts, ts // CHUNK


def _sb_logits(qb, ks, off, q_off, masked, trim=False):
    tq, ts = qb.shape[0], ks.shape[0]
    z2 = _dg(qb, ks, NT) * (SB_SCALE * LOG2_E)
    out = []
    for b in range(ts // CHUNK):
        mask, start = None, 0
        if masked:
            start = b * CHUNK if trim and tq == ts else 0
            mask = ((_iota2((tq - start, CHUNK), 1) + (off + b * CHUNK))
                    < (_iota2((tq - start, CHUNK), 0) + (q_off + start)))
        out.append(_sb_pieces(_rows_from(z2, start)[:, b * CHUNK:(b + 1) * CHUNK], mask) + (mask, start))
    return out


def _attn_fwd(name, q, k, v, gate, rider=None):
    t = q.shape[0]
    tq, ts, nb = _sb_tiles(t)

    def body(q_ref, k_ref, v_ref, g_ref, y_ref, tot_ref, yg_ref):
        i = pl.program_id(1)
        qb = q_ref[...]
        later2 = _tri2(lambda r, c: r > c)
        last = (i * tq + tq - 1) // ts

        def span(j, carry, masked):
            acc, after = carry
            off = pl.multiple_of(j * ts, ts)
            pieces = _sb_logits(qb, k_ref[pl.ds(off, ts), :], off, i * tq, masked)
            inside = _tri_sums([pc[0] for pc in pieces], later2)
            ws = [None] * nb
            for b in reversed(range(nb)):
                drop, ls, mask, start = pieces[b]
                ws[b] = _pad_top(_keep(mask, jnp.exp2(ls - (inside[b] + _rows_from(after, start)))).astype(BF16), start)
                after = after + _pad_top(jnp.sum(drop, axis=1, keepdims=True), start)
            acc = acc + _dg(jnp.concatenate(ws, axis=1), v_ref[pl.ds(off, ts), :], NN)
            return acc, after

        carry = span(last, (jnp.zeros((tq, LANES), F32), jnp.zeros((tq, 1), F32)), True)
        acc, tot = lax.fori_loop(0, last, lambda n, c: span(last - 1 - n, c, False), carry)
        y_ref[...] = acc
        tot_ref[...] = jnp.broadcast_to(tot, (tq, LANES))
        yg_ref[...] = (acc * _silu(g_ref[...])).astype(yg_ref.dtype)

    qsp = pl.BlockSpec((tq, LANES), lambda h, i: (i, h))
    kvsp = pl.BlockSpec((t, LANES), lambda h, i: (0, h))
    return _hosted(
        body, rider, name=name, grid=(SB_HEADS, t // tq),
        in_specs=[qsp, kvsp, kvsp, qsp], out_specs=[qsp, qsp, qsp],
        out_shape=[jax.ShapeDtypeStruct((t, D_MODEL), F32), jax.ShapeDtypeStruct((t, D_MODEL), F32),
                   jax.ShapeDtypeStruct((t, D_MODEL), BF16)],
        scratch_shapes=[], args=(q, k, v, gate))


def _attn_bwd(name, q, k, v, tot, y, gate, dyg, rider=None):
    t = q.shape[0]
    tq, ts, nb = _sb_tiles(t)
    nq, ns = t // tq, t // ts

    def body(q_ref, k_ref, v_ref, tot_ref, y_ref, g_ref, dyg_ref, dq_ref, dk_ref, dv_ref, dg_ref, dkt_acc, dvt_acc):
        i = pl.program_id(1)

        @pl.when(i == 0)
        def _():
            dkt_acc[...] = jnp.zeros_like(dkt_acc)
            dvt_acc[...] = jnp.zeros_like(dvt_acc)

        qb = q_ref[...]
        gate_v, dyg_v = g_ref[...], dyg_ref[...]
        dg_ref[...] = _dsilu(gate_v, dyg_v * y_ref[...]).astype(dg_ref.dtype)
        dy = dyg_v * _silu(gate_v)
        dyb = dy.astype(BF16)
        q_t = qb.astype(F32).T.astype(BF16)
        dy_t = dy.T.astype(BF16)
        totb = tot_ref[...]
        upto2 = _tri2(lambda r, c: r <= c)
        before2 = _tri2(lambda r, c: r < c)
        last = (i * tq + tq - 1) // ts

        def span(j, carry, masked):
            dq, lk_seen, e_seen = carry
            off = pl.multiple_of(j * ts, ts)
            ks = k_ref[pl.ds(off, ts), :]
            pieces = _sb_logits(qb, ks, off, i * tq, masked, trim=True)
            dw = _dg(dyb, v_ref[pl.ds(off, ts), :], NT)
            upto = _tri_sums([pc[0] for pc in pieces], upto2)
            ws, es = [], []
            for b in range(nb):
                drop, ls, mask, start = pieces[b]
                later = (_rows_from(totb, start) - upto[b]) - _rows_from(lk_seen, start)
                w = _keep(mask, jnp.exp2(ls - later))
                ws.append(_pad_top(w.astype(BF16), start))
                es.append(_rows_from(dw, start)[:, b * CHUNK:(b + 1) * CHUNK] * w)
                lk_seen = lk_seen + _pad_top(jnp.sum(drop, axis=1, keepdims=True), start)
            before = _tri_sums(es, before2)
            dzs = []
            for b in range(nb):
                _, ls, mask, start = pieces[b]
                sig = jnp.exp2(ls)
                dlk = _rows_from(e_seen, start) + before[b]
                dzs.append(_pad_top(_keep(mask, es[b] - (es[b] + dlk) * sig).astype(BF16), start))
                e_seen = e_seen + _pad_top(jnp.sum(es[b], axis=1, keepdims=True), start)
            dz = jnp.concatenate(dzs, axis=1)
            dq = dq + _dg(dz, ks, NN)
            dkt_acc[j] += _dg(q_t, dz, NN)
            dvt_acc[j] += _dg(dy_t, jnp.concatenate(ws, axis=1), NN)
            return dq, lk_seen, e_seen

        zero_col = jnp.zeros((tq, 1), F32)
        carry = lax.fori_loop(0, last, lambda j, c: span(j, c, False), (jnp.zeros((tq, LANES), F32), zero_col, zero_col))
        dq, _, _ = span(last, carry, True)
        dq_ref[...] = (dq * SB_SCALE).astype(dq_ref.dtype)

        @pl.when(i == nq - 1)
        def _():
            for s in range(ns):
                dk_ref[s * ts:(s + 1) * ts, :] = (dkt_acc[s].T * SB_SCALE).astype(dk_ref.dtype)
                dv_ref[s * ts:(s + 1) * ts, :] = dvt_acc[s].T.astype(dv_ref.dtype)

    qsp = pl.BlockSpec((tq, LANES), lambda h, i: (i, h))
    kvsp = pl.BlockSpec((t, LANES), lambda h, i: (0, h))
    return _hosted(
        body, rider, name=name, grid=(SB_HEADS, nq),
        in_specs=[qsp, kvsp, kvsp, qsp, qsp, qsp, qsp], out_specs=[qsp, kvsp, kvsp, qsp],
        out_shape=[jax.ShapeDtypeStruct((t, D_MODEL), BF16)] * 4,
        scratch_shapes=[pltpu.VMEM((ns, LANES, ts), F32), pltpu.VMEM((ns, LANES, ts), F32)],
        args=(q, k, v, tot, y, gate, dyg))


def _loss_head(name, h, target, w, tm):
    t, d = h.shape

    def body(h_ref, t_ref, w_ref, dh_ref, dw_ref, loss_ref):
        tgt = t_ref[...]

        def f(hv, wv):
            e = _rms(hv, wv) - tgt
            return 0.5 * jnp.mean(e * e, axis=-1, keepdims=True)

        row_loss, vjp = jax.vjp(f, h_ref[...], w_ref[...])
        dh, dw = vjp(jnp.ones_like(row_loss))
        dh_ref[...] = dh

        @pl.when(pl.program_id(0) == 0)
        def _():
            dw_ref[...] = jnp.zeros_like(dw_ref)
            loss_ref[...] = jnp.zeros_like(loss_ref)

        dw_ref[...] += dw
        loss_ref[...] += jnp.sum(row_loss, axis=0, keepdims=True)

    return pl.pallas_call(
        body, name=name, grid=(t // tm,),
        in_specs=[_row_spec(d, tm), _row_spec(d, tm), _full_spec(w)],
        out_specs=[_row_spec(d, tm), _full_spec(w), pl.BlockSpec((1, 1), lambda i: (0, 0))],
        out_shape=[jax.ShapeDtypeStruct((t, d), F32), jax.ShapeDtypeStruct(w.shape, F32),
                   jax.ShapeDtypeStruct((1, 1), F32)],
        compiler_params=_cparams("arbitrary"),
    )(h, target, w)


def _pick_tile(rows, cap):
    if rows <= cap:
        return rows
    for tm in range(cap - cap % 16, 0, -16):
        if rows % tm == 0:
            return tm
    raise ValueError(rows)


def _adamw(name, w, g, m, v, tm):
    n_l, r, c = w.shape
    tm = _pick_tile(r, tm)

    def body(w_ref, g_ref, m_ref, v_ref, d_ref, nm_ref, nv_ref):
        g_ = g_ref[...]
        m_ = ADAM_B1 * m_ref[...] + (1.0 - ADAM_B1) * g_
        v_ = ADAM_B2 * v_ref[...] + (1.0 - ADAM_B2) * (g_ * g_)
        m_hat = m_ / (1.0 - ADAM_B1 ** ADAM_STEP)
        v_hat = v_ / (1.0 - ADAM_B2 ** ADAM_STEP)
        d_ref[...] = -ADAM_LR * (m_hat / (jnp.sqrt(v_hat) + ADAM_EPS) + ADAM_WD * w_ref[...])
        nm_ref[...] = m_
        nv_ref[...] = v_

    spec = pl.BlockSpec((1, tm, c), lambda l, i: (l, i, 0))
    return pl.pallas_call(
        body, name=name, grid=(n_l, r // tm), in_specs=[spec] * 4, out_specs=[spec] * 3,
        out_shape=[jax.ShapeDtypeStruct((n_l, r, c), F32)] * 3, compiler_params=_cparams("parallel", "parallel"),
    )(w, g, m, v)


def _sum_parts(name, layers, tm):
    n, r, c = layers[0].shape
    tm = _pick_tile(r, tm)
    nblk = r // tm

    def body(*refs):
        o_ref = refs[-1]
        for li, p_ref in enumerate(refs[:-1]):
            @pl.when(pl.program_id(0) == li)
            def _(p_ref=p_ref):
                s = p_ref[0].astype(F32)
                for d in range(1, n):
                    s = s + p_ref[d].astype(F32)
                o_ref[0] = s

    def spec(li):
        return pl.BlockSpec((n, tm, c), lambda l, i: (0, jnp.where(l == li, i, jnp.where(l < li, 0, nblk - 1)), 0))

    return pl.pallas_call(
        body, name=name, grid=(len(layers), nblk),
        in_specs=[spec(li) for li in range(len(layers))],
        out_specs=pl.BlockSpec((1, tm, c), lambda l, i: (l, i, 0)),
        out_shape=jax.ShapeDtypeStruct((len(layers), r, c), F32), compiler_params=_cparams("arbitrary", "arbitrary"),
    )(*layers)


def _peer(k):
    x, y, c = lax.axis_index("x"), lax.axis_index("y"), lax.axis_index("c")
    px, py, pc = x ^ ((k >> 2) & 1), y ^ ((k >> 1) & 1), c ^ (k & 1)
    return (px, py, pc), 4 * px + 2 * py + pc


def _exchange(name, xs, gather):
    rider = _Rider(xs, gather)
    n = rider.n

    def body(*refs):
        x_refs, out_refs, sems = refs[:n], refs[n:2 * n], refs[2 * n:]
        _exchange_start(x_refs, out_refs, sems, gather)
        _exchange_finish(x_refs, out_refs, sems, gather)

    return pl.pallas_call(
        body, name=name, in_specs=[ANY_SPEC] * n, out_specs=[ANY_SPEC] * n,
        out_shape=rider.out_shape(), scratch_shapes=rider.scratch(),
    )(*xs)


ANY_SPEC = pl.BlockSpec(memory_space=pl.ANY)
SAME_CORE = (2, 4, 6)


class _Rider:
    def __init__(self, xs, gather):
        self.xs, self.gather, self.n = list(xs), gather, len(xs)

    def out_shape(self):
        return [jax.ShapeDtypeStruct((N_DEV,) + tuple(x.shape if self.gather else x.shape[1:]), x.dtype)
                for x in self.xs]

    def scratch(self):
        return [pltpu.SemaphoreType.DMA((self.n, N_DEV - 1)), pltpu.SemaphoreType.DMA((self.n, N_DEV - 1)),
                pltpu.SemaphoreType.DMA((self.n,))]


def _remote(src, dst, sems, a, k, dev):
    return pltpu.make_async_remote_copy(
        src_ref=src, dst_ref=dst, send_sem=sems[0].at[a, k - 1], recv_sem=sems[1].at[a, k - 1],
        device_id=dev, device_id_type=pl.DeviceIdType.MESH)


def _exchange_start(x_refs, out_refs, sems, gather):
    _, me = _peer(0)
    for a, (x, out) in enumerate(zip(x_refs, out_refs)):
        pltpu.make_async_copy(x if gather else x.at[me], out.at[me], sems[2].at[a]).start()
    for k in ((1,) + SAME_CORE if gather else range(1, N_DEV)):
        dev, idx = _peer(k)
        for a, (x, out) in enumerate(zip(x_refs, out_refs)):
            _remote(x if gather else x.at[idx], out.at[me], sems, a, k, dev).start()


def _exchange_finish(x_refs, out_refs, sems, gather):
    _, me = _peer(0)
    sibling, _ = _peer(1)
    pairs = list(enumerate(zip(x_refs, out_refs)))
    waited = ()
    if gather:
        for k in SAME_CORE:
            dev, idx = _peer(k)
            for a, (x, out) in pairs:
                _remote(x, out.at[idx], sems, a, k, dev).wait_recv()
                _remote(out.at[idx], out.at[idx], sems, a, k + 1, sibling).start()
        waited = SAME_CORE
    for k in range(1, N_DEV):
        dev, idx = _peer(k)
        for a, (x, out) in pairs:
            cp = _remote(x if gather else x.at[idx], out.at[idx], sems, a, k, dev)
            if k not in waited:
                cp.wait_recv()
            cp.wait_send()
    for a, (x, out) in pairs:
        pltpu.make_async_copy(x if gather else x.at[me], out.at[me], sems[2].at[a]).wait()


def _hosted(body, rider, *, name, grid, in_specs, out_specs, out_shape, scratch_shapes, args):
    sem = ("arbitrary",) * len(grid)
    if rider is None:
        return pl.pallas_call(body, name=name, grid=grid, in_specs=in_specs, out_specs=out_specs, out_shape=out_shape,
                              scratch_shapes=scratch_shapes, compiler_params=_cparams(*sem))(*args), []
    n_in, n_out, n_scr, nr = len(in_specs), len(out_specs), len(scratch_shapes), rider.n

    def hosted(*refs):
        ins, refs = refs[:n_in], refs[n_in:]
        r_in, refs = refs[:nr], refs[nr:]
        outs, refs = refs[:n_out], refs[n_out:]
        r_out, refs = refs[:nr], refs[nr:]
        scr, sems = refs[:n_scr], refs[n_scr:]
        first = pl.program_id(0) == 0
        last = pl.program_id(0) == grid[0] - 1
        for ax in range(1, len(grid)):
            first = first & (pl.program_id(ax) == 0)
            last = last & (pl.program_id(ax) == grid[ax] - 1)

        @pl.when(first)
        def _():
            _exchange_start(r_in, r_out, sems, rider.gather)

        body(*ins, *outs, *scr)

        @pl.when(last)
        def _():
            _exchange_finish(r_in, r_out, sems, rider.gather)

    res = pl.pallas_call(
        hosted, name=name, grid=grid, in_specs=list(in_specs) + [ANY_SPEC] * nr,
        out_specs=list(out_specs) + [ANY_SPEC] * nr, out_shape=list(out_shape) + rider.out_shape(),
        scratch_shapes=list(scratch_shapes) + rider.scratch(), compiler_params=_cparams(*sem),
    )(*args, *rider.xs)
    return list(res[:n_out]), list(res[n_out:])


PACK_ALIGN = 8 * LANES
PACK_ROWS = 512


def _pack(arrays, dtype):
    pieces, total = [], 0
    for a in arrays:
        f = a.reshape(-1).astype(dtype)
        pad = (-f.shape[0]) % PACK_ALIGN
        pieces.append(jnp.pad(f, (0, pad)) if pad else f)
        total += f.shape[0] + pad
    tail = (-total) % (PACK_ROWS * LANES)
    if tail:
        pieces.append(jnp.zeros((tail,), dtype))
    return jnp.concatenate(pieces).reshape(-1, LANES)


def _unpack(packed, shapes, lead=()):
    flat = packed.reshape(lead + (-1,))
    out, off = [], 0
    for s in shapes:
        n = 1
        for d in s:
            n *= d
        out.append(flat[..., off:off + n].reshape(lead + tuple(s)))
        off += n + ((-n) % PACK_ALIGN)
    return out


def _pad_lanes(a):
    return jnp.pad(a, (0, LANES - a.shape[0])).reshape(1, LANES)


ROW_TM = 256
EVEN_SEGS = (("z", D_MODEL), ("xbc", CONV_DIM), ("dt", SSD_HEADS), ("g", D_MODEL), ("u", D_MODEL), ("v", D_MODEL))
ODD_SEGS = (("q", D_MODEL), ("k", D_MODEL), ("v", D_MODEL), ("g", D_MODEL))
TAIL_ROWS = (368, 512, 64, 368, 368, 368)


def _split_cols(w, segs):
    out, off = {}, 0
    for nm, n in segs:
        out[nm] = w[:, off:off + n]
        off += n
    return out


def _rms_fn(h, w):
    return (_rms(h, w),)


def _even_fwd(tag, h, p, riders=None, late=None):
    riders = riders or {}
    got = {}
    hn, = _rows_fwd(tag + "_norm", _rms_fn, [h], [p["norm_w"]], [(D_MODEL, BF16)], ROW_TM)
    proj = {nm: _matmul(f"{tag}_in_{nm}", hn, p["w_in"][nm], NN, F32, rider=riders.get("in_" + nm))
            for nm, _ in EVEN_SEGS}
    for nm, _ in EVEN_SEGS:
        if "in_" + nm in riders:
            proj[nm], got["in_" + nm] = proj[nm]
    (pre, xs, bm, cm), got["conv"] = _conv_fwd(tag + "_conv", proj["xbc"], p["conv_w"], p["conv_b"], riders.get("conv"))
    (y_ssd, states), got["ssd"] = _ssd_fwd(tag + "_ssd", xs, bm, cm, proj["dt"], p["dt_bias"], p["a_log"],
                                           p["d_skip"], riders.get("ssd"))
    if late is not None:
        p = dict(p, **late(got))
    ya, = _rows_fwd(tag + "_ssdgate", _ssd_gate, [y_ssd, proj["z"]], [p["ssd_norm_w"]], [(D_MODEL, BF16)], ROW_TM)
    vn, = _rows_fwd(tag + "_sgunorm", _sgu_norm, [proj["v"]], [p["sgu_ln_w"], p["sgu_ln_b"]], [(D_MODEL, F32)], ROW_TM)
    yb = _sgu_fwd(tag + "_sgu", proj["u"], proj["g"], vn, p["sgu_w"], p["sgu_b"])
    h1 = _matmul(tag + "_out_a", ya, p["w_out_a"], NN, F32, add=h)
    h2 = _matmul(tag + "_out_b", yb, p["w_out_b"], NN, F32, add=h1)
    saved = dict(h=h, hn=hn, proj=proj, pre=pre, xs=xs, bm=bm, cm=cm, y_ssd=y_ssd, states=states,
                 ya=ya, vn=vn, yb=yb)
    return h2, saved, p, got


def _even_bwd(tag, dh, s, p, riders=None):
    riders = riders or {}
    g, got = {}, {}

    def rider_for(host):
        return riders[host](g) if host in riders else None

    dh16 = dh.astype(BF16)
    proj = s["proj"]
    dya = _matmul(tag + "_dya", dh16, p["w_out_a"], NT, F32)
    dyb = _matmul(tag + "_dyb", dh16, p["w_out_b"], NT, F32)
    g["w_out_a"] = _matmul(tag + "_dwout_a", s["ya"], dh16, TN, BF16)
    g["w_out_b"] = _matmul(tag + "_dwout_b", s["yb"], dh16, TN, BF16)
    (du, dg, dvn, g["sgu_w"], g["sgu_b"]), got["sgu_b"] = _sgu_bwd(
        tag + "_sgu_b", proj["u"], proj["g"], s["vn"], p["sgu_w"], p["sgu_b"], dyb, rider_for("sgu_b"))
    dv, g["sgu_ln_w"], g["sgu_ln_b"] = _rows_bwd(tag + "_sgunorm_b", _sgu_norm, [proj["v"]],
                                                 [p["sgu_ln_w"], p["sgu_ln_b"]], [dvn], [BF16], ROW_TM // 2)
    dy_ssd, dz, g["ssd_norm_w"] = _rows_bwd(tag + "_ssdgate_b", _ssd_gate, [s["y_ssd"], proj["z"]],
                                            [p["ssd_norm_w"]], [dya], [F32, BF16], ROW_TM // 2)
    (dxs, dbm, dcm, ddt, g["dt_bias"], g["a_log"], g["d_skip"]), got["ssd_b"] = _ssd_bwd(
        tag + "_ssd_b", s["xs"], s["bm"], s["cm"], proj["dt"], p["dt_bias"], p["a_log"], p["d_skip"],
        s["states"], dy_ssd, rider_for("ssd_b"))
    (dxbc, g["conv_w"], g["conv_b"]), got["conv_b"] = _conv_bwd(
        tag + "_conv_b", proj["xbc"], s["pre"], p["conv_w"], dxs, dbm, dcm, rider_for("conv_b"))
    dproj = dict(z=dz, xbc=dxbc, dt=ddt.astype(BF16), g=dg, u=du, v=dv)
    g["w_in"] = {nm: _matmul(f"{tag}_dwin_{nm}", s["hn"], dproj[nm], TN, BF16) for nm, _ in EVEN_SEGS}
    tail = rider_for("tail") or [None] * len(EVEN_SEGS)
    dhn, got["tail"] = None, []
    for (nm, _), rider in zip(EVEN_SEGS, tail):
        dhn = _matmul(f"{tag}_dhn_{nm}", dproj[nm], p["w_in"][nm], NT, F32, add=dhn, rider=rider)
        if rider is not None:
            dhn, arrived = dhn
            got["tail"] += arrived
    dh_in, g["norm_w"] = _rows_bwd(tag + "_norm_b", _rms_fn, [s["h"]], [p["norm_w"]], [dhn], [F32],
                                   ROW_TM // 2, add=(0, dh))
    return dh_in, g, got


def _odd_fwd(tag, h, p, rider=None):
    hn, = _rows_fwd(tag + "_norm", _rms_fn, [h], [p["norm_w"]], [(D_MODEL, BF16)], ROW_TM)
    q = _matmul(tag + "_in_q", hn, p["w_in"]["q"], NN, BF16)
    k = _matmul(tag + "_in_k", hn, p["w_in"]["k"], NN, BF16)
    v = _matmul(tag + "_in_v", hn, p["w_in"]["v"], NN, BF16)
    gate = _matmul(tag + "_in_g", hn, p["w_in"]["g"], NN, F32)
    (y, tot, yg), got = _attn_fwd(tag + "_attn", q, k, v, gate, rider)
    h1 = _matmul(tag + "_out", yg, p["w_out"], NN, F32, add=h)
    return h1, dict(h=h, hn=hn, q=q, k=k, v=v, gate=gate, y=y, tot=tot, yg=yg), got


def _odd_bwd(tag, dh, s, p, rider=None):
    g = {}
    dh16 = dh.astype(BF16)
    dyg = _matmul(tag + "_dyg", dh16, p["w_out"], NT, F32)
    g["w_out"] = _matmul(tag + "_dwout", s["yg"], dh16, TN, BF16)
    (dq, dk, dv, dgate), got = _attn_bwd(tag + "_attn_b", s["q"], s["k"], s["v"], s["tot"], s["y"], s["gate"],
                                         dyg, rider)
    dproj = dict(q=dq, k=dk, v=dv, g=dgate)
    dhn = None
    g["w_in"] = {}
    for nm, _ in ODD_SEGS:
        dhn = _matmul(f"{tag}_dhn_{nm}", dproj[nm], p["w_in"][nm], NT, F32, add=dhn)
        g["w_in"][nm] = _matmul(f"{tag}_dwin_{nm}", s["hn"], dproj[nm], TN, BF16)
    dh_in, g["norm_w"] = _rows_bwd(tag + "_norm_b", _rms_fn, [s["h"]], [p["norm_w"]], [dhn], [F32],
                                   ROW_TM // 2, add=(0, dh))
    return dh_in, g, got


BIG = ("ev_w_in", "ev_w_out", "od_w_in", "od_w_out")
SMALL = ("norm_w", "final_norm_w", "ev_conv_b", "ev_dt_bias", "ev_a_log", "ev_d_skip", "ev_ssd_norm_w",
         "ev_sgu_ln_w", "ev_sgu_ln_b", "ev_sgu_w", "ev_sgu_b")
WEIGHTS = ("norm_w", "final_norm_w", "ev_w_in", "ev_conv_w", "ev_conv_b", "ev_dt_bias", "ev_a_log", "ev_d_skip",
           "ev_ssd_norm_w", "ev_sgu_ln_w", "ev_sgu_ln_b", "ev_sgu_w", "ev_sgu_b", "ev_w_out", "od_w_in", "od_w_out")


def _step(w, m, v, x, loss_target):
    h = x[0]
    tgt = loss_target[0]
    n_even, n_odd = w["ev_w_in"].shape[0], w["od_w_in"].shape[0]
    depth = n_even + n_odd
    assert (n_even, n_odd) == (2, 2), "the exchange schedule below is written for the four-layer trunk"

    def shard(n, i):
        return w[n][i].astype(BF16)

    def take_cols(blocks, widths, lo, hi, r0=0, r1=None):
        pieces, start = [], 0
        for blk, wd in zip(blocks, widths):
            a, b = max(lo - start, 0), min(hi - start, wd)
            if a < b:
                pieces.append(blk[r0:r1, a:b])
            start += wd
        return pieces[0] if len(pieces) == 1 else jnp.concatenate(pieces, axis=1)

    def segments(gathered, segs):
        n = gathered.shape[2]
        out, lo = {}, 0
        for nm, wd in segs:
            out[nm] = take_cols([gathered[d] for d in range(N_DEV)], [n] * N_DEV, lo, lo + wd)
            lo += wd
        return out

    def rows(gathered):
        return gathered.reshape(-1, gathered.shape[2])

    def by_owner_cols(g_in, segs, r0=0, r1=None):
        blocks, widths = [g_in[nm] for nm, _ in segs], [wd for _, wd in segs]
        n = sum(widths) // N_DEV
        return jnp.stack([take_cols(blocks, widths, d * n, (d + 1) * n, r0, r1) for d in range(N_DEV)])

    def by_owner_rows(full):
        return full.reshape(N_DEV, full.shape[0] // N_DEV, full.shape[1])

    def even_params(layer, ev_in):
        i = layer // 2
        w_in = segments(ev_in, EVEN_SEGS)
        w_in["dt"] = jnp.pad(w_in["dt"], ((0, 0), (0, LANES - SSD_HEADS)))
        return dict(norm_w=w["norm_w"][layer][None], w_in=w_in, conv_w=conv_w[i], conv_b=w["ev_conv_b"][i][None],
                    dt_bias=_pad_lanes(w["ev_dt_bias"][i]), a_log=_pad_lanes(w["ev_a_log"][i]),
                    d_skip=_pad_lanes(w["ev_d_skip"][i]), ssd_norm_w=w["ev_ssd_norm_w"][i][None],
                    sgu_ln_w=w["ev_sgu_ln_w"][i][None], sgu_ln_b=w["ev_sgu_ln_b"][i][None],
                    sgu_w=w["ev_sgu_w"][i], sgu_b=w["ev_sgu_b"][i][:, :, None])

    def even_out_params(ev_out):
        full = rows(ev_out)
        return dict(w_out_a=full[:D_MODEL], w_out_b=full[D_MODEL:])

    def odd_params(layer, od_in, od_out):
        return dict(norm_w=w["norm_w"][layer][None], w_in=segments(od_in, ODD_SEGS), w_out=rows(od_out))

    def even_in_grads(g):
        return by_owner_cols(g["w_in"], EVEN_SEGS)

    def even_out_grads(g):
        return by_owner_rows(jnp.concatenate([g["w_out_a"], g["w_out_b"]], axis=0))

    def odd_grads(g):
        return [by_owner_cols(g["w_in"], ODD_SEGS), by_owner_rows(g["w_out"])]

    half = w["ev_w_out"].shape[1] // 2
    ev_in0, conv_w = _exchange("gather_first", [shard("ev_w_in", 0), w["ev_conv_w"]], gather=True)
    conv_w = jnp.moveaxis(conv_w, 0, 2).reshape(n_even, CONV_WIDTH, CONV_DIM)
    h, s0, p0, got = _even_fwd(
        "l0", h, even_params(0, ev_in0),
        {"in_xbc": _Rider([shard("ev_w_out", 0)[:half]], True),
         "conv": _Rider([shard("ev_w_out", 0)[half:]], True),
         "ssd": _Rider([shard("od_w_in", 0), shard("od_w_out", 0)], True)},
        late=lambda arrived: even_out_params(jnp.concatenate([arrived["in_xbc"][0], arrived["conv"][0]], axis=1)))
    p1 = odd_params(1, *got["ssd"])
    h, s1, got = _odd_fwd("l1", h, p1, _Rider([shard("ev_w_in", 1), shard("ev_w_out", 1)], True))
    p2 = dict(even_params(2, got[0]), **even_out_params(got[1]))
    h, s2, p2, got = _even_fwd("l2", h, p2, {"ssd": _Rider([shard("od_w_in", 1), shard("od_w_out", 1)], True)})
    p3 = odd_params(3, *got["ssd"])
    h, s3, _ = _odd_fwd("l3", h, p3)

    dh, d_final, loss_part = _loss_head("loss_head", h, tgt, w["final_norm_w"][None], ROW_TM // 2)
    loss = lax.psum(loss_part[0, 0], ("x", "y", "c"))

    def last_grads(g):
        bounds = [0]
        for rows_here in TAIL_ROWS:
            bounds.append(bounds[-1] + rows_here)
        assert bounds[-1] == D_MODEL
        return [_Rider([by_owner_cols(g["w_in"], EVEN_SEGS, r0, r1)], False) for r0, r1 in zip(bounds, bounds[1:])]

    dh, g3, _ = _odd_bwd("l3", dh, s3, p3)
    dh, g2, got = _even_bwd("l2", dh, s2, p2, {"ssd_b": lambda g: _Rider(odd_grads(g3), False)})
    from3 = got["ssd_b"]
    dh, g1, from2 = _odd_bwd("l1", dh, s1, p1, _Rider([even_in_grads(g2), even_out_grads(g2)], False))
    dh, g0, got = _even_bwd("l0", dh, s0, p0, {
        "sgu_b": lambda g: _Rider([even_out_grads(g)[:, :half]], False),
        "ssd_b": lambda g: _Rider(odd_grads(g1), False),
        "conv_b": lambda g: _Rider([even_out_grads(g)[:, half:]], False),
        "tail": last_grads})
    from1 = got["ssd_b"] + [jnp.concatenate([got["sgu_b"][0], got["conv_b"][0]], axis=1)]
    from0 = jnp.concatenate(got["tail"], axis=1)
    grad_x = dh[None]
    lg = [g0, g1, g2, g3]
    ev, od = [g0, g2], [g1, g3]

    def total(tag, parts):
        return _sum_parts("sum_" + tag, parts, 256)

    big_grads = {
        "ev_w_in": total("ev_w_in", [from0, from2[0]]),
        "ev_w_out": total("ev_w_out", [from1[2], from2[1]]),
        "od_w_in": total("od_w_in", [from1[0], from3[0]]),
        "od_w_out": total("od_w_out", [from1[1], from3[1]]),
    }

    small_g = {
        "norm_w": jnp.concatenate([lg[l]["norm_w"] for l in range(depth)], axis=0),
        "final_norm_w": d_final[0],
        "ev_conv_b": jnp.concatenate([e["conv_b"] for e in ev], axis=0),
        "ev_dt_bias": jnp.concatenate([e["dt_bias"][:, :SSD_HEADS] for e in ev], axis=0),
        "ev_a_log": jnp.concatenate([e["a_log"][:, :SSD_HEADS] for e in ev], axis=0),
        "ev_d_skip": jnp.concatenate([e["d_skip"][:, :SSD_HEADS] for e in ev], axis=0),
        "ev_ssd_norm_w": jnp.concatenate([e["ssd_norm_w"] for e in ev], axis=0),
        "ev_sgu_ln_w": jnp.concatenate([e["sgu_ln_w"] for e in ev], axis=0),
        "ev_sgu_ln_b": jnp.concatenate([e["sgu_ln_b"] for e in ev], axis=0),
        "ev_sgu_w": jnp.stack([e["sgu_w"] for e in ev]),
        "ev_sgu_b": jnp.stack([e["sgu_b"][:, :, 0] for e in ev]),
    }
    conv_g = jnp.stack([e["conv_w"] for e in ev])
    small_shapes = [w[n].shape for n in SMALL]
    small_parts, = _exchange("gather_small", [_pack([small_g[n] for n in SMALL] + [conv_g], F32)], gather=True)
    small_sum = _sum_parts("sum_small", [small_parts], 1024)[0]
    *small_list, conv_full = _unpack(small_sum, small_shapes + [conv_g.shape])
    grads = dict(zip(SMALL, small_list))
    grads.update(big_grads)
    me = 4 * lax.axis_index("x") + 2 * lax.axis_index("y") + lax.axis_index("c")
    n_cv = w["ev_conv_w"].shape[2]
    grads["ev_conv_w"] = lax.dynamic_slice_in_dim(conv_full, me * n_cv, n_cv, axis=2)

    deltas, new_m, new_v = {}, {}, {}
    for n in BIG + ("ev_conv_w",):
        deltas[n], new_m[n], new_v[n] = _adamw("adamw_" + n, w[n], grads[n], m[n], v[n], 256)
    packs = [_pack([src[n] for n in SMALL], F32)[None] for src in (w, grads, m, v)]
    outs = _adamw("adamw_small", *packs, 1024)
    for dst, packed in zip((deltas, new_m, new_v), outs):
        dst.update(zip(SMALL, _unpack(packed[0], small_shapes)))
    return loss, grad_x, grads, deltas, new_m, new_v


def kernel(x, norm_w, final_norm_w, ev_w_in, ev_conv_w, ev_conv_b, ev_dt_bias, ev_a_log, ev_d_skip, ev_ssd_norm_w, ev_sgu_ln_w, ev_sgu_ln_b, ev_sgu_w, ev_sgu_b, ev_w_out, od_w_in, od_w_out, loss_target, m_norm_w, m_final_norm_w, m_ev_w_in, m_ev_conv_w, m_ev_conv_b, m_ev_dt_bias, m_ev_a_log, m_ev_d_skip, m_ev_ssd_norm_w, m_ev_sgu_ln_w, m_ev_sgu_ln_b, m_ev_sgu_w, m_ev_sgu_b, m_ev_w_out, m_od_w_in, m_od_w_out, v_norm_w, v_final_norm_w, v_ev_w_in, v_ev_conv_w, v_ev_conv_b, v_ev_dt_bias, v_ev_a_log, v_ev_d_skip, v_ev_ssd_norm_w, v_ev_sgu_ln_w, v_ev_sgu_ln_b, v_ev_sgu_w, v_ev_sgu_b, v_ev_w_out, v_od_w_in, v_od_w_out):
    w = dict(zip(WEIGHTS, (norm_w, final_norm_w, ev_w_in, ev_conv_w, ev_conv_b, ev_dt_bias, ev_a_log, ev_d_skip,
                           ev_ssd_norm_w, ev_sgu_ln_w, ev_sgu_ln_b, ev_sgu_w, ev_sgu_b, ev_w_out, od_w_in, od_w_out)))
    m = dict(zip(WEIGHTS, (m_norm_w, m_final_norm_w, m_ev_w_in, m_ev_conv_w, m_ev_conv_b, m_ev_dt_bias, m_ev_a_log,
                           m_ev_d_skip, m_ev_ssd_norm_w, m_ev_sgu_ln_w, m_ev_sgu_ln_b, m_ev_sgu_w, m_ev_sgu_b,
                           m_ev_w_out, m_od_w_in, m_od_w_out)))
    v = dict(zip(WEIGHTS, (v_norm_w, v_final_norm_w, v_ev_w_in, v_ev_conv_w, v_ev_conv_b, v_ev_dt_bias, v_ev_a_log,
                           v_ev_d_skip, v_ev_ssd_norm_w, v_ev_sgu_ln_w, v_ev_sgu_ln_b, v_ev_sgu_w, v_ev_sgu_b,
                           v_ev_w_out, v_od_w_in, v_od_w_out)))
    loss, grad_x, grads, deltas, new_m, new_v = _step(w, m, v, x, loss_target)
    return (loss, grad_x, *[grads[n] for n in WEIGHTS], *[deltas[n] for n in WEIGHTS],
            *[new_m[n] for n in WEIGHTS], *[new_v[n] for n in WEIGHTS])
```
